```python
import math
import jax, jax.numpy as jnp
from jax import lax
import numpy as np

D_MODEL = 1024
BATCH = 4
SEQ = 8192
DEPTH = 4

HY_WIDTH = 512
HY_ORDER = 2
HY_EMB = 33
HY_FILT_HIDDEN = 64
HY_FAST_DECAY = 0.3
HY_SLOW_DECAY = 1.5
HY_DECAY_TARGET = 1e-2
RET_HEADS = 4
RET_DK = 64
RET_DV = 128
RET_QK = RET_HEADS * RET_DK
RET_V = RET_HEADS * RET_DV
RET_CHUNK = 128
RET_DECAY_FWD = 5.0
RET_DECAY_BWD = 5.5
ROPE_BASE = 10000.0
N_BRANCH = 2
IN_COLS = 3 * HY_WIDTH + 2 * RET_QK + 2 * RET_V + N_BRANCH * D_MODEL
N_EXPERTS = 32
TOP_K = 4
D_FF = D_MODEL
SWIGLU_ALPHA = 1.702
SWIGLU_LIMIT = 7.0
MOE_BLOCK = 256
N_MOD = 6
EPS = 1e-6

kernel_name = "hyena_retention_moe_adaln_encoder"


def rms_norm(x, g):
    xf = x.astype(jnp.float32)
    y = xf * lax.rsqrt(jnp.mean(xf * xf, axis=-1, keepdims=True) + EPS)
    return (y * g.astype(jnp.float32)).astype(x.dtype)


def hyena_pos_features(L):
    f32 = jnp.float32
    t = jnp.linspace(0.0, 1.0, L, dtype=f32)[:, None]
    bands = (HY_EMB - 1) // 2
    w = 2.0 * jnp.pi * jnp.arange(L, dtype=f32)[:, None] / L
    f = jnp.linspace(1e-4, bands - 1, bands, dtype=f32)[None, :]
    feats = jnp.concatenate([t, jnp.cos(f * w), -jnp.sin(f * w)], axis=-1)
    return feats, t[:, 0]


def hyena_filters(pos_feat, t_norm, w1, b1, w2, b2, w3, b3, w4, freq):
    f32 = jnp.float32
    L = pos_feat.shape[0]
    fr = freq.astype(f32)
    h = jnp.sin(fr * (pos_feat @ w1.astype(f32) + b1.astype(f32)))
    h = jnp.sin(fr * (h @ w2.astype(f32) + b2.astype(f32)))
    h = jnp.sin(fr * (h @ w3.astype(f32) + b3.astype(f32)))
    h = (h @ w4.astype(f32)).reshape(L, 2, HY_ORDER, HY_WIDTH)
    max_decay = math.log(HY_DECAY_TARGET) / HY_FAST_DECAY
    min_decay = math.log(HY_DECAY_TARGET) / HY_SLOW_DECAY
    deltas = jnp.abs(jnp.linspace(min_decay, max_decay, HY_WIDTH, dtype=f32))
    window = jnp.exp(-t_norm[:, None] * deltas[None, :])
    h = h * window[:, None, None, :]
    h_fwd = h[:, 0]
    h_bwd = h[1:, 1]
    l1 = jnp.sum(jnp.abs(h_fwd), axis=0) + jnp.sum(jnp.abs(h_bwd), axis=0)
    kern = jnp.concatenate(
        [h_fwd, jnp.zeros((1, HY_ORDER, HY_WIDTH), f32), h_bwd[::-1]], axis=0) / l1
    return jnp.fft.rfft(kern, axis=0)


def fft_long_conv(z, k_f):
    S = z.shape[1]
    zf = jnp.fft.rfft(z.astype(jnp.float32), n=2 * S, axis=1)
    y = jnp.fft.irfft(zf * k_f[None], n=2 * S, axis=1)[:, :S]
    return y.astype(z.dtype)


def short_conv_centered(u, w, b):
    up = jnp.pad(u, ((0, 0), (1, 1), (0, 0)))
    return up[:, :-2] * w[0] + up[:, 1:-1] * w[1] + up[:, 2:] * w[2] + b


def hyena_mixer(u, conv_w, conv_b, w1, b1, w2, b2, w3, b3, w4, freq, bias, pos_feat, t_norm):
    u = short_conv_centered(u, conv_w, conv_b)
    x1, x2, v = jnp.split(u, 3, axis=-1)
    k_f = hyena_filters(pos_feat, t_norm, w1, b1, w2, b2, w3, b3, w4, freq)
    z = v
    for o, gate in enumerate((x1, x2)):
        z = gate * (fft_long_conv(z, k_f[:, o]) + z * bias[o])
    return z


def rope(x, cos, sin):
    half = x.shape[-1] // 2
    x1, x2 = x[..., :half], x[..., half:]
    return jnp.concatenate([x1 * cos - x2 * sin, x2 * cos + x1 * sin], axis=-1)


def retention_log_decay(offset):
    return jnp.log(1.0 - jnp.exp2(-(offset + jnp.arange(RET_HEADS, dtype=jnp.float32))))


def retention_one_direction(q, k, v, log_gamma, include_diag):
    B, H, S, DK = q.shape
    DV = v.shape[-1]
    C = RET_CHUNK
    N = S // C
    qc = q.reshape(B, H, N, C, DK)
    kc = k.reshape(B, H, N, C, DK)
    vc = v.reshape(B, H, N, C, DV)
    pos = jnp.arange(C, dtype=jnp.float32)
    diff = pos[:, None] - pos[None, :]
    mask = (diff >= 0) if include_diag else (diff > 0)
    decay = jnp.where(mask[None], jnp.exp(jnp.where(mask, diff, 0.0)[None] * log_gamma[:, None, None]), 0.0)
    scores = jnp.einsum('bhnid,bhnjd->bhnij', qc, kc) * decay[None, :, None]
    inner = jnp.einsum('bhnij,bhnjv->bhniv', scores, vc)
    k_w = jnp.exp((C - 1 - pos)[None, :] * log_gamma[:, None])
    chunk_sum = jnp.einsum('bhnjd,hj,bhnjv->bhndv', kc, k_w, vc)
    chunk_decay = jnp.exp(C * log_gamma)[None, :, None, None]

    def step(state, u):
        return state * chunk_decay + u, state

    init = jnp.zeros((B, H, DK, DV), chunk_sum.dtype)
    _, prev = lax.scan(step, init, jnp.moveaxis(chunk_sum, 2, 0))
    prev = jnp.moveaxis(prev, 0, 2)
    q_w = jnp.exp((pos + 1.0)[None, :] * log_gamma[:, None])
    cross = jnp.einsum('bhnid,hi,bhndv->bhniv', qc, q_w, prev)
    return (inner + cross).reshape(B, H, S, DV)


def retention_mixer(q, k, v, g, cos, sin):
    B, S, _ = q.shape

    def heads(t, d):
        return t.reshape(B, S, RET_HEADS, d).transpose(0, 2, 1, 3)

    qh = rope(heads(q, RET_DK), cos, sin)
    kh = rope(heads(k, RET_DK), cos, sin) * (RET_DK ** -0.5)
    vh = heads(v, RET_DV)
    fwd = retention_one_direction(qh, kh, vh, retention_log_decay(RET_DECAY_FWD), True)
    flip = lambda t: jnp.flip(t, axis=2)
    bwd = flip(retention_one_direction(flip(qh), flip(kh), flip(vh), retention_log_decay(RET_DECAY_BWD), False))
    o = (fwd + bwd).astype(jnp.float32)
    o = o * lax.rsqrt(jnp.mean(o * o, axis=-1, keepdims=True) + EPS)
    o = o.astype(q.dtype).transpose(0, 2, 1, 3).reshape(B, S, RET_V)
    return jax.nn.silu(g) * o


def mixer_block(h, w_in, hy_conv_w, hy_conv_b, hy_w1, hy_b1, hy_w2, hy_b2, hy_w3, hy_b3, hy_w4,
                hy_freq, hy_bias, w_hy_br, w_ret_br, w_out, pos_feat, t_norm, cos, sin):
    proj = h @ w_in
    s0 = 3 * HY_WIDTH
    s1 = s0 + RET_QK
    s2 = s1 + RET_QK
    s3 = s2 + RET_V
    s4 = s3 + RET_V
    u_hy, q, k, v, g_ret, g_merge = jnp.split(proj, [s0, s1, s2, s3, s4], axis=-1)
    y_hy = hyena_mixer(u_hy, hy_conv_w, hy_conv_b, hy_w1, hy_b1, hy_w2, hy_b2, hy_w3, hy_b3,
                       hy_w4, hy_freq, hy_bias, pos_feat, t_norm)
    y_ret = retention_mixer(q, k, v, g_ret, cos, sin)
    g_hy, g_rt = jnp.split(jax.nn.sigmoid(g_merge.astype(jnp.float32)).astype(h.dtype), N_BRANCH, axis=-1)
    merged = g_hy * (y_hy @ w_hy_br) + g_rt * (y_ret @ w_ret_br)
    return merged @ w_out


def moe_ffn(h, w_router, b_router, w_gate_up, b_gate_up, w_down, b_down):
    B, S, D = h.shape
    T = B * S
    xs = h.reshape(T, D)
    logits = (xs @ w_router + b_router).astype(jnp.float32)
    top_val, top_idx = lax.top_k(logits, TOP_K)
    gates = jax.nn.softmax(top_val, axis=-1)
    n_assign = T * TOP_K
    flat_e = top_idx.reshape(-1).astype(jnp.int32)
    flat_tok = (jnp.arange(n_assign, dtype=jnp.int32) // TOP_K)
    flat_g = gates.reshape(-1)
    order = jnp.argsort(flat_e, stable=True)
    se, stok, sg = flat_e[order], flat_tok[order], flat_g[order]
    counts = jnp.bincount(flat_e, length=N_EXPERTS)
    starts = jnp.cumsum(counts) - counts
    padded = ((counts + MOE_BLOCK - 1) // MOE_BLOCK) * MOE_BLOCK
    pad_ends = jnp.cumsum(padded)
    pad_starts = pad_ends - padded
    dest = (pad_starts[se] + jnp.arange(n_assign) - starts[se]).astype(jnp.int32)
    n_blocks = -(-(n_assign + N_EXPERTS * (MOE_BLOCK - 1)) // MOE_BLOCK)
    P = n_blocks * MOE_BLOCK
    slot_tok = jnp.full((P,), T, jnp.int32).at[dest].set(stok)
    xs_pad = jnp.concatenate([xs, jnp.zeros((1, D), xs.dtype)], axis=0)
    xb = xs_pad[slot_tok].reshape(n_blocks, MOE_BLOCK, D)
    block_e = jnp.minimum(
        jnp.searchsorted(pad_ends, jnp.arange(n_blocks) * MOE_BLOCK, side='right'), N_EXPERTS - 1)

    def expert_block(args):
        xblk, e = args
        gu = xblk @ w_gate_up[e] + b_gate_up[e]
        gate = jnp.minimum(gu[:, 0::2], SWIGLU_LIMIT)
        up = jnp.clip(gu[:, 1::2], -SWIGLU_LIMIT, SWIGLU_LIMIT)
        act = (up + 1.0) * (gate * jax.nn.sigmoid(SWIGLU_ALPHA * gate))
        return act @ w_down[e] + b_down[e]

    yb = lax.map(expert_block, (xb, block_e)).reshape(P, D)
    contrib = yb[dest].astype(jnp.float32) * sg[:, None]
    out = jax.ops.segment_sum(contrib, stok, num_segments=T)
    return out.astype(h.dtype).reshape(B, S, D)


def setup_inputs(seed: int = 0) -> dict:
    key = jax.random.key(seed)
    ks = jax.random.split(key, 32)
    f32 = jnp.float32
    L, D, E, F = DEPTH, D_MODEL, N_EXPERTS, D_FF

    def nrm(k, shape, scale):
        return jax.random.normal(k, shape, f32) * scale

    return {
        "x": nrm(ks[0], (BATCH, SEQ, D), 1.0),
        "c": nrm(ks[1], (BATCH, D), 1.0),
        "norm_mix_g": 1.0 + nrm(ks[2], (L, D), 0.01),
        "norm_ffn_g": 1.0 + nrm(ks[3], (L, D), 0.01),
        "w_mod": nrm(ks[4], (L, D, N_MOD * D), 0.5 * D ** -0.5),
        "b_mod": nrm(ks[5], (L, N_MOD * D), 0.01),
        "w_in": nrm(ks[6], (L, D, IN_COLS), D ** -0.5),
        "hy_conv_w": nrm(ks[7], (L, 3, 3 * HY_WIDTH), 3 ** -0.5),
        "hy_conv_b": nrm(ks[8], (L, 3 * HY_WIDTH), 0.01),
        "hy_w1": nrm(ks[9], (L, HY_EMB, HY_FILT_HIDDEN), HY_EMB ** -0.5),
        "hy_b1": nrm(ks[10], (L, HY_FILT_HIDDEN), 0.1),
        "hy_w2": nrm(ks[11], (L, HY_FILT_HIDDEN, HY_FILT_HIDDEN), HY_FILT_HIDDEN ** -0.5),
        "hy_b2": nrm(ks[12], (L, HY_FILT_HIDDEN), 0.1),
        "hy_w3": nrm(ks[13], (L, HY_FILT_HIDDEN, HY_FILT_HIDDEN), HY_FILT_HIDDEN ** -0.5),
        "hy_b3": nrm(ks[14], (L, HY_FILT_HIDDEN), 0.1),
        "hy_w4": nrm(ks[15], (L, HY_FILT_HIDDEN, 2 * HY_ORDER * HY_WIDTH), HY_FILT_HIDDEN ** -0.5),
        "hy_freq": 1.0 + nrm(ks[16], (L, HY_FILT_HIDDEN), 0.1),
        "hy_bias": nrm(ks[17], (L, HY_ORDER, HY_WIDTH), 1.0),
        "w_hy_br": nrm(ks[18], (L, HY_WIDTH, D), HY_WIDTH ** -0.5),
        "w_ret_br": nrm(ks[19], (L, RET_V, D), RET_V ** -0.5),
        "w_out": nrm(ks[20], (L, D, D), D ** -0.5),
        "w_router": nrm(ks[21], (L, D, E), D ** -0.5),
        "b_router": nrm(ks[22], (L, E), 0.01),
        "w_gate_up": nrm(ks[23], (L, E, D, 2 * F), D ** -0.5),
        "b_gate_up": nrm(ks[24], (L, E, 2 * F), 0.01),
        "w_down": nrm(ks[25], (L, E, F, D), F ** -0.5),
        "b_down": nrm(ks[26], (L, E, D), 0.01),
        "final_g": 1.0 + nrm(ks[27], (D,), 0.01),
    }


def reference(x, c, norm_mix_g, norm_ffn_g, w_mod, b_mod, w_in, hy_conv_w, hy_conv_b,
              hy_w1, hy_b1, hy_w2, hy_b2, hy_w3, hy_b3, hy_w4, hy_freq, hy_bias,
              w_hy_br, w_ret_br, w_out, w_router, b_router, w_gate_up, b_gate_up,
              w_down, b_down, final_g):
    S = x.shape[1]
    pos_feat, t_norm = hyena_pos_features(S)
    inv_freq = 1.0 / (ROPE_BASE ** (jnp.arange(0, RET_DK, 2, dtype=jnp.float32) / RET_DK))
    ang = jnp.arange(S, dtype=jnp.float32)[:, None] * inv_freq[None, :]
    cos, sin = jnp.cos(ang), jnp.sin(ang)
    c_act = jax.nn.silu(c)
    for l in range(DEPTH):
        mod = (c_act @ w_mod[l] + b_mod[l])[:, None, :]
        sh_a, sc_a, ga_a, sh_f, sc_f, ga_f = jnp.split(mod, N_MOD, axis=-1)
        h = rms_norm(x, norm_mix_g[l]) * (1.0 + sc_a) + sh_a
        x = x + ga_a * mixer_block(h, w_in[l], hy_conv_w[l], hy_conv_b[l], hy_w1[l], hy_b1[l],
                                   hy_w2[l], hy_b2[l], hy_w3[l], hy_b3[l], hy_w4[l], hy_freq[l],
                                   hy_bias[l], w_hy_br[l], w_ret_br[l], w_out[l],
                                   pos_feat, t_norm, cos, sin)
        h = rms_norm(x, norm_ffn_g[l]) * (1.0 + sc_f) + sh_f
        x = x + ga_f * moe_ffn(h, w_router[l], b_router[l], w_gate_up[l], b_gate_up[l],
                               w_down[l], b_down[l])
    return rms_norm(x, final_g)
```

```python
import functools
import math

import numpy as np
import jax
import jax.numpy as jnp
from jax import lax
from jax.experimental import pallas as pl
from jax.experimental.pallas import tpu as pltpu

F32 = jnp.float32
BF16 = jnp.bfloat16
HIGHEST = lax.Precision.HIGHEST

D_MODEL = 1024
BATCH = 4
SEQ = 8192
DEPTH = 4
HY_WIDTH = 512
HY_ORDER = 2
HY_EMB = 33
HY_EMB_PAD = 64
HY_FILT_HIDDEN = 64
HY_FAST_DECAY = 0.3
HY_SLOW_DECAY = 1.5
HY_DECAY_TARGET = 1e-2
RET_HEADS = 4
RET_DK = 64
RET_DV = 128
RET_QK = RET_HEADS * RET_DK
RET_V = RET_HEADS * RET_DV
RET_CHUNK = 128
RET_DECAY_FWD = 5.0
RET_DECAY_BWD = 5.5
ROPE_BASE = 10000.0
N_EXPERTS = 32
TOP_K = 4
D_FF = D_MODEL
SWIGLU_ALPHA = 1.702
SWIGLU_LIMIT = 7.0
N_MOD = 6
EPS = 1e-6
IN_COLS = 3 * HY_WIDTH + 2 * RET_QK + 2 * RET_V + 2 * D_MODEL

LANES = 128
VMEM_LIMIT = 56 * 1024 * 1024

FFT_N = 2 * SEQ
FFT_N1 = 128
FFT_N2 = FFT_N // FFT_N1
FFT_H1 = FFT_N1 // 2

TOK_TILE = 256
EXPERT_TILE = 512
RET_GROUP = 8
FFT_LANE_BLOCK = 4096
FFT_K1_GROUP = 4


def _cparams(sem):
    return pltpu.CompilerParams(dimension_semantics=sem, vmem_limit_bytes=VMEM_LIMIT)


def _const_spec(shape):
    return pl.BlockSpec(shape, lambda *_: (0,) * len(shape))


def _tables():
    n1 = np.arange(FFT_N1)
    n2 = np.arange(FFT_N2)
    ang1 = 2.0 * np.pi * np.outer(n1, n1) / FFT_N1
    f1 = np.concatenate([np.cos(ang1), -np.sin(ang1)], axis=0)
    ang2 = 2.0 * np.pi * np.outer(n2, n2) / FFT_N2
    angt = 2.0 * np.pi * np.outer(n1, n2) / FFT_N
    g = np.concatenate([np.cos(ang1), -np.sin(ang1)], axis=1)[:FFT_H1] / FFT_N

    L = SEQ
    t = np.linspace(0.0, 1.0, L)
    bands = (HY_EMB - 1) // 2
    w = 2.0 * np.pi * np.arange(L) / L
    f = np.linspace(1e-4, bands - 1, bands)
    feats = np.concatenate([t[:, None], np.cos(f[None] * w[:, None]), -np.sin(f[None] * w[:, None])], -1)
    idx = np.concatenate([np.arange(L), [0], L - np.arange(1, L)])
    feats2 = np.zeros((2 * L, HY_EMB_PAD))
    feats2[:, :HY_EMB] = feats[idx]
    max_decay = math.log(HY_DECAY_TARGET) / HY_FAST_DECAY
    min_decay = math.log(HY_DECAY_TARGET) / HY_SLOW_DECAY
    deltas = np.abs(np.linspace(min_decay, max_decay, HY_WIDTH))

    C = RET_CHUNK
    hh = np.arange(RET_HEADS)
    lgf = np.log(1.0 - np.exp2(-(RET_DECAY_FWD + hh)))
    lgb = np.log(1.0 - np.exp2(-(RET_DECAY_BWD + hh)))
    pos = np.arange(C)
    diff = pos[:, None] - pos[None, :]
    dec = np.where(diff[None] >= 0, np.exp(np.maximum(diff, 0)[None] * lgf[:, None, None]),
                   np.exp(np.maximum(-diff, 0)[None] * lgb[:, None, None]))
    lane_h = np.repeat(hh, RET_DK)
    wq = np.stack([np.exp((pos[:, None] + 1.0) * lgf[lane_h][None]),
                   np.exp((C - pos[:, None]) * lgb[lane_h][None])])
    wk = np.stack([np.exp((C - 1.0 - pos[:, None]) * lgf[lane_h][None]),
                   np.exp(pos[:, None] * lgb[lane_h][None])]) * (RET_DK ** -0.5)
    cd = np.stack([np.broadcast_to(np.exp(C * lgf[lane_h])[:, None], (RET_QK, RET_V)),
                   np.broadcast_to(np.exp(C * lgb[lane_h])[:, None], (RET_QK, RET_V))])
    col_h = np.repeat(hh, RET_DV)
    bd = (lane_h[:, None] == col_h[None, :]).astype(np.float64)
    inv_freq = 1.0 / (ROPE_BASE ** (np.arange(0, RET_DK, 2) / RET_DK))
    ang = np.arange(SEQ)[:, None] * inv_freq[None, :]
    cc = np.tile(np.concatenate([np.cos(ang), np.cos(ang)], -1), (1, RET_HEADS))
    ss = np.tile(np.concatenate([-np.sin(ang), np.sin(ang)], -1), (1, RET_HEADS))

    return dict(
        f1=jnp.asarray(f1, BF16), f1h=jnp.asarray(f1[:, :FFT_H1], BF16),
        f2r=jnp.asarray(np.cos(ang2), F32), f2i=jnp.asarray(-np.sin(ang2), F32),
        twr=jnp.asarray(np.cos(angt), F32), twi=jnp.asarray(-np.sin(angt), F32),
        g=jnp.asarray(g, BF16),
        feats2=jnp.asarray(feats2, F32), deltas=jnp.asarray(deltas[None], F32),
        dec=jnp.asarray(dec, F32), wq=jnp.asarray(wq, F32), wk=jnp.asarray(wk, F32),
        cd=jnp.asarray(cd, F32), bd=jnp.asarray(bd, F32),
        cc=jnp.asarray(cc, F32), ss=jnp.asarray(ss, F32),
    )


def _mod_kernel(c_ref, w_ref, b_ref, o_ref):
    c = c_ref[...]
    ca = c * jax.nn.sigmoid(c)
    o_ref[...] = jnp.dot(ca, w_ref[...], preferred_element_type=F32, precision=HIGHEST) + b_ref[...]


def modulation(c_pad, w_mod, b_mod):
    L, D, N = w_mod.shape
    tn = 1536
    rows = c_pad.shape[0]
    return pl.pallas_call(
        _mod_kernel,
        grid=(L, N // tn),
        in_specs=[
            _const_spec((rows, D)),
            pl.BlockSpec((None, D, tn), lambda l, j: (l, 0, j)),
            pl.BlockSpec((None, 1, tn), lambda l, j: (l, 0, j)),
        ],
        out_specs=pl.BlockSpec((None, rows, tn), lambda l, j: (l, 0, j)),
        out_shape=jax.ShapeDtypeStruct((L, rows, N), F32),
        compiler_params=_cparams(("arbitrary", "arbitrary")),
        name="modulation",
    )(c_pad, w_mod, b_mod.reshape(L, 1, N))


def _norm_mod(x, g, sc, sh):
    y = x * lax.rsqrt(jnp.mean(x * x, axis=-1, keepdims=True) + EPS)
    return (y * g) * (1.0 + sc) + sh


def _inproj_kernel(x_ref, g_ref, sh_ref, sc_ref, w_ref, u_ref, qk_ref, v_ref, gr_ref, gm_ref):
    h = _norm_mod(x_ref[...], g_ref[...], sc_ref[...], sh_ref[...]).astype(BF16)
    c0 = 0
    for o_ref in (u_ref, qk_ref, v_ref, gr_ref, gm_ref):
        c1 = c0 + o_ref.shape[-1]
        o_ref[...] = jnp.dot(h, w_ref[:, c0:c1], preferred_element_type=F32).astype(o_ref.dtype)
        c0 = c1


def in_projection(x2, g, mod, w_in_bf, layer):
    T, D = x2.shape
    tm = TOK_TILE
    per_b = SEQ // tm
    mspec = lambda piece: pl.BlockSpec((None, None, None, 1, D),
                                       lambda i: (layer, piece, i // per_b, 0, 0))
    widths = (3 * HY_WIDTH, 2 * RET_QK, RET_V, RET_V, 2 * D_MODEL)
    dtypes = (F32, F32, BF16, F32, F32)
    return pl.pallas_call(
        _inproj_kernel,
        grid=(T // tm,),
        in_specs=[
            pl.BlockSpec((tm, D), lambda i: (i, 0)),
            pl.BlockSpec((None, 1, D), lambda i: (layer, 0, 0)),
            mspec(0), mspec(1),
            pl.BlockSpec((None, D, IN_COLS), lambda i: (layer, 0, 0)),
        ],
        out_specs=[pl.BlockSpec((tm, w), lambda i: (i, 0)) for w in widths],
        out_shape=[jax.ShapeDtypeStruct((T, w), dt) for w, dt in zip(widths, dtypes)],
        compiler_params=_cparams(("arbitrary",)),
        name="in_projection",
    )(x2, g, mod, mod, w_in_bf)


def _shortconv_kernel(u_ref, w_ref, b_ref, o_ref):
    u = u_ref[...]
    s = u.shape[0]
    row = lax.broadcasted_iota(jnp.int32, u.shape, 0)
    prev = jnp.where(row == 0, 0.0, pltpu.roll(u, 1, axis=0))
    nxt = jnp.where(row == s - 1, 0.0, pltpu.roll(u, s - 1, axis=0))
    w = w_ref[...]
    o_ref[...] = prev * w[0:1] + u * w[1:2] + nxt * w[2:3] + b_ref[...]


def short_conv(u, conv_w, conv_b, layer):
    B, S, C3 = u.shape
    cb = LANES
    per = HY_WIDTH // cb
    return pl.pallas_call(
        _shortconv_kernel,
        grid=(B, C3 // cb),
        in_specs=[
            pl.BlockSpec((None, S, cb), lambda b, j: (b, 0, j)),
            pl.BlockSpec((None, 3, cb), lambda b, j: (layer, 0, j)),
            pl.BlockSpec((None, 1, cb), lambda b, j: (layer, 0, j)),
        ],
        out_specs=pl.BlockSpec((None, None, S, cb), lambda b, j: (j // per, b, 0, j % per)),
        out_shape=jax.ShapeDtypeStruct((3, B, S, HY_WIDTH), F32),
        compiler_params=_cparams(("arbitrary", "arbitrary")),
        name="short_conv",
    )(u, conv_w, conv_b)


def _filter_kernel(p_ref, w1_ref, b1_ref, w2_ref, b2_ref, w3_ref, b3_ref, w4_ref, fr_ref, dl_ref,
                   k_ref, l1_ref, *, rows):
    i = pl.program_id(0)
    p = p_ref[...]
    fr = fr_ref[...]
    dot = functools.partial(jnp.dot, preferred_element_type=F32, precision=HIGHEST)
    h = jnp.sin(fr * (dot(p, w1_ref[...]) + b1_ref[...]))
    h = jnp.sin(fr * (dot(h, w2_ref[...]) + b2_ref[...]))
    h = jnp.sin(fr * (dot(h, w3_ref[...]) + b3_ref[...]))
    h = dot(h, w4_ref[...])
    t = p[:, 0:1]
    win = jnp.exp(-t * dl_ref[...])
    win = jnp.concatenate([win] * HY_ORDER, axis=1)
    grow = i * rows + lax.broadcasted_iota(jnp.int32, (rows, 1), 0)
    k = jnp.where(grow == SEQ, 0.0, h * win)
    k_ref[...] = k

    @pl.when(i == 0)
    def _():
        l1_ref[...] = jnp.zeros_like(l1_ref)
    l1_ref[...] += jnp.sum(jnp.abs(k), axis=0, keepdims=True)


def hyena_filter(tabs, w1p, b1, w2, b2, w3, b3, w4, freq, layer):
    rows = 1024
    n = 2 * SEQ
    half = SEQ // rows
    H = HY_FILT_HIDDEN
    OC = HY_ORDER * HY_WIDTH
    vec = lambda: pl.BlockSpec((None, 1, H), lambda i: (layer, 0, 0))
    return pl.pallas_call(
        functools.partial(_filter_kernel, rows=rows),
        grid=(n // rows,),
        in_specs=[
            pl.BlockSpec((rows, HY_EMB_PAD), lambda i: (i, 0)),
            pl.BlockSpec((None, HY_EMB_PAD, H), lambda i: (layer, 0, 0)), vec(),
            pl.BlockSpec((None, H, H), lambda i: (layer, 0, 0)), vec(),
            pl.BlockSpec((None, H, H), lambda i: (layer, 0, 0)), vec(),
            pl.BlockSpec((None, H, OC), lambda i: (layer, 0, i // half)),
            vec(),
            _const_spec((1, HY_WIDTH)),
        ],
        out_specs=[pl.BlockSpec((rows, OC), lambda i: (i, 0)), _const_spec((1, OC))],
        out_shape=[jax.ShapeDtypeStruct((n, OC), F32), jax.ShapeDtypeStruct((1, OC), F32)],
        compiler_params=_cparams(("arbitrary",)),
        name="hyena_filter",
    )(tabs["feats2"], w1p, b1, w2, b2, w3, b3, w4, freq, tabs["deltas"])


def _dft1_kernel(x_ref, f_ref, o_ref):
    a = jnp.dot(f_ref[...], x_ref[...].astype(BF16), preferred_element_type=F32)
    o_ref[...] = a.reshape(o_ref.shape).astype(o_ref.dtype)


def dft_first_axis(x3, f1):
    B, rows, LT = x3.shape
    lb = FFT_LANE_BLOCK
    return pl.pallas_call(
        _dft1_kernel,
        grid=(B, LT // lb),
        in_specs=[pl.BlockSpec((None, rows, lb), lambda b, j: (b, 0, j)),
                  _const_spec((2 * FFT_N1, rows))],
        out_specs=pl.BlockSpec((None, 2, FFT_N1, lb), lambda b, j: (b, 0, 0, j)),
        out_shape=jax.ShapeDtypeStruct((B, 2, FFT_N1, LT), BF16),
        compiler_params=_cparams(("arbitrary", "arbitrary")),
        name="dft_first_axis",
    )(x3, f1)


def _twiddled_stack(f2r_ref, f2i_ref, twr_ref, twi_ref, k1):
    tr = twr_ref[pl.ds(k1, 1), :]
    ti = twi_ref[pl.ds(k1, 1), :]
    fr = f2r_ref[...]
    fi = f2i_ref[...]
    p = fr * tr - fi * ti
    q = fr * ti + fi * tr
    top = jnp.concatenate([p, -q], axis=1)
    bot = jnp.concatenate([q, p], axis=1)
    return jnp.concatenate([top, bot], axis=0).astype(BF16)


def _spectrum_kernel(a_ref, l1_ref, f2r_ref, f2i_ref, twr_ref, twi_ref, o_ref):
    base = pl.program_id(0) * FFT_K1_GROUP
    inv = 1.0 / l1_ref[...]
    for j in range(FFT_K1_GROUP):
        r = _twiddled_stack(f2r_ref, f2i_ref, twr_ref, twi_ref, base + j)
        a = jnp.concatenate([a_ref[0, j], a_ref[1, j]], axis=0)
        x = jnp.dot(r, a, preferred_element_type=F32) * inv
        o_ref[j, 0] = x[:FFT_N2]
        o_ref[j, 1] = x[FFT_N2:]


def filter_spectrum(a5, l1, tabs):
    OC = a5.shape[-1]
    g = FFT_K1_GROUP
    sq = lambda: _const_spec((FFT_N2, FFT_N2))
    return pl.pallas_call(
        _spectrum_kernel,
        grid=(FFT_N1 // g,),
        in_specs=[pl.BlockSpec((None, 2, g, FFT_N2, OC), lambda i: (0, 0, i, 0, 0)),
                  _const_spec((1, OC)), sq(), sq(), sq(), sq()],
        out_specs=pl.BlockSpec((g, 2, FFT_N2, OC), lambda i: (i, 0, 0, 0)),
        out_shape=jax.ShapeDtypeStruct((FFT_N1, 2, FFT_N2, OC), F32),
        compiler_params=_cparams(("arbitrary",)),
        name="filter_spectrum",
    )(a5, l1, tabs["f2r"], tabs["f2i"], tabs["twr"], tabs["twi"])


def _convmid_kernel(a_ref, ks_ref, f2r_ref, f2i_ref, twr_ref, twi_ref, o_ref):
    base = pl.program_id(0) * FFT_K1_GROUP
    for j in range(FFT_K1_GROUP):
        r = _twiddled_stack(f2r_ref, f2i_ref, twr_ref, twi_ref, base + j)
        a = jnp.concatenate([a_ref[0, j], a_ref[1, j]], axis=0)
        x = jnp.dot(r, a, preferred_element_type=F32)
        xr, xi = x[:FFT_N2], x[FFT_N2:]
        kr, ki = ks_ref[j, 0], ks_ref[j, 1]
        y = jnp.concatenate([xr * kr - xi * ki, xr * ki + xi * kr], axis=0).astype(BF16)
        b = lax.dot_general(r, y, (((0,), (0,)), ((), ())), preferred_element_type=F32)
        o_ref[0, j] = b[:FFT_N2].astype(o_ref.dtype)
        o_ref[1, j] = b[FFT_N2:].astype(o_ref.dtype)


def conv_mid(a5, kspec, order, tabs):
    B = a5.shape[0]
    C = a5.shape[-1]
    g = FFT_K1_GROUP
    sq = lambda: _const_spec((FFT_N2, FFT_N2))
    return pl.pallas_call(
        _convmid_kernel,
        grid=(FFT_N1 // g, B),
        in_specs=[pl.BlockSpec((None, 2, g, FFT_N2, C), lambda i, b: (b, 0, i, 0, 0)),
                  pl.BlockSpec((g, 2, FFT_N2, C), lambda i, b: (i, 0, 0, order)),
                  sq(), sq(), sq(), sq()],
        out_specs=pl.BlockSpec((None, 2, g, FFT_N2, C), lambda i, b: (b, 0, i, 0, 0)),
        out_shape=jax.ShapeDtypeStruct(a5.shape, BF16),
        compiler_params=_cparams(("arbitrary", "arbitrary")),
        name="conv_mid",
    )(a5, kspec, tabs["f2r"], tabs["f2i"], tabs["twr"], tabs["twi"])


def _convout_kernel(b_ref, g_ref, z_ref, gate_ref, bias_ref, o_ref):
    bp = b_ref[...].reshape(2 * FFT_N1, b_ref.shape[-1])
    y = jnp.dot(g_ref[...], bp, preferred_element_type=F32)
    z = z_ref[...]
    o_ref[...] = gate_ref[...] * (y + z * bias_ref[...])


def conv_out(bp4, z3, gate3, bias_row, g):
    B, rows, LT = z3.shape
    lb = FFT_LANE_BLOCK
    blk = lambda: pl.BlockSpec((None, rows, lb), lambda b, j: (b, 0, j))
    return pl.pallas_call(
        _convout_kernel,
        grid=(B, LT // lb),
        in_specs=[pl.BlockSpec((None, 2, FFT_N1, lb), lambda b, j: (b, 0, 0, j)),
                  _const_spec((rows, 2 * FFT_N1)), blk(), blk(), _const_spec((1, lb))],
        out_specs=blk(),
        out_shape=jax.ShapeDtypeStruct(z3.shape, F32),
        compiler_params=_cparams(("arbitrary", "arbitrary")),
        name="conv_out",
    )(bp4, g, z3, gate3, bias_row)


def hyena_mixer(xs, kspec, bias, tabs):
    B = xs.shape[1]
    lt = FFT_N2 * HY_WIDTH
    z = xs[2]
    for o in range(HY_ORDER):
        z3 = z.reshape(B, FFT_H1, lt)
        a = dft_first_axis(z3, tabs["f1h"])
        bp = conv_mid(a.reshape(B, 2, FFT_N1, FFT_N2, HY_WIDTH), kspec, o, tabs)
        bias_row = jnp.tile(bias[o][None, :], (1, FFT_LANE_BLOCK // HY_WIDTH))
        z = conv_out(bp.reshape(B, 2, FFT_N1, lt), z3, xs[o].reshape(B, FFT_H1, lt), bias_row,
                     tabs["g"]).reshape(B, SEQ, HY_WIDTH)
    return z


def _rope(x, cc, ss):
    lane = lax.broadcasted_iota(jnp.int32, x.shape, 1)
    n = x.shape[1]
    half = RET_DK // 2
    swapped = jnp.where(lane % RET_DK < half, pltpu.roll(x, n - half, axis=1),
                        pltpu.roll(x, half, axis=1))
    return x * cc + swapped * ss


def _retstate_kernel(k_ref, v_ref, cc_ref, ss_ref, wk_ref, cd_ref, o_ref, s_ref):
    d = pl.program_id(1)
    g = pl.program_id(2)
    C = RET_CHUNK

    @pl.when(g == 0)
    def _():
        s_ref[...] = jnp.zeros_like(s_ref)

    for j in range(RET_GROUP):
        ci = jnp.where(d == 0, j, RET_GROUP - 1 - j)
        r0 = pl.multiple_of(ci * C, C)
        s = s_ref[...]
        o_ref[ci] = jnp.concatenate(
            [s[h * RET_DK:(h + 1) * RET_DK, h * RET_DV:(h + 1) * RET_DV] for h in range(RET_HEADS)],
            axis=1)
        k = _rope(k_ref[pl.ds(r0, C), :], cc_ref[pl.ds(r0, C), :], ss_ref[pl.ds(r0, C), :])
        kw = (k * wk_ref[...]).astype(BF16)
        ds = lax.dot_general(kw, v_ref[pl.ds(r0, C), :], (((0,), (0,)), ((), ())),
                             preferred_element_type=F32)
        s_ref[...] = s * cd_ref[...] + ds


def retention_states(qk, v, tabs):
    B, S, _ = qk.shape
    rows = RET_GROUP * RET_CHUNK
    G = S // rows
    grp = lambda d, g: jnp.where(d == 0, g, G - 1 - g)
    return pl.pallas_call(
        _retstate_kernel,
        grid=(B, 2, G),
        in_specs=[
            pl.BlockSpec((None, rows, RET_QK), lambda b, d, g: (b, grp(d, g), 1)),
            pl.BlockSpec((None, rows, RET_V), lambda b, d, g: (b, grp(d, g), 0)),
            pl.BlockSpec((rows, RET_QK), lambda b, d, g: (grp(d, g), 0)),
            pl.BlockSpec((rows, RET_QK), lambda b, d, g: (grp(d, g), 0)),
            pl.BlockSpec((None, RET_CHUNK, RET_QK), lambda b, d, g: (d, 0, 0)),
            pl.BlockSpec((None, RET_QK, RET_V), lambda b, d, g: (d, 0, 0)),
        ],
        out_specs=pl.BlockSpec((None, None, RET_GROUP, RET_DK, RET_V),
                               lambda b, d, g: (b, d, grp(d, g), 0, 0)),
        out_shape=jax.ShapeDtypeStruct((B, 2, S // RET_CHUNK, RET_DK, RET_V), F32),
        scratch_shapes=[pltpu.VMEM((RET_QK, RET_V), F32)],
        compiler_params=_cparams(("arbitrary", "arbitrary", "arbitrary")),
        name="retention_states",
    )(qk, v, tabs["cc"], tabs["ss"], tabs["wk"], tabs["cd"])


def _retout_kernel(qk_ref, v_ref, gr_ref, cc_ref, ss_ref, st_ref, dec_ref, wq_ref, wk_ref, bd_ref,
                   o_ref):
    C = RET_CHUNK
    bd = bd_ref[...]
    lane = lax.broadcasted_iota(jnp.int32, (C, RET_QK), 1)
    for j in range(RET_GROUP):
        r0 = j * C
        cc = cc_ref[r0:r0 + C, :]
        ss = ss_ref[r0:r0 + C, :]
        q = _rope(qk_ref[r0:r0 + C, :RET_QK], cc, ss)
        k = (_rope(qk_ref[r0:r0 + C, RET_QK:], cc, ss) * (RET_DK ** -0.5)).astype(BF16)
        v = v_ref[r0:r0 + C, :]
        inner = []
        for h in range(RET_HEADS):
            qh = jnp.where(lane // RET_DK == h, q, 0.0).astype(BF16)
            s = lax.dot_general(qh, k, (((1,), (1,)), ((), ())), preferred_element_type=F32)
            s = (s * dec_ref[h]).astype(BF16)
            inner.append(jnp.dot(s, v[:, h * RET_DV:(h + 1) * RET_DV], preferred_element_type=F32))
        qq = jnp.concatenate([q * wq_ref[0], q * wq_ref[1]], axis=1).astype(BF16)
        sf = jnp.concatenate([st_ref[0, j]] * RET_HEADS, axis=0) * bd
        sb = jnp.concatenate([st_ref[1, j]] * RET_HEADS, axis=0) * bd
        sbd = jnp.concatenate([sf, sb], axis=0).astype(BF16)
        o = jnp.concatenate(inner, axis=1) + jnp.dot(qq, sbd, preferred_element_type=F32)
        outs = []
        for h in range(RET_HEADS):
            oh = o[:, h * RET_DV:(h + 1) * RET_DV]
            outs.append(oh * lax.rsqrt(jnp.mean(oh * oh, axis=-1, keepdims=True) + EPS))
        gr = gr_ref[r0:r0 + C, :]
        o_ref[r0:r0 + C, :] = (gr * jax.nn.sigmoid(gr)) * jnp.concatenate(outs, axis=1)


def retention_out(qk, v, gr, states, tabs):
    B, S, _ = qk.shape
    rows = RET_GROUP * RET_CHUNK
    G = S // rows
    return pl.pallas_call(
        _retout_kernel,
        grid=(B, G),
        in_specs=[
            pl.BlockSpec((None, rows, 2 * RET_QK), lambda b, g: (b, g, 0)),
            pl.BlockSpec((None, rows, RET_V), lambda b, g: (b, g, 0)),
            pl.BlockSpec((None, rows, RET_V), lambda b, g: (b, g, 0)),
            pl.BlockSpec((rows, RET_QK), lambda b, g: (g, 0)),
            pl.BlockSpec((rows, RET_QK), lambda b, g: (g, 0)),
            pl.BlockSpec((None, 2, RET_GROUP, RET_DK, RET_V), lambda b, g: (b, 0, g, 0, 0)),
            _const_spec((RET_HEADS, RET_CHUNK, RET_CHUNK)),
            _const_spec((2, RET_CHUNK, RET_QK)),
            _const_spec((2, RET_CHUNK, RET_QK)),
            _const_spec((RET_QK, RET_V)),
        ],
        out_specs=pl.BlockSpec((None, rows, RET_V), lambda b, g: (b, g, 0)),
        out_shape=jax.ShapeDtypeStruct((B, S, RET_V), F32),
        compiler_params=_cparams(("arbitrary", "arbitrary")),
        name="retention_out",
    )(qk, v, gr, tabs["cc"], tabs["ss"], states, tabs["dec"], tabs["wq"], tabs["wk"], tabs["bd"])


def _merge_kernel(x_ref, yh_ref, yr_ref, gm_ref, ga_ref, whb_ref, wrb_ref, wo_ref,
                  g_ref, sh_ref, sc_ref, wr_ref, br_ref, xo_ref, h_ref, rt_ref):
    gm = gm_ref[...]
    ghy = jax.nn.sigmoid(gm[:, :D_MODEL])
    grt = jax.nn.sigmoid(gm[:, D_MODEL:])
    m = (ghy * jnp.dot(yh_ref[...].astype(BF16), whb_ref[...], preferred_element_type=F32)
         + grt * jnp.dot(yr_ref[...].astype(BF16), wrb_ref[...], preferred_element_type=F32))
    out = jnp.dot(m.astype(BF16), wo_ref[...], preferred_element_type=F32)
    x = x_ref[...] + ga_ref[...] * out
    xo_ref[...] = x
    h = _norm_mod(x, g_ref[...], sc_ref[...], sh_ref[...])
    h_ref[...] = h.astype(h_ref.dtype)
    logits = jnp.dot(h, wr_ref[...], preferred_element_type=F32, precision=HIGHEST) + br_ref[...]
    tm = logits.shape[0]
    lane_e = lax.broadcasted_iota(jnp.int32, logits.shape, 1).astype(F32)
    lane_o = lax.broadcasted_iota(jnp.int32, (tm, LANES), 1)
    route = jnp.zeros((tm, LANES), F32)
    vals = []
    work = logits
    for r in range(TOP_K):
        m_r = jnp.max(work, axis=-1, keepdims=True)
        i_r = jnp.min(jnp.where(work == m_r, lane_e, float(N_EXPERTS)), axis=-1, keepdims=True)
        work = jnp.where(lane_e == i_r, -jnp.inf, work)
        vals.append(m_r)
        route = jnp.where(lane_o == r, i_r, route)
    exps = [jnp.exp(v - vals[0]) for v in vals]
    den = exps[0] + exps[1] + exps[2] + exps[3]
    for r in range(TOP_K):
        route = jnp.where(lane_o == TOP_K + r, exps[r] / den, route)
    rt_ref[...] = route


def merge_and_route(x2, yh, yr, gm, mod, g_ffn, whb, wrb, wo, w_router, b_router, layer):
    T, D = x2.shape
    tm = TOK_TILE
    per_b = SEQ // tm
    mspec = lambda piece: pl.BlockSpec((None, None, None, 1, D),
                                       lambda i: (layer, piece, i // per_b, 0, 0))
    row = lambda w: pl.BlockSpec((tm, w), lambda i: (i, 0))
    wsp = lambda a, b: pl.BlockSpec((None, a, b), lambda i: (layer, 0, 0))
    return pl.pallas_call(
        _merge_kernel,
        grid=(T // tm,),
        in_specs=[row(D), row(HY_WIDTH), row(RET_V), row(2 * D), mspec(2),
                  wsp(HY_WIDTH, D), wsp(RET_V, D), wsp(D, D),
                  wsp(1, D), mspec(3), mspec(4), wsp(D, N_EXPERTS), wsp(1, N_EXPERTS)],
        out_specs=[row(D), row(D), row(LANES)],
        out_shape=[jax.ShapeDtypeStruct((T, D), F32), jax.ShapeDtypeStruct((T, D), BF16),
                   jax.ShapeDtypeStruct((T, LANES), F32)],
        compiler_params=_cparams(("arbitrary",)),
        name="merge_and_route",
    )(x2, yh, yr, gm, mod, whb, wrb, wo, g_ffn, mod, mod, w_router, b_router)


def _expert_kernel(be_ref, x_ref, wgu_ref, bgu_ref, wd_ref, bd_ref, o_ref):
    gu = jnp.dot(x_ref[...], wgu_ref[...], preferred_element_type=F32) + bgu_ref[...]
    gate = jnp.minimum(gu[:, :D_FF], SWIGLU_LIMIT)
    up = jnp.clip(gu[:, D_FF:], -SWIGLU_LIMIT, SWIGLU_LIMIT)
    act = (up + 1.0) * (gate * jax.nn.sigmoid(SWIGLU_ALPHA * gate))
    o_ref[...] = jnp.dot(act.astype(BF16), wd_ref[...], preferred_element_type=F32) + bd_ref[...]


def expert_ffn(block_e, xb, wgu, bgu, wd, bd, layer):
    P, D = xb.shape
    tm = EXPERT_TILE
    grid_spec = pltpu.PrefetchScalarGridSpec(
        num_scalar_prefetch=1,
        grid=(P // tm,),
        in_specs=[
            pl.BlockSpec((tm, D), lambda i, be: (i, 0)),
            pl.BlockSpec((None, None, D, 2 * D_FF), lambda i, be: (layer, be[i], 0, 0)),
            pl.BlockSpec((None, None, 1, 2 * D_FF), lambda i, be: (layer, be[i], 0, 0)),
            pl.BlockSpec((None, None, D_FF, D), lambda i, be: (layer, be[i], 0, 0)),
            pl.BlockSpec((None, None, 1, D), lambda i, be: (layer, be[i], 0, 0)),
        ],
        out_specs=pl.BlockSpec((tm, D), lambda i, be: (i, 0)),
    )
    return pl.pallas_call(
        _expert_kernel,
        grid_spec=grid_spec,
        out_shape=jax.ShapeDtypeStruct((P, D), F32),
        compiler_params=_cparams(("arbitrary",)),
        name="expert_ffn",
    )(block_e, xb, wgu, bgu, wd, bd)


def _residual_kernel(x_ref, y_ref, ga_ref, g_ref, o_ref, *, final):
    x = x_ref[...] + ga_ref[...] * y_ref[...]
    if final:
        x = (x * lax.rsqrt(jnp.mean(x * x, axis=-1, keepdims=True) + EPS)) * g_ref[...]
    o_ref[...] = x


def residual(x2, y, mod, g_final, layer, final):
    T, D = x2.shape
    tm = TOK_TILE
    per_b = SEQ // tm
    row = lambda: pl.BlockSpec((tm, D), lambda i: (i, 0))
    return pl.pallas_call(
        functools.partial(_residual_kernel, final=final),
        grid=(T // tm,),
        in_specs=[row(), row(),
                  pl.BlockSpec((None, None, None, 1, D), lambda i: (layer, 5, i // per_b, 0, 0)),
                  _const_spec((1, D))],
        out_specs=row(),
        out_shape=jax.ShapeDtypeStruct((T, D), F32),
        compiler_params=_cparams(("arbitrary",)),
        name="residual",
    )(x2, y, mod, g_final)


def moe_dispatch_combine(h_bf, route, wgu, bgu, wd, bd, layer):
    T, D = h_bf.shape
    top_idx = route[:, :TOP_K].astype(jnp.int32)
    gates = route[:, TOP_K:2 * TOP_K]
    n_assign = T * TOP_K
    tm = EXPERT_TILE
    flat_e = top_idx.reshape(-1)
    order = jnp.argsort(flat_e, stable=True)
    se = flat_e[order]
    stok = (order // TOP_K).astype(jnp.int32)
    counts = jnp.bincount(flat_e, length=N_EXPERTS)
    starts = jnp.cumsum(counts) - counts
    padded = ((counts + tm - 1) // tm) * tm
    pad_ends = jnp.cumsum(padded)
    pad_starts = pad_ends - padded
    dest = (pad_starts[se] + jnp.arange(n_assign) - starts[se]).astype(jnp.int32)
    n_blocks = -(-(n_assign + N_EXPERTS * (tm - 1)) // tm)
    P = n_blocks * tm
    slot_tok = jnp.full((P,), T, jnp.int32).at[dest].set(stok)
    xs_pad = jnp.concatenate([h_bf, jnp.zeros((1, D), h_bf.dtype)], axis=0)
    xb = xs_pad[slot_tok]
    block_e = jnp.minimum(
        jnp.searchsorted(pad_ends, jnp.arange(n_blocks) * tm, side='right'), N_EXPERTS - 1
    ).astype(jnp.int32)
    yb = expert_ffn(block_e, xb, wgu, bgu, wd, bd, layer)
    dest_orig = jnp.zeros((n_assign,), jnp.int32).at[order].set(dest).reshape(T, TOP_K)
    out = jnp.zeros((T, D), F32)
    for r in range(TOP_K):
        out = out + yb[dest_orig[:, r]] * gates[:, r:r + 1]
    return out


def kernel(x, c, norm_mix_g, norm_ffn_g, w_mod, b_mod, w_in, hy_conv_w, hy_conv_b, hy_w1, hy_b1, hy_w2, hy_b2, hy_w3, hy_b3, hy_w4, hy_freq, hy_bias, w_hy_br, w_ret_br, w_out, w_router, b_router, w_gate_up, b_gate_up, w_down, b_down, final_g):
    B, S, D = x.shape
    L = w_mod.shape[0]
    T = B * S
    tabs = _tables()

    w_in_bf = w_in.astype(BF16)
    whb = w_hy_br.astype(BF16)
    wrb = w_ret_br.astype(BF16)
    wo = w_out.astype(BF16)
    E, F = N_EXPERTS, D_FF
    wgu = w_gate_up.reshape(L, E, D, F, 2).transpose(0, 1, 2, 4, 3).reshape(L, E, D, 2 * F).astype(BF16)
    bgu = b_gate_up.reshape(L, E, F, 2).transpose(0, 1, 3, 2).reshape(L, E, 1, 2 * F)
    wd = w_down.astype(BF16)
    bdn = b_down.reshape(L, E, 1, D)
    w1p = jnp.pad(hy_w1, ((0, 0), (0, HY_EMB_PAD - HY_EMB), (0, 0)))
    vec = lambda a: a.reshape(L, 1, -1)

    c_pad = jnp.pad(c, ((0, 8 - B), (0, 0)))
    mod = modulation(c_pad, w_mod, b_mod)[:, :B]
    mod = mod.reshape(L, B, N_MOD, 1, D).transpose(0, 2, 1, 3, 4)

    x2 = x.reshape(T, D)
    for l in range(L):
        u, qk, v, gr, gm = in_projection(x2, vec(norm_mix_g), mod, w_in_bf, l)
        xs = short_conv(u.reshape(B, S, -1), hy_conv_w, vec(hy_conv_b), l)
        kraw, l1 = hyena_filter(tabs, w1p, vec(hy_b1), hy_w2, vec(hy_b2), hy_w3, vec(hy_b3),
                                hy_w4, vec(hy_freq), l)
        ka = dft_first_axis(kraw.reshape(1, FFT_N1, -1), tabs["f1"])
        kspec = filter_spectrum(ka.reshape(1, 2, FFT_N1, FFT_N2, -1), l1, tabs)
        y_hy = hyena_mixer(xs, kspec, hy_bias[l], tabs)
        qk3, v3, gr3 = qk.reshape(B, S, -1), v.reshape(B, S, -1), gr.reshape(B, S, -1)
        states = retention_states(qk3, v3, tabs)
        y_ret = retention_out(qk3, v3, gr3, states, tabs)
        x2, h_bf, route = merge_and_route(
            x2, y_hy.reshape(T, -1), y_ret.reshape(T, -1), gm, mod, vec(norm_ffn_g),
            whb, wrb, wo, w_router, vec(b_router), l)
        y_moe = moe_dispatch_combine(h_bf, route, wgu, bgu, wd, bdn, l)
        x2 = residual(x2, y_moe, mod, final_g.reshape(1, D), l, final=(l == L - 1))
    return x2.reshape(B, S, D)
```

```python
import functools
import math

import numpy as np
import jax
import jax.numpy as jnp
from jax import lax
from jax.experimental import pallas as pl
from jax.experimental.pallas import tpu as pltpu

F32 = jnp.float32
BF16 = jnp.bfloat16
HIGHEST = lax.Precision.HIGHEST

D_MODEL = 1024
BATCH = 4
SEQ = 8192
DEPTH = 4
HY_WIDTH = 512
HY_ORDER = 2
HY_EMB = 33
HY_EMB_PAD = 64
HY_FILT_HIDDEN = 64
HY_FAST_DECAY = 0.3
HY_SLOW_DECAY = 1.5
HY_DECAY_TARGET = 1e-2
RET_HEADS = 4
RET_DK = 64
RET_DV = 128
RET_QK = RET_HEADS * RET_DK
RET_V = RET_HEADS * RET_DV
RET_CHUNK = 128
RET_DECAY_FWD = 5.0
RET_DECAY_BWD = 5.5
ROPE_BASE = 10000.0
N_EXPERTS = 32
TOP_K = 4
D_FF = D_MODEL
SWIGLU_ALPHA = 1.702
SWIGLU_LIMIT = 7.0
N_MOD = 6
EPS = 1e-6
IN_COLS = 3 * HY_WIDTH + 2 * RET_QK + 2 * RET_V + 2 * D_MODEL

LANES = 128
VMEM_LIMIT = 56 * 1024 * 1024

FFT_N = 2 * SEQ
FFT_N1 = 128
FFT_N2 = FFT_N // FFT_N1
FFT_H1 = FFT_N1 // 2

TOK_TILE = 256
EXPERT_TILE = 512
RANK_TILE = 512
DISPATCH_TILE = 512
COMBINE_TILE = 256
RET_GROUP = 8
FFT_LANE_BLOCK = 4096
FFT_K1_GROUP = 4


def _cparams(sem):
    return pltpu.CompilerParams(dimension_semantics=sem, vmem_limit_bytes=VMEM_LIMIT)


def _const_spec(shape):
    return pl.BlockSpec(shape, lambda *_: (0,) * len(shape))


def _tables():
    n1 = np.arange(FFT_N1)
    n2 = np.arange(FFT_N2)
    ang1 = 2.0 * np.pi * np.outer(n1, n1) / FFT_N1
    f1 = np.concatenate([np.cos(ang1), -np.sin(ang1)], axis=0)
    ang2 = 2.0 * np.pi * np.outer(n2, n2) / FFT_N2
    angt = 2.0 * np.pi * np.outer(n1, n2) / FFT_N
    g = np.concatenate([np.cos(ang1), -np.sin(ang1)], axis=1)[:FFT_H1] / FFT_N

    L = SEQ
    t = np.linspace(0.0, 1.0, L)
    bands = (HY_EMB - 1) // 2
    w = 2.0 * np.pi * np.arange(L) / L
    f = np.linspace(1e-4, bands - 1, bands)
    feats = np.concatenate([t[:, None], np.cos(f[None] * w[:, None]), -np.sin(f[None] * w[:, None])], -1)
    idx = np.concatenate([np.arange(L), [0], L - np.arange(1, L)])
    feats2 = np.zeros((2 * L, HY_EMB_PAD))
    feats2[:, :HY_EMB] = feats[idx]
    max_decay = math.log(HY_DECAY_TARGET) / HY_FAST_DECAY
    min_decay = math.log(HY_DECAY_TARGET) / HY_SLOW_DECAY
    deltas = np.abs(np.linspace(min_decay, max_decay, HY_WIDTH))

    C = RET_CHUNK
    hh = np.arange(RET_HEADS)
    lgf = np.log(1.0 - np.exp2(-(RET_DECAY_FWD + hh)))
    lgb = np.log(1.0 - np.exp2(-(RET_DECAY_BWD + hh)))
    pos = np.arange(C)
    diff = pos[:, None] - pos[None, :]
    dec = np.where(diff[None] >= 0, np.exp(np.maximum(diff, 0)[None] * lgf[:, None, None]),
                   np.exp(np.maximum(-diff, 0)[None] * lgb[:, None, None]))
    lane_h = np.repeat(hh, RET_DK)
    wq = np.stack([np.exp((pos[:, None] + 1.0) * lgf[lane_h][None]),
                   np.exp((C - pos[:, None]) * lgb[lane_h][None])])
    wk = np.stack([np.exp((C - 1.0 - pos[:, None]) * lgf[lane_h][None]),
                   np.exp(pos[:, None] * lgb[lane_h][None])]) * (RET_DK ** -0.5)
    cd = np.stack([np.broadcast_to(np.exp(C * lgf[lane_h])[:, None], (RET_QK, RET_V)),
                   np.broadcast_to(np.exp(C * lgb[lane_h])[:, None], (RET_QK, RET_V))])
    col_h = np.repeat(hh, RET_DV)
    bd = (lane_h[:, None] == col_h[None, :]).astype(np.float64)
    inv_freq = 1.0 / (ROPE_BASE ** (np.arange(0, RET_DK, 2) / RET_DK))
    ang = np.arange(SEQ)[:, None] * inv_freq[None, :]
    cc = np.tile(np.concatenate([np.cos(ang), np.cos(ang)], -1), (1, RET_HEADS))
    ss = np.tile(np.concatenate([-np.sin(ang), np.sin(ang)], -1), (1, RET_HEADS))

    return dict(
        f1=jnp.asarray(f1, BF16), f1h=jnp.asarray(f1[:, :FFT_H1], BF16),
        f2r=jnp.asarray(np.cos(ang2), F32), f2i=jnp.asarray(-np.sin(ang2), F32),
        twr=jnp.asarray(np.cos(angt), F32), twi=jnp.asarray(-np.sin(angt), F32),
        g=jnp.asarray(g, BF16),
        feats2=jnp.asarray(feats2, F32), deltas=jnp.asarray(deltas[None], F32),
        dec=jnp.asarray(dec, F32), wq=jnp.asarray(wq, F32), wk=jnp.asarray(wk, F32),
        cd=jnp.asarray(cd, F32), bd=jnp.asarray(bd, F32),
        cc=jnp.asarray(cc, F32), ss=jnp.asarray(ss, F32),
    )


def _mod_kernel(c_ref, w_ref, b_ref, o_ref):
    c = c_ref[...]
    ca = c * jax.nn.sigmoid(c)
    o_ref[...] = jnp.dot(ca, w_ref[...], preferred_element_type=F32, precision=HIGHEST) + b_ref[...]


def modulation(c_pad, w_mod, b_mod):
    L, D, N = w_mod.shape
    tn = 1536
    rows = c_pad.shape[0]
    return pl.pallas_call(
        _mod_kernel,
        grid=(L, N // tn),
        in_specs=[
            _const_spec((rows, D)),
            pl.BlockSpec((None, D, tn), lambda l, j: (l, 0, j)),
            pl.BlockSpec((None, 1, tn), lambda l, j: (l, 0, j)),
        ],
        out_specs=pl.BlockSpec((None, rows, tn), lambda l, j: (l, 0, j)),
        out_shape=jax.ShapeDtypeStruct((L, rows, N), F32),
        compiler_params=_cparams(("arbitrary", "arbitrary")),
        name="modulation",
    )(c_pad, w_mod, b_mod.reshape(L, 1, N))


def _norm_mod(x, g, sc, sh):
    y = x * lax.rsqrt(jnp.mean(x * x, axis=-1, keepdims=True) + EPS)
    return (y * g) * (1.0 + sc) + sh


def _inproj_kernel(x_ref, g_ref, sh_ref, sc_ref, w_ref, u_ref, qk_ref, v_ref, gr_ref, gm_ref):
    h = _norm_mod(x_ref[...], g_ref[...], sc_ref[...], sh_ref[...]).astype(BF16)
    c0 = 0
    for o_ref in (u_ref, qk_ref, v_ref, gr_ref, gm_ref):
        c1 = c0 + o_ref.shape[-1]
        o_ref[...] = jnp.dot(h, w_ref[:, c0:c1], preferred_element_type=F32).astype(o_ref.dtype)
        c0 = c1


def in_projection(x2, g, mod, w_in_bf, layer):
    T, D = x2.shape
    tm = TOK_TILE
    per_b = SEQ // tm
    mspec = lambda piece: pl.BlockSpec((None, None, None, 1, D),
                                       lambda i: (layer, piece, i // per_b, 0, 0))
    widths = (3 * HY_WIDTH, 2 * RET_QK, RET_V, RET_V, 2 * D_MODEL)
    dtypes = (F32, F32, BF16, F32, F32)
    return pl.pallas_call(
        _inproj_kernel,
        grid=(T // tm,),
        in_specs=[
            pl.BlockSpec((tm, D), lambda i: (i, 0)),
            pl.BlockSpec((None, 1, D), lambda i: (layer, 0, 0)),
            mspec(0), mspec(1),
            pl.BlockSpec((None, D, IN_COLS), lambda i: (layer, 0, 0)),
        ],
        out_specs=[pl.BlockSpec((tm, w), lambda i: (i, 0)) for w in widths],
        out_shape=[jax.ShapeDtypeStruct((T, w), dt) for w, dt in zip(widths, dtypes)],
        compiler_params=_cparams(("arbitrary",)),
        name="in_projection",
    )(x2, g, mod, mod, w_in_bf)


def _shortconv_kernel(u_ref, w_ref, b_ref, o_ref):
    u = u_ref[...]
    s = u.shape[0]
    row = lax.broadcasted_iota(jnp.int32, u.shape, 0)
    prev = jnp.where(row == 0, 0.0, pltpu.roll(u, 1, axis=0))
    nxt = jnp.where(row == s - 1, 0.0, pltpu.roll(u, s - 1, axis=0))
    w = w_ref[...]
    o_ref[...] = prev * w[0:1] + u * w[1:2] + nxt * w[2:3] + b_ref[...]


def short_conv(u, conv_w, conv_b, layer):
    B, S, C3 = u.shape
    cb = LANES
    per = HY_WIDTH // cb
    return pl.pallas_call(
        _shortconv_kernel,
        grid=(B, C3 // cb),
        in_specs=[
            pl.BlockSpec((None, S, cb), lambda b, j: (b, 0, j)),
            pl.BlockSpec((None, 3, cb), lambda b, j: (layer, 0, j)),
            pl.BlockSpec((None, 1, cb), lambda b, j: (layer, 0, j)),
        ],
        out_specs=pl.BlockSpec((None, None, S, cb), lambda b, j: (j // per, b, 0, j % per)),
        out_shape=jax.ShapeDtypeStruct((3, B, S, HY_WIDTH), F32),
        compiler_params=_cparams(("arbitrary", "arbitrary")),
        name="short_conv",
    )(u, conv_w, conv_b)


def _filter_kernel(p_ref, w1_ref, b1_ref, w2_ref, b2_ref, w3_ref, b3_ref, w4_ref, fr_ref, dl_ref,
                   k_ref, l1_ref, *, rows):
    i = pl.program_id(0)
    p = p_ref[...]
    fr = fr_ref[...]
    dot = functools.partial(jnp.dot, preferred_element_type=F32, precision=HIGHEST)
    h = jnp.sin(fr * (dot(p, w1_ref[...]) + b1_ref[...]))
    h = jnp.sin(fr * (dot(h, w2_ref[...]) + b2_ref[...]))
    h = jnp.sin(fr * (dot(h, w3_ref[...]) + b3_ref[...]))
    h = dot(h, w4_ref[...])
    t = p[:, 0:1]
    win = jnp.exp(-t * dl_ref[...])
    win = jnp.concatenate([win] * HY_ORDER, axis=1)
    grow = i * rows + lax.broadcasted_iota(jnp.int32, (rows, 1), 0)
    k = jnp.where(grow == SEQ, 0.0, h * win)
    k_ref[...] = k

    @pl.when(i == 0)
    def _():
        l1_ref[...] = jnp.zeros_like(l1_ref)
    l1_ref[...] += jnp.sum(jnp.abs(k), axis=0, keepdims=True)


def hyena_filter(tabs, w1p, b1, w2, b2, w3, b3, w4, freq, layer):
    rows = 1024
    n = 2 * SEQ
    half = SEQ // rows
    H = HY_FILT_HIDDEN
    OC = HY_ORDER * HY_WIDTH
    vec = lambda: pl.BlockSpec((None, 1, H), lambda i: (layer, 0, 0))
    return pl.pallas_call(
        functools.partial(_filter_kernel, rows=rows),
        grid=(n // rows,),
        in_specs=[
            pl.BlockSpec((rows, HY_EMB_PAD), lambda i: (i, 0)),
            pl.BlockSpec((None, HY_EMB_PAD, H), lambda i: (layer, 0, 0)), vec(),
            pl.BlockSpec((None, H, H), lambda i: (layer, 0, 0)), vec(),
            pl.BlockSpec((None, H, H), lambda i: (layer, 0, 0)), vec(),
            pl.BlockSpec((None, H, OC), lambda i: (layer, 0, i // half)),
            vec(),
            _const_spec((1, HY_WIDTH)),
        ],
        out_specs=[pl.BlockSpec((rows, OC), lambda i: (i, 0)), _const_spec((1, OC))],
        out_shape=[jax.ShapeDtypeStruct((n, OC), F32), jax.ShapeDtypeStruct((1, OC), F32)],
        compiler_params=_cparams(("arbitrary",)),
        name="hyena_filter",
    )(tabs["feats2"], w1p, b1, w2, b2, w3, b3, w4, freq, tabs["deltas"])


def _dft1_kernel(x_ref, f_ref, o_ref):
    a = jnp.dot(f_ref[...], x_ref[...].astype(BF16), preferred_element_type=F32)
    o_ref[...] = a.reshape(o_ref.shape).astype(o_ref.dtype)


def dft_first_axis(x3, f1):
    B, rows, LT = x3.shape
    lb = FFT_LANE_BLOCK
    return pl.pallas_call(
        _dft1_kernel,
        grid=(B, LT // lb),
        in_specs=[pl.BlockSpec((None, rows, lb), lambda b, j: (b, 0, j)),
                  _const_spec((2 * FFT_N1, rows))],
        out_specs=pl.BlockSpec((None, 2, FFT_N1, lb), lambda b, j: (b, 0, 0, j)),
        out_shape=jax.ShapeDtypeStruct((B, 2, FFT_N1, LT), BF16),
        compiler_params=_cparams(("arbitrary", "arbitrary")),
        name="dft_first_axis",
    )(x3, f1)


def _twiddled_stack(f2r_ref, f2i_ref, twr_ref, twi_ref, k1):
    tr = twr_ref[pl.ds(k1, 1), :]
    ti = twi_ref[pl.ds(k1, 1), :]
    fr = f2r_ref[...]
    fi = f2i_ref[...]
    p = fr * tr - fi * ti
    q = fr * ti + fi * tr
    top = jnp.concatenate([p, -q], axis=1)
    bot = jnp.concatenate([q, p], axis=1)
    return jnp.concatenate([top, bot], axis=0).astype(BF16)


def _spectrum_kernel(a_ref, l1_ref, f2r_ref, f2i_ref, twr_ref, twi_ref, o_ref):
    base = pl.program_id(0) * FFT_K1_GROUP
    inv = 1.0 / l1_ref[...]
    for j in range(FFT_K1_GROUP):
        r = _twiddled_stack(f2r_ref, f2i_ref, twr_ref, twi_ref, base + j)
        a = jnp.concatenate([a_ref[0, j], a_ref[1, j]], axis=0)
        x = jnp.dot(r, a, preferred_element_type=F32) * inv
        o_ref[j, 0] = x[:FFT_N2]
        o_ref[j, 1] = x[FFT_N2:]


def filter_spectrum(a5, l1, tabs):
    OC = a5.shape[-1]
    g = FFT_K1_GROUP
    sq = lambda: _const_spec((FFT_N2, FFT_N2))
    return pl.pallas_call(
        _spectrum_kernel,
        grid=(FFT_N1 // g,),
        in_specs=[pl.BlockSpec((None, 2, g, FFT_N2, OC), lambda i: (0, 0, i, 0, 0)),
                  _const_spec((1, OC)), sq(), sq(), sq(), sq()],
        out_specs=pl.BlockSpec((g, 2, FFT_N2, OC), lambda i: (i, 0, 0, 0)),
        out_shape=jax.ShapeDtypeStruct((FFT_N1, 2, FFT_N2, OC), F32),
        compiler_params=_cparams(("arbitrary",)),
        name="filter_spectrum",
    )(a5, l1, tabs["f2r"], tabs["f2i"], tabs["twr"], tabs["twi"])


def _convmid_kernel(a_ref, ks_ref, f2r_ref, f2i_ref, twr_ref, twi_ref, o_ref):
    base = pl.program_id(0) * FFT_K1_GROUP
    for j in range(FFT_K1_GROUP):
        r = _twiddled_stack(f2r_ref, f2i_ref, twr_ref, twi_ref, base + j)
        a = jnp.concatenate([a_ref[0, j], a_ref[1, j]], axis=0)
        x = jnp.dot(r, a, preferred_element_type=F32)
        xr, xi = x[:FFT_N2], x[FFT_N2:]
        kr, ki = ks_ref[j, 0], ks_ref[j, 1]
        y = jnp.concatenate([xr * kr - xi * ki, xr * ki + xi * kr], axis=0).astype(BF16)
        b = lax.dot_general(r, y, (((0,), (0,)), ((), ())), preferred_element_type=F32)
        o_ref[0, j] = b[:FFT_N2].astype(o_ref.dtype)
        o_ref[1, j] = b[FFT_N2:].astype(o_ref.dtype)


def conv_mid(a5, kspec, order, tabs):
    B = a5.shape[0]
    C = a5.shape[-1]
    g = FFT_K1_GROUP
    sq = lambda: _const_spec((FFT_N2, FFT_N2))
    return pl.pallas_call(
        _convmid_kernel,
        grid=(FFT_N1 // g, B),
        in_specs=[pl.BlockSpec((None, 2, g, FFT_N2, C), lambda i, b: (b, 0, i, 0, 0)),
                  pl.BlockSpec((g, 2, FFT_N2, C), lambda i, b: (i, 0, 0, order)),
                  sq(), sq(), sq(), sq()],
        out_specs=pl.BlockSpec((None, 2, g, FFT_N2, C), lambda i, b: (b, 0, i, 0, 0)),
        out_shape=jax.ShapeDtypeStruct(a5.shape, BF16),
        compiler_params=_cparams(("arbitrary", "arbitrary")),
        name="conv_mid",
    )(a5, kspec, tabs["f2r"], tabs["f2i"], tabs["twr"], tabs["twi"])


def _convout_kernel(b_ref, g_ref, z_ref, gate_ref, bias_ref, o_ref):
    bp = b_ref[...].reshape(2 * FFT_N1, b_ref.shape[-1])
    y = jnp.dot(g_ref[...], bp, preferred_element_type=F32)
    z = z_ref[...]
    o_ref[...] = gate_ref[...] * (y + z * bias_ref[...])


def conv_out(bp4, z3, gate3, bias_row, g):
    B, rows, LT = z3.shape
    lb = FFT_LANE_BLOCK
    blk = lambda: pl.BlockSpec((None, rows, lb), lambda b, j: (b, 0, j))
    return pl.pallas_call(
        _convout_kernel,
        grid=(B, LT // lb),
        in_specs=[pl.BlockSpec((None, 2, FFT_N1, lb), lambda b, j: (b, 0, 0, j)),
                  _const_spec((rows, 2 * FFT_N1)), blk(), blk(), _const_spec((1, lb))],
        out_specs=blk(),
        out_shape=jax.ShapeDtypeStruct(z3.shape, F32),
        compiler_params=_cparams(("arbitrary", "arbitrary")),
        name="conv_out",
    )(bp4, g, z3, gate3, bias_row)


def hyena_mixer(xs, kspec, bias, tabs):
    B = xs.shape[1]
    lt = FFT_N2 * HY_WIDTH
    z = xs[2]
    for o in range(HY_ORDER):
        z3 = z.reshape(B, FFT_H1, lt)
        a = dft_first_axis(z3, tabs["f1h"])
        bp = conv_mid(a.reshape(B, 2, FFT_N1, FFT_N2, HY_WIDTH), kspec, o, tabs)
        bias_row = jnp.tile(bias[o][None, :], (1, FFT_LANE_BLOCK // HY_WIDTH))
        z = conv_out(bp.reshape(B, 2, FFT_N1, lt), z3, xs[o].reshape(B, FFT_H1, lt), bias_row,
                     tabs["g"]).reshape(B, SEQ, HY_WIDTH)
    return z


def _rope(x, cc, ss):
    lane = lax.broadcasted_iota(jnp.int32, x.shape, 1)
    n = x.shape[1]
    half = RET_DK // 2
    swapped = jnp.where(lane % RET_DK < half, pltpu.roll(x, n - half, axis=1),
                        pltpu.roll(x, half, axis=1))
    return x * cc + swapped * ss


def _retstate_kernel(k_ref, v_ref, cc_ref, ss_ref, wk_ref, cd_ref, o_ref, s_ref):
    d = pl.program_id(1)
    g = pl.program_id(2)
    C = RET_CHUNK

    @pl.when(g == 0)
    def _():
        s_ref[...] = jnp.zeros_like(s_ref)

    for j in range(RET_GROUP):
        ci = jnp.where(d == 0, j, RET_GROUP - 1 - j)
        r0 = pl.multiple_of(ci * C, C)
        s = s_ref[...]
        o_ref[ci] = jnp.concatenate(
            [s[h * RET_DK:(h + 1) * RET_DK, h * RET_DV:(h + 1) * RET_DV] for h in range(RET_HEADS)],
            axis=1)
        k = _rope(k_ref[pl.ds(r0, C), :], cc_ref[pl.ds(r0, C), :], ss_ref[pl.ds(r0, C), :])
        kw = (k * wk_ref[...]).astype(BF16)
        ds = lax.dot_general(kw, v_ref[pl.ds(r0, C), :], (((0,), (0,)), ((), ())),
                             preferred_element_type=F32)
        s_ref[...] = s * cd_ref[...] + ds


def retention_states(qk, v, tabs):
    B, S, _ = qk.shape
    rows = RET_GROUP * RET_CHUNK
    G = S // rows
    grp = lambda d, g: jnp.where(d == 0, g, G - 1 - g)
    return pl.pallas_call(
        _retstate_kernel,
        grid=(B, 2, G),
        in_specs=[
            pl.BlockSpec((None, rows, RET_QK), lambda b, d, g: (b, grp(d, g), 1)),
            pl.BlockSpec((None, rows, RET_V), lambda b, d, g: (b, grp(d, g), 0)),
            pl.BlockSpec((rows, RET_QK), lambda b, d, g: (grp(d, g), 0)),
            pl.BlockSpec((rows, RET_QK), lambda b, d, g: (grp(d, g), 0)),
            pl.BlockSpec((None, RET_CHUNK, RET_QK), lambda b, d, g: (d, 0, 0)),
            pl.BlockSpec((None, RET_QK, RET_V), lambda b, d, g: (d, 0, 0)),
        ],
        out_specs=pl.BlockSpec((None, None, RET_GROUP, RET_DK, RET_V),
                               lambda b, d, g: (b, d, grp(d, g), 0, 0)),
        out_shape=jax.ShapeDtypeStruct((B, 2, S // RET_CHUNK, RET_DK, RET_V), F32),
        scratch_shapes=[pltpu.VMEM((RET_QK, RET_V), F32)],
        compiler_params=_cparams(("arbitrary", "arbitrary", "arbitrary")),
        name="retention_states",
    )(qk, v, tabs["cc"], tabs["ss"], tabs["wk"], tabs["cd"])


def _retout_kernel(qk_ref, v_ref, gr_ref, cc_ref, ss_ref, st_ref, dec_ref, wq_ref, wk_ref, bd_ref,
                   o_ref):
    C = RET_CHUNK
    bd = bd_ref[...]
    lane = lax.broadcasted_iota(jnp.int32, (C, RET_QK), 1)
    for j in range(RET_GROUP):
        r0 = j * C
        cc = cc_ref[r0:r0 + C, :]
        ss = ss_ref[r0:r0 + C, :]
        q = _rope(qk_ref[r0:r0 + C, :RET_QK], cc, ss)
        k = (_rope(qk_ref[r0:r0 + C, RET_QK:], cc, ss) * (RET_DK ** -0.5)).astype(BF16)
        v = v_ref[r0:r0 + C, :]
        inner = []
        for h in range(RET_HEADS):
            qh = jnp.where(lane // RET_DK == h, q, 0.0).astype(BF16)
            s = lax.dot_general(qh, k, (((1,), (1,)), ((), ())), preferred_element_type=F32)
            s = (s * dec_ref[h]).astype(BF16)
            inner.append(jnp.dot(s, v[:, h * RET_DV:(h + 1) * RET_DV], preferred_element_type=F32))
        qq = jnp.concatenate([q * wq_ref[0], q * wq_ref[1]], axis=1).astype(BF16)
        sf = jnp.concatenate([st_ref[0, j]] * RET_HEADS, axis=0) * bd
        sb = jnp.concatenate([st_ref[1, j]] * RET_HEADS, axis=0) * bd
        sbd = jnp.concatenate([sf, sb], axis=0).astype(BF16)
        o = jnp.concatenate(inner, axis=1) + jnp.dot(qq, sbd, preferred_element_type=F32)
        outs = []
        for h in range(RET_HEADS):
            oh = o[:, h * RET_DV:(h + 1) * RET_DV]
            outs.append(oh * lax.rsqrt(jnp.mean(oh * oh, axis=-1, keepdims=True) + EPS))
        gr = gr_ref[r0:r0 + C, :]
        o_ref[r0:r0 + C, :] = (gr * jax.nn.sigmoid(gr)) * jnp.concatenate(outs, axis=1)


def retention_out(qk, v, gr, states, tabs):
    B, S, _ = qk.shape
    rows = RET_GROUP * RET_CHUNK
    G = S // rows
    return pl.pallas_call(
        _retout_kernel,
        grid=(B, G),
        in_specs=[
            pl.BlockSpec((None, rows, 2 * RET_QK), lambda b, g: (b, g, 0)),
            pl.BlockSpec((None, rows, RET_V), lambda b, g: (b, g, 0)),
            pl.BlockSpec((None, rows, RET_V), lambda b, g: (b, g, 0)),
            pl.BlockSpec((rows, RET_QK), lambda b, g: (g, 0)),
            pl.BlockSpec((rows, RET_QK), lambda b, g: (g, 0)),
            pl.BlockSpec((None, 2, RET_GROUP, RET_DK, RET_V), lambda b, g: (b, 0, g, 0, 0)),
            _const_spec((RET_HEADS, RET_CHUNK, RET_CHUNK)),
            _const_spec((2, RET_CHUNK, RET_QK)),
            _const_spec((2, RET_CHUNK, RET_QK)),
            _const_spec((RET_QK, RET_V)),
        ],
        out_specs=pl.BlockSpec((None, rows, RET_V), lambda b, g: (b, g, 0)),
        out_shape=jax.ShapeDtypeStruct((B, S, RET_V), F32),
        compiler_params=_cparams(("arbitrary", "arbitrary")),
        name="retention_out",
    )(qk, v, gr, tabs["cc"], tabs["ss"], states, tabs["dec"], tabs["wq"], tabs["wk"], tabs["bd"])


def _merge_kernel(x_ref, yh_ref, yr_ref, gm_ref, ga_ref, whb_ref, wrb_ref, wo_ref,
                  g_ref, sh_ref, sc_ref, wr_ref, br_ref, xo_ref, h_ref, rt_ref):
    gm = gm_ref[...]
    ghy = jax.nn.sigmoid(gm[:, :D_MODEL])
    grt = jax.nn.sigmoid(gm[:, D_MODEL:])
    m = (ghy * jnp.dot(yh_ref[...].astype(BF16), whb_ref[...], preferred_element_type=F32)
         + grt * jnp.dot(yr_ref[...].astype(BF16), wrb_ref[...], preferred_element_type=F32))
    out = jnp.dot(m.astype(BF16), wo_ref[...], preferred_element_type=F32)
    x = x_ref[...] + ga_ref[...] * out
    xo_ref[...] = x
    h = _norm_mod(x, g_ref[...], sc_ref[...], sh_ref[...])
    h_ref[...] = h.astype(h_ref.dtype)
    logits = jnp.dot(h, wr_ref[...], preferred_element_type=F32, precision=HIGHEST) + br_ref[...]
    tm = logits.shape[0]
    lane_e = lax.broadcasted_iota(jnp.int32, logits.shape, 1).astype(F32)
    lane_o = lax.broadcasted_iota(jnp.int32, (tm, LANES), 1)
    route = jnp.zeros((tm, LANES), F32)
    vals = []
    work = logits
    for r in range(TOP_K):
        m_r = jnp.max(work, axis=-1, keepdims=True)
        i_r = jnp.min(jnp.where(work == m_r, lane_e, float(N_EXPERTS)), axis=-1, keepdims=True)
        work = jnp.where(lane_e == i_r, -jnp.inf, work)
        vals.append(m_r)
        route = jnp.where(lane_o == r, i_r, route)
    exps = [jnp.exp(v - vals[0]) for v in vals]
    den = exps[0] + exps[1] + exps[2] + exps[3]
    for r in range(TOP_K):
        route = jnp.where(lane_o == TOP_K + r, exps[r] / den, route)
    rt_ref[...] = route


def merge_and_route(x2, yh, yr, gm, mod, g_ffn, whb, wrb, wo, w_router, b_router, layer):
    T, D = x2.shape
    tm = TOK_TILE
    per_b = SEQ // tm
    mspec = lambda piece: pl.BlockSpec((None, None, None, 1, D),
                                       lambda i: (layer, piece, i // per_b, 0, 0))
    row = lambda w: pl.BlockSpec((tm, w), lambda i: (i, 0))
    wsp = lambda a, b: pl.BlockSpec((None, a, b), lambda i: (layer, 0, 0))
    return pl.pallas_call(
        _merge_kernel,
        grid=(T // tm,),
        in_specs=[row(D), row(HY_WIDTH), row(RET_V), row(2 * D), mspec(2),
                  wsp(HY_WIDTH, D), wsp(RET_V, D), wsp(D, D),
                  wsp(1, D), mspec(3), mspec(4), wsp(D, N_EXPERTS), wsp(1, N_EXPERTS)],
        out_specs=[row(D), row(D), row(LANES)],
        out_shape=[jax.ShapeDtypeStruct((T, D), F32), jax.ShapeDtypeStruct((T, D), F32),
                   jax.ShapeDtypeStruct((T, LANES), F32)],
        compiler_params=_cparams(("arbitrary",)),
        name="merge_and_route",
    )(x2, yh, yr, gm, mod, whb, wrb, wo, g_ffn, mod, mod, w_router, b_router)


def _rank_kernel(rt_ref, tril_ref, o_ref, cnt_ref, carry_ref):
    @pl.when(pl.program_id(0) == 0)
    def _():
        carry_ref[...] = jnp.zeros_like(carry_ref)

    rt = rt_ref[...]
    tm = rt.shape[0]
    lane_e = lax.broadcasted_iota(jnp.int32, (tm, N_EXPERTS), 1).astype(F32)
    lane_o = lax.broadcasted_iota(jnp.int32, (tm, LANES), 1)
    hits = [lane_e == rt[:, r:r + 1] for r in range(TOP_K)]
    onehot = sum(h.astype(F32) for h in hits)
    before = jnp.dot(tril_ref[...], onehot.astype(BF16), preferred_element_type=F32) + carry_ref[...]
    out = jnp.zeros((tm, LANES), F32)
    for r in range(TOP_K):
        rank_r = jnp.sum(jnp.where(hits[r], before, 0.0), axis=-1, keepdims=True)
        out = jnp.where(lane_o == r, rank_r, out)
    o_ref[...] = out
    carry_ref[...] += jnp.sum(onehot, axis=0, keepdims=True)
    cnt_ref[...] = carry_ref[...]


def assignment_ranks(route):
    T = route.shape[0]
    tm = RANK_TILE
    tril = jnp.asarray(np.tril(np.ones((tm, tm)), -1), BF16)
    return pl.pallas_call(
        _rank_kernel,
        grid=(T // tm,),
        in_specs=[pl.BlockSpec((tm, LANES), lambda i: (i, 0)), _const_spec((tm, tm))],
        out_specs=[pl.BlockSpec((tm, LANES), lambda i: (i, 0)), _const_spec((1, N_EXPERTS))],
        out_shape=[jax.ShapeDtypeStruct((T, LANES), F32),
                   jax.ShapeDtypeStruct((1, N_EXPERTS), F32)],
        scratch_shapes=[pltpu.VMEM((1, N_EXPERTS), F32)],
        compiler_params=_cparams(("arbitrary",)),
        name="assignment_ranks",
    )(route, tril)


def _dispatch_kernel(dest_ref, h_hbm, xs_init_hbm, xs_hbm, sem):
    del xs_init_hbm
    tm = DISPATCH_TILE
    t0 = pl.program_id(0) * tm

    def issue(i, carry):
        t = t0 + i
        for r in range(TOP_K):
            d = dest_ref[t * TOP_K + r]
            pltpu.make_async_copy(h_hbm.at[pl.ds(t, 1)], xs_hbm.at[pl.ds(d, 1)], sem).start()
        return carry
    lax.fori_loop(0, tm, issue, 0, unroll=4)

    def drain(i, carry):
        pltpu.make_async_copy(h_hbm.at[pl.ds(0, 1)], xs_hbm.at[pl.ds(0, 1)], sem).wait()
        return carry
    lax.fori_loop(0, tm * TOP_K, drain, 0, unroll=8)


def dispatch(dest, h, n_rows):
    T, D = h.shape
    tm = DISPATCH_TILE
    grid_spec = pltpu.PrefetchScalarGridSpec(
        num_scalar_prefetch=1,
        grid=(T // tm,),
        in_specs=[pl.BlockSpec(memory_space=pl.ANY), pl.BlockSpec(memory_space=pl.ANY)],
        out_specs=pl.BlockSpec(memory_space=pl.ANY),
        scratch_shapes=[pltpu.SemaphoreType.DMA],
    )
    return pl.pallas_call(
        _dispatch_kernel,
        grid_spec=grid_spec,
        out_shape=jax.ShapeDtypeStruct((n_rows, D), h.dtype),
        input_output_aliases={2: 0},
        compiler_params=_cparams(("arbitrary",)),
        name="dispatch",
    )(dest, h, jnp.zeros((n_rows, D), h.dtype))


def _expert_kernel(be_ref, nb_ref, x_ref, wgu_ref, bgu_ref, wd_ref, bd_ref, perm_ref, o_ref,
                   wgu_s, wd_s):
    i = pl.program_id(0)
    prev = be_ref[jnp.maximum(i - 1, 0)]
    new_expert = jnp.logical_or(i == 0, be_ref[i] != prev)
    n_chunk = 2 * D_FF // (2 * LANES)

    @pl.when(new_expert)
    def _():
        for c in range(n_chunk):
            cols = slice(c * 2 * LANES, (c + 1) * 2 * LANES)
            wgu_s[:, cols] = jnp.dot(wgu_ref[:, cols].astype(BF16), perm_ref[...],
                                     preferred_element_type=F32).astype(BF16)
        wd_s[...] = wd_ref[...].astype(BF16)

    @pl.when(i < nb_ref[0])
    def _():
        gu = jnp.dot(x_ref[...].astype(BF16), wgu_s[...], preferred_element_type=F32) + bgu_ref[...]
        acts = []
        for c in range(n_chunk):
            gate = jnp.minimum(gu[:, c * 2 * LANES:c * 2 * LANES + LANES], SWIGLU_LIMIT)
            up = jnp.clip(gu[:, c * 2 * LANES + LANES:(c + 1) * 2 * LANES], -SWIGLU_LIMIT, SWIGLU_LIMIT)
            acts.append((up + 1.0) * (gate * jax.nn.sigmoid(SWIGLU_ALPHA * gate)))
        act = jnp.concatenate(acts, axis=1).astype(BF16)
        o_ref[...] = jnp.dot(act, wd_s[...], preferred_element_type=F32) + bd_ref[...]

    @pl.when(i >= nb_ref[0])
    def _():
        o_ref[...] = jnp.zeros_like(o_ref)


def expert_ffn(block_e, n_used, xs, w_gate_up, bgu, w_down, bd, layer):
    P, D = xs.shape
    tm = EXPERT_TILE
    perm = np.zeros((2 * LANES, 2 * LANES))
    j = np.arange(LANES)
    perm[2 * j, j] = 1.0
    perm[2 * j + 1, LANES + j] = 1.0
    grid_spec = pltpu.PrefetchScalarGridSpec(
        num_scalar_prefetch=2,
        grid=(P // tm,),
        in_specs=[
            pl.BlockSpec((tm, D), lambda i, be, nb: (i, 0)),
            pl.BlockSpec((None, None, D, 2 * D_FF), lambda i, be, nb: (layer, be[i], 0, 0)),
            pl.BlockSpec((None, None, 1, 2 * D_FF), lambda i, be, nb: (layer, be[i], 0, 0)),
            pl.BlockSpec((None, None, D_FF, D), lambda i, be, nb: (layer, be[i], 0, 0)),
            pl.BlockSpec((None, None, 1, D), lambda i, be, nb: (layer, be[i], 0, 0)),
            pl.BlockSpec((2 * LANES, 2 * LANES), lambda i, be, nb: (0, 0)),
        ],
        out_specs=pl.BlockSpec((tm, D), lambda i, be, nb: (i, 0)),
        scratch_shapes=[pltpu.VMEM((D, 2 * D_FF), BF16), pltpu.VMEM((D_FF, D), BF16)],
    )
    return pl.pallas_call(
        _expert_kernel,
        grid_spec=grid_spec,
        out_shape=jax.ShapeDtypeStruct((P, D), F32),
        compiler_params=_cparams(("arbitrary",)),
        name="expert_ffn",
    )(block_e, n_used, xs, w_gate_up, bgu, w_down, bd, jnp.asarray(perm, BF16))


def _combine_kernel(dest_ref, x_ref, rt_ref, ga_ref, g_ref, yb_hbm, o_ref, buf, sem, *, final):
    tm = COMBINE_TILE
    t0 = pl.program_id(0) * tm

    def issue(i, carry):
        t = t0 + i
        for r in range(TOP_K):
            d = dest_ref[t * TOP_K + r]
            pltpu.make_async_copy(yb_hbm.at[pl.ds(d, 1)], buf.at[r, pl.ds(i, 1)], sem).start()
        return carry
    lax.fori_loop(0, tm, issue, 0, unroll=4)

    def drain(i, carry):
        pltpu.make_async_copy(yb_hbm.at[pl.ds(0, 1)], buf.at[0, pl.ds(0, 1)], sem).wait()
        return carry
    lax.fori_loop(0, tm * TOP_K, drain, 0, unroll=8)

    rt = rt_ref[...]
    y = buf[0] * rt[:, TOP_K:TOP_K + 1]
    for r in range(1, TOP_K):
        y = y + buf[r] * rt[:, TOP_K + r:TOP_K + r + 1]
    x = x_ref[...] + ga_ref[...] * y
    if final:
        x = (x * lax.rsqrt(jnp.mean(x * x, axis=-1, keepdims=True) + EPS)) * g_ref[...]
    o_ref[...] = x


def combine(dest, x2, route, mod, g_final, yb, layer, final):
    T, D = x2.shape
    tm = COMBINE_TILE
    per_b = SEQ // tm
    grid_spec = pltpu.PrefetchScalarGridSpec(
        num_scalar_prefetch=1,
        grid=(T // tm,),
        in_specs=[
            pl.BlockSpec((tm, D), lambda i, d: (i, 0)),
            pl.BlockSpec((tm, LANES), lambda i, d: (i, 0)),
            pl.BlockSpec((None, None, None, 1, D), lambda i, d: (layer, 5, i // per_b, 0, 0)),
            pl.BlockSpec((1, D), lambda i, d: (0, 0)),
            pl.BlockSpec(memory_space=pl.ANY),
        ],
        out_specs=pl.BlockSpec((tm, D), lambda i, d: (i, 0)),
        scratch_shapes=[pltpu.VMEM((TOP_K, tm, D), F32), pltpu.SemaphoreType.DMA],
    )
    return pl.pallas_call(
        functools.partial(_combine_kernel, final=final),
        grid_spec=grid_spec,
        out_shape=jax.ShapeDtypeStruct((T, D), F32),
        compiler_params=_cparams(("arbitrary",)),
        name="combine",
    )(dest, x2, route, mod, g_final, yb)


def moe_ffn_residual(x2, h, route, mod, g_final, w_gate_up, bgu, w_down, bd, layer, final):
    T, D = h.shape
    tm = EXPERT_TILE
    rank, counts = assignment_ranks(route)
    counts = counts[0].astype(jnp.int32)
    padded = ((counts + tm - 1) // tm) * tm
    pad_ends = jnp.cumsum(padded)
    pad_starts = pad_ends - padded
    n_blocks = -(-(T * TOP_K + N_EXPERTS * (tm - 1)) // tm)
    block_e = jnp.minimum(
        jnp.searchsorted(pad_ends, jnp.arange(n_blocks) * tm, side='right'), N_EXPERTS - 1
    ).astype(jnp.int32)
    n_used = (pad_ends[-1:] // tm).astype(jnp.int32)
    top_idx = route[:, :TOP_K].astype(jnp.int32)
    dest = (pad_starts[top_idx] + rank[:, :TOP_K].astype(jnp.int32)).reshape(-1)
    xs = dispatch(dest, h, n_blocks * tm)
    yb = expert_ffn(block_e, n_used, xs, w_gate_up, bgu, w_down, bd, layer)
    return combine(dest, x2, route, mod, g_final, yb, layer, final)


def kernel(x, c, norm_mix_g, norm_ffn_g, w_mod, b_mod, w_in, hy_conv_w, hy_conv_b, hy_w1, hy_b1, hy_w2, hy_b2, hy_w3, hy_b3, hy_w4, hy_freq, hy_bias, w_hy_br, w_ret_br, w_out, w_router, b_router, w_gate_up, b_gate_up, w_down, b_down, final_g):
    B, S, D = x.shape
    L = w_mod.shape[0]
    T = B * S
    tabs = _tables()

    w_in_bf = w_in.astype(BF16)
    whb = w_hy_br.astype(BF16)
    wrb = w_ret_br.astype(BF16)
    wo = w_out.astype(BF16)
    E, F = N_EXPERTS, D_FF
    bgu = b_gate_up.reshape(L, E, F // LANES, LANES, 2).transpose(0, 1, 2, 4, 3).reshape(L, E, 1, 2 * F)
    bdn = b_down.reshape(L, E, 1, D)
    w1p = jnp.pad(hy_w1, ((0, 0), (0, HY_EMB_PAD - HY_EMB), (0, 0)))
    vec = lambda a: a.reshape(L, 1, -1)

    c_pad = jnp.pad(c, ((0, 8 - B), (0, 0)))
    mod = modulation(c_pad, w_mod, b_mod)[:, :B]
    mod = mod.reshape(L, B, N_MOD, 1, D).transpose(0, 2, 1, 3, 4)

    x2 = x.reshape(T, D)
    for l in range(L):
        u, qk, v, gr, gm = in_projection(x2, vec(norm_mix_g), mod, w_in_bf, l)
        xs = short_conv(u.reshape(B, S, -1), hy_conv_w, vec(hy_conv_b), l)
        kraw, l1 = hyena_filter(tabs, w1p, vec(hy_b1), hy_w2, vec(hy_b2), hy_w3, vec(hy_b3),
                                hy_w4, vec(hy_freq), l)
        ka = dft_first_axis(kraw.reshape(1, FFT_N1, -1), tabs["f1"])
        kspec = filter_spectrum(ka.reshape(1, 2, FFT_N1, FFT_N2, -1), l1, tabs)
        y_hy = hyena_mixer(xs, kspec, hy_bias[l], tabs)
        qk3, v3, gr3 = qk.reshape(B, S, -1), v.reshape(B, S, -1), gr.reshape(B, S, -1)
        states = retention_states(qk3, v3, tabs)
        y_ret = retention_out(qk3, v3, gr3, states, tabs)
        x2, h_ffn, route = merge_and_route(
            x2, y_hy.reshape(T, -1), y_ret.reshape(T, -1), gm, mod, vec(norm_ffn_g),
            whb, wrb, wo, w_router, vec(b_router), l)
        x2 = moe_ffn_residual(x2, h_ffn, route, mod, final_g.reshape(1, D), w_gate_up, bgu,
                              w_down, bdn, l, final=(l == L - 1))
    return x2.reshape(B, S, D)
```

```python
import functools
import math

import numpy as np
import jax
import jax.numpy as jnp
from jax import lax
from jax.experimental import pallas as pl
from jax.experimental.pallas import tpu as pltpu

F32 = jnp.float32
BF16 = jnp.bfloat16
HIGHEST = lax.Precision.HIGHEST

D_MODEL = 1024
BATCH = 4
SEQ = 8192
DEPTH = 4
HY_WIDTH = 512
HY_ORDER = 2
HY_EMB = 33
HY_EMB_PAD = 64
HY_FILT_HIDDEN = 64
HY_FAST_DECAY = 0.3
HY_SLOW_DECAY = 1.5
HY_DECAY_TARGET = 1e-2
RET_HEADS = 4
RET_DK = 64
RET_DV = 128
RET_QK = RET_HEADS * RET_DK
RET_V = RET_HEADS * RET_DV
RET_CHUNK = 128
RET_DECAY_FWD = 5.0
RET_DECAY_BWD = 5.5
ROPE_BASE = 10000.0
N_EXPERTS = 32
TOP_K = 4
D_FF = D_MODEL
SWIGLU_ALPHA = 1.702
SWIGLU_LIMIT = 7.0
N_MOD = 6
EPS = 1e-6
IN_COLS = 3 * HY_WIDTH + 2 * RET_QK + 2 * RET_V + 2 * D_MODEL

LANES = 128
VMEM_LIMIT = 56 * 1024 * 1024

FFT_N = 2 * SEQ
FFT_N1 = 128
FFT_N2 = FFT_N // FFT_N1
FFT_H1 = FFT_N1 // 2

TOK_TILE = 256
EXPERT_TILE = 512
RANK_TILE = 512
DISPATCH_TILE = 512
COMBINE_TILE = 256
RET_GROUP = 8
FFT_LANE_BLOCK = 4096
FFT_K1_GROUP = 4


def _cparams(sem):
    return pltpu.CompilerParams(dimension_semantics=sem, vmem_limit_bytes=VMEM_LIMIT)


def _const_spec(shape):
    return pl.BlockSpec(shape, lambda *_: (0,) * len(shape))


def _tables():
    n1 = np.arange(FFT_N1)
    n2 = np.arange(FFT_N2)
    ang1 = 2.0 * np.pi * np.outer(n1, n1) / FFT_N1
    f1 = np.concatenate([np.cos(ang1), -np.sin(ang1)], axis=0)
    ang2 = 2.0 * np.pi * np.outer(n2, n2) / FFT_N2
    angt = 2.0 * np.pi * np.outer(n1, n2) / FFT_N
    g = np.concatenate([np.cos(ang1), -np.sin(ang1)], axis=1)[:FFT_H1] / FFT_N

    L = SEQ
    t = np.linspace(0.0, 1.0, L)
    bands = (HY_EMB - 1) // 2
    w = 2.0 * np.pi * np.arange(L) / L
    f = np.linspace(1e-4, bands - 1, bands)
    feats = np.concatenate([t[:, None], np.cos(f[None] * w[:, None]), -np.sin(f[None] * w[:, None])], -1)
    idx = np.concatenate([np.arange(L), [0], L - np.arange(1, L)])
    feats2 = np.zeros((2 * L, HY_EMB_PAD))
    feats2[:, :HY_EMB] = feats[idx]
    max_decay = math.log(HY_DECAY_TARGET) / HY_FAST_DECAY
    min_decay = math.log(HY_DECAY_TARGET) / HY_SLOW_DECAY
    deltas = np.abs(np.linspace(min_decay, max_decay, HY_WIDTH))

    C = RET_CHUNK
    hh = np.arange(RET_HEADS)
    lgf = np.log(1.0 - np.exp2(-(RET_DECAY_FWD + hh)))
    lgb = np.log(1.0 - np.exp2(-(RET_DECAY_BWD + hh)))
    pos = np.arange(C)
    diff = pos[:, None] - pos[None, :]
    dec = np.where(diff[None] >= 0, np.exp(np.maximum(diff, 0)[None] * lgf[:, None, None]),
                   np.exp(np.maximum(-diff, 0)[None] * lgb[:, None, None]))
    lane_h = np.repeat(hh, RET_DK)
    wq = np.stack([np.exp((pos[:, None] + 1.0) * lgf[lane_h][None]),
                   np.exp((C - pos[:, None]) * lgb[lane_h][None])])
    wk = np.stack([np.exp((C - 1.0 - pos[:, None]) * lgf[lane_h][None]),
                   np.exp(pos[:, None] * lgb[lane_h][None])]) * (RET_DK ** -0.5)
    cd = np.stack([np.broadcast_to(np.exp(C * lgf[lane_h])[:, None], (RET_QK, RET_V)),
                   np.broadcast_to(np.exp(C * lgb[lane_h])[:, None], (RET_QK, RET_V))])
    col_h = np.repeat(hh, RET_DV)
    bd = (lane_h[:, None] == col_h[None, :]).astype(np.float64)
    inv_freq = 1.0 / (ROPE_BASE ** (np.arange(0, RET_DK, 2) / RET_DK))
    ang = np.arange(SEQ)[:, None] * inv_freq[None, :]
    cc = np.tile(np.concatenate([np.cos(ang), np.cos(ang)], -1), (1, RET_HEADS))
    ss = np.tile(np.concatenate([-np.sin(ang), np.sin(ang)], -1), (1, RET_HEADS))

    return dict(
        f1=jnp.asarray(f1, BF16), f1h=jnp.asarray(f1[:, :FFT_H1], BF16),
        f2r=jnp.asarray(np.cos(ang2), F32), f2i=jnp.asarray(-np.sin(ang2), F32),
        twr=jnp.asarray(np.cos(angt), F32), twi=jnp.asarray(-np.sin(angt), F32),
        g=jnp.asarray(g, BF16),
        feats2=jnp.asarray(feats2, F32), deltas=jnp.asarray(deltas[None], F32),
        dec=jnp.asarray(dec, F32), wq=jnp.asarray(wq, F32), wk=jnp.asarray(wk, F32),
        cd=jnp.asarray(cd, F32), bd=jnp.asarray(bd, F32),
        cc=jnp.asarray(cc, F32), ss=jnp.asarray(ss, F32),
    )


def _mod_kernel(c_ref, w_ref, b_ref, o_ref):
    c = c_ref[...]
    ca = c * jax.nn.sigmoid(c)
    o_ref[...] = jnp.dot(ca, w_ref[...], preferred_element_type=F32, precision=HIGHEST) + b_ref[...]


def modulation(c_pad, w_mod, b_mod):
    L, D, N = w_mod.shape
    tn = 1536
    rows = c_pad.shape[0]
    return pl.pallas_call(
        _mod_kernel,
        grid=(L, N // tn),
        in_specs=[
            _const_spec((rows, D)),
            pl.BlockSpec((None, D, tn), lambda l, j: (l, 0, j)),
            pl.BlockSpec((None, 1, tn), lambda l, j: (l, 0, j)),
        ],
        out_specs=pl.BlockSpec((None, rows, tn), lambda l, j: (l, 0, j)),
        out_shape=jax.ShapeDtypeStruct((L, rows, N), F32),
        compiler_params=_cparams(("arbitrary", "arbitrary")),
        name="modulation",
    )(c_pad, w_mod, b_mod.reshape(L, 1, N))


def _norm_mod(x, g, sc, sh):
    y = x * lax.rsqrt(jnp.mean(x * x, axis=-1, keepdims=True) + EPS)
    return (y * g) * (1.0 + sc) + sh


def _inproj_kernel(x_ref, g_ref, sh_ref, sc_ref, w_ref, u_ref, qk_ref, v_ref, gr_ref, gm_ref):
    h = _norm_mod(x_ref[...], g_ref[...], sc_ref[...], sh_ref[...]).astype(BF16)
    c0 = 0
    for o_ref in (u_ref, qk_ref, v_ref, gr_ref, gm_ref):
        c1 = c0 + o_ref.shape[-1]
        o_ref[...] = jnp.dot(h, w_ref[:, c0:c1], preferred_element_type=F32).astype(o_ref.dtype)
        c0 = c1


def in_projection(x2, g, mod, w_in_bf, layer):
    T, D = x2.shape
    tm = TOK_TILE
    per_b = SEQ // tm
    mspec = lambda piece: pl.BlockSpec((None, None, None, 1, D),
                                       lambda i: (layer, piece, i // per_b, 0, 0))
    widths = (3 * HY_WIDTH, 2 * RET_QK, RET_V, RET_V, 2 * D_MODEL)
    dtypes = (F32, F32, BF16, F32, F32)
    return pl.pallas_call(
        _inproj_kernel,
        grid=(T // tm,),
        in_specs=[
            pl.BlockSpec((tm, D), lambda i: (i, 0)),
            pl.BlockSpec((None, 1, D), lambda i: (layer, 0, 0)),
            mspec(0), mspec(1),
            pl.BlockSpec((None, D, IN_COLS), lambda i: (layer, 0, 0)),
        ],
        out_specs=[pl.BlockSpec((tm, w), lambda i: (i, 0)) for w in widths],
        out_shape=[jax.ShapeDtypeStruct((T, w), dt) for w, dt in zip(widths, dtypes)],
        compiler_params=_cparams(("arbitrary",)),
        name="in_projection",
    )(x2, g, mod, mod, w_in_bf)


def _shortconv_kernel(u_ref, w_ref, b_ref, o_ref):
    u = u_ref[...]
    s = u.shape[0]
    row = lax.broadcasted_iota(jnp.int32, u.shape, 0)
    prev = jnp.where(row == 0, 0.0, pltpu.roll(u, 1, axis=0))
    nxt = jnp.where(row == s - 1, 0.0, pltpu.roll(u, s - 1, axis=0))
    w = w_ref[...]
    o_ref[...] = prev * w[0:1] + u * w[1:2] + nxt * w[2:3] + b_ref[...]


def short_conv(u, conv_w, conv_b, layer):
    B, S, C3 = u.shape
    cb = LANES
    per = HY_WIDTH // cb
    return pl.pallas_call(
        _shortconv_kernel,
        grid=(B, C3 // cb),
        in_specs=[
            pl.BlockSpec((None, S, cb), lambda b, j: (b, 0, j)),
            pl.BlockSpec((None, 3, cb), lambda b, j: (layer, 0, j)),
            pl.BlockSpec((None, 1, cb), lambda b, j: (layer, 0, j)),
        ],
        out_specs=pl.BlockSpec((None, None, S, cb), lambda b, j: (j // per, b, 0, j % per)),
        out_shape=jax.ShapeDtypeStruct((3, B, S, HY_WIDTH), F32),
        compiler_params=_cparams(("arbitrary", "arbitrary")),
        name="short_conv",
    )(u, conv_w, conv_b)


def _filter_kernel(p_ref, w1_ref, b1_ref, w2_ref, b2_ref, w3_ref, b3_ref, w4_ref, fr_ref, dl_ref,
                   k_ref, l1_ref, *, rows):
    i = pl.program_id(0)
    p = p_ref[...]
    fr = fr_ref[...]
    dot = functools.partial(jnp.dot, preferred_element_type=F32, precision=HIGHEST)
    h = jnp.sin(fr * (dot(p, w1_ref[...]) + b1_ref[...]))
    h = jnp.sin(fr * (dot(h, w2_ref[...]) + b2_ref[...]))
    h = jnp.sin(fr * (dot(h, w3_ref[...]) + b3_ref[...]))
    h = dot(h, w4_ref[...])
    t = p[:, 0:1]
    win = jnp.exp(-t * dl_ref[...])
    win = jnp.concatenate([win] * HY_ORDER, axis=1)
    grow = i * rows + lax.broadcasted_iota(jnp.int32, (rows, 1), 0)
    k = jnp.where(grow == SEQ, 0.0, h * win)
    k_ref[...] = k

    @pl.when(i == 0)
    def _():
        l1_ref[...] = jnp.zeros_like(l1_ref)
    l1_ref[...] += jnp.sum(jnp.abs(k), axis=0, keepdims=True)


def hyena_filter(tabs, w1p, b1, w2, b2, w3, b3, w4, freq, layer):
    rows = 1024
    n = 2 * SEQ
    half = SEQ // rows
    H = HY_FILT_HIDDEN
    OC = HY_ORDER * HY_WIDTH
    vec = lambda: pl.BlockSpec((None, 1, H), lambda i: (layer, 0, 0))
    return pl.pallas_call(
        functools.partial(_filter_kernel, rows=rows),
        grid=(n // rows,),
        in_specs=[
            pl.BlockSpec((rows, HY_EMB_PAD), lambda i: (i, 0)),
            pl.BlockSpec((None, HY_EMB_PAD, H), lambda i: (layer, 0, 0)), vec(),
            pl.BlockSpec((None, H, H), lambda i: (layer, 0, 0)), vec(),
            pl.BlockSpec((None, H, H), lambda i: (layer, 0, 0)), vec(),
            pl.BlockSpec((None, H, OC), lambda i: (layer, 0, i // half)),
            vec(),
            _const_spec((1, HY_WIDTH)),
        ],
        out_specs=[pl.BlockSpec((rows, OC), lambda i: (i, 0)), _const_spec((1, OC))],
        out_shape=[jax.ShapeDtypeStruct((n, OC), F32), jax.ShapeDtypeStruct((1, OC), F32)],
        compiler_params=_cparams(("arbitrary",)),
        name="hyena_filter",
    )(tabs["feats2"], w1p, b1, w2, b2, w3, b3, w4, freq, tabs["deltas"])


def _dft1_kernel(x_ref, f_ref, o_ref):
    a = jnp.dot(f_ref[...], x_ref[...].astype(BF16), preferred_element_type=F32)
    o_ref[...] = a.reshape(o_ref.shape).astype(o_ref.dtype)


def dft_first_axis(x3, f1):
    B, rows, LT = x3.shape
    lb = FFT_LANE_BLOCK
    return pl.pallas_call(
        _dft1_kernel,
        grid=(B, LT // lb),
        in_specs=[pl.BlockSpec((None, rows, lb), lambda b, j: (b, 0, j)),
                  _const_spec((2 * FFT_N1, rows))],
        out_specs=pl.BlockSpec((None, 2, FFT_N1, lb), lambda b, j: (b, 0, 0, j)),
        out_shape=jax.ShapeDtypeStruct((B, 2, FFT_N1, LT), BF16),
        compiler_params=_cparams(("arbitrary", "arbitrary")),
        name="dft_first_axis",
    )(x3, f1)


def _twiddled_stack(f2r_ref, f2i_ref, twr_ref, twi_ref, k1):
    tr = twr_ref[pl.ds(k1, 1), :]
    ti = twi_ref[pl.ds(k1, 1), :]
    fr = f2r_ref[...]
    fi = f2i_ref[...]
    p = fr * tr - fi * ti
    q = fr * ti + fi * tr
    top = jnp.concatenate([p, -q], axis=1)
    bot = jnp.concatenate([q, p], axis=1)
    return jnp.concatenate([top, bot], axis=0).astype(BF16)


def _spectrum_kernel(a_ref, l1_ref, f2r_ref, f2i_ref, twr_ref, twi_ref, o_ref):
    base = pl.program_id(0) * FFT_K1_GROUP
    inv = 1.0 / l1_ref[...]
    for j in range(FFT_K1_GROUP):
        r = _twiddled_stack(f2r_ref, f2i_ref, twr_ref, twi_ref, base + j)
        a = jnp.concatenate([a_ref[0, j], a_ref[1, j]], axis=0)
        x = jnp.dot(r, a, preferred_element_type=F32) * inv
        o_ref[j, 0] = x[:FFT_N2]
        o_ref[j, 1] = x[FFT_N2:]


def filter_spectrum(a5, l1, tabs):
    OC = a5.shape[-1]
    g = FFT_K1_GROUP
    sq = lambda: _const_spec((FFT_N2, FFT_N2))
    return pl.pallas_call(
        _spectrum_kernel,
        grid=(FFT_N1 // g,),
        in_specs=[pl.BlockSpec((None, 2, g, FFT_N2, OC), lambda i: (0, 0, i, 0, 0)),
                  _const_spec((1, OC)), sq(), sq(), sq(), sq()],
        out_specs=pl.BlockSpec((g, 2, FFT_N2, OC), lambda i: (i, 0, 0, 0)),
        out_shape=jax.ShapeDtypeStruct((FFT_N1, 2, FFT_N2, OC), F32),
        compiler_params=_cparams(("arbitrary",)),
        name="filter_spectrum",
    )(a5, l1, tabs["f2r"], tabs["f2i"], tabs["twr"], tabs["twi"])


def _convmid_kernel(a_ref, ks_ref, f2r_ref, f2i_ref, twr_ref, twi_ref, o_ref):
    base = pl.program_id(0) * FFT_K1_GROUP
    for j in range(FFT_K1_GROUP):
        r = _twiddled_stack(f2r_ref, f2i_ref, twr_ref, twi_ref, base + j)
        a = jnp.concatenate([a_ref[0, j], a_ref[1, j]], axis=0)
        x = jnp.dot(r, a, preferred_element_type=F32)
        xr, xi = x[:FFT_N2], x[FFT_N2:]
        kr, ki = ks_ref[j, 0], ks_ref[j, 1]
        y = jnp.concatenate([xr * kr - xi * ki, xr * ki + xi * kr], axis=0).astype(BF16)
        b = lax.dot_general(r, y, (((0,), (0,)), ((), ())), preferred_element_type=F32)
        o_ref[0, j] = b[:FFT_N2].astype(o_ref.dtype)
        o_ref[1, j] = b[FFT_N2:].astype(o_ref.dtype)


def conv_mid(a5, kspec, order, tabs):
    B = a5.shape[0]
    C = a5.shape[-1]
    g = FFT_K1_GROUP
    sq = lambda: _const_spec((FFT_N2, FFT_N2))
    return pl.pallas_call(
        _convmid_kernel,
        grid=(FFT_N1 // g, B),
        in_specs=[pl.BlockSpec((None, 2, g, FFT_N2, C), lambda i, b: (b, 0, i, 0, 0)),
                  pl.BlockSpec((g, 2, FFT_N2, C), lambda i, b: (i, 0, 0, order)),
                  sq(), sq(), sq(), sq()],
        out_specs=pl.BlockSpec((None, 2, g, FFT_N2, C), lambda i, b: (b, 0, i, 0, 0)),
        out_shape=jax.ShapeDtypeStruct(a5.shape, BF16),
        compiler_params=_cparams(("arbitrary", "arbitrary")),
        name="conv_mid",
    )(a5, kspec, tabs["f2r"], tabs["f2i"], tabs["twr"], tabs["twi"])


def _convout_kernel(b_ref, g_ref, z_ref, gate_ref, bias_ref, o_ref):
    bp = b_ref[...].reshape(2 * FFT_N1, b_ref.shape[-1])
    y = jnp.dot(g_ref[...], bp, preferred_element_type=F32)
    z = z_ref[...]
    o_ref[...] = gate_ref[...] * (y + z * bias_ref[...])


def conv_out(bp4, z3, gate3, bias_row, g):
    B, rows, LT = z3.shape
    lb = FFT_LANE_BLOCK
    blk = lambda: pl.BlockSpec((None, rows, lb), lambda b, j: (b, 0, j))
    return pl.pallas_call(
        _convout_kernel,
        grid=(B, LT // lb),
        in_specs=[pl.BlockSpec((None, 2, FFT_N1, lb), lambda b, j: (b, 0, 0, j)),
                  _const_spec((rows, 2 * FFT_N1)), blk(), blk(), _const_spec((1, lb))],
        out_specs=blk(),
        out_shape=jax.ShapeDtypeStruct(z3.shape, F32),
        compiler_params=_cparams(("arbitrary", "arbitrary")),
        name="conv_out",
    )(bp4, g, z3, gate3, bias_row)


def hyena_mixer(xs, kspec, bias, tabs):
    B = xs.shape[1]
    lt = FFT_N2 * HY_WIDTH
    z = xs[2]
    for o in range(HY_ORDER):
        z3 = z.reshape(B, FFT_H1, lt)
        a = dft_first_axis(z3, tabs["f1h"])
        bp = conv_mid(a.reshape(B, 2, FFT_N1, FFT_N2, HY_WIDTH), kspec, o, tabs)
        bias_row = jnp.tile(bias[o][None, :], (1, FFT_LANE_BLOCK // HY_WIDTH))
        z = conv_out(bp.reshape(B, 2, FFT_N1, lt), z3, xs[o].reshape(B, FFT_H1, lt), bias_row,
                     tabs["g"]).reshape(B, SEQ, HY_WIDTH)
    return z


def _rope(x, cc, ss):
    lane = lax.broadcasted_iota(jnp.int32, x.shape, 1)
    n = x.shape[1]
    half = RET_DK // 2
    swapped = jnp.where(lane % RET_DK < half, pltpu.roll(x, n - half, axis=1),
                        pltpu.roll(x, half, axis=1))
    return x * cc + swapped * ss


def _retstate_kernel(k_ref, v_ref, cc_ref, ss_ref, wk_ref, cd_ref, o_ref, s_ref):
    d = pl.program_id(1)
    g = pl.program_id(2)
    C = RET_CHUNK

    @pl.when(g == 0)
    def _():
        s_ref[...] = jnp.zeros_like(s_ref)

    for j in range(RET_GROUP):
        ci = jnp.where(d == 0, j, RET_GROUP - 1 - j)
        r0 = pl.multiple_of(ci * C, C)
        s = s_ref[...]
        o_ref[ci] = jnp.concatenate(
            [s[h * RET_DK:(h + 1) * RET_DK, h * RET_DV:(h + 1) * RET_DV] for h in range(RET_HEADS)],
            axis=1)
        k = _rope(k_ref[pl.ds(r0, C), :], cc_ref[pl.ds(r0, C), :], ss_ref[pl.ds(r0, C), :])
        kw = (k * wk_ref[...]).astype(BF16)
        ds = lax.dot_general(kw, v_ref[pl.ds(r0, C), :], (((0,), (0,)), ((), ())),
                             preferred_element_type=F32)
        s_ref[...] = s * cd_ref[...] + ds


def retention_states(qk, v, tabs):
    B, S, _ = qk.shape
    rows = RET_GROUP * RET_CHUNK
    G = S // rows
    grp = lambda d, g: jnp.where(d == 0, g, G - 1 - g)
    return pl.pallas_call(
        _retstate_kernel,
        grid=(B, 2, G),
        in_specs=[
            pl.BlockSpec((None, rows, RET_QK), lambda b, d, g: (b, grp(d, g), 1)),
            pl.BlockSpec((None, rows, RET_V), lambda b, d, g: (b, grp(d, g), 0)),
            pl.BlockSpec((rows, RET_QK), lambda b, d, g: (grp(d, g), 0)),
            pl.BlockSpec((rows, RET_QK), lambda b, d, g: (grp(d, g), 0)),
            pl.BlockSpec((None, RET_CHUNK, RET_QK), lambda b, d, g: (d, 0, 0)),
            pl.BlockSpec((None, RET_QK, RET_V), lambda b, d, g: (d, 0, 0)),
        ],
        out_specs=pl.BlockSpec((None, None, RET_GROUP, RET_DK, RET_V),
                               lambda b, d, g: (b, d, grp(d, g), 0, 0)),
        out_shape=jax.ShapeDtypeStruct((B, 2, S // RET_CHUNK, RET_DK, RET_V), F32),
        scratch_shapes=[pltpu.VMEM((RET_QK, RET_V), F32)],
        compiler_params=_cparams(("arbitrary", "arbitrary", "arbitrary")),
        name="retention_states",
    )(qk, v, tabs["cc"], tabs["ss"], tabs["wk"], tabs["cd"])


def _retout_kernel(qk_ref, v_ref, gr_ref, cc_ref, ss_ref, st_ref, dec_ref, wq_ref, wk_ref, bd_ref,
                   o_ref):
    C = RET_CHUNK
    bd = bd_ref[...]
    lane = lax.broadcasted_iota(jnp.int32, (C, RET_QK), 1)
    for j in range(RET_GROUP):
        r0 = j * C
        cc = cc_ref[r0:r0 + C, :]
        ss = ss_ref[r0:r0 + C, :]
        q = _rope(qk_ref[r0:r0 + C, :RET_QK], cc, ss)
        k = (_rope(qk_ref[r0:r0 + C, RET_QK:], cc, ss) * (RET_DK ** -0.5)).astype(BF16)
        v = v_ref[r0:r0 + C, :]
        inner = []
        for h in range(RET_HEADS):
            qh = jnp.where(lane // RET_DK == h, q, 0.0).astype(BF16)
            s = lax.dot_general(qh, k, (((1,), (1,)), ((), ())), preferred_element_type=F32)
            s = (s * dec_ref[h]).astype(BF16)
            inner.append(jnp.dot(s, v[:, h * RET_DV:(h + 1) * RET_DV], preferred_element_type=F32))
        qq = jnp.concatenate([q * wq_ref[0], q * wq_ref[1]], axis=1).astype(BF16)
        sf = jnp.concatenate([st_ref[0, j]] * RET_HEADS, axis=0) * bd
        sb = jnp.concatenate([st_ref[1, j]] * RET_HEADS, axis=0) * bd
        sbd = jnp.concatenate([sf, sb], axis=0).astype(BF16)
        o = jnp.concatenate(inner, axis=1) + jnp.dot(qq, sbd, preferred_element_type=F32)
        outs = []
        for h in range(RET_HEADS):
            oh = o[:, h * RET_DV:(h + 1) * RET_DV]
            outs.append(oh * lax.rsqrt(jnp.mean(oh * oh, axis=-1, keepdims=True) + EPS))
        gr = gr_ref[r0:r0 + C, :]
        o_ref[r0:r0 + C, :] = (gr * jax.nn.sigmoid(gr)) * jnp.concatenate(outs, axis=1)


def retention_out(qk, v, gr, states, tabs):
    B, S, _ = qk.shape
    rows = RET_GROUP * RET_CHUNK
    G = S // rows
    return pl.pallas_call(
        _retout_kernel,
        grid=(B, G),
        in_specs=[
            pl.BlockSpec((None, rows, 2 * RET_QK), lambda b, g: (b, g, 0)),
            pl.BlockSpec((None, rows, RET_V), lambda b, g: (b, g, 0)),
            pl.BlockSpec((None, rows, RET_V), lambda b, g: (b, g, 0)),
            pl.BlockSpec((rows, RET_QK), lambda b, g: (g, 0)),
            pl.BlockSpec((rows, RET_QK), lambda b, g: (g, 0)),
            pl.BlockSpec((None, 2, RET_GROUP, RET_DK, RET_V), lambda b, g: (b, 0, g, 0, 0)),
            _const_spec((RET_HEADS, RET_CHUNK, RET_CHUNK)),
            _const_spec((2, RET_CHUNK, RET_QK)),
            _const_spec((2, RET_CHUNK, RET_QK)),
            _const_spec((RET_QK, RET_V)),
        ],
        out_specs=pl.BlockSpec((None, rows, RET_V), lambda b, g: (b, g, 0)),
        out_shape=jax.ShapeDtypeStruct((B, S, RET_V), F32),
        compiler_params=_cparams(("arbitrary", "arbitrary")),
        name="retention_out",
    )(qk, v, gr, tabs["cc"], tabs["ss"], states, tabs["dec"], tabs["wq"], tabs["wk"], tabs["bd"])


def _merge_kernel(x_ref, yh_ref, yr_ref, gm_ref, ga_ref, whb_ref, wrb_ref, wo_ref,
                  g_ref, sh_ref, sc_ref, wr_ref, br_ref, xo_ref, h_ref, rt_ref):
    gm = gm_ref[...]
    ghy = jax.nn.sigmoid(gm[:, :D_MODEL])
    grt = jax.nn.sigmoid(gm[:, D_MODEL:])
    m = (ghy * jnp.dot(yh_ref[...].astype(BF16), whb_ref[...], preferred_element_type=F32)
         + grt * jnp.dot(yr_ref[...].astype(BF16), wrb_ref[...], preferred_element_type=F32))
    out = jnp.dot(m.astype(BF16), wo_ref[...], preferred_element_type=F32)
    x = x_ref[...] + ga_ref[...] * out
    xo_ref[...] = x
    h = _norm_mod(x, g_ref[...], sc_ref[...], sh_ref[...])
    h_ref[...] = h.astype(h_ref.dtype)
    logits = jnp.dot(h, wr_ref[...], preferred_element_type=F32, precision=HIGHEST) + br_ref[...]
    tm = logits.shape[0]
    lane_e = lax.broadcasted_iota(jnp.int32, logits.shape, 1).astype(F32)
    lane_o = lax.broadcasted_iota(jnp.int32, (tm, LANES), 1)
    route = jnp.zeros((tm, LANES), F32)
    vals = []
    work = logits
    for r in range(TOP_K):
        m_r = jnp.max(work, axis=-1, keepdims=True)
        i_r = jnp.min(jnp.where(work == m_r, lane_e, float(N_EXPERTS)), axis=-1, keepdims=True)
        work = jnp.where(lane_e == i_r, -jnp.inf, work)
        vals.append(m_r)
        route = jnp.where(lane_o == r, i_r, route)
    exps = [jnp.exp(v - vals[0]) for v in vals]
    den = exps[0] + exps[1] + exps[2] + exps[3]
    for r in range(TOP_K):
        route = jnp.where(lane_o == TOP_K + r, exps[r] / den, route)
    rt_ref[...] = route


def merge_and_route(x2, yh, yr, gm, mod, g_ffn, whb, wrb, wo, w_router, b_router, layer):
    T, D = x2.shape
    tm = TOK_TILE
    per_b = SEQ // tm
    mspec = lambda piece: pl.BlockSpec((None, None, None, 1, D),
                                       lambda i: (layer, piece, i // per_b, 0, 0))
    row = lambda w: pl.BlockSpec((tm, w), lambda i: (i, 0))
    wsp = lambda a, b: pl.BlockSpec((None, a, b), lambda i: (layer, 0, 0))
    return pl.pallas_call(
        _merge_kernel,
        grid=(T // tm,),
        in_specs=[row(D), row(HY_WIDTH), row(RET_V), row(2 * D), mspec(2),
                  wsp(HY_WIDTH, D), wsp(RET_V, D), wsp(D, D),
                  wsp(1, D), mspec(3), mspec(4), wsp(D, N_EXPERTS), wsp(1, N_EXPERTS)],
        out_specs=[row(D), row(D), row(LANES)],
        out_shape=[jax.ShapeDtypeStruct((T, D), F32), jax.ShapeDtypeStruct((T, D), F32),
                   jax.ShapeDtypeStruct((T, LANES), F32)],
        compiler_params=_cparams(("arbitrary",)),
        name="merge_and_route",
    )(x2, yh, yr, gm, mod, whb, wrb, wo, g_ffn, mod, mod, w_router, b_router)


def _rank_kernel(rt_ref, tril_ref, o_ref, cnt_ref, carry_ref):
    @pl.when(pl.program_id(0) == 0)
    def _():
        carry_ref[...] = jnp.zeros_like(carry_ref)

    rt = rt_ref[...]
    tm = rt.shape[0]
    lane_e = lax.broadcasted_iota(jnp.int32, (tm, N_EXPERTS), 1).astype(F32)
    lane_o = lax.broadcasted_iota(jnp.int32, (tm, LANES), 1)
    hits = [lane_e == rt[:, r:r + 1] for r in range(TOP_K)]
    onehot = sum(h.astype(F32) for h in hits)
    before = jnp.dot(tril_ref[...], onehot.astype(BF16), preferred_element_type=F32) + carry_ref[...]
    out = jnp.zeros((tm, LANES), F32)
    for r in range(TOP_K):
        rank_r = jnp.sum(jnp.where(hits[r], before, 0.0), axis=-1, keepdims=True)
        out = jnp.where(lane_o == r, rank_r, out)
    o_ref[...] = out
    carry_ref[...] += jnp.sum(onehot, axis=0, keepdims=True)
    cnt_ref[...] = carry_ref[...]


def assignment_ranks(route):
    T = route.shape[0]
    tm = RANK_TILE
    tril = jnp.asarray(np.tril(np.ones((tm, tm)), -1), BF16)
    return pl.pallas_call(
        _rank_kernel,
        grid=(T // tm,),
        in_specs=[pl.BlockSpec((tm, LANES), lambda i: (i, 0)), _const_spec((tm, tm))],
        out_specs=[pl.BlockSpec((tm, LANES), lambda i: (i, 0)), _const_spec((1, N_EXPERTS))],
        out_shape=[jax.ShapeDtypeStruct((T, LANES), F32),
                   jax.ShapeDtypeStruct((1, N_EXPERTS), F32)],
        scratch_shapes=[pltpu.VMEM((1, N_EXPERTS), F32)],
        compiler_params=_cparams(("arbitrary",)),
        name="assignment_ranks",
    )(route, tril)


def _dispatch_kernel(dest_ref, pe_ref, h_ref, xs_hbm, zeros, sem, zsem):
    tm = DISPATCH_TILE
    t0 = pl.program_id(0) * tm

    @pl.when(pl.program_id(0) == 0)
    def _():
        zeros[...] = jnp.zeros_like(zeros)

        def zero_block(e):
            end = pe_ref[e]
            begin = pe_ref[jnp.maximum(e - 1, 0)]
            nonempty = jnp.logical_or(jnp.logical_and(e == 0, end > 0), end > begin)
            start = pl.multiple_of(jnp.maximum(end - EXPERT_TILE, 0), EXPERT_TILE)
            return nonempty, pltpu.make_async_copy(zeros, xs_hbm.at[pl.ds(start, EXPERT_TILE)], zsem)

        def start_zero(e, carry):
            nonempty, cp = zero_block(e)

            @pl.when(nonempty)
            def _():
                cp.start()
            return carry
        lax.fori_loop(0, N_EXPERTS, start_zero, 0)

        def wait_zero(e, carry):
            nonempty, cp = zero_block(e)

            @pl.when(nonempty)
            def _():
                cp.wait()
            return carry
        lax.fori_loop(0, N_EXPERTS, wait_zero, 0)

    def row_copy(i, d):
        return pltpu.make_async_copy(h_ref.at[pl.ds(i, 1)], xs_hbm.at[pl.ds(d, 1)], sem)

    def issue(i, carry):
        for r in range(TOP_K):
            row_copy(i, dest_ref[(t0 + i) * TOP_K + r]).start()
        return carry
    lax.fori_loop(0, tm, issue, 0, unroll=4)

    def drain(i, carry):
        row_copy(0, 0).wait()
        return carry
    lax.fori_loop(0, tm * TOP_K, drain, 0, unroll=8)


def dispatch(dest, pad_ends, h, n_rows):
    T, D = h.shape
    tm = DISPATCH_TILE
    grid_spec = pltpu.PrefetchScalarGridSpec(
        num_scalar_prefetch=2,
        grid=(T // tm,),
        in_specs=[pl.BlockSpec((tm, D), lambda i, d, pe: (i, 0))],
        out_specs=pl.BlockSpec(memory_space=pl.ANY),
        scratch_shapes=[pltpu.VMEM((EXPERT_TILE, D), h.dtype), pltpu.SemaphoreType.DMA,
                        pltpu.SemaphoreType.DMA],
    )
    return pl.pallas_call(
        _dispatch_kernel,
        grid_spec=grid_spec,
        out_shape=jax.ShapeDtypeStruct((n_rows, D), h.dtype),
        compiler_params=_cparams(("arbitrary",)),
        name="dispatch",
    )(dest, pad_ends, h)


def _expert_kernel(be_ref, nb_ref, x_ref, wgu_ref, bgu_ref, wd_ref, bd_ref, perm_ref, o_ref,
                   wgu_s, wd_s):
    i = pl.program_id(0)
    prev = be_ref[jnp.maximum(i - 1, 0)]
    new_expert = jnp.logical_or(i == 0, be_ref[i] != prev)
    n_chunk = 2 * D_FF // (2 * LANES)

    @pl.when(new_expert)
    def _():
        for c in range(n_chunk):
            cols = slice(c * 2 * LANES, (c + 1) * 2 * LANES)
            wgu_s[:, cols] = jnp.dot(wgu_ref[:, cols].astype(BF16), perm_ref[...],
                                     preferred_element_type=F32).astype(BF16)
        wd_s[...] = wd_ref[...].astype(BF16)

    @pl.when(i < nb_ref[0])
    def _():
        gu = jnp.dot(x_ref[...].astype(BF16), wgu_s[...], preferred_element_type=F32) + bgu_ref[...]
        acts = []
        for c in range(n_chunk):
            gate = jnp.minimum(gu[:, c * 2 * LANES:c * 2 * LANES + LANES], SWIGLU_LIMIT)
            up = jnp.clip(gu[:, c * 2 * LANES + LANES:(c + 1) * 2 * LANES], -SWIGLU_LIMIT, SWIGLU_LIMIT)
            acts.append((up + 1.0) * (gate * jax.nn.sigmoid(SWIGLU_ALPHA * gate)))
        act = jnp.concatenate(acts, axis=1).astype(BF16)
        o_ref[...] = jnp.dot(act, wd_s[...], preferred_element_type=F32) + bd_ref[...]

    @pl.when(i >= nb_ref[0])
    def _():
        o_ref[...] = jnp.zeros_like(o_ref)


def expert_ffn(block_e, n_used, xs, w_gate_up, bgu, w_down, bd, layer):
    P, D = xs.shape
    tm = EXPERT_TILE
    perm = np.zeros((2 * LANES, 2 * LANES))
    j = np.arange(LANES)
    perm[2 * j, j] = 1.0
    perm[2 * j + 1, LANES + j] = 1.0
    grid_spec = pltpu.PrefetchScalarGridSpec(
        num_scalar_prefetch=2,
        grid=(P // tm,),
        in_specs=[
            pl.BlockSpec((tm, D), lambda i, be, nb: (i, 0)),
            pl.BlockSpec((None, None, D, 2 * D_FF), lambda i, be, nb: (layer, be[i], 0, 0)),
            pl.BlockSpec((None, None, 1, 2 * D_FF), lambda i, be, nb: (layer, be[i], 0, 0)),
            pl.BlockSpec((None, None, D_FF, D), lambda i, be, nb: (layer, be[i], 0, 0)),
            pl.BlockSpec((None, None, 1, D), lambda i, be, nb: (layer, be[i], 0, 0)),
            pl.BlockSpec((2 * LANES, 2 * LANES), lambda i, be, nb: (0, 0)),
        ],
        out_specs=pl.BlockSpec((tm, D), lambda i, be, nb: (i, 0)),
        scratch_shapes=[pltpu.VMEM((D, 2 * D_FF), BF16), pltpu.VMEM((D_FF, D), BF16)],
    )
    return pl.pallas_call(
        _expert_kernel,
        grid_spec=grid_spec,
        out_shape=jax.ShapeDtypeStruct((P, D), F32),
        compiler_params=_cparams(("arbitrary",)),
        name="expert_ffn",
    )(block_e, n_used, xs, w_gate_up, bgu, w_down, bd, jnp.asarray(perm, BF16))


def _combine_kernel(dest_ref, x_ref, rt_ref, ga_ref, g_ref, yb_hbm, o_ref, buf, sem, *, final):
    tm = COMBINE_TILE
    t0 = pl.program_id(0) * tm

    def issue(i, carry):
        t = t0 + i
        for r in range(TOP_K):
            d = dest_ref[t * TOP_K + r]
            pltpu.make_async_copy(yb_hbm.at[pl.ds(d, 1)], buf.at[r, pl.ds(i, 1)], sem).start()
        return carry
    lax.fori_loop(0, tm, issue, 0, unroll=4)

    def drain(i, carry):
        pltpu.make_async_copy(yb_hbm.at[pl.ds(0, 1)], buf.at[0, pl.ds(0, 1)], sem).wait()
        return carry
    lax.fori_loop(0, tm * TOP_K, drain, 0, unroll=8)

    rt = rt_ref[...]
    y = buf[0] * rt[:, TOP_K:TOP_K + 1]
    for r in range(1, TOP_K):
        y = y + buf[r] * rt[:, TOP_K + r:TOP_K + r + 1]
    x = x_ref[...] + ga_ref[...] * y
    if final:
        x = (x * lax.rsqrt(jnp.mean(x * x, axis=-1, keepdims=True) + EPS)) * g_ref[...]
    o_ref[...] = x


def combine(dest, x2, route, mod, g_final, yb, layer, final):
    T, D = x2.shape
    tm = COMBINE_TILE
    per_b = SEQ // tm
    grid_spec = pltpu.PrefetchScalarGridSpec(
        num_scalar_prefetch=1,
        grid=(T // tm,),
        in_specs=[
            pl.BlockSpec((tm, D), lambda i, d: (i, 0)),
            pl.BlockSpec((tm, LANES), lambda i, d: (i, 0)),
            pl.BlockSpec((None, None, None, 1, D), lambda i, d: (layer, 5, i // per_b, 0, 0)),
            pl.BlockSpec((1, D), lambda i, d: (0, 0)),
            pl.BlockSpec(memory_space=pl.ANY),
        ],
        out_specs=pl.BlockSpec((tm, D), lambda i, d: (i, 0)),
        scratch_shapes=[pltpu.VMEM((TOP_K, tm, D), F32), pltpu.SemaphoreType.DMA],
    )
    return pl.pallas_call(
        functools.partial(_combine_kernel, final=final),
        grid_spec=grid_spec,
        out_shape=jax.ShapeDtypeStruct((T, D), F32),
        compiler_params=_cparams(("arbitrary",)),
        name="combine",
    )(dest, x2, route, mod, g_final, yb)


def moe_ffn_residual(x2, h, route, mod, g_final, w_gate_up, bgu, w_down, bd, layer, final):
    T, D = h.shape
    tm = EXPERT_TILE
    rank, counts = assignment_ranks(route)
    counts = counts[0].astype(jnp.int32)
    padded = ((counts + tm - 1) // tm) * tm
    pad_ends = jnp.cumsum(padded)
    pad_starts = pad_ends - padded
    n_blocks = -(-(T * TOP_K + N_EXPERTS * (tm - 1)) // tm)
    block_e = jnp.minimum(
        jnp.searchsorted(pad_ends, jnp.arange(n_blocks) * tm, side='right'), N_EXPERTS - 1
    ).astype(jnp.int32)
    n_used = (pad_ends[-1:] // tm).astype(jnp.int32)
    top_idx = route[:, :TOP_K].astype(jnp.int32)
    dest = (pad_starts[top_idx] + rank[:, :TOP_K].astype(jnp.int32)).reshape(-1)
    xs = dispatch(dest, pad_ends.astype(jnp.int32), h, n_blocks * tm)
    yb = expert_ffn(block_e, n_used, xs, w_gate_up, bgu, w_down, bd, layer)
    return combine(dest, x2, route, mod, g_final, yb, layer, final)


def kernel(x, c, norm_mix_g, norm_ffn_g, w_mod, b_mod, w_in, hy_conv_w, hy_conv_b, hy_w1, hy_b1, hy_w2, hy_b2, hy_w3, hy_b3, hy_w4, hy_freq, hy_bias, w_hy_br, w_ret_br, w_out, w_router, b_router, w_gate_up, b_gate_up, w_down, b_down, final_g):
    B, S, D = x.shape
    L = w_mod.shape[0]
    T = B * S
    tabs = _tables()

    w_in_bf = w_in.astype(BF16)
    whb = w_hy_br.astype(BF16)
    wrb = w_ret_br.astype(BF16)
    wo = w_out.astype(BF16)
    E, F = N_EXPERTS, D_FF
    bgu = b_gate_up.reshape(L, E, F // LANES, LANES, 2).transpose(0, 1, 2, 4, 3).reshape(L, E, 1, 2 * F)
    bdn = b_down.reshape(L, E, 1, D)
    w1p = jnp.pad(hy_w1, ((0, 0), (0, HY_EMB_PAD - HY_EMB), (0, 0)))
    vec = lambda a: a.reshape(L, 1, -1)

    c_pad = jnp.pad(c, ((0, 8 - B), (0, 0)))
    mod = modulation(c_pad, w_mod, b_mod)[:, :B]
    mod = mod.reshape(L, B, N_MOD, 1, D).transpose(0, 2, 1, 3, 4)

    x2 = x.reshape(T, D)
    for l in range(L):
        u, qk, v, gr, gm = in_projection(x2, vec(norm_mix_g), mod, w_in_bf, l)
        xs = short_conv(u.reshape(B, S, -1), hy_conv_w, vec(hy_conv_b), l)
        kraw, l1 = hyena_filter(tabs, w1p, vec(hy_b1), hy_w2, vec(hy_b2), hy_w3, vec(hy_b3),
                                hy_w4, vec(hy_freq), l)
        ka = dft_first_axis(kraw.reshape(1, FFT_N1, -1), tabs["f1"])
        kspec = filter_spectrum(ka.reshape(1, 2, FFT_N1, FFT_N2, -1), l1, tabs)
        y_hy = hyena_mixer(xs, kspec, hy_bias[l], tabs)
        qk3, v3, gr3 = qk.reshape(B, S, -1), v.reshape(B, S, -1), gr.reshape(B, S, -1)
        states = retention_states(qk3, v3, tabs)
        y_ret = retention_out(qk3, v3, gr3, states, tabs)
        x2, h_ffn, route = merge_and_route(
            x2, y_hy.reshape(T, -1), y_ret.reshape(T, -1), gm, mod, vec(norm_ffn_g),
            whb, wrb, wo, w_router, vec(b_router), l)
        x2 = moe_ffn_residual(x2, h_ffn, route, mod, final_g.reshape(1, D), w_gate_up, bgu,
                              w_down, bdn, l, final=(l == L - 1))
    return x2.reshape(B, S, D)
```

```python
import functools
import math

import numpy as np
import jax
import jax.numpy as jnp
from jax import lax
from jax.experimental import pallas as pl
from jax.experimental.pallas import tpu as pltpu

F32 = jnp.float32
BF16 = jnp.bfloat16
HIGHEST = lax.Precision.HIGHEST

D_MODEL = 1024
BATCH = 4
SEQ = 8192
DEPTH = 4
HY_WIDTH = 512
HY_ORDER = 2
HY_EMB = 33
HY_EMB_PAD = 64
HY_FILT_HIDDEN = 64
HY_FAST_DECAY = 0.3
HY_SLOW_DECAY = 1.5
HY_DECAY_TARGET = 1e-2
RET_HEADS = 4
RET_DK = 64
RET_DV = 128
RET_QK = RET_HEADS * RET_DK
RET_V = RET_HEADS * RET_DV
RET_CHUNK = 128
RET_DECAY_FWD = 5.0
RET_DECAY_BWD = 5.5
ROPE_BASE = 10000.0
N_EXPERTS = 32
TOP_K = 4
D_FF = D_MODEL
SWIGLU_ALPHA = 1.702
SWIGLU_LIMIT = 7.0
N_MOD = 6
EPS = 1e-6
IN_COLS = 3 * HY_WIDTH + 2 * RET_QK + 2 * RET_V + 2 * D_MODEL

LANES = 128
HY_SLABS = HY_WIDTH // LANES
ROW_TILES = D_MODEL // LANES
VMEM_LIMIT = 56 * 1024 * 1024

FFT_N = 2 * SEQ
FFT_N1 = 128
FFT_N2 = FFT_N // FFT_N1
FFT_H1 = FFT_N1 // 2

TOK_TILE = 256
EXPERT_TILE = 512
RANK_TILE = 512
DISPATCH_TILE = 512
COMBINE_TILE = 256
RET_GROUP = 8
FFT_N2_BLOCK = 8
FFT_K1_GROUP = 4


def _cparams(sem):
    return pltpu.CompilerParams(dimension_semantics=sem, vmem_limit_bytes=VMEM_LIMIT)


def _const_spec(shape):
    return pl.BlockSpec(shape, lambda *_: (0,) * len(shape))


def _tables():
    n1 = np.arange(FFT_N1)
    n2 = np.arange(FFT_N2)
    ang1 = 2.0 * np.pi * np.outer(n1, n1) / FFT_N1
    f1 = np.concatenate([np.cos(ang1), -np.sin(ang1)], axis=0)
    ang2 = 2.0 * np.pi * np.outer(n2, n2) / FFT_N2
    angt = 2.0 * np.pi * np.outer(n1, n2) / FFT_N
    g = np.concatenate([np.cos(ang1), -np.sin(ang1)], axis=1)[:FFT_H1] / FFT_N

    L = SEQ
    t = np.linspace(0.0, 1.0, L)
    bands = (HY_EMB - 1) // 2
    w = 2.0 * np.pi * np.arange(L) / L
    f = np.linspace(1e-4, bands - 1, bands)
    feats = np.concatenate([t[:, None], np.cos(f[None] * w[:, None]), -np.sin(f[None] * w[:, None])], -1)
    idx = np.concatenate([np.arange(L), [0], L - np.arange(1, L)])
    feats2 = np.zeros((2 * L, HY_EMB_PAD))
    feats2[:, :HY_EMB] = feats[idx]
    max_decay = math.log(HY_DECAY_TARGET) / HY_FAST_DECAY
    min_decay = math.log(HY_DECAY_TARGET) / HY_SLOW_DECAY
    deltas = np.abs(np.linspace(min_decay, max_decay, HY_WIDTH))

    C = RET_CHUNK
    hh = np.arange(RET_HEADS)
    lgf = np.log(1.0 - np.exp2(-(RET_DECAY_FWD + hh)))
    lgb = np.log(1.0 - np.exp2(-(RET_DECAY_BWD + hh)))
    pos = np.arange(C)
    diff = pos[:, None] - pos[None, :]
    dec = np.where(diff[None] >= 0, np.exp(np.maximum(diff, 0)[None] * lgf[:, None, None]),
                   np.exp(np.maximum(-diff, 0)[None] * lgb[:, None, None]))
    lane_h = np.repeat(hh, RET_DK)
    wq = np.stack([np.exp((pos[:, None] + 1.0) * lgf[lane_h][None]),
                   np.exp((C - pos[:, None]) * lgb[lane_h][None])])
    wk = np.stack([np.exp((C - 1.0 - pos[:, None]) * lgf[lane_h][None]),
                   np.exp(pos[:, None] * lgb[lane_h][None])]) * (RET_DK ** -0.5)
    cd = np.stack([np.broadcast_to(np.exp(C * lgf[lane_h])[:, None], (RET_QK, RET_V)),
                   np.broadcast_to(np.exp(C * lgb[lane_h])[:, None], (RET_QK, RET_V))])
    col_h = np.repeat(hh, RET_DV)
    bd = (lane_h[:, None] == col_h[None, :]).astype(np.float64)
    inv_freq = 1.0 / (ROPE_BASE ** (np.arange(0, RET_DK, 2) / RET_DK))
    ang = np.arange(SEQ)[:, None] * inv_freq[None, :]
    cc = np.tile(np.concatenate([np.cos(ang), np.cos(ang)], -1), (1, RET_HEADS))
    ss = np.tile(np.concatenate([-np.sin(ang), np.sin(ang)], -1), (1, RET_HEADS))

    return dict(
        f1=jnp.asarray(f1, BF16), f1h=jnp.asarray(f1[:, :FFT_H1], BF16),
        f2r=jnp.asarray(np.cos(ang2), F32), f2i=jnp.asarray(-np.sin(ang2), F32),
        twr=jnp.asarray(np.cos(angt), F32), twi=jnp.asarray(-np.sin(angt), F32),
        g=jnp.asarray(g, BF16),
        feats2=jnp.asarray(feats2, F32), deltas=jnp.asarray(deltas[None], F32),
        dec=jnp.asarray(dec, F32), wq=jnp.asarray(wq, F32), wk=jnp.asarray(wk, F32),
        cd=jnp.asarray(cd, F32), bd=jnp.asarray(bd, F32),
        cc=jnp.asarray(cc, F32), ss=jnp.asarray(ss, F32),
    )


def _mod_kernel(c_ref, w_ref, b_ref, o_ref):
    c = c_ref[...]
    ca = c * jax.nn.sigmoid(c)
    o_ref[...] = jnp.dot(ca, w_ref[...], preferred_element_type=F32, precision=HIGHEST) + b_ref[...]


def modulation(c_pad, w_mod, b_mod):
    L, D, N = w_mod.shape
    tn = 1536
    rows = c_pad.shape[0]
    return pl.pallas_call(
        _mod_kernel,
        grid=(L, N // tn),
        in_specs=[
            _const_spec((rows, D)),
            pl.BlockSpec((None, D, tn), lambda l, j: (l, 0, j)),
            pl.BlockSpec((None, 1, tn), lambda l, j: (l, 0, j)),
        ],
        out_specs=pl.BlockSpec((None, rows, tn), lambda l, j: (l, 0, j)),
        out_shape=jax.ShapeDtypeStruct((L, rows, N), F32),
        compiler_params=_cparams(("arbitrary", "arbitrary")),
        name="modulation",
    )(c_pad, w_mod, b_mod.reshape(L, 1, N))


def _norm_mod(x, g, sc, sh):
    y = x * lax.rsqrt(jnp.mean(x * x, axis=-1, keepdims=True) + EPS)
    return (y * g) * (1.0 + sc) + sh


def _inproj_kernel(x_ref, g_ref, sh_ref, sc_ref, w_ref, u_ref, qk_ref, v_ref, gr_ref, gm_ref):
    h = _norm_mod(x_ref[...], g_ref[...], sc_ref[...], sh_ref[...]).astype(BF16)
    c0 = 0
    for o_ref in (u_ref, qk_ref, v_ref, gr_ref, gm_ref):
        c1 = c0 + o_ref.shape[-1]
        o_ref[...] = jnp.dot(h, w_ref[:, c0:c1], preferred_element_type=F32).astype(o_ref.dtype)
        c0 = c1


def in_projection(x2, g, mod, w_in_bf, layer):
    T, D = x2.shape
    tm = TOK_TILE
    per_b = SEQ // tm
    mspec = lambda piece: pl.BlockSpec((None, None, None, 1, D),
                                       lambda i: (layer, piece, i // per_b, 0, 0))
    widths = (3 * HY_WIDTH, 2 * RET_QK, RET_V, RET_V, 2 * D_MODEL)
    dtypes = (F32, F32, BF16, F32, F32)
    return pl.pallas_call(
        _inproj_kernel,
        grid=(T // tm,),
        in_specs=[
            pl.BlockSpec((tm, D), lambda i: (i, 0)),
            pl.BlockSpec((None, 1, D), lambda i: (layer, 0, 0)),
            mspec(0), mspec(1),
            pl.BlockSpec((None, D, IN_COLS), lambda i: (layer, 0, 0)),
        ],
        out_specs=[pl.BlockSpec((tm, w), lambda i: (i, 0)) for w in widths],
        out_shape=[jax.ShapeDtypeStruct((T, w), dt) for w, dt in zip(widths, dtypes)],
        compiler_params=_cparams(("arbitrary",)),
        name="in_projection",
    )(x2, g, mod, mod, w_in_bf)


def _shortconv_kernel(u_ref, w_ref, b_ref, o_ref):
    u = u_ref[...]
    s = u.shape[0]
    row = lax.broadcasted_iota(jnp.int32, u.shape, 0)
    prev = jnp.where(row == 0, 0.0, pltpu.roll(u, 1, axis=0))
    nxt = jnp.where(row == s - 1, 0.0, pltpu.roll(u, s - 1, axis=0))
    w = w_ref[...]
    o_ref[...] = prev * w[0:1] + u * w[1:2] + nxt * w[2:3] + b_ref[...]


def short_conv(u, conv_w, conv_b, layer):
    B, S, C3 = u.shape
    cb = LANES
    per = HY_WIDTH // cb
    return pl.pallas_call(
        _shortconv_kernel,
        grid=(B, C3 // cb),
        in_specs=[
            pl.BlockSpec((None, S, cb), lambda b, j: (b, 0, j)),
            pl.BlockSpec((None, 3, cb), lambda b, j: (layer, 0, j)),
            pl.BlockSpec((None, 1, cb), lambda b, j: (layer, 0, j)),
        ],
        out_specs=pl.BlockSpec((None, None, None, S, cb), lambda b, j: (j // per, b, j % per, 0, 0)),
        out_shape=jax.ShapeDtypeStruct((3, B, per, S, cb), F32),
        compiler_params=_cparams(("arbitrary", "arbitrary")),
        name="short_conv",
    )(u, conv_w, conv_b)


def _filter_kernel(p_ref, w1_ref, b1_ref, w2_ref, b2_ref, w3_ref, b3_ref, w4_ref, fr_ref, dl_ref,
                   k_ref, l1_ref, *, rows):
    i = pl.program_id(0)
    p = p_ref[...]
    fr = fr_ref[...]
    dot = functools.partial(jnp.dot, preferred_element_type=F32, precision=HIGHEST)
    h = jnp.sin(fr * (dot(p, w1_ref[...]) + b1_ref[...]))
    h = jnp.sin(fr * (dot(h, w2_ref[...]) + b2_ref[...]))
    h = jnp.sin(fr * (dot(h, w3_ref[...]) + b3_ref[...]))
    h = dot(h, w4_ref[...])
    t = p[:, 0:1]
    win = jnp.exp(-t * dl_ref[...])
    win = jnp.concatenate([win] * HY_ORDER, axis=1)
    grow = i * rows + lax.broadcasted_iota(jnp.int32, (rows, 1), 0)
    k = jnp.where(grow == SEQ, 0.0, h * win)
    for s in range(k_ref.shape[0]):
        k_ref[s] = k[:, s * LANES:(s + 1) * LANES]

    @pl.when(i == 0)
    def _():
        l1_ref[...] = jnp.zeros_like(l1_ref)
    l1_ref[...] += jnp.sum(jnp.abs(k), axis=0, keepdims=True)


def hyena_filter(tabs, w1p, b1, w2, b2, w3, b3, w4, freq, layer):
    rows = 1024
    n = 2 * SEQ
    half = SEQ // rows
    H = HY_FILT_HIDDEN
    OC = HY_ORDER * HY_WIDTH
    vec = lambda: pl.BlockSpec((None, 1, H), lambda i: (layer, 0, 0))
    return pl.pallas_call(
        functools.partial(_filter_kernel, rows=rows),
        grid=(n // rows,),
        in_specs=[
            pl.BlockSpec((rows, HY_EMB_PAD), lambda i: (i, 0)),
            pl.BlockSpec((None, HY_EMB_PAD, H), lambda i: (layer, 0, 0)), vec(),
            pl.BlockSpec((None, H, H), lambda i: (layer, 0, 0)), vec(),
            pl.BlockSpec((None, H, H), lambda i: (layer, 0, 0)), vec(),
            pl.BlockSpec((None, H, OC), lambda i: (layer, 0, i // half)),
            vec(),
            _const_spec((1, HY_WIDTH)),
        ],
        out_specs=[pl.BlockSpec((OC // LANES, rows, LANES), lambda i: (0, i, 0)), _const_spec((1, OC))],
        out_shape=[jax.ShapeDtypeStruct((OC // LANES, n, LANES), F32),
                   jax.ShapeDtypeStruct((1, OC), F32)],
        compiler_params=_cparams(("arbitrary",)),
        name="hyena_filter",
    )(tabs["feats2"], w1p, b1, w2, b2, w3, b3, w4, freq, tabs["deltas"])


def _load_sub(ref, lead, j):
    *outer, r, s, l = ref.shape
    flat = ref.reshape(*outer, r * s, l)
    return flat[(*lead, pl.ds(j, r, stride=s), slice(None))]


def _store_sub(ref, lead, j, val):
    *outer, r, s, l = ref.shape
    flat = ref.reshape(*outer, r * s, l)
    flat[(*lead, pl.ds(j, r, stride=s), slice(None))] = val


def _pack_pair(re, im):
    r = lax.bitcast_convert_type(re.astype(BF16).astype(F32), jnp.uint32)
    i = lax.bitcast_convert_type(im.astype(BF16).astype(F32), jnp.uint32)
    return r | (i >> 16)


def _unpack_pair(p):
    re = lax.bitcast_convert_type(p & jnp.uint32(0xFFFF0000), F32)
    im = lax.bitcast_convert_type(p << 16, F32)
    return re, im


def _dft1_kernel(x_ref, f_ref, o_ref):
    for j in range(FFT_N2_BLOCK):
        xs = jnp.concatenate([_load_sub(x_ref, (c,), j) for c in range(HY_SLABS)], axis=1)
        a = jnp.dot(f_ref[...], xs.astype(BF16), preferred_element_type=F32)
        p = _pack_pair(a[:FFT_N1], a[FFT_N1:])
        for c in range(HY_SLABS):
            _store_sub(o_ref, (c,), j, p[:, c * LANES:(c + 1) * LANES])


def dft_first_axis(x6, which, f1):
    _, B, _, rows, _, _ = x6.shape
    nb = FFT_N2_BLOCK
    return pl.pallas_call(
        _dft1_kernel,
        grid=(B, FFT_N2 // nb),
        in_specs=[pl.BlockSpec((None, None, HY_SLABS, rows, nb, LANES),
                               lambda b, j: (which, b, 0, 0, j, 0)),
                  _const_spec((2 * FFT_N1, rows))],
        out_specs=pl.BlockSpec((None, HY_SLABS, FFT_N1, nb, LANES), lambda b, j: (b, 0, 0, j, 0)),
        out_shape=jax.ShapeDtypeStruct((B, HY_SLABS, FFT_N1, FFT_N2, LANES), jnp.uint32),
        compiler_params=_cparams(("arbitrary", "arbitrary")),
        name="dft_first_axis",
    )(x6, f1)


def _twiddled_stack(f2r_ref, f2i_ref, twr_ref, twi_ref, k1):
    tr = twr_ref[pl.ds(k1, 1), :]
    ti = twi_ref[pl.ds(k1, 1), :]
    fr = f2r_ref[...]
    fi = f2i_ref[...]
    p = fr * tr - fi * ti
    q = fr * ti + fi * tr
    top = jnp.concatenate([p, -q], axis=1)
    bot = jnp.concatenate([q, p], axis=1)
    return jnp.concatenate([top, bot], axis=0).astype(BF16)


def _load_k1(a_ref, j):
    p = jnp.concatenate([a_ref[c, j] for c in range(HY_SLABS)], axis=1)
    re, im = _unpack_pair(p)
    return jnp.concatenate([re, im], axis=0).astype(BF16)


def _spectrum_kernel(a_ref, l1_ref, f2r_ref, f2i_ref, twr_ref, twi_ref, o_ref):
    base = pl.program_id(0) * FFT_K1_GROUP
    inv = 1.0 / l1_ref[...]
    for j in range(FFT_K1_GROUP):
        r = _twiddled_stack(f2r_ref, f2i_ref, twr_ref, twi_ref, base + j)
        x = jnp.dot(r, _load_k1(a_ref, j), preferred_element_type=F32) * inv
        o_ref[j, 0] = x[:FFT_N2]
        o_ref[j, 1] = x[FFT_N2:]


def filter_spectrum(a5, l1, tabs):
    g = FFT_K1_GROUP
    sq = lambda: _const_spec((FFT_N2, FFT_N2))
    return pl.pallas_call(
        _spectrum_kernel,
        grid=(FFT_N1 // g, HY_ORDER),
        in_specs=[pl.BlockSpec((None, HY_SLABS, g, FFT_N2, LANES), lambda i, o: (o, 0, i, 0, 0)),
                  pl.BlockSpec((1, HY_WIDTH), lambda i, o: (0, o)), sq(), sq(), sq(), sq()],
        out_specs=pl.BlockSpec((g, 2, FFT_N2, HY_WIDTH), lambda i, o: (i, 0, 0, o)),
        out_shape=jax.ShapeDtypeStruct((FFT_N1, 2, FFT_N2, HY_ORDER * HY_WIDTH), F32),
        compiler_params=_cparams(("arbitrary", "arbitrary")),
        name="filter_spectrum",
    )(a5, l1, tabs["f2r"], tabs["f2i"], tabs["twr"], tabs["twi"])


def _convmid_kernel(a_ref, ks_ref, f2r_ref, f2i_ref, twr_ref, twi_ref, o_ref):
    base = pl.program_id(0) * FFT_K1_GROUP
    for j in range(FFT_K1_GROUP):
        r = _twiddled_stack(f2r_ref, f2i_ref, twr_ref, twi_ref, base + j)
        x = jnp.dot(r, _load_k1(a_ref, j), preferred_element_type=F32)
        xr, xi = x[:FFT_N2], x[FFT_N2:]
        kr, ki = ks_ref[j, 0], ks_ref[j, 1]
        y = jnp.concatenate([xr * kr - xi * ki, xr * ki + xi * kr], axis=0).astype(BF16)
        b = lax.dot_general(r, y, (((0,), (0,)), ((), ())), preferred_element_type=F32)
        p = _pack_pair(b[:FFT_N2], b[FFT_N2:])
        for c in range(HY_SLABS):
            o_ref[c, j] = p[:, c * LANES:(c + 1) * LANES]


def conv_mid(a5, kspec, order, tabs):
    B = a5.shape[0]
    g = FFT_K1_GROUP
    sq = lambda: _const_spec((FFT_N2, FFT_N2))
    blk = lambda: pl.BlockSpec((None, HY_SLABS, g, FFT_N2, LANES), lambda i, b: (b, 0, i, 0, 0))
    return pl.pallas_call(
        _convmid_kernel,
        grid=(FFT_N1 // g, B),
        in_specs=[blk(), pl.BlockSpec((g, 2, FFT_N2, HY_WIDTH), lambda i, b: (i, 0, 0, order)),
                  sq(), sq(), sq(), sq()],
        out_specs=blk(),
        out_shape=jax.ShapeDtypeStruct(a5.shape, jnp.uint32),
        compiler_params=_cparams(("arbitrary", "arbitrary")),
        name="conv_mid",
    )(a5, kspec, tabs["f2r"], tabs["f2i"], tabs["twr"], tabs["twi"])


def _convout_kernel(b_ref, g_ref, z_ref, gate_ref, bias_ref, o_ref):
    for j in range(FFT_N2_BLOCK):
        p = jnp.concatenate([_load_sub(b_ref, (c,), j) for c in range(HY_SLABS)], axis=1)
        re, im = _unpack_pair(p)
        bp = jnp.concatenate([re, im], axis=0).astype(BF16)
        y = jnp.dot(g_ref[...], bp, preferred_element_type=F32)
        for c in range(HY_SLABS):
            _store_sub(o_ref, (c,), j, y[:, c * LANES:(c + 1) * LANES])
    o_ref[...] = gate_ref[...] * (o_ref[...] + z_ref[...] * bias_ref[...])


def conv_out(bp5, z6, z_which, gate6, gate_which, bias4, g):
    B = bp5.shape[0]
    nb = FFT_N2_BLOCK
    nat = lambda which: pl.BlockSpec((None, None, HY_SLABS, FFT_H1, nb, LANES),
                                     lambda b, j: (which, b, 0, 0, j, 0))
    return pl.pallas_call(
        _convout_kernel,
        grid=(B, FFT_N2 // nb),
        in_specs=[pl.BlockSpec((None, HY_SLABS, FFT_N1, nb, LANES), lambda b, j: (b, 0, 0, j, 0)),
                  _const_spec((FFT_H1, 2 * FFT_N1)), nat(z_which), nat(gate_which),
                  _const_spec((HY_SLABS, 1, 1, LANES))],
        out_specs=pl.BlockSpec((None, HY_SLABS, FFT_H1, nb, LANES), lambda b, j: (b, 0, 0, j, 0)),
        out_shape=jax.ShapeDtypeStruct((B, HY_SLABS, FFT_H1, FFT_N2, LANES), F32),
        compiler_params=_cparams(("arbitrary", "arbitrary")),
        name="conv_out",
    )(bp5, g, z6, gate6, bias4)


def hyena_mixer(xs, kspec, bias, tabs):
    B = xs.shape[1]
    xs6 = xs.reshape(3, B, HY_SLABS, FFT_H1, FFT_N2, LANES)
    z6, z_which = xs6, 2
    for o in range(HY_ORDER):
        a = dft_first_axis(z6, z_which, tabs["f1h"])
        bp = conv_mid(a, kspec, o, tabs)
        z = conv_out(bp, z6, z_which, xs6, o, bias[o].reshape(HY_SLABS, 1, 1, LANES), tabs["g"])
        z6, z_which = z[None], 0
    return z.reshape(B, HY_SLABS, SEQ, LANES)


def _rope(x, cc, ss):
    lane = lax.broadcasted_iota(jnp.int32, x.shape, 1)
    n = x.shape[1]
    half = RET_DK // 2
    swapped = jnp.where(lane % RET_DK < half, pltpu.roll(x, n - half, axis=1),
                        pltpu.roll(x, half, axis=1))
    return x * cc + swapped * ss


def _retstate_kernel(k_ref, v_ref, cc_ref, ss_ref, wk_ref, cd_ref, o_ref, s_ref):
    d = pl.program_id(1)
    g = pl.program_id(2)
    C = RET_CHUNK

    @pl.when(g == 0)
    def _():
        s_ref[...] = jnp.zeros_like(s_ref)

    for j in range(RET_GROUP):
        ci = jnp.where(d == 0, j, RET_GROUP - 1 - j)
        r0 = pl.multiple_of(ci * C, C)
        s = s_ref[...]
        o_ref[ci] = jnp.concatenate(
            [s[h * RET_DK:(h + 1) * RET_DK, h * RET_DV:(h + 1) * RET_DV] for h in range(RET_HEADS)],
            axis=1)
        k = _rope(k_ref[pl.ds(r0, C), :], cc_ref[pl.ds(r0, C), :], ss_ref[pl.ds(r0, C), :])
        kw = (k * wk_ref[...]).astype(BF16)
        ds = lax.dot_general(kw, v_ref[pl.ds(r0, C), :], (((0,), (0,)), ((), ())),
                             preferred_element_type=F32)
        s_ref[...] = s * cd_ref[...] + ds


def retention_states(qk, v, tabs):
    B, S, _ = qk.shape
    rows = RET_GROUP * RET_CHUNK
    G = S // rows
    grp = lambda d, g: jnp.where(d == 0, g, G - 1 - g)
    return pl.pallas_call(
        _retstate_kernel,
        grid=(B, 2, G),
        in_specs=[
            pl.BlockSpec((None, rows, RET_QK), lambda b, d, g: (b, grp(d, g), 1)),
            pl.BlockSpec((None, rows, RET_V), lambda b, d, g: (b, grp(d, g), 0)),
            pl.BlockSpec((rows, RET_QK), lambda b, d, g: (grp(d, g), 0)),
            pl.BlockSpec((rows, RET_QK), lambda b, d, g: (grp(d, g), 0)),
            pl.BlockSpec((None, RET_CHUNK, RET_QK), lambda b, d, g: (d, 0, 0)),
            pl.BlockSpec((None, RET_QK, RET_V), lambda b, d, g: (d, 0, 0)),
        ],
        out_specs=pl.BlockSpec((None, None, RET_GROUP, RET_DK, RET_V),
                               lambda b, d, g: (b, d, grp(d, g), 0, 0)),
        out_shape=jax.ShapeDtypeStruct((B, 2, S // RET_CHUNK, RET_DK, RET_V), F32),
        scratch_shapes=[pltpu.VMEM((RET_QK, RET_V), F32)],
        compiler_params=_cparams(("arbitrary", "arbitrary", "arbitrary")),
        name="retention_states",
    )(qk, v, tabs["cc"], tabs["ss"], tabs["wk"], tabs["cd"])


def _retout_kernel(qk_ref, v_ref, gr_ref, cc_ref, ss_ref, st_ref, dec_ref, wq_ref, wk_ref, bd_ref,
                   o_ref):
    C = RET_CHUNK
    bd = bd_ref[...]
    lane = lax.broadcasted_iota(jnp.int32, (C, RET_QK), 1)
    for j in range(RET_GROUP):
        r0 = j * C
        cc = cc_ref[r0:r0 + C, :]
        ss = ss_ref[r0:r0 + C, :]
        q = _rope(qk_ref[r0:r0 + C, :RET_QK], cc, ss)
        k = (_rope(qk_ref[r0:r0 + C, RET_QK:], cc, ss) * (RET_DK ** -0.5)).astype(BF16)
        v = v_ref[r0:r0 + C, :]
        inner = []
        for h in range(RET_HEADS):
            qh = jnp.where(lane // RET_DK == h, q, 0.0).astype(BF16)
            s = lax.dot_general(qh, k, (((1,), (1,)), ((), ())), preferred_element_type=F32)
            s = (s * dec_ref[h]).astype(BF16)
            inner.append(jnp.dot(s, v[:, h * RET_DV:(h + 1) * RET_DV], preferred_element_type=F32))
        qq = jnp.concatenate([q * wq_ref[0], q * wq_ref[1]], axis=1).astype(BF16)
        sf = jnp.concatenate([st_ref[0, j]] * RET_HEADS, axis=0) * bd
        sb = jnp.concatenate([st_ref[1, j]] * RET_HEADS, axis=0) * bd
        sbd = jnp.concatenate([sf, sb], axis=0).astype(BF16)
        o = jnp.concatenate(inner, axis=1) + jnp.dot(qq, sbd, preferred_element_type=F32)
        outs = []
        for h in range(RET_HEADS):
            oh = o[:, h * RET_DV:(h + 1) * RET_DV]
            outs.append(oh * lax.rsqrt(jnp.mean(oh * oh, axis=-1, keepdims=True) + EPS))
        gr = gr_ref[r0:r0 + C, :]
        o_ref[r0:r0 + C, :] = (gr * jax.nn.sigmoid(gr)) * jnp.concatenate(outs, axis=1)


def retention_out(qk, v, gr, states, tabs):
    B, S, _ = qk.shape
    rows = RET_GROUP * RET_CHUNK
    G = S // rows
    return pl.pallas_call(
        _retout_kernel,
        grid=(B, G),
        in_specs=[
            pl.BlockSpec((None, rows, 2 * RET_QK), lambda b, g: (b, g, 0)),
            pl.BlockSpec((None, rows, RET_V), lambda b, g: (b, g, 0)),
            pl.BlockSpec((None, rows, RET_V), lambda b, g: (b, g, 0)),
            pl.BlockSpec((rows, RET_QK), lambda b, g: (g, 0)),
            pl.BlockSpec((rows, RET_QK), lambda b, g: (g, 0)),
            pl.BlockSpec((None, 2, RET_GROUP, RET_DK, RET_V), lambda b, g: (b, 0, g, 0, 0)),
            _const_spec((RET_HEADS, RET_CHUNK, RET_CHUNK)),
            _const_spec((2, RET_CHUNK, RET_QK)),
            _const_spec((2, RET_CHUNK, RET_QK)),
            _const_spec((RET_QK, RET_V)),
        ],
        out_specs=pl.BlockSpec((None, rows, RET_V), lambda b, g: (b, g, 0)),
        out_shape=jax.ShapeDtypeStruct((B, S, RET_V), F32),
        compiler_params=_cparams(("arbitrary", "arbitrary")),
        name="retention_out",
    )(qk, v, gr, tabs["cc"], tabs["ss"], states, tabs["dec"], tabs["wq"], tabs["wk"], tabs["bd"])


def _merge_kernel(x_ref, yh_ref, yr_ref, gm_ref, ga_ref, whb_ref, wrb_ref, wo_ref,
                  g_ref, sh_ref, sc_ref, wr_ref, br_ref, xo_ref, h_ref, rt_ref):
    gm = gm_ref[...]
    ghy = jax.nn.sigmoid(gm[:, :D_MODEL])
    grt = jax.nn.sigmoid(gm[:, D_MODEL:])
    yh = jnp.concatenate([yh_ref[c] for c in range(HY_SLABS)], axis=1)
    m = (ghy * jnp.dot(yh.astype(BF16), whb_ref[...], preferred_element_type=F32)
         + grt * jnp.dot(yr_ref[...].astype(BF16), wrb_ref[...], preferred_element_type=F32))
    out = jnp.dot(m.astype(BF16), wo_ref[...], preferred_element_type=F32)
    x = x_ref[...] + ga_ref[...] * out
    xo_ref[...] = x
    h = _norm_mod(x, g_ref[...], sc_ref[...], sh_ref[...])
    for j in range(ROW_TILES):
        _store_sub(h_ref, (), j, h[:, j * LANES:(j + 1) * LANES])
    logits = jnp.dot(h, wr_ref[...], preferred_element_type=F32, precision=HIGHEST) + br_ref[...]
    tm = logits.shape[0]
    lane_e = lax.broadcasted_iota(jnp.int32, logits.shape, 1).astype(F32)
    lane_o = lax.broadcasted_iota(jnp.int32, (tm, LANES), 1)
    route = jnp.zeros((tm, LANES), F32)
    vals = []
    work = logits
    for r in range(TOP_K):
        m_r = jnp.max(work, axis=-1, keepdims=True)
        i_r = jnp.min(jnp.where(work == m_r, lane_e, float(N_EXPERTS)), axis=-1, keepdims=True)
        work = jnp.where(lane_e == i_r, -jnp.inf, work)
        vals.append(m_r)
        route = jnp.where(lane_o == r, i_r, route)
    exps = [jnp.exp(v - vals[0]) for v in vals]
    den = exps[0] + exps[1] + exps[2] + exps[3]
    for r in range(TOP_K):
        route = jnp.where(lane_o == TOP_K + r, exps[r] / den, route)
    rt_ref[...] = route


def merge_and_route(x2, yh, yr, gm, mod, g_ffn, whb, wrb, wo, w_router, b_router, layer):
    T, D = x2.shape
    tm = TOK_TILE
    per_b = SEQ // tm
    mspec = lambda piece: pl.BlockSpec((None, None, None, 1, D),
                                       lambda i: (layer, piece, i // per_b, 0, 0))
    row = lambda w: pl.BlockSpec((tm, w), lambda i: (i, 0))
    wsp = lambda a, b: pl.BlockSpec((None, a, b), lambda i: (layer, 0, 0))
    return pl.pallas_call(
        _merge_kernel,
        grid=(T // tm,),
        in_specs=[row(D),
                  pl.BlockSpec((None, HY_SLABS, tm, LANES), lambda i: (i // per_b, 0, i % per_b, 0)),
                  row(RET_V), row(2 * D), mspec(2),
                  wsp(HY_WIDTH, D), wsp(RET_V, D), wsp(D, D),
                  wsp(1, D), mspec(3), mspec(4), wsp(D, N_EXPERTS), wsp(1, N_EXPERTS)],
        out_specs=[row(D), pl.BlockSpec((tm, ROW_TILES, LANES), lambda i: (i, 0, 0)), row(LANES)],
        out_shape=[jax.ShapeDtypeStruct((T, D), F32), jax.ShapeDtypeStruct((T, ROW_TILES, LANES), F32),
                   jax.ShapeDtypeStruct((T, LANES), F32)],
        compiler_params=_cparams(("arbitrary",)),
        name="merge_and_route",
    )(x2, yh, yr, gm, mod, whb, wrb, wo, g_ffn, mod, mod, w_router, b_router)


def _rank_kernel(rt_ref, tril_ref, o_ref, cnt_ref, carry_ref):
    @pl.when(pl.program_id(0) == 0)
    def _():
        carry_ref[...] = jnp.zeros_like(carry_ref)

    rt = rt_ref[...]
    tm = rt.shape[0]
    lane_e = lax.broadcasted_iota(jnp.int32, (tm, N_EXPERTS), 1).astype(F32)
    lane_o = lax.broadcasted_iota(jnp.int32, (tm, LANES), 1)
    hits = [lane_e == rt[:, r:r + 1] for r in range(TOP_K)]
    onehot = sum(h.astype(F32) for h in hits)
    before = jnp.dot(tril_ref[...], onehot.astype(BF16), preferred_element_type=F32) + carry_ref[...]
    out = jnp.zeros((tm, LANES), F32)
    for r in range(TOP_K):
        rank_r = jnp.sum(jnp.where(hits[r], before, 0.0), axis=-1, keepdims=True)
        out = jnp.where(lane_o == r, rank_r, out)
    o_ref[...] = out
    carry_ref[...] += jnp.sum(onehot, axis=0, keepdims=True)
    cnt_ref[...] = carry_ref[...]


def assignment_ranks(route):
    T = route.shape[0]
    tm = RANK_TILE
    tril = jnp.asarray(np.tril(np.ones((tm, tm)), -1), BF16)
    return pl.pallas_call(
        _rank_kernel,
        grid=(T // tm,),
        in_specs=[pl.BlockSpec((tm, LANES), lambda i: (i, 0)), _const_spec((tm, tm))],
        out_specs=[pl.BlockSpec((tm, LANES), lambda i: (i, 0)), _const_spec((1, N_EXPERTS))],
        out_shape=[jax.ShapeDtypeStruct((T, LANES), F32),
                   jax.ShapeDtypeStruct((1, N_EXPERTS), F32)],
        scratch_shapes=[pltpu.VMEM((1, N_EXPERTS), F32)],
        compiler_params=_cparams(("arbitrary",)),
        name="assignment_ranks",
    )(route, tril)


def _dispatch_kernel(dest_ref, pe_ref, h_ref, xs_hbm, zeros, sem, zsem):
    tm = DISPATCH_TILE
    t0 = pl.program_id(0) * tm

    @pl.when(pl.program_id(0) == 0)
    def _():
        zeros[...] = jnp.zeros_like(zeros)

        def zero_block(e):
            end = pe_ref[e]
            begin = pe_ref[jnp.maximum(e - 1, 0)]
            nonempty = jnp.logical_or(jnp.logical_and(e == 0, end > 0), end > begin)
            start = pl.multiple_of(jnp.maximum(end - EXPERT_TILE, 0), EXPERT_TILE)
            return nonempty, pltpu.make_async_copy(zeros, xs_hbm.at[pl.ds(start, EXPERT_TILE)], zsem)

        def start_zero(e, carry):
            nonempty, cp = zero_block(e)

            @pl.when(nonempty)
            def _():
                cp.start()
            return carry
        lax.fori_loop(0, N_EXPERTS, start_zero, 0)

        def wait_zero(e, carry):
            nonempty, cp = zero_block(e)

            @pl.when(nonempty)
            def _():
                cp.wait()
            return carry
        lax.fori_loop(0, N_EXPERTS, wait_zero, 0)

    def row_copy(i, d):
        return pltpu.make_async_copy(h_ref.at[i], xs_hbm.at[d], sem)

    def issue(i, carry):
        for r in range(TOP_K):
            row_copy(i, dest_ref[(t0 + i) * TOP_K + r]).start()
        return carry
    lax.fori_loop(0, tm, issue, 0, unroll=4)

    def drain(i, carry):
        row_copy(0, 0).wait()
        return carry
    lax.fori_loop(0, tm * TOP_K, drain, 0, unroll=8)


def dispatch(dest, pad_ends, h, n_rows):
    T = h.shape[0]
    tm = DISPATCH_TILE
    grid_spec = pltpu.PrefetchScalarGridSpec(
        num_scalar_prefetch=2,
        grid=(T // tm,),
        in_specs=[pl.BlockSpec((tm, ROW_TILES, LANES), lambda i, d, pe: (i, 0, 0))],
        out_specs=pl.BlockSpec(memory_space=pl.ANY),
        scratch_shapes=[pltpu.VMEM((EXPERT_TILE, ROW_TILES, LANES), h.dtype), pltpu.SemaphoreType.DMA,
                        pltpu.SemaphoreType.DMA],
    )
    return pl.pallas_call(
        _dispatch_kernel,
        grid_spec=grid_spec,
        out_shape=jax.ShapeDtypeStruct((n_rows, ROW_TILES, LANES), h.dtype),
        compiler_params=_cparams(("arbitrary",)),
        name="dispatch",
    )(dest, pad_ends, h)


def _expert_kernel(be_ref, nb_ref, x_ref, wgu_ref, bgu_ref, wd_ref, bd_ref, perm_ref, o_ref,
                   wgu_s, wd_s):
    i = pl.program_id(0)
    prev = be_ref[jnp.maximum(i - 1, 0)]
    new_expert = jnp.logical_or(i == 0, be_ref[i] != prev)
    n_chunk = 2 * D_FF // (2 * LANES)

    @pl.when(new_expert)
    def _():
        for c in range(n_chunk):
            cols = slice(c * 2 * LANES, (c + 1) * 2 * LANES)
            wgu_s[:, cols] = jnp.dot(wgu_ref[:, cols].astype(BF16), perm_ref[...],
                                     preferred_element_type=F32).astype(BF16)
        wd_s[...] = wd_ref[...].astype(BF16)

    @pl.when(i < nb_ref[0])
    def _():
        x = jnp.concatenate([_load_sub(x_ref, (), j) for j in range(ROW_TILES)], axis=1)
        gu = jnp.dot(x.astype(BF16), wgu_s[...], preferred_element_type=F32) + bgu_ref[...]
        acts = []
        for c in range(n_chunk):
            gate = jnp.minimum(gu[:, c * 2 * LANES:c * 2 * LANES + LANES], SWIGLU_LIMIT)
            up = jnp.clip(gu[:, c * 2 * LANES + LANES:(c + 1) * 2 * LANES], -SWIGLU_LIMIT, SWIGLU_LIMIT)
            acts.append((up + 1.0) * (gate * jax.nn.sigmoid(SWIGLU_ALPHA * gate)))
        act = jnp.concatenate(acts, axis=1).astype(BF16)
        y = jnp.dot(act, wd_s[...], preferred_element_type=F32) + bd_ref[...]
        for j in range(ROW_TILES):
            _store_sub(o_ref, (), j, y[:, j * LANES:(j + 1) * LANES])

    @pl.when(i >= nb_ref[0])
    def _():
        o_ref[...] = jnp.zeros_like(o_ref)


def expert_ffn(block_e, n_used, xs, w_gate_up, bgu, w_down, bd, layer):
    P = xs.shape[0]
    D = D_MODEL
    tm = EXPERT_TILE
    perm = np.zeros((2 * LANES, 2 * LANES))
    j = np.arange(LANES)
    perm[2 * j, j] = 1.0
    perm[2 * j + 1, LANES + j] = 1.0
    grid_spec = pltpu.PrefetchScalarGridSpec(
        num_scalar_prefetch=2,
        grid=(P // tm,),
        in_specs=[
            pl.BlockSpec((tm, ROW_TILES, LANES), lambda i, be, nb: (i, 0, 0)),
            pl.BlockSpec((None, None, D, 2 * D_FF), lambda i, be, nb: (layer, be[i], 0, 0)),
            pl.BlockSpec((None, None, 1, 2 * D_FF), lambda i, be, nb: (layer, be[i], 0, 0)),
            pl.BlockSpec((None, None, D_FF, D), lambda i, be, nb: (layer, be[i], 0, 0)),
            pl.BlockSpec((None, None, 1, D), lambda i, be, nb: (layer, be[i], 0, 0)),
            pl.BlockSpec((2 * LANES, 2 * LANES), lambda i, be, nb: (0, 0)),
        ],
        out_specs=pl.BlockSpec((tm, ROW_TILES, LANES), lambda i, be, nb: (i, 0, 0)),
        scratch_shapes=[pltpu.VMEM((D, 2 * D_FF), BF16), pltpu.VMEM((D_FF, D), BF16)],
    )
    return pl.pallas_call(
        _expert_kernel,
        grid_spec=grid_spec,
        out_shape=jax.ShapeDtypeStruct((P, ROW_TILES, LANES), F32),
        compiler_params=_cparams(("arbitrary",)),
        name="expert_ffn",
    )(block_e, n_used, xs, w_gate_up, bgu, w_down, bd, jnp.asarray(perm, BF16))


def _combine_kernel(dest_ref, x_ref, rt_ref, ga_ref, g_ref, yb_hbm, o_ref, buf, sem, *, final):
    tm = COMBINE_TILE
    t0 = pl.program_id(0) * tm

    def row_copy(i, r, d):
        return pltpu.make_async_copy(yb_hbm.at[d], buf.at[r, i], sem)

    def issue(i, carry):
        for r in range(TOP_K):
            row_copy(i, r, dest_ref[(t0 + i) * TOP_K + r]).start()
        return carry
    lax.fori_loop(0, tm, issue, 0, unroll=4)

    def drain(i, carry):
        row_copy(0, 0, 0).wait()
        return carry
    lax.fori_loop(0, tm * TOP_K, drain, 0, unroll=8)

    rt = rt_ref[...]
    gates = [jnp.broadcast_to(rt[:, TOP_K + r:TOP_K + r + 1], (tm, LANES)) for r in range(TOP_K)]
    cols = []
    for j in range(ROW_TILES):
        s = _load_sub(buf, (0,), j) * gates[0]
        for r in range(1, TOP_K):
            s = s + _load_sub(buf, (r,), j) * gates[r]
        cols.append(s)
    y = jnp.concatenate(cols, axis=1)
    x = x_ref[...] + ga_ref[...] * y
    if final:
        x = (x * lax.rsqrt(jnp.mean(x * x, axis=-1, keepdims=True) + EPS)) * g_ref[...]
    o_ref[...] = x


def combine(dest, x2, route, mod, g_final, yb, layer, final):
    T, D = x2.shape
    tm = COMBINE_TILE
    per_b = SEQ // tm
    grid_spec = pltpu.PrefetchScalarGridSpec(
        num_scalar_prefetch=1,
        grid=(T // tm,),
        in_specs=[
            pl.BlockSpec((tm, D), lambda i, d: (i, 0)),
            pl.BlockSpec((tm, LANES), lambda i, d: (i, 0)),
            pl.BlockSpec((None, None, None, 1, D), lambda i, d: (layer, 5, i // per_b, 0, 0)),
            pl.BlockSpec((1, D), lambda i, d: (0, 0)),
            pl.BlockSpec(memory_space=pl.ANY),
        ],
        out_specs=pl.BlockSpec((tm, D), lambda i, d: (i, 0)),
        scratch_shapes=[pltpu.VMEM((TOP_K, tm, ROW_TILES, LANES), F32), pltpu.SemaphoreType.DMA],
    )
    return pl.pallas_call(
        functools.partial(_combine_kernel, final=final),
        grid_spec=grid_spec,
        out_shape=jax.ShapeDtypeStruct((T, D), F32),
        compiler_params=_cparams(("arbitrary",)),
        name="combine",
    )(dest, x2, route, mod, g_final, yb)


def moe_ffn_residual(x2, h, route, mod, g_final, w_gate_up, bgu, w_down, bd, layer, final):
    T = h.shape[0]
    tm = EXPERT_TILE
    rank, counts = assignment_ranks(route)
    counts = counts[0].astype(jnp.int32)
    padded = ((counts + tm - 1) // tm) * tm
    pad_ends = jnp.cumsum(padded)
    pad_starts = pad_ends - padded
    n_blocks = -(-(T * TOP_K + N_EXPERTS * (tm - 1)) // tm)
    block_start = jnp.arange(n_blocks, dtype=jnp.int32) * tm
    block_e = jnp.minimum(jnp.sum(pad_ends[None, :] <= block_start[:, None], axis=1),
                          N_EXPERTS - 1).astype(jnp.int32)
    n_used = (pad_ends[-1:] // tm).astype(jnp.int32)
    top_idx = route[:, :TOP_K].astype(jnp.int32)
    dest = (pad_starts[top_idx] + rank[:, :TOP_K].astype(jnp.int32)).reshape(-1)
    xs = dispatch(dest, pad_ends.astype(jnp.int32), h, n_blocks * tm)
    yb = expert_ffn(block_e, n_used, xs, w_gate_up, bgu, w_down, bd, layer)
    return combine(dest, x2, route, mod, g_final, yb, layer, final)


def kernel(x, c, norm_mix_g, norm_ffn_g, w_mod, b_mod, w_in, hy_conv_w, hy_conv_b, hy_w1, hy_b1, hy_w2, hy_b2, hy_w3, hy_b3, hy_w4, hy_freq, hy_bias, w_hy_br, w_ret_br, w_out, w_router, b_router, w_gate_up, b_gate_up, w_down, b_down, final_g):
    B, S, D = x.shape
    L = w_mod.shape[0]
    T = B * S
    tabs = _tables()

    w_in_bf = w_in.astype(BF16)
    whb = w_hy_br.astype(BF16)
    wrb = w_ret_br.astype(BF16)
    wo = w_out.astype(BF16)
    E, F = N_EXPERTS, D_FF
    bgu = b_gate_up.reshape(L, E, F // LANES, LANES, 2).transpose(0, 1, 2, 4, 3).reshape(L, E, 1, 2 * F)
    bdn = b_down.reshape(L, E, 1, D)
    w1p = jnp.pad(hy_w1, ((0, 0), (0, HY_EMB_PAD - HY_EMB), (0, 0)))
    vec = lambda a: a.reshape(L, 1, -1)

    c_pad = jnp.pad(c, ((0, 8 - B), (0, 0)))
    mod = modulation(c_pad, w_mod, b_mod)[:, :B]
    mod = mod.reshape(L, B, N_MOD, 1, D).transpose(0, 2, 1, 3, 4)

    x2 = x.reshape(T, D)
    for l in range(L):
        u, qk, v, gr, gm = in_projection(x2, vec(norm_mix_g), mod, w_in_bf, l)
        xs = short_conv(u.reshape(B, S, -1), hy_conv_w, vec(hy_conv_b), l)
        kraw, l1 = hyena_filter(tabs, w1p, vec(hy_b1), hy_w2, vec(hy_b2), hy_w3, vec(hy_b3),
                                hy_w4, vec(hy_freq), l)
        ka = dft_first_axis(kraw.reshape(1, HY_ORDER, HY_SLABS, FFT_N1, FFT_N2, LANES), 0, tabs["f1"])
        kspec = filter_spectrum(ka, l1, tabs)
        y_hy = hyena_mixer(xs, kspec, hy_bias[l], tabs)
        qk3, v3, gr3 = qk.reshape(B, S, -1), v.reshape(B, S, -1), gr.reshape(B, S, -1)
        states = retention_states(qk3, v3, tabs)
        y_ret = retention_out(qk3, v3, gr3, states, tabs)
        x2, h_ffn, route = merge_and_route(
            x2, y_hy, y_ret.reshape(T, -1), gm, mod, vec(norm_ffn_g),
            whb, wrb, wo, w_router, vec(b_router), l)
        x2 = moe_ffn_residual(x2, h_ffn, route, mod, final_g.reshape(1, D), w_gate_up, bgu,
                              w_down, bdn, l, final=(l == L - 1))
    return x2.reshape(B, S, D)
```

```python
import functools
import math

import numpy as np
import jax
import jax.numpy as jnp
from jax import lax
from jax.experimental import pallas as pl
from jax.experimental.pallas import tpu as pltpu

F32 = jnp.float32
BF16 = jnp.bfloat16
HIGHEST = lax.Precision.HIGHEST

D_MODEL = 1024
BATCH = 4
SEQ = 8192
DEPTH = 4
HY_WIDTH = 512
HY_ORDER = 2
HY_EMB = 33
HY_EMB_PAD = 64
HY_FILT_HIDDEN = 64
HY_FAST_DECAY = 0.3
HY_SLOW_DECAY = 1.5
HY_DECAY_TARGET = 1e-2
RET_HEADS = 4
RET_DK = 64
RET_DV = 128
RET_QK = RET_HEADS * RET_DK
RET_V = RET_HEADS * RET_DV
RET_CHUNK = 128
RET_DECAY_FWD = 5.0
RET_DECAY_BWD = 5.5
ROPE_BASE = 10000.0
N_EXPERTS = 32
TOP_K = 4
D_FF = D_MODEL
SWIGLU_ALPHA = 1.702
SWIGLU_LIMIT = 7.0
N_MOD = 6
EPS = 1e-6
IN_COLS = 3 * HY_WIDTH + 2 * RET_QK + 2 * RET_V + 2 * D_MODEL

LANES = 128
HY_SLABS = HY_WIDTH // LANES
ROW_TILES = D_MODEL // LANES
VMEM_LIMIT = 56 * 1024 * 1024

FFT_N = 2 * SEQ
FFT_N1 = 128
FFT_N2 = FFT_N // FFT_N1
FFT_H1 = FFT_N1 // 2

TOK_TILE = 256
EXPERT_TILE = 512
RANK_TILE = 512
DISPATCH_TILE = 512
COMBINE_TILE = 256
RET_GROUP = 8
FFT_N2_BLOCK = 8
FFT_K1_GROUP = 4


def _cparams(sem):
    return pltpu.CompilerParams(dimension_semantics=sem, vmem_limit_bytes=VMEM_LIMIT)


def _const_spec(shape):
    return pl.BlockSpec(shape, lambda *_: (0,) * len(shape))


def _tables():
    n1 = np.arange(FFT_N1)
    n2 = np.arange(FFT_N2)
    ang1 = 2.0 * np.pi * np.outer(n1, n1) / FFT_N1
    f1 = np.concatenate([np.cos(ang1), -np.sin(ang1)], axis=0)
    ang2 = 2.0 * np.pi * np.outer(n2, n2) / FFT_N2
    angt = 2.0 * np.pi * np.outer(n1, n2) / FFT_N
    f1r, f1i = np.cos(ang1)[:, :FFT_H1], -np.sin(ang1)[:, :FFT_H1]
    f1c = np.block([[f1r, -f1i], [f1i, f1r]])
    gr, gi = np.cos(ang1)[:FFT_H1] / FFT_N, np.sin(ang1)[:FFT_H1] / FFT_N
    gc = np.block([[gr, -gi], [gi, gr]])

    L = SEQ
    t = np.linspace(0.0, 1.0, L)
    bands = (HY_EMB - 1) // 2
    w = 2.0 * np.pi * np.arange(L) / L
    f = np.linspace(1e-4, bands - 1, bands)
    feats = np.concatenate([t[:, None], np.cos(f[None] * w[:, None]), -np.sin(f[None] * w[:, None])], -1)
    idx = np.concatenate([np.arange(L), [0], L - np.arange(1, L)])
    feats2 = np.zeros((2 * L, HY_EMB_PAD))
    feats2[:, :HY_EMB] = feats[idx]
    max_decay = math.log(HY_DECAY_TARGET) / HY_FAST_DECAY
    min_decay = math.log(HY_DECAY_TARGET) / HY_SLOW_DECAY
    deltas = np.abs(np.linspace(min_decay, max_decay, HY_WIDTH))

    C = RET_CHUNK
    hh = np.arange(RET_HEADS)
    lgf = np.log(1.0 - np.exp2(-(RET_DECAY_FWD + hh)))
    lgb = np.log(1.0 - np.exp2(-(RET_DECAY_BWD + hh)))
    pos = np.arange(C)
    diff = pos[:, None] - pos[None, :]
    dec = np.where(diff[None] >= 0, np.exp(np.maximum(diff, 0)[None] * lgf[:, None, None]),
                   np.exp(np.maximum(-diff, 0)[None] * lgb[:, None, None]))
    lane_h = np.repeat(hh, RET_DK)
    wq = np.stack([np.exp((pos[:, None] + 1.0) * lgf[lane_h][None]),
                   np.exp((C - pos[:, None]) * lgb[lane_h][None])])
    wk = np.stack([np.exp((C - 1.0 - pos[:, None]) * lgf[lane_h][None]),
                   np.exp(pos[:, None] * lgb[lane_h][None])]) * (RET_DK ** -0.5)
    cd = np.stack([np.broadcast_to(np.exp(C * lgf[lane_h])[:, None], (RET_QK, RET_V)),
                   np.broadcast_to(np.exp(C * lgb[lane_h])[:, None], (RET_QK, RET_V))])
    col_h = np.repeat(hh, RET_DV)
    bd = (lane_h[:, None] == col_h[None, :]).astype(np.float64)
    inv_freq = 1.0 / (ROPE_BASE ** (np.arange(0, RET_DK, 2) / RET_DK))
    ang = np.arange(SEQ)[:, None] * inv_freq[None, :]
    cc = np.tile(np.concatenate([np.cos(ang), np.cos(ang)], -1), (1, RET_HEADS))
    ss = np.tile(np.concatenate([-np.sin(ang), np.sin(ang)], -1), (1, RET_HEADS))

    return dict(
        f1=jnp.asarray(f1, BF16), f1c=jnp.asarray(f1c, BF16),
        f2r=jnp.asarray(np.cos(ang2), F32), f2i=jnp.asarray(-np.sin(ang2), F32),
        twr=jnp.asarray(np.cos(angt), F32), twi=jnp.asarray(-np.sin(angt), F32),
        gc=jnp.asarray(gc, BF16),
        feats2=jnp.asarray(feats2, F32), deltas=jnp.asarray(deltas[None], F32),
        dec=jnp.asarray(dec, F32), wq=jnp.asarray(wq, F32), wk=jnp.asarray(wk, F32),
        cd=jnp.asarray(cd, F32), bd=jnp.asarray(bd, F32),
        cc=jnp.asarray(cc, F32), ss=jnp.asarray(ss, F32),
    )


def _mod_kernel(c_ref, w_ref, b_ref, o_ref):
    c = c_ref[...]
    ca = c * jax.nn.sigmoid(c)
    o_ref[...] = jnp.dot(ca, w_ref[...], preferred_element_type=F32, precision=HIGHEST) + b_ref[...]


def modulation(c_pad, w_mod, b_mod):
    L, D, N = w_mod.shape
    tn = 1536
    rows = c_pad.shape[0]
    return pl.pallas_call(
        _mod_kernel,
        grid=(L, N // tn),
        in_specs=[
            _const_spec((rows, D)),
            pl.BlockSpec((None, D, tn), lambda l, j: (l, 0, j)),
            pl.BlockSpec((None, 1, tn), lambda l, j: (l, 0, j)),
        ],
        out_specs=pl.BlockSpec((None, rows, tn), lambda l, j: (l, 0, j)),
        out_shape=jax.ShapeDtypeStruct((L, rows, N), F32),
        compiler_params=_cparams(("arbitrary", "arbitrary")),
        name="modulation",
    )(c_pad, w_mod, b_mod.reshape(L, 1, N))


def _norm_mod(x, g, sc, sh):
    y = x * lax.rsqrt(jnp.mean(x * x, axis=-1, keepdims=True) + EPS)
    return (y * g) * (1.0 + sc) + sh


def _inproj_kernel(x_ref, g_ref, sh_ref, sc_ref, w_ref, u_ref, qk_ref, v_ref, gr_ref, gm_ref):
    h = _norm_mod(x_ref[...], g_ref[...], sc_ref[...], sh_ref[...]).astype(BF16)
    c0 = 0
    for o_ref in (u_ref, qk_ref, v_ref, gr_ref, gm_ref):
        c1 = c0 + o_ref.shape[-1]
        o_ref[...] = jnp.dot(h, w_ref[:, c0:c1], preferred_element_type=F32).astype(o_ref.dtype)
        c0 = c1


def in_projection(x2, g, mod, w_in_bf, layer):
    T, D = x2.shape
    tm = TOK_TILE
    per_b = SEQ // tm
    mspec = lambda piece: pl.BlockSpec((None, None, None, 1, D),
                                       lambda i: (layer, piece, i // per_b, 0, 0))
    widths = (3 * HY_WIDTH, 2 * RET_QK, RET_V, RET_V, 2 * D_MODEL)
    dtypes = (F32, F32, BF16, F32, F32)
    return pl.pallas_call(
        _inproj_kernel,
        grid=(T // tm,),
        in_specs=[
            pl.BlockSpec((tm, D), lambda i: (i, 0)),
            pl.BlockSpec((None, 1, D), lambda i: (layer, 0, 0)),
            mspec(0), mspec(1),
            pl.BlockSpec((None, D, IN_COLS), lambda i: (layer, 0, 0)),
        ],
        out_specs=[pl.BlockSpec((tm, w), lambda i: (i, 0)) for w in widths],
        out_shape=[jax.ShapeDtypeStruct((T, w), dt) for w, dt in zip(widths, dtypes)],
        compiler_params=_cparams(("arbitrary",)),
        name="in_projection",
    )(x2, g, mod, mod, w_in_bf)


def _shortconv_kernel(u_ref, w_ref, b_ref, o_ref):
    u = u_ref[...]
    s = u.shape[0]
    row = lax.broadcasted_iota(jnp.int32, u.shape, 0)
    prev = jnp.where(row == 0, 0.0, pltpu.roll(u, 1, axis=0))
    nxt = jnp.where(row == s - 1, 0.0, pltpu.roll(u, s - 1, axis=0))
    w = w_ref[...]
    o_ref[...] = prev * w[0:1] + u * w[1:2] + nxt * w[2:3] + b_ref[...]


def short_conv(u, conv_w, conv_b, layer):
    B, S, C3 = u.shape
    cb = LANES
    per = HY_WIDTH // cb
    return pl.pallas_call(
        _shortconv_kernel,
        grid=(B, C3 // cb),
        in_specs=[
            pl.BlockSpec((None, S, cb), lambda b, j: (b, 0, j)),
            pl.BlockSpec((None, 3, cb), lambda b, j: (layer, 0, j)),
            pl.BlockSpec((None, 1, cb), lambda b, j: (layer, 0, j)),
        ],
        out_specs=pl.BlockSpec((None, None, None, S, cb), lambda b, j: (j // per, b, j % per, 0, 0)),
        out_shape=jax.ShapeDtypeStruct((3, B, per, S, cb), F32),
        compiler_params=_cparams(("arbitrary", "arbitrary")),
        name="short_conv",
    )(u, conv_w, conv_b)


def _filter_kernel(p_ref, w1_ref, b1_ref, w2_ref, b2_ref, w3_ref, b3_ref, w4_ref, fr_ref, dl_ref,
                   k_ref, l1_ref, *, rows):
    i = pl.program_id(0)
    p = p_ref[...]
    fr = fr_ref[...]
    dot = functools.partial(jnp.dot, preferred_element_type=F32, precision=HIGHEST)
    h = jnp.sin(fr * (dot(p, w1_ref[...]) + b1_ref[...]))
    h = jnp.sin(fr * (dot(h, w2_ref[...]) + b2_ref[...]))
    h = jnp.sin(fr * (dot(h, w3_ref[...]) + b3_ref[...]))
    h = dot(h, w4_ref[...])
    t = p[:, 0:1]
    win = jnp.exp(-t * dl_ref[...])
    win = jnp.concatenate([win] * HY_ORDER, axis=1)
    grow = i * rows + lax.broadcasted_iota(jnp.int32, (rows, 1), 0)
    k = jnp.where(grow == SEQ, 0.0, h * win)
    for s in range(k_ref.shape[0]):
        k_ref[s] = k[:, s * LANES:(s + 1) * LANES]

    @pl.when(i == 0)
    def _():
        l1_ref[...] = jnp.zeros_like(l1_ref)
    l1_ref[...] += jnp.sum(jnp.abs(k), axis=0, keepdims=True)


def hyena_filter(tabs, w1p, b1, w2, b2, w3, b3, w4, freq, layer):
    rows = 1024
    n = 2 * SEQ
    half = SEQ // rows
    H = HY_FILT_HIDDEN
    OC = HY_ORDER * HY_WIDTH
    vec = lambda: pl.BlockSpec((None, 1, H), lambda i: (layer, 0, 0))
    return pl.pallas_call(
        functools.partial(_filter_kernel, rows=rows),
        grid=(n // rows,),
        in_specs=[
            pl.BlockSpec((rows, HY_EMB_PAD), lambda i: (i, 0)),
            pl.BlockSpec((None, HY_EMB_PAD, H), lambda i: (layer, 0, 0)), vec(),
            pl.BlockSpec((None, H, H), lambda i: (layer, 0, 0)), vec(),
            pl.BlockSpec((None, H, H), lambda i: (layer, 0, 0)), vec(),
            pl.BlockSpec((None, H, OC), lambda i: (layer, 0, i // half)),
            vec(),
            _const_spec((1, HY_WIDTH)),
        ],
        out_specs=[pl.BlockSpec((OC // LANES, rows, LANES), lambda i: (0, i, 0)), _const_spec((1, OC))],
        out_shape=[jax.ShapeDtypeStruct((OC // LANES, n, LANES), F32),
                   jax.ShapeDtypeStruct((1, OC), F32)],
        compiler_params=_cparams(("arbitrary",)),
        name="hyena_filter",
    )(tabs["feats2"], w1p, b1, w2, b2, w3, b3, w4, freq, tabs["deltas"])


def _load_sub(ref, lead, j):
    *outer, r, s, l = ref.shape
    flat = ref.reshape(*outer, r * s, l)
    return flat[(*lead, pl.ds(j, r, stride=s), slice(None))]


def _store_sub(ref, lead, j, val):
    *outer, r, s, l = ref.shape
    flat = ref.reshape(*outer, r * s, l)
    flat[(*lead, pl.ds(j, r, stride=s), slice(None))] = val


def _pack_pair(re, im):
    r = lax.bitcast_convert_type(re.astype(BF16).astype(F32), jnp.uint32)
    i = lax.bitcast_convert_type(im.astype(BF16).astype(F32), jnp.uint32)
    return r | (i >> 16)


def _unpack_pair(p):
    re = lax.bitcast_convert_type(p & jnp.uint32(0xFFFF0000), F32)
    im = lax.bitcast_convert_type(p << 16, F32)
    return re, im


def _dft1_kernel(x_ref, f_ref, o_ref):
    for j in range(FFT_N2_BLOCK):
        xs = jnp.concatenate(
            [jnp.concatenate([_load_sub(x_ref, (p, c), j) for c in range(HY_SLABS)], axis=1)
             for p in range(x_ref.shape[0])], axis=0)
        a = jnp.dot(f_ref[...], xs.astype(BF16), preferred_element_type=F32)
        p = _pack_pair(a[:FFT_N1], a[FFT_N1:])
        for c in range(HY_SLABS):
            _store_sub(o_ref, (c,), j, p[:, c * LANES:(c + 1) * LANES])


def dft_first_axis(x6, which, f1, pair):
    _, B, _, rows, _, _ = x6.shape
    nb = FFT_N2_BLOCK
    return pl.pallas_call(
        _dft1_kernel,
        grid=(B // pair, FFT_N2 // nb),
        in_specs=[pl.BlockSpec((None, pair, HY_SLABS, rows, nb, LANES),
                               lambda b, j: (which, b, 0, 0, j, 0)),
                  _const_spec((2 * FFT_N1, pair * rows))],
        out_specs=pl.BlockSpec((None, HY_SLABS, FFT_N1, nb, LANES), lambda b, j: (b, 0, 0, j, 0)),
        out_shape=jax.ShapeDtypeStruct((B // pair, HY_SLABS, FFT_N1, FFT_N2, LANES), jnp.uint32),
        compiler_params=_cparams(("arbitrary", "arbitrary")),
        name="dft_first_axis",
    )(x6, f1)


def _twiddled_stack(f2r_ref, f2i_ref, twr_ref, twi_ref, k1):
    tr = twr_ref[pl.ds(k1, 1), :]
    ti = twi_ref[pl.ds(k1, 1), :]
    fr = f2r_ref[...]
    fi = f2i_ref[...]
    p = fr * tr - fi * ti
    q = fr * ti + fi * tr
    top = jnp.concatenate([p, -q], axis=1)
    bot = jnp.concatenate([q, p], axis=1)
    return jnp.concatenate([top, bot], axis=0).astype(BF16)


def _load_k1(a_ref, j):
    p = jnp.concatenate([a_ref[c, j] for c in range(HY_SLABS)], axis=1)
    re, im = _unpack_pair(p)
    return jnp.concatenate([re, im], axis=0).astype(BF16)


def _spectrum_kernel(a_ref, l1_ref, f2r_ref, f2i_ref, twr_ref, twi_ref, o_ref):
    base = pl.program_id(0) * FFT_K1_GROUP
    inv = 1.0 / l1_ref[...]
    for j in range(FFT_K1_GROUP):
        r = _twiddled_stack(f2r_ref, f2i_ref, twr_ref, twi_ref, base + j)
        x = jnp.dot(r, _load_k1(a_ref, j), preferred_element_type=F32) * inv
        o_ref[j, 0] = x[:FFT_N2]
        o_ref[j, 1] = x[FFT_N2:]


def filter_spectrum(a5, l1, tabs):
    g = FFT_K1_GROUP
    sq = lambda: _const_spec((FFT_N2, FFT_N2))
    return pl.pallas_call(
        _spectrum_kernel,
        grid=(FFT_N1 // g, HY_ORDER),
        in_specs=[pl.BlockSpec((None, HY_SLABS, g, FFT_N2, LANES), lambda i, o: (o, 0, i, 0, 0)),
                  pl.BlockSpec((1, HY_WIDTH), lambda i, o: (0, o)), sq(), sq(), sq(), sq()],
        out_specs=pl.BlockSpec((g, 2, FFT_N2, HY_WIDTH), lambda i, o: (i, 0, 0, o)),
        out_shape=jax.ShapeDtypeStruct((FFT_N1, 2, FFT_N2, HY_ORDER * HY_WIDTH), F32),
        compiler_params=_cparams(("arbitrary", "arbitrary")),
        name="filter_spectrum",
    )(a5, l1, tabs["f2r"], tabs["f2i"], tabs["twr"], tabs["twi"])


def _convmid_kernel(a_ref, ks_ref, f2r_ref, f2i_ref, twr_ref, twi_ref, o_ref):
    base = pl.program_id(0) * FFT_K1_GROUP
    for j in range(FFT_K1_GROUP):
        r = _twiddled_stack(f2r_ref, f2i_ref, twr_ref, twi_ref, base + j)
        x = jnp.dot(r, _load_k1(a_ref, j), preferred_element_type=F32)
        xr, xi = x[:FFT_N2], x[FFT_N2:]
        kr, ki = ks_ref[j, 0], ks_ref[j, 1]
        y = jnp.concatenate([xr * kr - xi * ki, xr * ki + xi * kr], axis=0).astype(BF16)
        b = lax.dot_general(r, y, (((0,), (0,)), ((), ())), preferred_element_type=F32)
        p = _pack_pair(b[:FFT_N2], b[FFT_N2:])
        for c in range(HY_SLABS):
            o_ref[c, j] = p[:, c * LANES:(c + 1) * LANES]


def conv_mid(a5, kspec, order, tabs):
    B = a5.shape[0]
    g = FFT_K1_GROUP
    sq = lambda: _const_spec((FFT_N2, FFT_N2))
    blk = lambda: pl.BlockSpec((None, HY_SLABS, g, FFT_N2, LANES), lambda i, b: (b, 0, i, 0, 0))
    return pl.pallas_call(
        _convmid_kernel,
        grid=(FFT_N1 // g, B),
        in_specs=[blk(), pl.BlockSpec((g, 2, FFT_N2, HY_WIDTH), lambda i, b: (i, 0, 0, order)),
                  sq(), sq(), sq(), sq()],
        out_specs=blk(),
        out_shape=jax.ShapeDtypeStruct(a5.shape, jnp.uint32),
        compiler_params=_cparams(("arbitrary", "arbitrary")),
        name="conv_mid",
    )(a5, kspec, tabs["f2r"], tabs["f2i"], tabs["twr"], tabs["twi"])


def _convout_kernel(b_ref, g_ref, z_ref, gate_ref, bias_ref, o_ref):
    for j in range(FFT_N2_BLOCK):
        p = jnp.concatenate([_load_sub(b_ref, (c,), j) for c in range(HY_SLABS)], axis=1)
        re, im = _unpack_pair(p)
        bp = jnp.concatenate([re, im], axis=0).astype(BF16)
        y = jnp.dot(g_ref[...], bp, preferred_element_type=F32)
        for q in range(2):
            for c in range(HY_SLABS):
                _store_sub(o_ref, (q, c), j, y[q * FFT_H1:(q + 1) * FFT_H1, c * LANES:(c + 1) * LANES])
    o_ref[...] = gate_ref[...] * (o_ref[...] + z_ref[...] * bias_ref[...])


def conv_out(bp5, z6, z_which, gate6, gate_which, bias4, g):
    B2 = bp5.shape[0]
    nb = FFT_N2_BLOCK
    nat = lambda which: pl.BlockSpec((None, 2, HY_SLABS, FFT_H1, nb, LANES),
                                     lambda b, j: (which, b, 0, 0, j, 0))
    return pl.pallas_call(
        _convout_kernel,
        grid=(B2, FFT_N2 // nb),
        in_specs=[pl.BlockSpec((None, HY_SLABS, FFT_N1, nb, LANES), lambda b, j: (b, 0, 0, j, 0)),
                  _const_spec((2 * FFT_H1, 2 * FFT_N1)), nat(z_which), nat(gate_which),
                  _const_spec((HY_SLABS, 1, 1, LANES))],
        out_specs=pl.BlockSpec((2, HY_SLABS, FFT_H1, nb, LANES), lambda b, j: (b, 0, 0, j, 0)),
        out_shape=jax.ShapeDtypeStruct((2 * B2, HY_SLABS, FFT_H1, FFT_N2, LANES), F32),
        compiler_params=_cparams(("arbitrary", "arbitrary")),
        name="conv_out",
    )(bp5, g, z6, gate6, bias4)


def hyena_mixer(xs, kspec, bias, tabs):
    B = xs.shape[1]
    xs6 = xs.reshape(3, B, HY_SLABS, FFT_H1, FFT_N2, LANES)
    z6, z_which = xs6, 2
    for o in range(HY_ORDER):
        a = dft_first_axis(z6, z_which, tabs["f1c"], pair=2)
        bp = conv_mid(a, kspec, o, tabs)
        z = conv_out(bp, z6, z_which, xs6, o, bias[o].reshape(HY_SLABS, 1, 1, LANES), tabs["gc"])
        z6, z_which = z[None], 0
    return z.reshape(B, HY_SLABS, SEQ, LANES)


def _rope(x, cc, ss):
    lane = lax.broadcasted_iota(jnp.int32, x.shape, 1)
    n = x.shape[1]
    half = RET_DK // 2
    swapped = jnp.where(lane % RET_DK < half, pltpu.roll(x, n - half, axis=1),
                        pltpu.roll(x, half, axis=1))
    return x * cc + swapped * ss


def _retstate_kernel(k_ref, v_ref, cc_ref, ss_ref, wk_ref, cd_ref, o_ref, s_ref):
    d = pl.program_id(1)
    g = pl.program_id(2)
    C = RET_CHUNK

    @pl.when(g == 0)
    def _():
        s_ref[...] = jnp.zeros_like(s_ref)

    for j in range(RET_GROUP):
        ci = jnp.where(d == 0, j, RET_GROUP - 1 - j)
        r0 = pl.multiple_of(ci * C, C)
        s = s_ref[...]
        o_ref[ci] = jnp.concatenate(
            [s[h * RET_DK:(h + 1) * RET_DK, h * RET_DV:(h + 1) * RET_DV] for h in range(RET_HEADS)],
            axis=1)
        k = _rope(k_ref[pl.ds(r0, C), :], cc_ref[pl.ds(r0, C), :], ss_ref[pl.ds(r0, C), :])
        kw = (k * wk_ref[...]).astype(BF16)
        ds = lax.dot_general(kw, v_ref[pl.ds(r0, C), :], (((0,), (0,)), ((), ())),
                             preferred_element_type=F32)
        s_ref[...] = s * cd_ref[...] + ds


def retention_states(qk, v, tabs):
    B, S, _ = qk.shape
    rows = RET_GROUP * RET_CHUNK
    G = S // rows
    grp = lambda d, g: jnp.where(d == 0, g, G - 1 - g)
    return pl.pallas_call(
        _retstate_kernel,
        grid=(B, 2, G),
        in_specs=[
            pl.BlockSpec((None, rows, RET_QK), lambda b, d, g: (b, grp(d, g), 1)),
            pl.BlockSpec((None, rows, RET_V), lambda b, d, g: (b, grp(d, g), 0)),
            pl.BlockSpec((rows, RET_QK), lambda b, d, g: (grp(d, g), 0)),
            pl.BlockSpec((rows, RET_QK), lambda b, d, g: (grp(d, g), 0)),
            pl.BlockSpec((None, RET_CHUNK, RET_QK), lambda b, d, g: (d, 0, 0)),
            pl.BlockSpec((None, RET_QK, RET_V), lambda b, d, g: (d, 0, 0)),
        ],
        out_specs=pl.BlockSpec((None, None, RET_GROUP, RET_DK, RET_V),
                               lambda b, d, g: (b, d, grp(d, g), 0, 0)),
        out_shape=jax.ShapeDtypeStruct((B, 2, S // RET_CHUNK, RET_DK, RET_V), F32),
        scratch_shapes=[pltpu.VMEM((RET_QK, RET_V), F32)],
        compiler_params=_cparams(("arbitrary", "arbitrary", "arbitrary")),
        name="retention_states",
    )(qk, v, tabs["cc"], tabs["ss"], tabs["wk"], tabs["cd"])


def _retout_kernel(qk_ref, v_ref, gr_ref, cc_ref, ss_ref, st_ref, dec_ref, wq_ref, wk_ref, bd_ref,
                   o_ref):
    C = RET_CHUNK
    bd = bd_ref[...]
    lane = lax.broadcasted_iota(jnp.int32, (C, RET_QK), 1)
    for j in range(RET_GROUP):
        r0 = j * C
        cc = cc_ref[r0:r0 + C, :]
        ss = ss_ref[r0:r0 + C, :]
        q = _rope(qk_ref[r0:r0 + C, :RET_QK], cc, ss)
        k = (_rope(qk_ref[r0:r0 + C, RET_QK:], cc, ss) * (RET_DK ** -0.5)).astype(BF16)
        v = v_ref[r0:r0 + C, :]
        inner = []
        for h in range(RET_HEADS):
            qh = jnp.where(lane // RET_DK == h, q, 0.0).astype(BF16)
            s = lax.dot_general(qh, k, (((1,), (1,)), ((), ())), preferred_element_type=F32)
            s = (s * dec_ref[h]).astype(BF16)
            inner.append(jnp.dot(s, v[:, h * RET_DV:(h + 1) * RET_DV], preferred_element_type=F32))
        qq = jnp.concatenate([q * wq_ref[0], q * wq_ref[1]], axis=1).astype(BF16)
        sf = jnp.concatenate([st_ref[0, j]] * RET_HEADS, axis=0) * bd
        sb = jnp.concatenate([st_ref[1, j]] * RET_HEADS, axis=0) * bd
        sbd = jnp.concatenate([sf, sb], axis=0).astype(BF16)
        o = jnp.concatenate(inner, axis=1) + jnp.dot(qq, sbd, preferred_element_type=F32)
        outs = []
        for h in range(RET_HEADS):
            oh = o[:, h * RET_DV:(h + 1) * RET_DV]
            outs.append(oh * lax.rsqrt(jnp.mean(oh * oh, axis=-1, keepdims=True) + EPS))
        gr = gr_ref[r0:r0 + C, :]
        o_ref[r0:r0 + C, :] = (gr * jax.nn.sigmoid(gr)) * jnp.concatenate(outs, axis=1)


def retention_out(qk, v, gr, states, tabs):
    B, S, _ = qk.shape
    rows = RET_GROUP * RET_CHUNK
    G = S // rows
    return pl.pallas_call(
        _retout_kernel,
        grid=(B, G),
        in_specs=[
            pl.BlockSpec((None, rows, 2 * RET_QK), lambda b, g: (b, g, 0)),
            pl.BlockSpec((None, rows, RET_V), lambda b, g: (b, g, 0)),
            pl.BlockSpec((None, rows, RET_V), lambda b, g: (b, g, 0)),
            pl.BlockSpec((rows, RET_QK), lambda b, g: (g, 0)),
            pl.BlockSpec((rows, RET_QK), lambda b, g: (g, 0)),
            pl.BlockSpec((None, 2, RET_GROUP, RET_DK, RET_V), lambda b, g: (b, 0, g, 0, 0)),
            _const_spec((RET_HEADS, RET_CHUNK, RET_CHUNK)),
            _const_spec((2, RET_CHUNK, RET_QK)),
            _const_spec((2, RET_CHUNK, RET_QK)),
            _const_spec((RET_QK, RET_V)),
        ],
        out_specs=pl.BlockSpec((None, rows, RET_V), lambda b, g: (b, g, 0)),
        out_shape=jax.ShapeDtypeStruct((B, S, RET_V), F32),
        compiler_params=_cparams(("arbitrary", "arbitrary")),
        name="retention_out",
    )(qk, v, gr, tabs["cc"], tabs["ss"], states, tabs["dec"], tabs["wq"], tabs["wk"], tabs["bd"])


def _merge_kernel(x_ref, yh_ref, yr_ref, gm_ref, ga_ref, whb_ref, wrb_ref, wo_ref,
                  g_ref, sh_ref, sc_ref, wr_ref, br_ref, xo_ref, h_ref, rt_ref):
    gm = gm_ref[...]
    ghy = 0.5 * jnp.tanh(0.5 * gm[:, :D_MODEL]) + 0.5
    grt = 0.5 * jnp.tanh(0.5 * gm[:, D_MODEL:]) + 0.5
    yh = jnp.concatenate([yh_ref[c] for c in range(HY_SLABS)], axis=1)
    m = (ghy * jnp.dot(yh.astype(BF16), whb_ref[...], preferred_element_type=F32)
         + grt * jnp.dot(yr_ref[...].astype(BF16), wrb_ref[...], preferred_element_type=F32))
    out = jnp.dot(m.astype(BF16), wo_ref[...], preferred_element_type=F32)
    x = x_ref[...] + ga_ref[...] * out
    xo_ref[...] = x
    h = _norm_mod(x, g_ref[...], sc_ref[...], sh_ref[...])
    for j in range(ROW_TILES):
        _store_sub(h_ref, (), j, h[:, j * LANES:(j + 1) * LANES])
    h_hi = h.astype(BF16)
    h_lo = (h - h_hi.astype(F32)).astype(BF16)
    logits = (jnp.dot(h_hi, wr_ref[0], preferred_element_type=F32)
              + jnp.dot(h_hi, wr_ref[1], preferred_element_type=F32)
              + jnp.dot(h_lo, wr_ref[0], preferred_element_type=F32)) + br_ref[...]
    tm = logits.shape[0]
    lane_e = lax.broadcasted_iota(jnp.int32, logits.shape, 1).astype(F32)
    lane_o = lax.broadcasted_iota(jnp.int32, (tm, LANES), 1)
    route = jnp.zeros((tm, LANES), F32)
    vals = []
    work = logits
    for r in range(TOP_K):
        m_r = jnp.max(work, axis=-1, keepdims=True)
        i_r = jnp.min(jnp.where(work == m_r, lane_e, float(N_EXPERTS)), axis=-1, keepdims=True)
        work = jnp.where(lane_e == i_r, -jnp.inf, work)
        vals.append(m_r)
        route = jnp.where(lane_o == r, i_r, route)
    exps = [jnp.exp(v - vals[0]) for v in vals]
    den = exps[0] + exps[1] + exps[2] + exps[3]
    for r in range(TOP_K):
        route = jnp.where(lane_o == TOP_K + r, exps[r] / den, route)
    rt_ref[...] = route


def merge_and_route(x2, yh, yr, gm, mod, g_ffn, whb, wrb, wo, w_router, b_router, layer):
    T, D = x2.shape
    tm = TOK_TILE
    per_b = SEQ // tm
    mspec = lambda piece: pl.BlockSpec((None, None, None, 1, D),
                                       lambda i: (layer, piece, i // per_b, 0, 0))
    row = lambda w: pl.BlockSpec((tm, w), lambda i: (i, 0))
    wsp = lambda a, b: pl.BlockSpec((None, a, b), lambda i: (layer, 0, 0))
    return pl.pallas_call(
        _merge_kernel,
        grid=(T // tm,),
        in_specs=[row(D),
                  pl.BlockSpec((None, HY_SLABS, tm, LANES), lambda i: (i // per_b, 0, i % per_b, 0)),
                  row(RET_V), row(2 * D), mspec(2),
                  wsp(HY_WIDTH, D), wsp(RET_V, D), wsp(D, D),
                  wsp(1, D), mspec(3), mspec(4),
                  pl.BlockSpec((None, 2, D, N_EXPERTS), lambda i: (layer, 0, 0, 0)),
                  wsp(1, N_EXPERTS)],
        out_specs=[row(D), pl.BlockSpec((tm, ROW_TILES, LANES), lambda i: (i, 0, 0)), row(LANES)],
        out_shape=[jax.ShapeDtypeStruct((T, D), F32), jax.ShapeDtypeStruct((T, ROW_TILES, LANES), F32),
                   jax.ShapeDtypeStruct((T, LANES), F32)],
        compiler_params=_cparams(("arbitrary",)),
        name="merge_and_route",
    )(x2, yh, yr, gm, mod, whb, wrb, wo, g_ffn, mod, mod, w_router, b_router)


def _rank_kernel(rt_ref, tril_ref, o_ref, cnt_ref, carry_ref):
    @pl.when(pl.program_id(0) == 0)
    def _():
        carry_ref[...] = jnp.zeros_like(carry_ref)

    rt = rt_ref[...]
    tm = rt.shape[0]
    lane_e = lax.broadcasted_iota(jnp.int32, (tm, N_EXPERTS), 1).astype(F32)
    lane_o = lax.broadcasted_iota(jnp.int32, (tm, LANES), 1)
    hits = [lane_e == rt[:, r:r + 1] for r in range(TOP_K)]
    onehot = sum(h.astype(F32) for h in hits)
    before = jnp.dot(tril_ref[...], onehot.astype(BF16), preferred_element_type=F32) + carry_ref[...]
    out = jnp.zeros((tm, LANES), F32)
    for r in range(TOP_K):
        rank_r = jnp.sum(jnp.where(hits[r], before, 0.0), axis=-1, keepdims=True)
        out = jnp.where(lane_o == r, rank_r, out)
    o_ref[...] = out
    carry_ref[...] += jnp.sum(onehot, axis=0, keepdims=True)
    cnt_ref[...] = carry_ref[...]


def assignment_ranks(route):
    T = route.shape[0]
    tm = RANK_TILE
    tril = jnp.asarray(np.tril(np.ones((tm, tm)), -1), BF16)
    return pl.pallas_call(
        _rank_kernel,
        grid=(T // tm,),
        in_specs=[pl.BlockSpec((tm, LANES), lambda i: (i, 0)), _const_spec((tm, tm))],
        out_specs=[pl.BlockSpec((tm, LANES), lambda i: (i, 0)), _const_spec((1, N_EXPERTS))],
        out_shape=[jax.ShapeDtypeStruct((T, LANES), F32),
                   jax.ShapeDtypeStruct((1, N_EXPERTS), F32)],
        scratch_shapes=[pltpu.VMEM((1, N_EXPERTS), F32)],
        compiler_params=_cparams(("arbitrary",)),
        name="assignment_ranks",
    )(route, tril)


def _dispatch_kernel(dest_ref, pe_ref, h_ref, xs_hbm, zeros, sem, zsem):
    tm = DISPATCH_TILE
    t0 = pl.program_id(0) * tm

    @pl.when(pl.program_id(0) == 0)
    def _():
        zeros[...] = jnp.zeros_like(zeros)

        def zero_block(e):
            end = pe_ref[e]
            begin = pe_ref[jnp.maximum(e - 1, 0)]
            nonempty = jnp.logical_or(jnp.logical_and(e == 0, end > 0), end > begin)
            start = pl.multiple_of(jnp.maximum(end - EXPERT_TILE, 0), EXPERT_TILE)
            return nonempty, pltpu.make_async_copy(zeros, xs_hbm.at[pl.ds(start, EXPERT_TILE)], zsem)

        def start_zero(e, carry):
            nonempty, cp = zero_block(e)

            @pl.when(nonempty)
            def _():
                cp.start()
            return carry
        lax.fori_loop(0, N_EXPERTS, start_zero, 0)

        def wait_zero(e, carry):
            nonempty, cp = zero_block(e)

            @pl.when(nonempty)
            def _():
                cp.wait()
            return carry
        lax.fori_loop(0, N_EXPERTS, wait_zero, 0)

    def row_copy(i, d):
        return pltpu.make_async_copy(h_ref.at[i], xs_hbm.at[d], sem)

    def issue(i, carry):
        for r in range(TOP_K):
            row_copy(i, dest_ref[(t0 + i) * TOP_K + r]).start(priority=r % 2)
        return carry
    lax.fori_loop(0, tm, issue, 0, unroll=4)

    def drain(i, carry):
        row_copy(0, 0).wait()
        return carry
    lax.fori_loop(0, tm * TOP_K, drain, 0, unroll=8)


def dispatch(dest, pad_ends, h, n_rows):
    T = h.shape[0]
    tm = DISPATCH_TILE
    grid_spec = pltpu.PrefetchScalarGridSpec(
        num_scalar_prefetch=2,
        grid=(T // tm,),
        in_specs=[pl.BlockSpec((tm, ROW_TILES, LANES), lambda i, d, pe: (i, 0, 0))],
        out_specs=pl.BlockSpec(memory_space=pl.ANY),
        scratch_shapes=[pltpu.VMEM((EXPERT_TILE, ROW_TILES, LANES), h.dtype), pltpu.SemaphoreType.DMA,
                        pltpu.SemaphoreType.DMA],
    )
    return pl.pallas_call(
        _dispatch_kernel,
        grid_spec=grid_spec,
        out_shape=jax.ShapeDtypeStruct((n_rows, ROW_TILES, LANES), h.dtype),
        compiler_params=_cparams(("arbitrary",)),
        name="dispatch",
    )(dest, pad_ends, h)


def _expert_kernel(be_ref, nb_ref, x_ref, wgu_ref, bgu_ref, wd_ref, bd_ref, perm_ref, o_ref,
                   wgu_s, wd_s):
    i = pl.program_id(0)
    prev = be_ref[jnp.maximum(i - 1, 0)]
    new_expert = jnp.logical_or(i == 0, be_ref[i] != prev)
    n_chunk = 2 * D_FF // (2 * LANES)

    @pl.when(new_expert)
    def _():
        for c in range(n_chunk):
            cols = slice(c * 2 * LANES, (c + 1) * 2 * LANES)
            wgu_s[:, cols] = jnp.dot(wgu_ref[:, cols].astype(BF16), perm_ref[...],
                                     preferred_element_type=F32).astype(BF16)
        wd_s[...] = wd_ref[...].astype(BF16)

    @pl.when(i < nb_ref[0])
    def _():
        x = jnp.concatenate([_load_sub(x_ref, (), j) for j in range(ROW_TILES)], axis=1)
        gu = jnp.dot(x.astype(BF16), wgu_s[...], preferred_element_type=F32) + bgu_ref[...]
        acts = []
        for c in range(n_chunk):
            gate = jnp.minimum(gu[:, c * 2 * LANES:c * 2 * LANES + LANES], SWIGLU_LIMIT)
            up = jnp.clip(gu[:, c * 2 * LANES + LANES:(c + 1) * 2 * LANES], -SWIGLU_LIMIT, SWIGLU_LIMIT)
            acts.append((up + 1.0) * (gate * jax.nn.sigmoid(SWIGLU_ALPHA * gate)))
        act = jnp.concatenate(acts, axis=1).astype(BF16)
        y = jnp.dot(act, wd_s[...], preferred_element_type=F32) + bd_ref[...]
        for j in range(ROW_TILES):
            _store_sub(o_ref, (), j, y[:, j * LANES:(j + 1) * LANES])

    @pl.when(i >= nb_ref[0])
    def _():
        o_ref[...] = jnp.zeros_like(o_ref)


def expert_ffn(block_e, n_used, xs, w_gate_up, bgu, w_down, bd, layer):
    P = xs.shape[0]
    D = D_MODEL
    tm = EXPERT_TILE
    perm = np.zeros((2 * LANES, 2 * LANES))
    j = np.arange(LANES)
    perm[2 * j, j] = 1.0
    perm[2 * j + 1, LANES + j] = 1.0
    grid_spec = pltpu.PrefetchScalarGridSpec(
        num_scalar_prefetch=2,
        grid=(P // tm,),
        in_specs=[
            pl.BlockSpec((tm, ROW_TILES, LANES), lambda i, be, nb: (i, 0, 0)),
            pl.BlockSpec((None, None, D, 2 * D_FF), lambda i, be, nb: (layer, be[i], 0, 0)),
            pl.BlockSpec((None, None, 1, 2 * D_FF), lambda i, be, nb: (layer, be[i], 0, 0)),
            pl.BlockSpec((None, None, D_FF, D), lambda i, be, nb: (layer, be[i], 0, 0)),
            pl.BlockSpec((None, None, 1, D), lambda i, be, nb: (layer, be[i], 0, 0)),
            pl.BlockSpec((2 * LANES, 2 * LANES), lambda i, be, nb: (0, 0)),
        ],
        out_specs=pl.BlockSpec((tm, ROW_TILES, LANES), lambda i, be, nb: (i, 0, 0)),
        scratch_shapes=[pltpu.VMEM((D, 2 * D_FF), BF16), pltpu.VMEM((D_FF, D), BF16)],
    )
    return pl.pallas_call(
        _expert_kernel,
        grid_spec=grid_spec,
        out_shape=jax.ShapeDtypeStruct((P, ROW_TILES, LANES), F32),
        compiler_params=_cparams(("arbitrary",)),
        name="expert_ffn",
    )(block_e, n_used, xs, w_gate_up, bgu, w_down, bd, jnp.asarray(perm, BF16))


def _combine_kernel(dest_ref, x_ref, rt_ref, ga_ref, g_ref, yb_hbm, o_ref, buf, sem, *, final):
    tm = COMBINE_TILE
    t0 = pl.program_id(0) * tm

    def row_copy(i, r, d):
        return pltpu.make_async_copy(yb_hbm.at[d], buf.at[r, i], sem)

    def issue(i, carry):
        for r in range(TOP_K):
            row_copy(i, r, dest_ref[(t0 + i) * TOP_K + r]).start(priority=r % 2)
        return carry
    lax.fori_loop(0, tm, issue, 0, unroll=4)

    def drain(i, carry):
        row_copy(0, 0, 0).wait()
        return carry
    lax.fori_loop(0, tm * TOP_K, drain, 0, unroll=8)

    rt = rt_ref[...]
    gates = [jnp.broadcast_to(rt[:, TOP_K + r:TOP_K + r + 1], (tm, LANES)) for r in range(TOP_K)]
    cols = []
    for j in range(ROW_TILES):
        s = _load_sub(buf, (0,), j) * gates[0]
        for r in range(1, TOP_K):
            s = s + _load_sub(buf, (r,), j) * gates[r]
        cols.append(s)
    y = jnp.concatenate(cols, axis=1)
    x = x_ref[...] + ga_ref[...] * y
    if final:
        x = (x * lax.rsqrt(jnp.mean(x * x, axis=-1, keepdims=True) + EPS)) * g_ref[...]
    o_ref[...] = x


def combine(dest, x2, route, mod, g_final, yb, layer, final):
    T, D = x2.shape
    tm = COMBINE_TILE
    per_b = SEQ // tm
    grid_spec = pltpu.PrefetchScalarGridSpec(
        num_scalar_prefetch=1,
        grid=(T // tm,),
        in_specs=[
            pl.BlockSpec((tm, D), lambda i, d: (i, 0)),
            pl.BlockSpec((tm, LANES), lambda i, d: (i, 0)),
            pl.BlockSpec((None, None, None, 1, D), lambda i, d: (layer, 5, i // per_b, 0, 0)),
            pl.BlockSpec((1, D), lambda i, d: (0, 0)),
            pl.BlockSpec(memory_space=pl.ANY),
        ],
        out_specs=pl.BlockSpec((tm, D), lambda i, d: (i, 0)),
        scratch_shapes=[pltpu.VMEM((TOP_K, tm, ROW_TILES, LANES), F32), pltpu.SemaphoreType.DMA],
    )
    return pl.pallas_call(
        functools.partial(_combine_kernel, final=final),
        grid_spec=grid_spec,
        out_shape=jax.ShapeDtypeStruct((T, D), F32),
        compiler_params=_cparams(("arbitrary",)),
        name="combine",
    )(dest, x2, route, mod, g_final, yb)


def moe_ffn_residual(x2, h, route, mod, g_final, w_gate_up, bgu, w_down, bd, layer, final):
    T = h.shape[0]
    tm = EXPERT_TILE
    rank, counts = assignment_ranks(route)
    counts = counts[0].astype(jnp.int32)
    padded = ((counts + tm - 1) // tm) * tm
    pad_ends = jnp.cumsum(padded)
    pad_starts = pad_ends - padded
    n_blocks = -(-(T * TOP_K + N_EXPERTS * (tm - 1)) // tm)
    block_start = jnp.arange(n_blocks, dtype=jnp.int32) * tm
    block_e = jnp.minimum(jnp.sum(pad_ends[None, :] <= block_start[:, None], axis=1),
                          N_EXPERTS - 1).astype(jnp.int32)
    n_used = (pad_ends[-1:] // tm).astype(jnp.int32)
    top_idx = route[:, :TOP_K].astype(jnp.int32)
    dest = (pad_starts[top_idx] + rank[:, :TOP_K].astype(jnp.int32)).reshape(-1)
    xs = dispatch(dest, pad_ends.astype(jnp.int32), h, n_blocks * tm)
    yb = expert_ffn(block_e, n_used, xs, w_gate_up, bgu, w_down, bd, layer)
    return combine(dest, x2, route, mod, g_final, yb, layer, final)


def kernel(x, c, norm_mix_g, norm_ffn_g, w_mod, b_mod, w_in, hy_conv_w, hy_conv_b, hy_w1, hy_b1, hy_w2, hy_b2, hy_w3, hy_b3, hy_w4, hy_freq, hy_bias, w_hy_br, w_ret_br, w_out, w_router, b_router, w_gate_up, b_gate_up, w_down, b_down, final_g):
    B, S, D = x.shape
    L = w_mod.shape[0]
    T = B * S
    tabs = _tables()

    w_in_bf = w_in.astype(BF16)
    whb = w_hy_br.astype(BF16)
    wrb = w_ret_br.astype(BF16)
    wo = w_out.astype(BF16)
    wr_hi = w_router.astype(BF16)
    wr_split = jnp.stack([wr_hi, (w_router - wr_hi.astype(F32)).astype(BF16)], axis=1)
    E, F = N_EXPERTS, D_FF
    bgu = b_gate_up.reshape(L, E, F // LANES, LANES, 2).transpose(0, 1, 2, 4, 3).reshape(L, E, 1, 2 * F)
    bdn = b_down.reshape(L, E, 1, D)
    w1p = jnp.pad(hy_w1, ((0, 0), (0, HY_EMB_PAD - HY_EMB), (0, 0)))
    vec = lambda a: a.reshape(L, 1, -1)

    c_pad = jnp.pad(c, ((0, 8 - B), (0, 0)))
    mod = modulation(c_pad, w_mod, b_mod)[:, :B]
    mod = mod.reshape(L, B, N_MOD, 1, D).transpose(0, 2, 1, 3, 4)

    x2 = x.reshape(T, D)
    for l in range(L):
        u, qk, v, gr, gm = in_projection(x2, vec(norm_mix_g), mod, w_in_bf, l)
        xs = short_conv(u.reshape(B, S, -1), hy_conv_w, vec(hy_conv_b), l)
        kraw, l1 = hyena_filter(tabs, w1p, vec(hy_b1), hy_w2, vec(hy_b2), hy_w3, vec(hy_b3),
                                hy_w4, vec(hy_freq), l)
        ka = dft_first_axis(kraw.reshape(1, HY_ORDER, HY_SLABS, FFT_N1, FFT_N2, LANES), 0, tabs["f1"],
                            pair=1)
        kspec = filter_spectrum(ka, l1, tabs)
        y_hy = hyena_mixer(xs, kspec, hy_bias[l], tabs)
        qk3, v3, gr3 = qk.reshape(B, S, -1), v.reshape(B, S, -1), gr.reshape(B, S, -1)
        states = retention_states(qk3, v3, tabs)
        y_ret = retention_out(qk3, v3, gr3, states, tabs)
        x2, h_ffn, route = merge_and_route(
            x2, y_hy, y_ret.reshape(T, -1), gm, mod, vec(norm_ffn_g),
            whb, wrb, wo, wr_split, vec(b_router), l)
        x2 = moe_ffn_residual(x2, h_ffn, route, mod, final_g.reshape(1, D), w_gate_up, bgu,
                              w_down, bdn, l, final=(l == L - 1))
    return x2.reshape(B, S, D)
```

```python
import functools
import math

import numpy as np
import jax
import jax.numpy as jnp
from jax import lax
from jax.experimental import pallas as pl
from jax.experimental.pallas import tpu as pltpu

F32 = jnp.float32
BF16 = jnp.bfloat16
HIGHEST = lax.Precision.HIGHEST

D_MODEL = 1024
BATCH = 4
SEQ = 8192
DEPTH = 4
HY_WIDTH = 512
HY_ORDER = 2
HY_EMB = 33
HY_EMB_PAD = 64
HY_FILT_HIDDEN = 64
HY_FAST_DECAY = 0.3
HY_SLOW_DECAY = 1.5
HY_DECAY_TARGET = 1e-2
RET_HEADS = 4
RET_DK = 64
RET_DV = 128
RET_QK = RET_HEADS * RET_DK
RET_V = RET_HEADS * RET_DV
RET_CHUNK = 128
RET_DECAY_FWD = 5.0
RET_DECAY_BWD = 5.5
ROPE_BASE = 10000.0
N_EXPERTS = 32
TOP_K = 4
D_FF = D_MODEL
SWIGLU_ALPHA = 1.702
SWIGLU_LIMIT = 7.0
N_MOD = 6
EPS = 1e-6
IN_COLS = 3 * HY_WIDTH + 2 * RET_QK + 2 * RET_V + 2 * D_MODEL

LANES = 128
HY_SLABS = HY_WIDTH // LANES
ROW_TILES = D_MODEL // LANES
VMEM_LIMIT = 56 * 1024 * 1024

FFT_N = 2 * SEQ
FFT_N1 = 128
FFT_N2 = FFT_N // FFT_N1
FFT_H1 = FFT_N1 // 2

TOK_TILE = 256
EXPERT_TILE = 512
RANK_TILE = 512
DISPATCH_TILE = 512
COMBINE_TILE = 256
RET_GROUP = 8
FFT_N2_BLOCK = 8
FFT_K1_GROUP = 8


def _cparams(sem):
    return pltpu.CompilerParams(dimension_semantics=sem, vmem_limit_bytes=VMEM_LIMIT)


def _const_spec(shape):
    return pl.BlockSpec(shape, lambda *_: (0,) * len(shape))


def _tables():
    n1 = np.arange(FFT_N1)
    n2 = np.arange(FFT_N2)
    ang1 = 2.0 * np.pi * np.outer(n1, n1) / FFT_N1
    f1 = np.concatenate([np.cos(ang1), -np.sin(ang1)], axis=0)
    ang2 = 2.0 * np.pi * np.outer(n2, n2) / FFT_N2
    angt = 2.0 * np.pi * np.outer(n1, n2) / FFT_N
    f1r, f1i = np.cos(ang1)[:, :FFT_H1], -np.sin(ang1)[:, :FFT_H1]
    f1c = np.block([[f1r, -f1i], [f1i, f1r]])
    gr, gi = np.cos(ang1)[:FFT_H1] / FFT_N, np.sin(ang1)[:FFT_H1] / FFT_N
    gc = np.block([[gr, -gi], [gi, gr]])

    L = SEQ
    t = np.linspace(0.0, 1.0, L)
    bands = (HY_EMB - 1) // 2
    w = 2.0 * np.pi * np.arange(L) / L
    f = np.linspace(1e-4, bands - 1, bands)
    feats = np.concatenate([t[:, None], np.cos(f[None] * w[:, None]), -np.sin(f[None] * w[:, None])], -1)
    idx = np.concatenate([np.arange(L), [0], L - np.arange(1, L)])
    feats2 = np.zeros((2 * L, HY_EMB_PAD))
    feats2[:, :HY_EMB] = feats[idx]
    max_decay = math.log(HY_DECAY_TARGET) / HY_FAST_DECAY
    min_decay = math.log(HY_DECAY_TARGET) / HY_SLOW_DECAY
    deltas = np.abs(np.linspace(min_decay, max_decay, HY_WIDTH))

    C = RET_CHUNK
    hh = np.arange(RET_HEADS)
    lgf = np.log(1.0 - np.exp2(-(RET_DECAY_FWD + hh)))
    lgb = np.log(1.0 - np.exp2(-(RET_DECAY_BWD + hh)))
    pos = np.arange(C)
    diff = pos[:, None] - pos[None, :]
    dec = np.where(diff[None] >= 0, np.exp(np.maximum(diff, 0)[None] * lgf[:, None, None]),
                   np.exp(np.maximum(-diff, 0)[None] * lgb[:, None, None]))
    lane_h = np.repeat(hh, RET_DK)
    wq = np.stack([np.exp((pos[:, None] + 1.0) * lgf[lane_h][None]),
                   np.exp((C - pos[:, None]) * lgb[lane_h][None])])
    wk = np.stack([np.exp((C - 1.0 - pos[:, None]) * lgf[lane_h][None]),
                   np.exp(pos[:, None] * lgb[lane_h][None])]) * (RET_DK ** -0.5)
    cd = np.stack([np.broadcast_to(np.exp(C * lgf[lane_h])[:, None], (RET_QK, RET_V)),
                   np.broadcast_to(np.exp(C * lgb[lane_h])[:, None], (RET_QK, RET_V))])
    col_h = np.repeat(hh, RET_DV)
    bd = (lane_h[:, None] == col_h[None, :]).astype(np.float64)
    inv_freq = 1.0 / (ROPE_BASE ** (np.arange(0, RET_DK, 2) / RET_DK))
    ang = np.arange(SEQ)[:, None] * inv_freq[None, :]
    cc = np.tile(np.concatenate([np.cos(ang), np.cos(ang)], -1), (1, RET_HEADS))
    ss = np.tile(np.concatenate([-np.sin(ang), np.sin(ang)], -1), (1, RET_HEADS))

    return dict(
        f1=jnp.asarray(f1, BF16), f1c=jnp.asarray(f1c, BF16),
        f2r=jnp.asarray(np.cos(ang2), F32), f2i=jnp.asarray(-np.sin(ang2), F32),
        twr=jnp.asarray(np.cos(angt), F32), twi=jnp.asarray(-np.sin(angt), F32),
        gc=jnp.asarray(gc, BF16),
        feats2=jnp.asarray(feats2, F32), deltas=jnp.asarray(deltas[None], F32),
        dec=jnp.asarray(dec, F32), wq=jnp.asarray(wq, F32), wk=jnp.asarray(wk, F32),
        cd=jnp.asarray(cd, F32), bd=jnp.asarray(bd, F32),
        cc=jnp.asarray(cc, F32), ss=jnp.asarray(ss, F32),
    )


def _mod_kernel(c_ref, w_ref, b_ref, o_ref):
    c = c_ref[...]
    ca = c * jax.nn.sigmoid(c)
    o_ref[...] = jnp.dot(ca, w_ref[...], preferred_element_type=F32, precision=HIGHEST) + b_ref[...]


def modulation(c_pad, w_mod, b_mod):
    L, D, N = w_mod.shape
    tn = 1536
    rows = c_pad.shape[0]
    return pl.pallas_call(
        _mod_kernel,
        grid=(L, N // tn),
        in_specs=[
            _const_spec((rows, D)),
            pl.BlockSpec((None, D, tn), lambda l, j: (l, 0, j)),
            pl.BlockSpec((None, 1, tn), lambda l, j: (l, 0, j)),
        ],
        out_specs=pl.BlockSpec((None, rows, tn), lambda l, j: (l, 0, j)),
        out_shape=jax.ShapeDtypeStruct((L, rows, N), F32),
        compiler_params=_cparams(("arbitrary", "arbitrary")),
        name="modulation",
    )(c_pad, w_mod, b_mod.reshape(L, 1, N))


def _norm_mod(x, g, sc, sh):
    y = x * lax.rsqrt(jnp.mean(x * x, axis=-1, keepdims=True) + EPS)
    return (y * g) * (1.0 + sc) + sh


def _inproj_kernel(x_ref, g_ref, sh_ref, sc_ref, w_ref, u_ref, qk_ref, v_ref, gr_ref, gm_ref):
    h = _norm_mod(x_ref[...], g_ref[...], sc_ref[...], sh_ref[...]).astype(BF16)
    c0 = 0
    for o_ref in (u_ref, qk_ref, v_ref, gr_ref, gm_ref):
        c1 = c0 + o_ref.shape[-1]
        o_ref[...] = jnp.dot(h, w_ref[:, c0:c1], preferred_element_type=F32).astype(o_ref.dtype)
        c0 = c1


def in_projection(x2, g, mod, w_in_bf, layer):
    T, D = x2.shape
    tm = TOK_TILE
    per_b = SEQ // tm
    mspec = lambda piece: pl.BlockSpec((None, None, None, 1, D),
                                       lambda i: (layer, piece, i // per_b, 0, 0))
    widths = (3 * HY_WIDTH, 2 * RET_QK, RET_V, RET_V, 2 * D_MODEL)
    dtypes = (F32, F32, BF16, F32, F32)
    return pl.pallas_call(
        _inproj_kernel,
        grid=(T // tm,),
        in_specs=[
            pl.BlockSpec((tm, D), lambda i: (i, 0)),
            pl.BlockSpec((None, 1, D), lambda i: (layer, 0, 0)),
            mspec(0), mspec(1),
            pl.BlockSpec((None, D, IN_COLS), lambda i: (layer, 0, 0)),
        ],
        out_specs=[pl.BlockSpec((tm, w), lambda i: (i, 0)) for w in widths],
        out_shape=[jax.ShapeDtypeStruct((T, w), dt) for w, dt in zip(widths, dtypes)],
        compiler_params=_cparams(("arbitrary",)),
        name="in_projection",
    )(x2, g, mod, mod, w_in_bf)


def _shortconv_kernel(u_ref, w_ref, b_ref, o_ref):
    u = u_ref[...]
    s = u.shape[0]
    row = lax.broadcasted_iota(jnp.int32, u.shape, 0)
    prev = jnp.where(row == 0, 0.0, pltpu.roll(u, 1, axis=0))
    nxt = jnp.where(row == s - 1, 0.0, pltpu.roll(u, s - 1, axis=0))
    w = w_ref[...]
    o_ref[...] = prev * w[0:1] + u * w[1:2] + nxt * w[2:3] + b_ref[...]


def short_conv(u, conv_w, conv_b, layer):
    B, S, C3 = u.shape
    cb = LANES
    per = HY_WIDTH // cb
    return pl.pallas_call(
        _shortconv_kernel,
        grid=(B, C3 // cb),
        in_specs=[
            pl.BlockSpec((None, S, cb), lambda b, j: (b, 0, j)),
            pl.BlockSpec((None, 3, cb), lambda b, j: (layer, 0, j)),
            pl.BlockSpec((None, 1, cb), lambda b, j: (layer, 0, j)),
        ],
        out_specs=pl.BlockSpec((None, None, None, S, cb), lambda b, j: (j // per, b, j % per, 0, 0)),
        out_shape=jax.ShapeDtypeStruct((3, B, per, S, cb), F32),
        compiler_params=_cparams(("arbitrary", "arbitrary")),
        name="short_conv",
    )(u, conv_w, conv_b)


def _filter_kernel(p_ref, w1_ref, b1_ref, w2_ref, b2_ref, w3_ref, b3_ref, w4_ref, fr_ref, dl_ref,
                   k_ref, l1_ref, *, rows):
    i = pl.program_id(0)
    p = p_ref[...]
    fr = fr_ref[...]
    dot = functools.partial(jnp.dot, preferred_element_type=F32, precision=HIGHEST)
    h = jnp.sin(fr * (dot(p, w1_ref[...]) + b1_ref[...]))
    h = jnp.sin(fr * (dot(h, w2_ref[...]) + b2_ref[...]))
    h = jnp.sin(fr * (dot(h, w3_ref[...]) + b3_ref[...]))
    h = dot(h, w4_ref[...])
    t = p[:, 0:1]
    win = jnp.exp(-t * dl_ref[...])
    win = jnp.concatenate([win] * HY_ORDER, axis=1)
    grow = i * rows + lax.broadcasted_iota(jnp.int32, (rows, 1), 0)
    k = jnp.where(grow == SEQ, 0.0, h * win)
    for s in range(k_ref.shape[0]):
        k_ref[s] = k[:, s * LANES:(s + 1) * LANES]

    @pl.when(i == 0)
    def _():
        l1_ref[...] = jnp.zeros_like(l1_ref)
    l1_ref[...] += jnp.sum(jnp.abs(k), axis=0, keepdims=True)


def hyena_filter(tabs, w1p, b1, w2, b2, w3, b3, w4, freq, layer):
    rows = 1024
    n = 2 * SEQ
    half = SEQ // rows
    H = HY_FILT_HIDDEN
    OC = HY_ORDER * HY_WIDTH
    vec = lambda: pl.BlockSpec((None, 1, H), lambda i: (layer, 0, 0))
    return pl.pallas_call(
        functools.partial(_filter_kernel, rows=rows),
        grid=(n // rows,),
        in_specs=[
            pl.BlockSpec((rows, HY_EMB_PAD), lambda i: (i, 0)),
            pl.BlockSpec((None, HY_EMB_PAD, H), lambda i: (layer, 0, 0)), vec(),
            pl.BlockSpec((None, H, H), lambda i: (layer, 0, 0)), vec(),
            pl.BlockSpec((None, H, H), lambda i: (layer, 0, 0)), vec(),
            pl.BlockSpec((None, H, OC), lambda i: (layer, 0, i // half)),
            vec(),
            _const_spec((1, HY_WIDTH)),
        ],
        out_specs=[pl.BlockSpec((OC // LANES, rows, LANES), lambda i: (0, i, 0)), _const_spec((1, OC))],
        out_shape=[jax.ShapeDtypeStruct((OC // LANES, n, LANES), F32),
                   jax.ShapeDtypeStruct((1, OC), F32)],
        compiler_params=_cparams(("arbitrary",)),
        name="hyena_filter",
    )(tabs["feats2"], w1p, b1, w2, b2, w3, b3, w4, freq, tabs["deltas"])


def _load_sub(ref, lead, j):
    *outer, r, s, l = ref.shape
    flat = ref.reshape(*outer, r * s, l)
    return flat[(*lead, pl.ds(j, r, stride=s), slice(None))]


def _store_sub(ref, lead, j, val):
    *outer, r, s, l = ref.shape
    flat = ref.reshape(*outer, r * s, l)
    flat[(*lead, pl.ds(j, r, stride=s), slice(None))] = val


def _pack_pair(re, im):
    r = lax.bitcast_convert_type(re.astype(BF16).astype(F32), jnp.uint32)
    i = lax.bitcast_convert_type(im.astype(BF16).astype(F32), jnp.uint32)
    return r | (i >> 16)


def _unpack_pair(p):
    re = lax.bitcast_convert_type(p & jnp.uint32(0xFFFF0000), F32)
    im = lax.bitcast_convert_type(p << 16, F32)
    return re, im


def _dft1_kernel(x_ref, f_ref, o_ref):
    for j in range(FFT_N2_BLOCK):
        xs = jnp.concatenate(
            [jnp.concatenate([_load_sub(x_ref, (p, c), j) for c in range(HY_SLABS)], axis=1)
             for p in range(x_ref.shape[0])], axis=0)
        a = jnp.dot(f_ref[...], xs.astype(BF16), preferred_element_type=F32)
        p = _pack_pair(a[:FFT_N1], a[FFT_N1:])
        for c in range(HY_SLABS):
            _store_sub(o_ref, (c,), j, p[:, c * LANES:(c + 1) * LANES])


def dft_first_axis(x6, which, f1, pair):
    _, B, _, rows, _, _ = x6.shape
    nb = FFT_N2_BLOCK
    return pl.pallas_call(
        _dft1_kernel,
        grid=(B // pair, FFT_N2 // nb),
        in_specs=[pl.BlockSpec((None, pair, HY_SLABS, rows, nb, LANES),
                               lambda b, j: (which, b, 0, 0, j, 0)),
                  _const_spec((2 * FFT_N1, pair * rows))],
        out_specs=pl.BlockSpec((None, HY_SLABS, FFT_N1, nb, LANES), lambda b, j: (b, 0, 0, j, 0)),
        out_shape=jax.ShapeDtypeStruct((B // pair, HY_SLABS, FFT_N1, FFT_N2, LANES), jnp.uint32),
        compiler_params=_cparams(("arbitrary", "arbitrary")),
        name="dft_first_axis",
    )(x6, f1)


def _twiddled_stack(f2r_ref, f2i_ref, twr_ref, twi_ref, k1):
    tr = twr_ref[pl.ds(k1, 1), :]
    ti = twi_ref[pl.ds(k1, 1), :]
    fr = f2r_ref[...]
    fi = f2i_ref[...]
    p = fr * tr - fi * ti
    q = fr * ti + fi * tr
    top = jnp.concatenate([p, -q], axis=1)
    bot = jnp.concatenate([q, p], axis=1)
    return jnp.concatenate([top, bot], axis=0).astype(BF16)


def _load_k1(a_ref, j):
    p = jnp.concatenate([a_ref[c, j] for c in range(HY_SLABS)], axis=1)
    re, im = _unpack_pair(p)
    return jnp.concatenate([re, im], axis=0).astype(BF16)


def _spectrum_kernel(a_ref, l1_ref, f2r_ref, f2i_ref, twr_ref, twi_ref, o_ref):
    base = pl.program_id(0) * FFT_K1_GROUP
    inv = 1.0 / l1_ref[...]
    for j in range(FFT_K1_GROUP):
        r = _twiddled_stack(f2r_ref, f2i_ref, twr_ref, twi_ref, base + j)
        x = jnp.dot(r, _load_k1(a_ref, j), preferred_element_type=F32) * inv
        o_ref[j, 0] = x[:FFT_N2].astype(o_ref.dtype)
        o_ref[j, 1] = x[FFT_N2:].astype(o_ref.dtype)


def filter_spectrum(a5, l1, tabs):
    g = FFT_K1_GROUP
    sq = lambda: _const_spec((FFT_N2, FFT_N2))
    return pl.pallas_call(
        _spectrum_kernel,
        grid=(FFT_N1 // g, HY_ORDER),
        in_specs=[pl.BlockSpec((None, HY_SLABS, g, FFT_N2, LANES), lambda i, o: (o, 0, i, 0, 0)),
                  pl.BlockSpec((1, HY_WIDTH), lambda i, o: (0, o)), sq(), sq(), sq(), sq()],
        out_specs=pl.BlockSpec((g, 2, FFT_N2, HY_WIDTH), lambda i, o: (i, 0, 0, o)),
        out_shape=jax.ShapeDtypeStruct((FFT_N1, 2, FFT_N2, HY_ORDER * HY_WIDTH), BF16),
        compiler_params=_cparams(("arbitrary", "arbitrary")),
        name="filter_spectrum",
    )(a5, l1, tabs["f2r"], tabs["f2i"], tabs["twr"], tabs["twi"])


def _convmid_kernel(a_ref, ks_ref, f2r_ref, f2i_ref, twr_ref, twi_ref, o_ref):
    base = pl.program_id(0) * FFT_K1_GROUP
    for j in range(FFT_K1_GROUP):
        r = _twiddled_stack(f2r_ref, f2i_ref, twr_ref, twi_ref, base + j)
        x = jnp.dot(r, _load_k1(a_ref, j), preferred_element_type=F32)
        xr, xi = x[:FFT_N2], x[FFT_N2:]
        kr, ki = ks_ref[j, 0].astype(F32), ks_ref[j, 1].astype(F32)
        y = jnp.concatenate([xr * kr - xi * ki, xr * ki + xi * kr], axis=0).astype(BF16)
        b = lax.dot_general(r, y, (((0,), (0,)), ((), ())), preferred_element_type=F32)
        p = _pack_pair(b[:FFT_N2], b[FFT_N2:])
        for c in range(HY_SLABS):
            o_ref[c, j] = p[:, c * LANES:(c + 1) * LANES]


def conv_mid(a5, kspec, order, tabs):
    B = a5.shape[0]
    g = FFT_K1_GROUP
    sq = lambda: _const_spec((FFT_N2, FFT_N2))
    blk = lambda: pl.BlockSpec((None, HY_SLABS, g, FFT_N2, LANES), lambda i, b: (b, 0, i, 0, 0))
    return pl.pallas_call(
        _convmid_kernel,
        grid=(FFT_N1 // g, B),
        in_specs=[blk(), pl.BlockSpec((g, 2, FFT_N2, HY_WIDTH), lambda i, b: (i, 0, 0, order)),
                  sq(), sq(), sq(), sq()],
        out_specs=blk(),
        out_shape=jax.ShapeDtypeStruct(a5.shape, jnp.uint32),
        compiler_params=_cparams(("arbitrary", "arbitrary")),
        name="conv_mid",
    )(a5, kspec, tabs["f2r"], tabs["f2i"], tabs["twr"], tabs["twi"])


def _convout_kernel(b_ref, g_ref, z_ref, gate_ref, bias_ref, o_ref):
    for j in range(FFT_N2_BLOCK):
        p = jnp.concatenate([_load_sub(b_ref, (c,), j) for c in range(HY_SLABS)], axis=1)
        re, im = _unpack_pair(p)
        bp = jnp.concatenate([re, im], axis=0).astype(BF16)
        y = jnp.dot(g_ref[...], bp, preferred_element_type=F32)
        for q in range(2):
            for c in range(HY_SLABS):
                _store_sub(o_ref, (q, c), j, y[q * FFT_H1:(q + 1) * FFT_H1, c * LANES:(c + 1) * LANES])
    o_ref[...] = gate_ref[...] * (o_ref[...] + z_ref[...] * bias_ref[...])


def conv_out(bp5, z6, z_which, gate6, gate_which, bias4, g):
    B2 = bp5.shape[0]
    nb = FFT_N2_BLOCK
    nat = lambda which: pl.BlockSpec((None, 2, HY_SLABS, FFT_H1, nb, LANES),
                                     lambda b, j: (which, b, 0, 0, j, 0))
    return pl.pallas_call(
        _convout_kernel,
        grid=(B2, FFT_N2 // nb),
        in_specs=[pl.BlockSpec((None, HY_SLABS, FFT_N1, nb, LANES), lambda b, j: (b, 0, 0, j, 0)),
                  _const_spec((2 * FFT_H1, 2 * FFT_N1)), nat(z_which), nat(gate_which),
                  _const_spec((HY_SLABS, 1, 1, LANES))],
        out_specs=pl.BlockSpec((2, HY_SLABS, FFT_H1, nb, LANES), lambda b, j: (b, 0, 0, j, 0)),
        out_shape=jax.ShapeDtypeStruct((2 * B2, HY_SLABS, FFT_H1, FFT_N2, LANES), F32),
        compiler_params=_cparams(("arbitrary", "arbitrary")),
        name="conv_out",
    )(bp5, g, z6, gate6, bias4)


def hyena_mixer(xs, kspec, bias, tabs):
    B = xs.shape[1]
    xs6 = xs.reshape(3, B, HY_SLABS, FFT_H1, FFT_N2, LANES)
    z6, z_which = xs6, 2
    for o in range(HY_ORDER):
        a = dft_first_axis(z6, z_which, tabs["f1c"], pair=2)
        bp = conv_mid(a, kspec, o, tabs)
        z = conv_out(bp, z6, z_which, xs6, o, bias[o].reshape(HY_SLABS, 1, 1, LANES), tabs["gc"])
        z6, z_which = z[None], 0
    return z.reshape(B, HY_SLABS, SEQ, LANES)


def _rope(x, cc, ss):
    lane = lax.broadcasted_iota(jnp.int32, x.shape, 1)
    n = x.shape[1]
    half = RET_DK // 2
    swapped = jnp.where(lane % RET_DK < half, pltpu.roll(x, n - half, axis=1),
                        pltpu.roll(x, half, axis=1))
    return x * cc + swapped * ss


def _retstate_kernel(k_ref, v_ref, cc_ref, ss_ref, wk_ref, cd_ref, o_ref, s_ref):
    d = pl.program_id(1)
    g = pl.program_id(2)
    C = RET_CHUNK

    @pl.when(g == 0)
    def _():
        s_ref[...] = jnp.zeros_like(s_ref)

    for j in range(RET_GROUP):
        ci = jnp.where(d == 0, j, RET_GROUP - 1 - j)
        r0 = pl.multiple_of(ci * C, C)
        s = s_ref[...]
        o_ref[ci] = jnp.concatenate(
            [s[h * RET_DK:(h + 1) * RET_DK, h * RET_DV:(h + 1) * RET_DV] for h in range(RET_HEADS)],
            axis=1)
        k = _rope(k_ref[pl.ds(r0, C), :], cc_ref[pl.ds(r0, C), :], ss_ref[pl.ds(r0, C), :])
        kw = (k * wk_ref[...]).astype(BF16)
        ds = lax.dot_general(kw, v_ref[pl.ds(r0, C), :], (((0,), (0,)), ((), ())),
                             preferred_element_type=F32)
        s_ref[...] = s * cd_ref[...] + ds


def retention_states(qk, v, tabs):
    B, S, _ = qk.shape
    rows = RET_GROUP * RET_CHUNK
    G = S // rows
    grp = lambda d, g: jnp.where(d == 0, g, G - 1 - g)
    return pl.pallas_call(
        _retstate_kernel,
        grid=(B, 2, G),
        in_specs=[
            pl.BlockSpec((None, rows, RET_QK), lambda b, d, g: (b, grp(d, g), 1)),
            pl.BlockSpec((None, rows, RET_V), lambda b, d, g: (b, grp(d, g), 0)),
            pl.BlockSpec((rows, RET_QK), lambda b, d, g: (grp(d, g), 0)),
            pl.BlockSpec((rows, RET_QK), lambda b, d, g: (grp(d, g), 0)),
            pl.BlockSpec((None, RET_CHUNK, RET_QK), lambda b, d, g: (d, 0, 0)),
            pl.BlockSpec((None, RET_QK, RET_V), lambda b, d, g: (d, 0, 0)),
        ],
        out_specs=pl.BlockSpec((None, None, RET_GROUP, RET_DK, RET_V),
                               lambda b, d, g: (b, d, grp(d, g), 0, 0)),
        out_shape=jax.ShapeDtypeStruct((B, 2, S // RET_CHUNK, RET_DK, RET_V), F32),
        scratch_shapes=[pltpu.VMEM((RET_QK, RET_V), F32)],
        compiler_params=_cparams(("arbitrary", "arbitrary", "arbitrary")),
        name="retention_states",
    )(qk, v, tabs["cc"], tabs["ss"], tabs["wk"], tabs["cd"])


def _retout_kernel(qk_ref, v_ref, gr_ref, cc_ref, ss_ref, st_ref, dec_ref, wq_ref, wk_ref, bd_ref,
                   o_ref):
    C = RET_CHUNK
    bd = bd_ref[...]
    lane = lax.broadcasted_iota(jnp.int32, (C, RET_QK), 1)
    for j in range(RET_GROUP):
        r0 = j * C
        cc = cc_ref[r0:r0 + C, :]
        ss = ss_ref[r0:r0 + C, :]
        q = _rope(qk_ref[r0:r0 + C, :RET_QK], cc, ss)
        k = (_rope(qk_ref[r0:r0 + C, RET_QK:], cc, ss) * (RET_DK ** -0.5)).astype(BF16)
        v = v_ref[r0:r0 + C, :]
        inner = []
        for h in range(RET_HEADS):
            qh = jnp.where(lane // RET_DK == h, q, 0.0).astype(BF16)
            s = lax.dot_general(qh, k, (((1,), (1,)), ((), ())), preferred_element_type=F32)
            s = (s * dec_ref[h]).astype(BF16)
            inner.append(jnp.dot(s, v[:, h * RET_DV:(h + 1) * RET_DV], preferred_element_type=F32))
        qq = jnp.concatenate([q * wq_ref[0], q * wq_ref[1]], axis=1).astype(BF16)
        sf = jnp.concatenate([st_ref[0, j]] * RET_HEADS, axis=0) * bd
        sb = jnp.concatenate([st_ref[1, j]] * RET_HEADS, axis=0) * bd
        sbd = jnp.concatenate([sf, sb], axis=0).astype(BF16)
        o = jnp.concatenate(inner, axis=1) + jnp.dot(qq, sbd, preferred_element_type=F32)
        outs = []
        for h in range(RET_HEADS):
            oh = o[:, h * RET_DV:(h + 1) * RET_DV]
            outs.append(oh * lax.rsqrt(jnp.mean(oh * oh, axis=-1, keepdims=True) + EPS))
        gr = gr_ref[r0:r0 + C, :]
        o_ref[r0:r0 + C, :] = (gr * jax.nn.sigmoid(gr)) * jnp.concatenate(outs, axis=1)


def retention_out(qk, v, gr, states, tabs):
    B, S, _ = qk.shape
    rows = RET_GROUP * RET_CHUNK
    G = S // rows
    return pl.pallas_call(
        _retout_kernel,
        grid=(B, G),
        in_specs=[
            pl.BlockSpec((None, rows, 2 * RET_QK), lambda b, g: (b, g, 0)),
            pl.BlockSpec((None, rows, RET_V), lambda b, g: (b, g, 0)),
            pl.BlockSpec((None, rows, RET_V), lambda b, g: (b, g, 0)),
            pl.BlockSpec((rows, RET_QK), lambda b, g: (g, 0)),
            pl.BlockSpec((rows, RET_QK), lambda b, g: (g, 0)),
            pl.BlockSpec((None, 2, RET_GROUP, RET_DK, RET_V), lambda b, g: (b, 0, g, 0, 0)),
            _const_spec((RET_HEADS, RET_CHUNK, RET_CHUNK)),
            _const_spec((2, RET_CHUNK, RET_QK)),
            _const_spec((2, RET_CHUNK, RET_QK)),
            _const_spec((RET_QK, RET_V)),
        ],
        out_specs=pl.BlockSpec((None, rows, RET_V), lambda b, g: (b, g, 0)),
        out_shape=jax.ShapeDtypeStruct((B, S, RET_V), F32),
        compiler_params=_cparams(("arbitrary", "arbitrary")),
        name="retention_out",
    )(qk, v, gr, tabs["cc"], tabs["ss"], states, tabs["dec"], tabs["wq"], tabs["wk"], tabs["bd"])


def _merge_kernel(x_ref, yh_ref, yr_ref, gm_ref, ga_ref, whb_ref, wrb_ref, wo_ref,
                  g_ref, sh_ref, sc_ref, wr_ref, br_ref, xo_ref, h_ref, rt_ref):
    gm = gm_ref[...]
    ghy = 0.5 * jnp.tanh(0.5 * gm[:, :D_MODEL]) + 0.5
    grt = 0.5 * jnp.tanh(0.5 * gm[:, D_MODEL:]) + 0.5
    yh = jnp.concatenate([yh_ref[c] for c in range(HY_SLABS)], axis=1)
    m = (ghy * jnp.dot(yh.astype(BF16), whb_ref[...], preferred_element_type=F32)
         + grt * jnp.dot(yr_ref[...].astype(BF16), wrb_ref[...], preferred_element_type=F32))
    out = jnp.dot(m.astype(BF16), wo_ref[...], preferred_element_type=F32)
    x = x_ref[...] + ga_ref[...] * out
    xo_ref[...] = x
    h = _norm_mod(x, g_ref[...], sc_ref[...], sh_ref[...])
    for j in range(ROW_TILES):
        _store_sub(h_ref, (), j, h[:, j * LANES:(j + 1) * LANES])
    h_hi = h.astype(BF16)
    h_lo = (h - h_hi.astype(F32)).astype(BF16)
    logits = (jnp.dot(h_hi, wr_ref[0], preferred_element_type=F32)
              + jnp.dot(h_hi, wr_ref[1], preferred_element_type=F32)
              + jnp.dot(h_lo, wr_ref[0], preferred_element_type=F32)) + br_ref[...]
    tm = logits.shape[0]
    lane_e = lax.broadcasted_iota(jnp.int32, logits.shape, 1).astype(F32)
    lane_o = lax.broadcasted_iota(jnp.int32, (tm, LANES), 1)
    route = jnp.zeros((tm, LANES), F32)
    vals = []
    work = logits
    for r in range(TOP_K):
        m_r = jnp.max(work, axis=-1, keepdims=True)
        i_r = jnp.min(jnp.where(work == m_r, lane_e, float(N_EXPERTS)), axis=-1, keepdims=True)
        work = jnp.where(lane_e == i_r, -jnp.inf, work)
        vals.append(m_r)
        route = jnp.where(lane_o == r, i_r, route)
    exps = [jnp.exp(v - vals[0]) for v in vals]
    den = exps[0] + exps[1] + exps[2] + exps[3]
    for r in range(TOP_K):
        route = jnp.where(lane_o == TOP_K + r, exps[r] / den, route)
    rt_ref[...] = route


def merge_and_route(x2, yh, yr, gm, mod, g_ffn, whb, wrb, wo, w_router, b_router, layer):
    T, D = x2.shape
    tm = TOK_TILE
    per_b = SEQ // tm
    mspec = lambda piece: pl.BlockSpec((None, None, None, 1, D),
                                       lambda i: (layer, piece, i // per_b, 0, 0))
    row = lambda w: pl.BlockSpec((tm, w), lambda i: (i, 0))
    wsp = lambda a, b: pl.BlockSpec((None, a, b), lambda i: (layer, 0, 0))
    return pl.pallas_call(
        _merge_kernel,
        grid=(T // tm,),
        in_specs=[row(D),
                  pl.BlockSpec((None, HY_SLABS, tm, LANES), lambda i: (i // per_b, 0, i % per_b, 0)),
                  row(RET_V), row(2 * D), mspec(2),
                  wsp(HY_WIDTH, D), wsp(RET_V, D), wsp(D, D),
                  wsp(1, D), mspec(3), mspec(4),
                  pl.BlockSpec((None, 2, D, N_EXPERTS), lambda i: (layer, 0, 0, 0)),
                  wsp(1, N_EXPERTS)],
        out_specs=[row(D), pl.BlockSpec((tm, ROW_TILES, LANES), lambda i: (i, 0, 0)), row(LANES)],
        out_shape=[jax.ShapeDtypeStruct((T, D), F32), jax.ShapeDtypeStruct((T, ROW_TILES, LANES), F32),
                   jax.ShapeDtypeStruct((T, LANES), F32)],
        compiler_params=_cparams(("arbitrary",)),
        name="merge_and_route",
    )(x2, yh, yr, gm, mod, whb, wrb, wo, g_ffn, mod, mod, w_router, b_router)


def _rank_kernel(rt_ref, tril_ref, o_ref, cnt_ref, carry_ref):
    @pl.when(pl.program_id(0) == 0)
    def _():
        carry_ref[...] = jnp.zeros_like(carry_ref)

    rt = rt_ref[...]
    tm = rt.shape[0]
    lane_e = lax.broadcasted_iota(jnp.int32, (tm, N_EXPERTS), 1).astype(F32)
    lane_o = lax.broadcasted_iota(jnp.int32, (tm, LANES), 1)
    hits = [lane_e == rt[:, r:r + 1] for r in range(TOP_K)]
    onehot = sum(h.astype(F32) for h in hits)
    before = jnp.dot(tril_ref[...], onehot.astype(BF16), preferred_element_type=F32) + carry_ref[...]
    out = jnp.zeros((tm, LANES), F32)
    for r in range(TOP_K):
        rank_r = jnp.sum(jnp.where(hits[r], before, 0.0), axis=-1, keepdims=True)
        out = jnp.where(lane_o == r, rank_r, out)
    o_ref[...] = out
    carry_ref[...] += jnp.sum(onehot, axis=0, keepdims=True)
    cnt_ref[...] = carry_ref[...]


def assignment_ranks(route):
    T = route.shape[0]
    tm = RANK_TILE
    tril = jnp.asarray(np.tril(np.ones((tm, tm)), -1), BF16)
    return pl.pallas_call(
        _rank_kernel,
        grid=(T // tm,),
        in_specs=[pl.BlockSpec((tm, LANES), lambda i: (i, 0)), _const_spec((tm, tm))],
        out_specs=[pl.BlockSpec((tm, LANES), lambda i: (i, 0)), _const_spec((1, N_EXPERTS))],
        out_shape=[jax.ShapeDtypeStruct((T, LANES), F32),
                   jax.ShapeDtypeStruct((1, N_EXPERTS), F32)],
        scratch_shapes=[pltpu.VMEM((1, N_EXPERTS), F32)],
        compiler_params=_cparams(("arbitrary",)),
        name="assignment_ranks",
    )(route, tril)


def _dispatch_kernel(dest_ref, pe_ref, h_ref, xs_hbm, zeros, sem, zsem):
    tm = DISPATCH_TILE
    t0 = pl.program_id(0) * tm

    @pl.when(pl.program_id(0) == 0)
    def _():
        zeros[...] = jnp.zeros_like(zeros)

        def zero_block(e):
            end = pe_ref[e]
            begin = pe_ref[jnp.maximum(e - 1, 0)]
            nonempty = jnp.logical_or(jnp.logical_and(e == 0, end > 0), end > begin)
            start = pl.multiple_of(jnp.maximum(end - EXPERT_TILE, 0), EXPERT_TILE)
            return nonempty, pltpu.make_async_copy(zeros, xs_hbm.at[pl.ds(start, EXPERT_TILE)], zsem)

        def start_zero(e, carry):
            nonempty, cp = zero_block(e)

            @pl.when(nonempty)
            def _():
                cp.start()
            return carry
        lax.fori_loop(0, N_EXPERTS, start_zero, 0)

        def wait_zero(e, carry):
            nonempty, cp = zero_block(e)

            @pl.when(nonempty)
            def _():
                cp.wait()
            return carry
        lax.fori_loop(0, N_EXPERTS, wait_zero, 0)

    def row_copy(i, d):
        return pltpu.make_async_copy(h_ref.at[i], xs_hbm.at[d], sem)

    def issue(i, carry):
        for r in range(TOP_K):
            row_copy(i, dest_ref[(t0 + i) * TOP_K + r]).start(priority=r % 2)
        return carry
    lax.fori_loop(0, tm, issue, 0, unroll=4)

    def drain(i, carry):
        row_copy(0, 0).wait()
        return carry
    lax.fori_loop(0, tm * TOP_K, drain, 0, unroll=8)


def dispatch(dest, pad_ends, h, n_rows):
    T = h.shape[0]
    tm = DISPATCH_TILE
    grid_spec = pltpu.PrefetchScalarGridSpec(
        num_scalar_prefetch=2,
        grid=(T // tm,),
        in_specs=[pl.BlockSpec((tm, ROW_TILES, LANES), lambda i, d, pe: (i, 0, 0))],
        out_specs=pl.BlockSpec(memory_space=pl.ANY),
        scratch_shapes=[pltpu.VMEM((EXPERT_TILE, ROW_TILES, LANES), h.dtype), pltpu.SemaphoreType.DMA,
                        pltpu.SemaphoreType.DMA],
    )
    return pl.pallas_call(
        _dispatch_kernel,
        grid_spec=grid_spec,
        out_shape=jax.ShapeDtypeStruct((n_rows, ROW_TILES, LANES), h.dtype),
        compiler_params=_cparams(("arbitrary",)),
        name="dispatch",
    )(dest, pad_ends, h)


def _expert_kernel(be_ref, nb_ref, x_ref, wgu_ref, bgu_ref, wd_ref, bd_ref, perm_ref, o_ref,
                   wgu_s, wd_s):
    i = pl.program_id(0)
    prev = be_ref[jnp.maximum(i - 1, 0)]
    new_expert = jnp.logical_or(i == 0, be_ref[i] != prev)
    n_chunk = 2 * D_FF // (2 * LANES)

    @pl.when(new_expert)
    def _():
        for c in range(n_chunk):
            cols = slice(c * 2 * LANES, (c + 1) * 2 * LANES)
            wgu_s[:, cols] = jnp.dot(wgu_ref[:, cols].astype(BF16), perm_ref[...],
                                     preferred_element_type=F32).astype(BF16)
        wd_s[...] = wd_ref[...].astype(BF16)

    @pl.when(i < nb_ref[0])
    def _():
        x = jnp.concatenate([_load_sub(x_ref, (), j) for j in range(ROW_TILES)], axis=1)
        gu = jnp.dot(x.astype(BF16), wgu_s[...], preferred_element_type=F32) + bgu_ref[...]
        acts = []
        for c in range(n_chunk):
            gate = jnp.minimum(gu[:, c * 2 * LANES:c * 2 * LANES + LANES], SWIGLU_LIMIT)
            up = jnp.clip(gu[:, c * 2 * LANES + LANES:(c + 1) * 2 * LANES], -SWIGLU_LIMIT, SWIGLU_LIMIT)
            acts.append((up + 1.0) * (gate * jax.nn.sigmoid(SWIGLU_ALPHA * gate)))
        act = jnp.concatenate(acts, axis=1).astype(BF16)
        y = jnp.dot(act, wd_s[...], preferred_element_type=F32) + bd_ref[...]
        for j in range(ROW_TILES):
            _store_sub(o_ref, (), j, y[:, j * LANES:(j + 1) * LANES])

    @pl.when(i >= nb_ref[0])
    def _():
        o_ref[...] = jnp.zeros_like(o_ref)


def expert_ffn(block_e, n_used, xs, w_gate_up, bgu, w_down, bd, layer):
    P = xs.shape[0]
    D = D_MODEL
    tm = EXPERT_TILE
    perm = np.zeros((2 * LANES, 2 * LANES))
    j = np.arange(LANES)
    perm[2 * j, j] = 1.0
    perm[2 * j + 1, LANES + j] = 1.0
    grid_spec = pltpu.PrefetchScalarGridSpec(
        num_scalar_prefetch=2,
        grid=(P // tm,),
        in_specs=[
            pl.BlockSpec((tm, ROW_TILES, LANES), lambda i, be, nb: (i, 0, 0)),
            pl.BlockSpec((None, None, D, 2 * D_FF), lambda i, be, nb: (layer, be[i], 0, 0)),
            pl.BlockSpec((None, None, 1, 2 * D_FF), lambda i, be, nb: (layer, be[i], 0, 0)),
            pl.BlockSpec((None, None, D_FF, D), lambda i, be, nb: (layer, be[i], 0, 0)),
            pl.BlockSpec((None, None, 1, D), lambda i, be, nb: (layer, be[i], 0, 0)),
            pl.BlockSpec((2 * LANES, 2 * LANES), lambda i, be, nb: (0, 0)),
        ],
        out_specs=pl.BlockSpec((tm, ROW_TILES, LANES), lambda i, be, nb: (i, 0, 0)),
        scratch_shapes=[pltpu.VMEM((D, 2 * D_FF), BF16), pltpu.VMEM((D_FF, D), BF16)],
    )
    return pl.pallas_call(
        _expert_kernel,
        grid_spec=grid_spec,
        out_shape=jax.ShapeDtypeStruct((P, ROW_TILES, LANES), F32),
        compiler_params=_cparams(("arbitrary",)),
        name="expert_ffn",
    )(block_e, n_used, xs, w_gate_up, bgu, w_down, bd, jnp.asarray(perm, BF16))


def _combine_kernel(dest_ref, x_ref, rt_ref, ga_ref, g_ref, yb_hbm, o_ref, buf, sem, *, final):
    tm = COMBINE_TILE
    step = pl.program_id(0)
    slot = step % 2

    def row_copy(sl, i, r, d):
        return pltpu.make_async_copy(yb_hbm.at[d], buf.at[sl, r, i], sem.at[sl])

    def issue_step(st, sl):
        def issue(i, carry):
            for r in range(TOP_K):
                row_copy(sl, i, r, dest_ref[(st * tm + i) * TOP_K + r]).start(priority=r % 2)
            return carry
        lax.fori_loop(0, tm, issue, 0, unroll=4)

    @pl.when(step == 0)
    def _():
        issue_step(0, 0)

    @pl.when(step + 1 < pl.num_programs(0))
    def _():
        issue_step(step + 1, 1 - slot)

    def drain(i, carry):
        row_copy(slot, 0, 0, 0).wait()
        return carry
    lax.fori_loop(0, tm * TOP_K, drain, 0, unroll=8)

    rt = rt_ref[...]
    gates = [jnp.broadcast_to(rt[:, TOP_K + r:TOP_K + r + 1], (tm, LANES)) for r in range(TOP_K)]
    cols = []
    for j in range(ROW_TILES):
        s = _load_sub(buf, (slot, 0), j) * gates[0]
        for r in range(1, TOP_K):
            s = s + _load_sub(buf, (slot, r), j) * gates[r]
        cols.append(s)
    y = jnp.concatenate(cols, axis=1)
    x = x_ref[...] + ga_ref[...] * y
    if final:
        x = (x * lax.rsqrt(jnp.mean(x * x, axis=-1, keepdims=True) + EPS)) * g_ref[...]
    o_ref[...] = x


def combine(dest, x2, route, mod, g_final, yb, layer, final):
    T, D = x2.shape
    tm = COMBINE_TILE
    per_b = SEQ // tm
    grid_spec = pltpu.PrefetchScalarGridSpec(
        num_scalar_prefetch=1,
        grid=(T // tm,),
        in_specs=[
            pl.BlockSpec((tm, D), lambda i, d: (i, 0)),
            pl.BlockSpec((tm, LANES), lambda i, d: (i, 0)),
            pl.BlockSpec((None, None, None, 1, D), lambda i, d: (layer, 5, i // per_b, 0, 0)),
            pl.BlockSpec((1, D), lambda i, d: (0, 0)),
            pl.BlockSpec(memory_space=pl.ANY),
        ],
        out_specs=pl.BlockSpec((tm, D), lambda i, d: (i, 0)),
        scratch_shapes=[pltpu.VMEM((2, TOP_K, tm, ROW_TILES, LANES), F32),
                        pltpu.SemaphoreType.DMA((2,))],
    )
    return pl.pallas_call(
        functools.partial(_combine_kernel, final=final),
        grid_spec=grid_spec,
        out_shape=jax.ShapeDtypeStruct((T, D), F32),
        compiler_params=_cparams(("arbitrary",)),
        name="combine",
    )(dest, x2, route, mod, g_final, yb)


def moe_ffn_residual(x2, h, route, mod, g_final, w_gate_up, bgu, w_down, bd, layer, final):
    T = h.shape[0]
    tm = EXPERT_TILE
    rank, counts = assignment_ranks(route)
    counts = counts[0].astype(jnp.int32)
    padded = ((counts + tm - 1) // tm) * tm
    pad_ends = jnp.cumsum(padded)
    pad_starts = pad_ends - padded
    n_blocks = -(-(T * TOP_K + N_EXPERTS * (tm - 1)) // tm)
    block_start = jnp.arange(n_blocks, dtype=jnp.int32) * tm
    block_e = jnp.minimum(jnp.sum(pad_ends[None, :] <= block_start[:, None], axis=1),
                          N_EXPERTS - 1).astype(jnp.int32)
    n_used = (pad_ends[-1:] // tm).astype(jnp.int32)
    top_idx = route[:, :TOP_K].astype(jnp.int32)
    dest = (pad_starts[top_idx] + rank[:, :TOP_K].astype(jnp.int32)).reshape(-1)
    xs = dispatch(dest, pad_ends.astype(jnp.int32), h, n_blocks * tm)
    yb = expert_ffn(block_e, n_used, xs, w_gate_up, bgu, w_down, bd, layer)
    return combine(dest, x2, route, mod, g_final, yb, layer, final)


def kernel(x, c, norm_mix_g, norm_ffn_g, w_mod, b_mod, w_in, hy_conv_w, hy_conv_b, hy_w1, hy_b1, hy_w2, hy_b2, hy_w3, hy_b3, hy_w4, hy_freq, hy_bias, w_hy_br, w_ret_br, w_out, w_router, b_router, w_gate_up, b_gate_up, w_down, b_down, final_g):
    B, S, D = x.shape
    L = w_mod.shape[0]
    T = B * S
    tabs = _tables()

    w_in_bf = w_in.astype(BF16)
    whb = w_hy_br.astype(BF16)
    wrb = w_ret_br.astype(BF16)
    wo = w_out.astype(BF16)
    wr_hi = w_router.astype(BF16)
    wr_split = jnp.stack([wr_hi, (w_router - wr_hi.astype(F32)).astype(BF16)], axis=1)
    E, F = N_EXPERTS, D_FF
    bgu = b_gate_up.reshape(L, E, F // LANES, LANES, 2).transpose(0, 1, 2, 4, 3).reshape(L, E, 1, 2 * F)
    bdn = b_down.reshape(L, E, 1, D)
    w1p = jnp.pad(hy_w1, ((0, 0), (0, HY_EMB_PAD - HY_EMB), (0, 0)))
    vec = lambda a: a.reshape(L, 1, -1)

    c_pad = jnp.pad(c, ((0, 8 - B), (0, 0)))
    mod = modulation(c_pad, w_mod, b_mod)[:, :B]
    mod = mod.reshape(L, B, N_MOD, 1, D).transpose(0, 2, 1, 3, 4)

    x2 = x.reshape(T, D)
    for l in range(L):
        u, qk, v, gr, gm = in_projection(x2, vec(norm_mix_g), mod, w_in_bf, l)
        xs = short_conv(u.reshape(B, S, -1), hy_conv_w, vec(hy_conv_b), l)
        kraw, l1 = hyena_filter(tabs, w1p, vec(hy_b1), hy_w2, vec(hy_b2), hy_w3, vec(hy_b3),
                                hy_w4, vec(hy_freq), l)
        ka = dft_first_axis(kraw.reshape(1, HY_ORDER, HY_SLABS, FFT_N1, FFT_N2, LANES), 0, tabs["f1"],
                            pair=1)
        kspec = filter_spectrum(ka, l1, tabs)
        y_hy = hyena_mixer(xs, kspec, hy_bias[l], tabs)
        qk3, v3, gr3 = qk.reshape(B, S, -1), v.reshape(B, S, -1), gr.reshape(B, S, -1)
        states = retention_states(qk3, v3, tabs)
        y_ret = retention_out(qk3, v3, gr3, states, tabs)
        x2, h_ffn, route = merge_and_route(
            x2, y_hy, y_ret.reshape(T, -1), gm, mod, vec(norm_ffn_g),
            whb, wrb, wo, wr_split, vec(b_router), l)
        x2 = moe_ffn_residual(x2, h_ffn, route, mod, final_g.reshape(1, D), w_gate_up, bgu,
                              w_down, bdn, l, final=(l == L - 1))
    return x2.reshape(B, S, D)
```

```python
import functools
import math

import numpy as np
import jax
import jax.numpy as jnp
from jax import lax
from jax.experimental import pallas as pl
from jax.experimental.pallas import tpu as pltpu

F32 = jnp.float32
BF16 = jnp.bfloat16
HIGHEST = lax.Precision.HIGHEST

D_MODEL = 1024
BATCH = 4
SEQ = 8192
DEPTH = 4
HY_WIDTH = 512
HY_ORDER = 2
HY_EMB = 33
HY_EMB_PAD = 64
HY_FILT_HIDDEN = 64
HY_FAST_DECAY = 0.3
HY_SLOW_DECAY = 1.5
HY_DECAY_TARGET = 1e-2
RET_HEADS = 4
RET_DK = 64
RET_DV = 128
RET_QK = RET_HEADS * RET_DK
RET_V = RET_HEADS * RET_DV
RET_CHUNK = 128
RET_DECAY_FWD = 5.0
RET_DECAY_BWD = 5.5
ROPE_BASE = 10000.0
N_EXPERTS = 32
TOP_K = 4
D_FF = D_MODEL
SWIGLU_ALPHA = 1.702
SWIGLU_LIMIT = 7.0
N_MOD = 6
EPS = 1e-6
IN_COLS = 3 * HY_WIDTH + 2 * RET_QK + 2 * RET_V + 2 * D_MODEL

LANES = 128
HY_SLABS = HY_WIDTH // LANES
ROW_TILES = D_MODEL // LANES
VMEM_LIMIT = 56 * 1024 * 1024

FFT_N = 2 * SEQ
FFT_N1 = 128
FFT_N2 = FFT_N // FFT_N1
FFT_H1 = FFT_N1 // 2

TOK_TILE = 256
MERGE_HALVES = 2
EXPERT_TILE = 512
RANK_TILE = 512
DISPATCH_TILE = 512
COMBINE_TILE = 256
RET_GROUP = 8
FFT_N2_BLOCK = 8
FFT_K1_GROUP = 8


def _cparams(sem):
    return pltpu.CompilerParams(dimension_semantics=sem, vmem_limit_bytes=VMEM_LIMIT)


def _const_spec(shape):
    return pl.BlockSpec(shape, lambda *_: (0,) * len(shape))


def _tables():
    n1 = np.arange(FFT_N1)
    n2 = np.arange(FFT_N2)
    ang1 = 2.0 * np.pi * np.outer(n1, n1) / FFT_N1
    f1 = np.concatenate([np.cos(ang1), -np.sin(ang1)], axis=0)
    ang2 = 2.0 * np.pi * np.outer(n2, n2) / FFT_N2
    angt = 2.0 * np.pi * np.outer(n1, n2) / FFT_N
    f1r, f1i = np.cos(ang1)[:, :FFT_H1], -np.sin(ang1)[:, :FFT_H1]
    f1c = np.block([[f1r, -f1i], [f1i, f1r]])
    gr, gi = np.cos(ang1)[:FFT_H1] / FFT_N, np.sin(ang1)[:FFT_H1] / FFT_N
    gc = np.block([[gr, -gi], [gi, gr]])

    L = SEQ
    t = np.linspace(0.0, 1.0, L)
    bands = (HY_EMB - 1) // 2
    w = 2.0 * np.pi * np.arange(L) / L
    f = np.linspace(1e-4, bands - 1, bands)
    feats = np.concatenate([t[:, None], np.cos(f[None] * w[:, None]), -np.sin(f[None] * w[:, None])], -1)
    idx = np.concatenate([np.arange(L), [0], L - np.arange(1, L)])
    feats2 = np.zeros((2 * L, HY_EMB_PAD))
    feats2[:, :HY_EMB] = feats[idx]
    max_decay = math.log(HY_DECAY_TARGET) / HY_FAST_DECAY
    min_decay = math.log(HY_DECAY_TARGET) / HY_SLOW_DECAY
    deltas = np.abs(np.linspace(min_decay, max_decay, HY_WIDTH))

    C = RET_CHUNK
    hh = np.arange(RET_HEADS)
    lgf = np.log(1.0 - np.exp2(-(RET_DECAY_FWD + hh)))
    lgb = np.log(1.0 - np.exp2(-(RET_DECAY_BWD + hh)))
    pos = np.arange(C)
    diff = pos[:, None] - pos[None, :]
    dec = np.where(diff[None] >= 0, np.exp(np.maximum(diff, 0)[None] * lgf[:, None, None]),
                   np.exp(np.maximum(-diff, 0)[None] * lgb[:, None, None]))
    lane_h = np.repeat(hh, RET_DK)
    wq = np.stack([np.exp((pos[:, None] + 1.0) * lgf[lane_h][None]),
                   np.exp((C - pos[:, None]) * lgb[lane_h][None])])
    wk = np.stack([np.exp((C - 1.0 - pos[:, None]) * lgf[lane_h][None]),
                   np.exp(pos[:, None] * lgb[lane_h][None])]) * (RET_DK ** -0.5)
    cd = np.stack([np.broadcast_to(np.exp(C * lgf[lane_h])[:, None], (RET_QK, RET_V)),
                   np.broadcast_to(np.exp(C * lgb[lane_h])[:, None], (RET_QK, RET_V))])
    col_h = np.repeat(hh, RET_DV)
    bd = (lane_h[:, None] == col_h[None, :]).astype(np.float64)
    inv_freq = 1.0 / (ROPE_BASE ** (np.arange(0, RET_DK, 2) / RET_DK))
    ang = np.arange(SEQ)[:, None] * inv_freq[None, :]
    cc = np.tile(np.concatenate([np.cos(ang), np.cos(ang)], -1), (1, RET_HEADS))
    ss = np.tile(np.concatenate([-np.sin(ang), np.sin(ang)], -1), (1, RET_HEADS))

    return dict(
        f1=jnp.asarray(f1, BF16), f1c=jnp.asarray(f1c, BF16),
        f2r=jnp.asarray(np.cos(ang2), F32), f2i=jnp.asarray(-np.sin(ang2), F32),
        twr=jnp.asarray(np.cos(angt), F32), twi=jnp.asarray(-np.sin(angt), F32),
        gc=jnp.asarray(gc, BF16),
        feats2=jnp.asarray(feats2, F32), deltas=jnp.asarray(deltas[None], F32),
        dec=jnp.asarray(dec, F32), wq=jnp.asarray(wq, F32), wk=jnp.asarray(wk, F32),
        cd=jnp.asarray(cd, F32), bd=jnp.asarray(bd, F32),
        cc=jnp.asarray(cc, F32), ss=jnp.asarray(ss, F32),
    )


def _mod_kernel(c_ref, w_ref, b_ref, o_ref):
    c = c_ref[...]
    ca = c * jax.nn.sigmoid(c)
    o_ref[...] = jnp.dot(ca, w_ref[...], preferred_element_type=F32, precision=HIGHEST) + b_ref[...]


def modulation(c_pad, w_mod, b_mod):
    L, D, N = w_mod.shape
    tn = 1536
    rows = c_pad.shape[0]
    return pl.pallas_call(
        _mod_kernel,
        grid=(L, N // tn),
        in_specs=[
            _const_spec((rows, D)),
            pl.BlockSpec((None, D, tn), lambda l, j: (l, 0, j)),
            pl.BlockSpec((None, 1, tn), lambda l, j: (l, 0, j)),
        ],
        out_specs=pl.BlockSpec((None, rows, tn), lambda l, j: (l, 0, j)),
        out_shape=jax.ShapeDtypeStruct((L, rows, N), F32),
        compiler_params=_cparams(("arbitrary", "arbitrary")),
        name="modulation",
    )(c_pad, w_mod, b_mod.reshape(L, 1, N))


def _norm_mod(x, g, sc, sh):
    y = x * lax.rsqrt(jnp.mean(x * x, axis=-1, keepdims=True) + EPS)
    return (y * g) * (1.0 + sc) + sh


def _inproj_kernel(x_ref, g_ref, sh_ref, sc_ref, w_ref, u_ref, qk_ref, v_ref, gr_ref, gm_ref):
    h = _norm_mod(x_ref[...], g_ref[...], sc_ref[...], sh_ref[...]).astype(BF16)
    c0 = 0
    for o_ref in (u_ref, qk_ref, v_ref, gr_ref, gm_ref):
        c1 = c0 + o_ref.shape[-1]
        o_ref[...] = jnp.dot(h, w_ref[:, c0:c1], preferred_element_type=F32).astype(o_ref.dtype)
        c0 = c1


def in_projection(x2, g, mod, w_in_bf, layer):
    T, D = x2.shape
    tm = TOK_TILE
    per_b = SEQ // tm
    mspec = lambda piece: pl.BlockSpec((None, None, None, 1, D),
                                       lambda i: (layer, piece, i // per_b, 0, 0))
    widths = (3 * HY_WIDTH, 2 * RET_QK, RET_V, RET_V, 2 * D_MODEL)
    dtypes = (BF16, F32, BF16, F32, F32)
    return pl.pallas_call(
        _inproj_kernel,
        grid=(T // tm,),
        in_specs=[
            pl.BlockSpec((tm, D), lambda i: (i, 0)),
            pl.BlockSpec((None, 1, D), lambda i: (layer, 0, 0)),
            mspec(0), mspec(1),
            pl.BlockSpec((None, D, IN_COLS), lambda i: (layer, 0, 0)),
        ],
        out_specs=[pl.BlockSpec((tm, w), lambda i: (i, 0)) for w in widths],
        out_shape=[jax.ShapeDtypeStruct((T, w), dt) for w, dt in zip(widths, dtypes)],
        compiler_params=_cparams(("arbitrary",)),
        name="in_projection",
    )(x2, g, mod, mod, w_in_bf)


def _shortconv_kernel(u_ref, w_ref, b_ref, o_ref):
    u = u_ref[...].astype(F32)
    s = u.shape[0]
    row = lax.broadcasted_iota(jnp.int32, u.shape, 0)
    prev = jnp.where(row == 0, 0.0, pltpu.roll(u, 1, axis=0))
    nxt = jnp.where(row == s - 1, 0.0, pltpu.roll(u, s - 1, axis=0))
    w = w_ref[...]
    o_ref[...] = prev * w[0:1] + u * w[1:2] + nxt * w[2:3] + b_ref[...]


def short_conv(u, conv_w, conv_b, layer):
    B, S, C3 = u.shape
    cb = LANES
    per = HY_WIDTH // cb
    return pl.pallas_call(
        _shortconv_kernel,
        grid=(B, C3 // cb),
        in_specs=[
            pl.BlockSpec((None, S, cb), lambda b, j: (b, 0, j)),
            pl.BlockSpec((None, 3, cb), lambda b, j: (layer, 0, j)),
            pl.BlockSpec((None, 1, cb), lambda b, j: (layer, 0, j)),
        ],
        out_specs=pl.BlockSpec((None, None, None, S, cb), lambda b, j: (j // per, b, j % per, 0, 0)),
        out_shape=jax.ShapeDtypeStruct((3, B, per, S, cb), F32),
        compiler_params=_cparams(("arbitrary", "arbitrary")),
        name="short_conv",
    )(u, conv_w, conv_b)


def _filter_kernel(p_ref, w1_ref, b1_ref, w2_ref, b2_ref, w3_ref, b3_ref, w4_ref, fr_ref, dl_ref,
                   k_ref, l1_ref, *, rows):
    i = pl.program_id(0)
    p = p_ref[...]
    fr = fr_ref[...]
    dot = functools.partial(jnp.dot, preferred_element_type=F32, precision=HIGHEST)
    h = jnp.sin(fr * (dot(p, w1_ref[...]) + b1_ref[...]))
    h = jnp.sin(fr * (dot(h, w2_ref[...]) + b2_ref[...]))
    h = jnp.sin(fr * (dot(h, w3_ref[...]) + b3_ref[...]))
    h_hi = h.astype(BF16)
    h_lo = (h - h_hi.astype(F32)).astype(BF16)
    h = (jnp.dot(h_hi, w4_ref[0], preferred_element_type=F32)
         + jnp.dot(h_hi, w4_ref[1], preferred_element_type=F32)
         + jnp.dot(h_lo, w4_ref[0], preferred_element_type=F32))
    t = p[:, 0:1]
    win = jnp.exp(-t * dl_ref[...])
    win = jnp.concatenate([win] * HY_ORDER, axis=1)
    grow = i * rows + lax.broadcasted_iota(jnp.int32, (rows, 1), 0)
    k = jnp.where(grow == SEQ, 0.0, h * win)
    for s in range(k_ref.shape[0]):
        k_ref[s] = k[:, s * LANES:(s + 1) * LANES]

    @pl.when(i == 0)
    def _():
        l1_ref[...] = jnp.zeros_like(l1_ref)
    l1_ref[...] += jnp.sum(jnp.abs(k), axis=0, keepdims=True)


def hyena_filter(tabs, w1p, b1, w2, b2, w3, b3, w4, freq, layer):
    rows = 1024
    n = 2 * SEQ
    half = SEQ // rows
    H = HY_FILT_HIDDEN
    OC = HY_ORDER * HY_WIDTH
    vec = lambda: pl.BlockSpec((None, 1, H), lambda i: (layer, 0, 0))
    return pl.pallas_call(
        functools.partial(_filter_kernel, rows=rows),
        grid=(n // rows,),
        in_specs=[
            pl.BlockSpec((rows, HY_EMB_PAD), lambda i: (i, 0)),
            pl.BlockSpec((None, HY_EMB_PAD, H), lambda i: (layer, 0, 0)), vec(),
            pl.BlockSpec((None, H, H), lambda i: (layer, 0, 0)), vec(),
            pl.BlockSpec((None, H, H), lambda i: (layer, 0, 0)), vec(),
            pl.BlockSpec((None, 2, H, OC), lambda i: (layer, 0, 0, i // half)),
            vec(),
            _const_spec((1, HY_WIDTH)),
        ],
        out_specs=[pl.BlockSpec((OC // LANES, rows, LANES), lambda i: (0, i, 0)), _const_spec((1, OC))],
        out_shape=[jax.ShapeDtypeStruct((OC // LANES, n, LANES), F32),
                   jax.ShapeDtypeStruct((1, OC), F32)],
        compiler_params=_cparams(("arbitrary",)),
        name="hyena_filter",
    )(tabs["feats2"], w1p, b1, w2, b2, w3, b3, w4, freq, tabs["deltas"])


def _load_sub(ref, lead, j):
    *outer, r, s, l = ref.shape
    flat = ref.reshape(*outer, r * s, l)
    return flat[(*lead, pl.ds(j, r, stride=s), slice(None))]


def _store_sub(ref, lead, j, val):
    *outer, r, s, l = ref.shape
    flat = ref.reshape(*outer, r * s, l)
    flat[(*lead, pl.ds(j, r, stride=s), slice(None))] = val


def _pack_pair(re, im):
    return lax.bitcast_convert_type(pltpu.pack_elementwise([im, re], packed_dtype=BF16), jnp.uint32)


def _unpack_pair(p):
    p = lax.bitcast_convert_type(p, jnp.int32)
    im = pltpu.unpack_elementwise(p, index=0, packed_dtype=BF16, unpacked_dtype=F32)
    re = pltpu.unpack_elementwise(p, index=1, packed_dtype=BF16, unpacked_dtype=F32)
    return re, im


def _dft1_kernel(x_ref, f_ref, o_ref):
    for j in range(FFT_N2_BLOCK):
        xs = jnp.concatenate(
            [jnp.concatenate([_load_sub(x_ref, (p, c), j) for c in range(HY_SLABS)], axis=1)
             for p in range(x_ref.shape[0])], axis=0)
        a = jnp.dot(f_ref[...], xs.astype(BF16), preferred_element_type=F32)
        p = _pack_pair(a[:FFT_N1], a[FFT_N1:])
        for c in range(HY_SLABS):
            _store_sub(o_ref, (c,), j, p[:, c * LANES:(c + 1) * LANES])


def dft_first_axis(x6, which, f1, pair):
    _, B, _, rows, _, _ = x6.shape
    nb = FFT_N2_BLOCK
    return pl.pallas_call(
        _dft1_kernel,
        grid=(B // pair, FFT_N2 // nb),
        in_specs=[pl.BlockSpec((None, pair, HY_SLABS, rows, nb, LANES),
                               lambda b, j: (which, b, 0, 0, j, 0)),
                  _const_spec((2 * FFT_N1, pair * rows))],
        out_specs=pl.BlockSpec((None, HY_SLABS, FFT_N1, nb, LANES), lambda b, j: (b, 0, 0, j, 0)),
        out_shape=jax.ShapeDtypeStruct((B // pair, HY_SLABS, FFT_N1, FFT_N2, LANES), jnp.uint32),
        compiler_params=_cparams(("arbitrary", "arbitrary")),
        name="dft_first_axis",
    )(x6, f1)


def _twiddled_stack(f2r_ref, f2i_ref, twr_ref, twi_ref, k1):
    tr = twr_ref[pl.ds(k1, 1), :]
    ti = twi_ref[pl.ds(k1, 1), :]
    fr = f2r_ref[...]
    fi = f2i_ref[...]
    p = fr * tr - fi * ti
    q = fr * ti + fi * tr
    top = jnp.concatenate([p, -q], axis=1)
    bot = jnp.concatenate([q, p], axis=1)
    return jnp.concatenate([top, bot], axis=0).astype(BF16)


def _load_k1(a_ref, j):
    p = jnp.concatenate([a_ref[c, j] for c in range(HY_SLABS)], axis=1)
    re, im = _unpack_pair(p)
    return jnp.concatenate([re, im], axis=0).astype(BF16)


def _spectrum_kernel(a_ref, l1_ref, f2r_ref, f2i_ref, twr_ref, twi_ref, o_ref):
    base = pl.program_id(0) * FFT_K1_GROUP
    inv = 1.0 / l1_ref[...]
    for j in range(FFT_K1_GROUP):
        r = _twiddled_stack(f2r_ref, f2i_ref, twr_ref, twi_ref, base + j)
        x = jnp.dot(r, _load_k1(a_ref, j), preferred_element_type=F32) * inv
        o_ref[j, 0] = x[:FFT_N2].astype(o_ref.dtype)
        o_ref[j, 1] = x[FFT_N2:].astype(o_ref.dtype)


def filter_spectrum(a5, l1, tabs):
    g = FFT_K1_GROUP
    sq = lambda: _const_spec((FFT_N2, FFT_N2))
    return pl.pallas_call(
        _spectrum_kernel,
        grid=(FFT_N1 // g, HY_ORDER),
        in_specs=[pl.BlockSpec((None, HY_SLABS, g, FFT_N2, LANES), lambda i, o: (o, 0, i, 0, 0)),
                  pl.BlockSpec((1, HY_WIDTH), lambda i, o: (0, o)), sq(), sq(), sq(), sq()],
        out_specs=pl.BlockSpec((g, 2, FFT_N2, HY_WIDTH), lambda i, o: (i, 0, 0, o)),
        out_shape=jax.ShapeDtypeStruct((FFT_N1, 2, FFT_N2, HY_ORDER * HY_WIDTH), BF16),
        compiler_params=_cparams(("arbitrary", "arbitrary")),
        name="filter_spectrum",
    )(a5, l1, tabs["f2r"], tabs["f2i"], tabs["twr"], tabs["twi"])


def _convmid_kernel(a_ref, ks_ref, f2r_ref, f2i_ref, twr_ref, twi_ref, o_ref):
    base = pl.program_id(0) * FFT_K1_GROUP
    for j in range(FFT_K1_GROUP):
        r = _twiddled_stack(f2r_ref, f2i_ref, twr_ref, twi_ref, base + j)
        x = jnp.dot(r, _load_k1(a_ref, j), preferred_element_type=F32)
        xr, xi = x[:FFT_N2], x[FFT_N2:]
        kr, ki = ks_ref[j, 0].astype(F32), ks_ref[j, 1].astype(F32)
        y = jnp.concatenate([xr * kr - xi * ki, xr * ki + xi * kr], axis=0).astype(BF16)
        b = lax.dot_general(r, y, (((0,), (0,)), ((), ())), preferred_element_type=F32)
        p = _pack_pair(b[:FFT_N2], b[FFT_N2:])
        for c in range(HY_SLABS):
            o_ref[c, j] = p[:, c * LANES:(c + 1) * LANES]


def conv_mid(a5, kspec, order, tabs):
    B = a5.shape[0]
    g = FFT_K1_GROUP
    sq = lambda: _const_spec((FFT_N2, FFT_N2))
    blk = lambda: pl.BlockSpec((None, HY_SLABS, g, FFT_N2, LANES), lambda i, b: (b, 0, i, 0, 0))
    return pl.pallas_call(
        _convmid_kernel,
        grid=(FFT_N1 // g, B),
        in_specs=[blk(), pl.BlockSpec((g, 2, FFT_N2, HY_WIDTH), lambda i, b: (i, 0, 0, order)),
                  sq(), sq(), sq(), sq()],
        out_specs=blk(),
        out_shape=jax.ShapeDtypeStruct(a5.shape, jnp.uint32),
        compiler_params=_cparams(("arbitrary", "arbitrary")),
        name="conv_mid",
    )(a5, kspec, tabs["f2r"], tabs["f2i"], tabs["twr"], tabs["twi"])


def _convout_kernel(b_ref, g_ref, z_ref, gate_ref, bias_ref, o_ref):
    for j in range(FFT_N2_BLOCK):
        p = jnp.concatenate([_load_sub(b_ref, (c,), j) for c in range(HY_SLABS)], axis=1)
        re, im = _unpack_pair(p)
        bp = jnp.concatenate([re, im], axis=0).astype(BF16)
        y = jnp.dot(g_ref[...], bp, preferred_element_type=F32)
        for q in range(2):
            for c in range(HY_SLABS):
                _store_sub(o_ref, (q, c), j, y[q * FFT_H1:(q + 1) * FFT_H1, c * LANES:(c + 1) * LANES])
    o_ref[...] = gate_ref[...] * (o_ref[...] + z_ref[...] * bias_ref[...])


def conv_out(bp5, z6, z_which, gate6, gate_which, bias4, g):
    B2 = bp5.shape[0]
    nb = FFT_N2_BLOCK
    nat = lambda which: pl.BlockSpec((None, 2, HY_SLABS, FFT_H1, nb, LANES),
                                     lambda b, j: (which, b, 0, 0, j, 0))
    return pl.pallas_call(
        _convout_kernel,
        grid=(B2, FFT_N2 // nb),
        in_specs=[pl.BlockSpec((None, HY_SLABS, FFT_N1, nb, LANES), lambda b, j: (b, 0, 0, j, 0)),
                  _const_spec((2 * FFT_H1, 2 * FFT_N1)), nat(z_which), nat(gate_which),
                  _const_spec((HY_SLABS, 1, 1, LANES))],
        out_specs=pl.BlockSpec((2, HY_SLABS, FFT_H1, nb, LANES), lambda b, j: (b, 0, 0, j, 0)),
        out_shape=jax.ShapeDtypeStruct((2 * B2, HY_SLABS, FFT_H1, FFT_N2, LANES), F32),
        compiler_params=_cparams(("arbitrary", "arbitrary")),
        name="conv_out",
    )(bp5, g, z6, gate6, bias4)


def hyena_mixer(xs, kspec, bias, tabs):
    B = xs.shape[1]
    xs6 = xs.reshape(3, B, HY_SLABS, FFT_H1, FFT_N2, LANES)
    z6, z_which = xs6, 2
    for o in range(HY_ORDER):
        a = dft_first_axis(z6, z_which, tabs["f1c"], pair=2)
        bp = conv_mid(a, kspec, o, tabs)
        z = conv_out(bp, z6, z_which, xs6, o, bias[o].reshape(HY_SLABS, 1, 1, LANES), tabs["gc"])
        z6, z_which = z[None], 0
    return z.reshape(B, HY_SLABS, SEQ, LANES)


def _rope(x, cc, ss):
    lane = lax.broadcasted_iota(jnp.int32, x.shape, 1)
    n = x.shape[1]
    half = RET_DK // 2
    swapped = jnp.where(lane % RET_DK < half, pltpu.roll(x, n - half, axis=1),
                        pltpu.roll(x, half, axis=1))
    return x * cc + swapped * ss


def _retstate_kernel(k_ref, v_ref, cc_ref, ss_ref, wk_ref, cd_ref, o_ref, s_ref):
    d = pl.program_id(1)
    g = pl.program_id(2)
    C = RET_CHUNK

    @pl.when(g == 0)
    def _():
        s_ref[...] = jnp.zeros_like(s_ref)

    for j in range(RET_GROUP):
        ci = jnp.where(d == 0, j, RET_GROUP - 1 - j)
        r0 = pl.multiple_of(ci * C, C)
        s = s_ref[...]
        o_ref[ci] = jnp.concatenate(
            [s[h * RET_DK:(h + 1) * RET_DK, h * RET_DV:(h + 1) * RET_DV] for h in range(RET_HEADS)],
            axis=1)
        k = _rope(k_ref[pl.ds(r0, C), :], cc_ref[pl.ds(r0, C), :], ss_ref[pl.ds(r0, C), :])
        kw = (k * wk_ref[...]).astype(BF16)
        ds = lax.dot_general(kw, v_ref[pl.ds(r0, C), :], (((0,), (0,)), ((), ())),
                             preferred_element_type=F32)
        s_ref[...] = s * cd_ref[...] + ds


def retention_states(qk, v, tabs):
    B, S, _ = qk.shape
    rows = RET_GROUP * RET_CHUNK
    G = S // rows
    grp = lambda d, g: jnp.where(d == 0, g, G - 1 - g)
    return pl.pallas_call(
        _retstate_kernel,
        grid=(B, 2, G),
        in_specs=[
            pl.BlockSpec((None, rows, RET_QK), lambda b, d, g: (b, grp(d, g), 1)),
            pl.BlockSpec((None, rows, RET_V), lambda b, d, g: (b, grp(d, g), 0)),
            pl.BlockSpec((rows, RET_QK), lambda b, d, g: (grp(d, g), 0)),
            pl.BlockSpec((rows, RET_QK), lambda b, d, g: (grp(d, g), 0)),
            pl.BlockSpec((None, RET_CHUNK, RET_QK), lambda b, d, g: (d, 0, 0)),
            pl.BlockSpec((None, RET_QK, RET_V), lambda b, d, g: (d, 0, 0)),
        ],
        out_specs=pl.BlockSpec((None, None, RET_GROUP, RET_DK, RET_V),
                               lambda b, d, g: (b, d, grp(d, g), 0, 0)),
        out_shape=jax.ShapeDtypeStruct((B, 2, S // RET_CHUNK, RET_DK, RET_V), F32),
        scratch_shapes=[pltpu.VMEM((RET_QK, RET_V), F32)],
        compiler_params=_cparams(("arbitrary", "arbitrary", "arbitrary")),
        name="retention_states",
    )(qk, v, tabs["cc"], tabs["ss"], tabs["wk"], tabs["cd"])


def _retout_kernel(qk_ref, v_ref, gr_ref, cc_ref, ss_ref, st_ref, dec_ref, wq_ref, wk_ref, bd_ref,
                   o_ref):
    C = RET_CHUNK
    bd = bd_ref[...]
    lane = lax.broadcasted_iota(jnp.int32, (C, RET_QK), 1)
    for j in range(RET_GROUP):
        r0 = j * C
        cc = cc_ref[r0:r0 + C, :]
        ss = ss_ref[r0:r0 + C, :]
        q = _rope(qk_ref[r0:r0 + C, :RET_QK], cc, ss)
        k = (_rope(qk_ref[r0:r0 + C, RET_QK:], cc, ss) * (RET_DK ** -0.5)).astype(BF16)
        v = v_ref[r0:r0 + C, :]
        inner = []
        for h in range(RET_HEADS):
            qh = jnp.where(lane // RET_DK == h, q, 0.0).astype(BF16)
            s = lax.dot_general(qh, k, (((1,), (1,)), ((), ())), preferred_element_type=F32)
            s = (s * dec_ref[h]).astype(BF16)
            inner.append(jnp.dot(s, v[:, h * RET_DV:(h + 1) * RET_DV], preferred_element_type=F32))
        qq = jnp.concatenate([q * wq_ref[0], q * wq_ref[1]], axis=1).astype(BF16)
        sf = jnp.concatenate([st_ref[0, j]] * RET_HEADS, axis=0) * bd
        sb = jnp.concatenate([st_ref[1, j]] * RET_HEADS, axis=0) * bd
        sbd = jnp.concatenate([sf, sb], axis=0).astype(BF16)
        o = jnp.concatenate(inner, axis=1) + jnp.dot(qq, sbd, preferred_element_type=F32)
        outs = []
        for h in range(RET_HEADS):
            oh = o[:, h * RET_DV:(h + 1) * RET_DV]
            outs.append(oh * lax.rsqrt(jnp.mean(oh * oh, axis=-1, keepdims=True) + EPS))
        gr = gr_ref[r0:r0 + C, :]
        o_ref[r0:r0 + C, :] = (gr * jax.nn.sigmoid(gr)) * jnp.concatenate(outs, axis=1)


def retention_out(qk, v, gr, states, tabs):
    B, S, _ = qk.shape
    rows = RET_GROUP * RET_CHUNK
    G = S // rows
    return pl.pallas_call(
        _retout_kernel,
        grid=(B, G),
        in_specs=[
            pl.BlockSpec((None, rows, 2 * RET_QK), lambda b, g: (b, g, 0)),
            pl.BlockSpec((None, rows, RET_V), lambda b, g: (b, g, 0)),
            pl.BlockSpec((None, rows, RET_V), lambda b, g: (b, g, 0)),
            pl.BlockSpec((rows, RET_QK), lambda b, g: (g, 0)),
            pl.BlockSpec((rows, RET_QK), lambda b, g: (g, 0)),
            pl.BlockSpec((None, 2, RET_GROUP, RET_DK, RET_V), lambda b, g: (b, 0, g, 0, 0)),
            _const_spec((RET_HEADS, RET_CHUNK, RET_CHUNK)),
            _const_spec((2, RET_CHUNK, RET_QK)),
            _const_spec((2, RET_CHUNK, RET_QK)),
            _const_spec((RET_QK, RET_V)),
        ],
        out_specs=pl.BlockSpec((None, rows, RET_V), lambda b, g: (b, g, 0)),
        out_shape=jax.ShapeDtypeStruct((B, S, RET_V), F32),
        compiler_params=_cparams(("arbitrary", "arbitrary")),
        name="retention_out",
    )(qk, v, gr, tabs["cc"], tabs["ss"], states, tabs["dec"], tabs["wq"], tabs["wk"], tabs["bd"])


def _merge_rows(x, yh, yr, gm, ga, whb_ref, wrb_ref, wo_ref, g, sh, sc, wr_ref, br):
    ghy = 0.5 * jnp.tanh(0.5 * gm[:, :D_MODEL]) + 0.5
    grt = 0.5 * jnp.tanh(0.5 * gm[:, D_MODEL:]) + 0.5
    m = (ghy * jnp.dot(yh.astype(BF16), whb_ref[...], preferred_element_type=F32)
         + grt * jnp.dot(yr.astype(BF16), wrb_ref[...], preferred_element_type=F32))
    out = jnp.dot(m.astype(BF16), wo_ref[...], preferred_element_type=F32)
    x = x + ga * out
    h = _norm_mod(x, g, sc, sh)
    h_hi = h.astype(BF16)
    h_lo = (h - h_hi.astype(F32)).astype(BF16)
    logits = (jnp.dot(h_hi, wr_ref[0], preferred_element_type=F32)
              + jnp.dot(h_hi, wr_ref[1], preferred_element_type=F32)
              + jnp.dot(h_lo, wr_ref[0], preferred_element_type=F32)) + br
    tm = logits.shape[0]
    lane_e = lax.broadcasted_iota(jnp.int32, logits.shape, 1).astype(F32)
    lane_o = lax.broadcasted_iota(jnp.int32, (tm, LANES), 1)
    route = jnp.zeros((tm, LANES), F32)
    vals = []
    work = logits
    for r in range(TOP_K):
        m_r = jnp.max(work, axis=-1, keepdims=True)
        i_r = jnp.min(jnp.where(work == m_r, lane_e, float(N_EXPERTS)), axis=-1, keepdims=True)
        work = jnp.where(lane_e == i_r, -jnp.inf, work)
        vals.append(m_r)
        route = jnp.where(lane_o == r, i_r, route)
    exps = [jnp.exp(v - vals[0]) for v in vals]
    den = exps[0] + exps[1] + exps[2] + exps[3]
    for r in range(TOP_K):
        route = jnp.where(lane_o == TOP_K + r, exps[r] / den, route)
    return x, h, route


def _merge_kernel(x_ref, yh_ref, yr_ref, gm_ref, ga_ref, whb_ref, wrb_ref, wo_ref,
                  g_ref, sh_ref, sc_ref, wr_ref, br_ref, xo_ref, h_ref, rt_ref):
    tm = TOK_TILE
    for half in range(MERGE_HALVES):
        rows = slice(half * tm, (half + 1) * tm)
        yh = jnp.concatenate([yh_ref[c, rows, :] for c in range(HY_SLABS)], axis=1)
        x, h, route = _merge_rows(x_ref[rows, :], yh, yr_ref[rows, :], gm_ref[rows, :], ga_ref[...],
                                  whb_ref, wrb_ref, wo_ref, g_ref[...], sh_ref[...], sc_ref[...],
                                  wr_ref, br_ref[...])
        xo_ref[rows, :] = x
        rt_ref[rows, :] = route
        h_half = h_ref.at[rows]
        for j in range(ROW_TILES):
            _store_sub(h_half, (), j, h[:, j * LANES:(j + 1) * LANES])


def merge_and_route(x2, yh, yr, gm, mod, g_ffn, whb, wrb, wo, w_router, b_router, layer):
    T, D = x2.shape
    tm = TOK_TILE * MERGE_HALVES
    per_b = SEQ // tm
    mspec = lambda piece: pl.BlockSpec((None, None, None, 1, D),
                                       lambda i: (layer, piece, i // per_b, 0, 0))
    row = lambda w: pl.BlockSpec((tm, w), lambda i: (i, 0))
    wsp = lambda a, b: pl.BlockSpec((None, a, b), lambda i: (layer, 0, 0))
    return pl.pallas_call(
        _merge_kernel,
        grid=(T // tm,),
        in_specs=[row(D),
                  pl.BlockSpec((None, HY_SLABS, tm, LANES), lambda i: (i // per_b, 0, i % per_b, 0)),
                  row(RET_V), row(2 * D), mspec(2),
                  wsp(HY_WIDTH, D), wsp(RET_V, D), wsp(D, D),
                  wsp(1, D), mspec(3), mspec(4),
                  pl.BlockSpec((None, 2, D, N_EXPERTS), lambda i: (layer, 0, 0, 0)),
                  wsp(1, N_EXPERTS)],
        out_specs=[row(D), pl.BlockSpec((tm, ROW_TILES, LANES), lambda i: (i, 0, 0)), row(LANES)],
        out_shape=[jax.ShapeDtypeStruct((T, D), F32), jax.ShapeDtypeStruct((T, ROW_TILES, LANES), F32),
                   jax.ShapeDtypeStruct((T, LANES), F32)],
        compiler_params=_cparams(("arbitrary",)),
        name="merge_and_route",
    )(x2, yh, yr, gm, mod, whb, wrb, wo, g_ffn, mod, mod, w_router, b_router)


def _rank_kernel(rt_ref, tril_ref, o_ref, cnt_ref, carry_ref):
    @pl.when(pl.program_id(0) == 0)
    def _():
        carry_ref[...] = jnp.zeros_like(carry_ref)

    rt = rt_ref[...]
    tm = rt.shape[0]
    lane_e = lax.broadcasted_iota(jnp.int32, (tm, N_EXPERTS), 1).astype(F32)
    lane_o = lax.broadcasted_iota(jnp.int32, (tm, LANES), 1)
    hits = [lane_e == rt[:, r:r + 1] for r in range(TOP_K)]
    onehot = sum(h.astype(F32) for h in hits)
    before = jnp.dot(tril_ref[...], onehot.astype(BF16), preferred_element_type=F32) + carry_ref[...]
    out = jnp.zeros((tm, LANES), F32)
    for r in range(TOP_K):
        rank_r = jnp.sum(jnp.where(hits[r], before, 0.0), axis=-1, keepdims=True)
        out = jnp.where(lane_o == r, rank_r, out)
    o_ref[...] = out
    carry_ref[...] += jnp.sum(onehot, axis=0, keepdims=True)
    cnt_ref[...] = carry_ref[...]


def assignment_ranks(route):
    T = route.shape[0]
    tm = RANK_TILE
    tril = jnp.asarray(np.tril(np.ones((tm, tm)), -1), BF16)
    return pl.pallas_call(
        _rank_kernel,
        grid=(T // tm,),
        in_specs=[pl.BlockSpec((tm, LANES), lambda i: (i, 0)), _const_spec((tm, tm))],
        out_specs=[pl.BlockSpec((tm, LANES), lambda i: (i, 0)), _const_spec((1, N_EXPERTS))],
        out_shape=[jax.ShapeDtypeStruct((T, LANES), F32),
                   jax.ShapeDtypeStruct((1, N_EXPERTS), F32)],
        scratch_shapes=[pltpu.VMEM((1, N_EXPERTS), F32)],
        compiler_params=_cparams(("arbitrary",)),
        name="assignment_ranks",
    )(route, tril)


def _slot_kernel(rt_ref, rk_ref, ps_ref, o_ref):
    rt = rt_ref[...]
    rk = rk_ref[...]
    tm = rt.shape[0]
    lane_e = lax.broadcasted_iota(jnp.int32, (tm, N_EXPERTS), 1).astype(F32)
    lane_o = lax.broadcasted_iota(jnp.int32, (tm, LANES), 1)
    out = jnp.zeros((tm, LANES), F32)
    for r in range(TOP_K):
        start_r = jnp.sum(jnp.where(lane_e == rt[:, r:r + 1], ps_ref[...], 0.0), axis=-1, keepdims=True)
        out = jnp.where(lane_o == r, start_r + rk[:, r:r + 1], out)
    o_ref[...] = out.astype(jnp.int32)


def assignment_slots(route, rank, pad_starts):
    T = route.shape[0]
    tm = RANK_TILE
    row = lambda: pl.BlockSpec((tm, LANES), lambda i: (i, 0))
    return pl.pallas_call(
        _slot_kernel,
        grid=(T // tm,),
        in_specs=[row(), row(), _const_spec((1, N_EXPERTS))],
        out_specs=row(),
        out_shape=jax.ShapeDtypeStruct((T, LANES), jnp.int32),
        compiler_params=_cparams(("arbitrary",)),
        name="assignment_slots",
    )(route, rank, pad_starts)


def _dispatch_kernel(dest_ref, pe_ref, h_ref, xs_hbm, zeros, sem, zsem):
    tm = DISPATCH_TILE
    t0 = pl.program_id(0) * tm

    @pl.when(pl.program_id(0) == 0)
    def _():
        zeros[...] = jnp.zeros_like(zeros)

        def zero_block(e):
            end = pe_ref[e]
            begin = pe_ref[jnp.maximum(e - 1, 0)]
            nonempty = jnp.logical_or(jnp.logical_and(e == 0, end > 0), end > begin)
            start = pl.multiple_of(jnp.maximum(end - EXPERT_TILE, 0), EXPERT_TILE)
            return nonempty, pltpu.make_async_copy(zeros, xs_hbm.at[pl.ds(start, EXPERT_TILE)], zsem)

        def start_zero(e, carry):
            nonempty, cp = zero_block(e)

            @pl.when(nonempty)
            def _():
                cp.start()
            return carry
        lax.fori_loop(0, N_EXPERTS, start_zero, 0)

        def wait_zero(e, carry):
            nonempty, cp = zero_block(e)

            @pl.when(nonempty)
            def _():
                cp.wait()
            return carry
        lax.fori_loop(0, N_EXPERTS, wait_zero, 0)

    def row_copy(i, d):
        return pltpu.make_async_copy(h_ref.at[i], xs_hbm.at[d], sem)

    def issue(i, carry):
        for r in range(TOP_K):
            row_copy(i, dest_ref[(t0 + i) * TOP_K + r]).start(priority=r % 2)
        return carry
    lax.fori_loop(0, tm, issue, 0, unroll=4)

    def drain(i, carry):
        row_copy(0, 0).wait()
        return carry
    lax.fori_loop(0, tm * TOP_K, drain, 0, unroll=8)


def dispatch(dest, pad_ends, h, n_rows):
    T = h.shape[0]
    tm = DISPATCH_TILE
    grid_spec = pltpu.PrefetchScalarGridSpec(
        num_scalar_prefetch=2,
        grid=(T // tm,),
        in_specs=[pl.BlockSpec((tm, ROW_TILES, LANES), lambda i, d, pe: (i, 0, 0))],
        out_specs=pl.BlockSpec(memory_space=pl.ANY),
        scratch_shapes=[pltpu.VMEM((EXPERT_TILE, ROW_TILES, LANES), h.dtype), pltpu.SemaphoreType.DMA,
                        pltpu.SemaphoreType.DMA],
    )
    return pl.pallas_call(
        _dispatch_kernel,
        grid_spec=grid_spec,
        out_shape=jax.ShapeDtypeStruct((n_rows, ROW_TILES, LANES), h.dtype),
        compiler_params=_cparams(("arbitrary",)),
        name="dispatch",
    )(dest, pad_ends, h)


def _expert_kernel(be_ref, nb_ref, x_ref, wgu_ref, bgu_ref, wd_ref, bd_ref, perm_ref, o_ref,
                   wgu_s, wd_s):
    i = pl.program_id(0)
    prev = be_ref[jnp.maximum(i - 1, 0)]
    new_expert = jnp.logical_or(i == 0, be_ref[i] != prev)
    n_chunk = 2 * D_FF // (2 * LANES)

    @pl.when(new_expert)
    def _():
        for c in range(n_chunk):
            cols = slice(c * 2 * LANES, (c + 1) * 2 * LANES)
            wgu_s[:, cols] = jnp.dot(wgu_ref[:, cols].astype(BF16), perm_ref[...],
                                     preferred_element_type=F32).astype(BF16)
        wd_s[...] = wd_ref[...].astype(BF16)

    @pl.when(i < nb_ref[0])
    def _():
        x = jnp.concatenate([_load_sub(x_ref, (), j) for j in range(ROW_TILES)], axis=1)
        gu = jnp.dot(x.astype(BF16), wgu_s[...], preferred_element_type=F32) + bgu_ref[...]
        acts = []
        for c in range(n_chunk):
            gate = jnp.minimum(gu[:, c * 2 * LANES:c * 2 * LANES + LANES], SWIGLU_LIMIT)
            up = jnp.clip(gu[:, c * 2 * LANES + LANES:(c + 1) * 2 * LANES], -SWIGLU_LIMIT, SWIGLU_LIMIT)
            acts.append((up + 1.0) * (gate * jax.nn.sigmoid(SWIGLU_ALPHA * gate)))
        act = jnp.concatenate(acts, axis=1).astype(BF16)
        y = jnp.dot(act, wd_s[...], preferred_element_type=F32) + bd_ref[...]
        for j in range(ROW_TILES):
            _store_sub(o_ref, (), j, y[:, j * LANES:(j + 1) * LANES])

    @pl.when(i >= nb_ref[0])
    def _():
        o_ref[...] = jnp.zeros_like(o_ref)


def expert_ffn(block_e, n_used, xs, w_gate_up, bgu, w_down, bd, layer):
    P = xs.shape[0]
    D = D_MODEL
    tm = EXPERT_TILE
    perm = np.zeros((2 * LANES, 2 * LANES))
    j = np.arange(LANES)
    perm[2 * j, j] = 1.0
    perm[2 * j + 1, LANES + j] = 1.0
    grid_spec = pltpu.PrefetchScalarGridSpec(
        num_scalar_prefetch=2,
        grid=(P // tm,),
        in_specs=[
            pl.BlockSpec((tm, ROW_TILES, LANES), lambda i, be, nb: (jnp.minimum(i, nb[0] - 1), 0, 0)),
            pl.BlockSpec((None, None, D, 2 * D_FF), lambda i, be, nb: (layer, be[i], 0, 0)),
            pl.BlockSpec((None, None, 1, 2 * D_FF), lambda i, be, nb: (layer, be[i], 0, 0)),
            pl.BlockSpec((None, None, D_FF, D), lambda i, be, nb: (layer, be[i], 0, 0)),
            pl.BlockSpec((None, None, 1, D), lambda i, be, nb: (layer, be[i], 0, 0)),
            pl.BlockSpec((2 * LANES, 2 * LANES), lambda i, be, nb: (0, 0)),
        ],
        out_specs=pl.BlockSpec((tm, ROW_TILES, LANES), lambda i, be, nb: (i, 0, 0)),
        scratch_shapes=[pltpu.VMEM((D, 2 * D_FF), BF16), pltpu.VMEM((D_FF, D), BF16)],
    )
    return pl.pallas_call(
        _expert_kernel,
        grid_spec=grid_spec,
        out_shape=jax.ShapeDtypeStruct((P, ROW_TILES, LANES), F32),
        compiler_params=_cparams(("arbitrary",)),
        name="expert_ffn",
    )(block_e, n_used, xs, w_gate_up, bgu, w_down, bd, jnp.asarray(perm, BF16))


def _combine_kernel(dest_ref, x_ref, rt_ref, ga_ref, g_ref, yb_hbm, o_ref, buf, sem, *, final):
    tm = COMBINE_TILE
    step = pl.program_id(0)
    slot = step % 2

    def row_copy(sl, i, r, d):
        return pltpu.make_async_copy(yb_hbm.at[d], buf.at[sl, r, i], sem.at[sl])

    def issue_step(st, sl):
        def issue(i, carry):
            for r in range(TOP_K):
                row_copy(sl, i, r, dest_ref[(st * tm + i) * TOP_K + r]).start(priority=r % 2)
            return carry
        lax.fori_loop(0, tm, issue, 0, unroll=4)

    @pl.when(step == 0)
    def _():
        issue_step(0, 0)

    @pl.when(step + 1 < pl.num_programs(0))
    def _():
        issue_step(step + 1, 1 - slot)

    def drain(i, carry):
        row_copy(slot, 0, 0, 0).wait()
        return carry
    lax.fori_loop(0, tm * TOP_K, drain, 0, unroll=8)

    rt = rt_ref[...]
    gates = [jnp.broadcast_to(rt[:, TOP_K + r:TOP_K + r + 1], (tm, LANES)) for r in range(TOP_K)]
    cols = []
    for j in range(ROW_TILES):
        s = _load_sub(buf, (slot, 0), j) * gates[0]
        for r in range(1, TOP_K):
            s = s + _load_sub(buf, (slot, r), j) * gates[r]
        cols.append(s)
    y = jnp.concatenate(cols, axis=1)
    x = x_ref[...] + ga_ref[...] * y
    if final:
        x = (x * lax.rsqrt(jnp.mean(x * x, axis=-1, keepdims=True) + EPS)) * g_ref[...]
    o_ref[...] = x


def combine(dest, x2, route, mod, g_final, yb, layer, final):
    T, D = x2.shape
    tm = COMBINE_TILE
    per_b = SEQ // tm
    grid_spec = pltpu.PrefetchScalarGridSpec(
        num_scalar_prefetch=1,
        grid=(T // tm,),
        in_specs=[
            pl.BlockSpec((tm, D), lambda i, d: (i, 0)),
            pl.BlockSpec((tm, LANES), lambda i, d: (i, 0)),
            pl.BlockSpec((None, None, None, 1, D), lambda i, d: (layer, 5, i // per_b, 0, 0)),
            pl.BlockSpec((1, D), lambda i, d: (0, 0)),
            pl.BlockSpec(memory_space=pl.ANY),
        ],
        out_specs=pl.BlockSpec((tm, D), lambda i, d: (i, 0)),
        scratch_shapes=[pltpu.VMEM((2, TOP_K, tm, ROW_TILES, LANES), F32),
                        pltpu.SemaphoreType.DMA((2,))],
    )
    return pl.pallas_call(
        functools.partial(_combine_kernel, final=final),
        grid_spec=grid_spec,
        out_shape=jax.ShapeDtypeStruct((T, D), F32),
        compiler_params=_cparams(("arbitrary",)),
        name="combine",
    )(dest, x2, route, mod, g_final, yb)


def moe_ffn_residual(x2, h, route, mod, g_final, w_gate_up, bgu, w_down, bd, layer, final):
    T = h.shape[0]
    tm = EXPERT_TILE
    rank, counts = assignment_ranks(route)
    counts = counts[0].astype(jnp.int32)
    padded = ((counts + tm - 1) // tm) * tm
    pad_ends = jnp.cumsum(padded)
    pad_starts = pad_ends - padded
    n_blocks = -(-(T * TOP_K + N_EXPERTS * (tm - 1)) // tm)
    block_start = jnp.arange(n_blocks, dtype=jnp.int32) * tm
    block_e = jnp.minimum(jnp.sum(pad_ends[None, :] <= block_start[:, None], axis=1),
                          N_EXPERTS - 1).astype(jnp.int32)
    n_used = (pad_ends[-1:] // tm).astype(jnp.int32)
    slots = assignment_slots(route, rank, pad_starts.astype(F32).reshape(1, N_EXPERTS))
    dest = slots[:, :TOP_K].reshape(-1)
    xs = dispatch(dest, pad_ends.astype(jnp.int32), h, n_blocks * tm)
    yb = expert_ffn(block_e, n_used, xs, w_gate_up, bgu, w_down, bd, layer)
    return combine(dest, x2, route, mod, g_final, yb, layer, final)


def kernel(x, c, norm_mix_g, norm_ffn_g, w_mod, b_mod, w_in, hy_conv_w, hy_conv_b, hy_w1, hy_b1, hy_w2, hy_b2, hy_w3, hy_b3, hy_w4, hy_freq, hy_bias, w_hy_br, w_ret_br, w_out, w_router, b_router, w_gate_up, b_gate_up, w_down, b_down, final_g):
    B, S, D = x.shape
    L = w_mod.shape[0]
    T = B * S
    tabs = _tables()

    w_in_bf = w_in.astype(BF16)
    whb = w_hy_br.astype(BF16)
    wrb = w_ret_br.astype(BF16)
    wo = w_out.astype(BF16)
    wr_hi = w_router.astype(BF16)
    wr_split = jnp.stack([wr_hi, (w_router - wr_hi.astype(F32)).astype(BF16)], axis=1)
    E, F = N_EXPERTS, D_FF
    bgu = b_gate_up.reshape(L, E, F // LANES, LANES, 2).transpose(0, 1, 2, 4, 3).reshape(L, E, 1, 2 * F)
    bdn = b_down.reshape(L, E, 1, D)
    w1p = jnp.pad(hy_w1, ((0, 0), (0, HY_EMB_PAD - HY_EMB), (0, 0)))
    w4_hi = hy_w4.astype(BF16)
    w4_split = jnp.stack([w4_hi, (hy_w4 - w4_hi.astype(F32)).astype(BF16)], axis=1)
    vec = lambda a: a.reshape(L, 1, -1)

    c_pad = jnp.pad(c, ((0, 8 - B), (0, 0)))
    mod = modulation(c_pad, w_mod, b_mod)[:, :B]
    mod = mod.reshape(L, B, N_MOD, 1, D).transpose(0, 2, 1, 3, 4)

    x2 = x.reshape(T, D)
    for l in range(L):
        u, qk, v, gr, gm = in_projection(x2, vec(norm_mix_g), mod, w_in_bf, l)
        xs = short_conv(u.reshape(B, S, -1), hy_conv_w, vec(hy_conv_b), l)
        kraw, l1 = hyena_filter(tabs, w1p, vec(hy_b1), hy_w2, vec(hy_b2), hy_w3, vec(hy_b3),
                                w4_split, vec(hy_freq), l)
        ka = dft_first_axis(kraw.reshape(1, HY_ORDER, HY_SLABS, FFT_N1, FFT_N2, LANES), 0, tabs["f1"],
                            pair=1)
        kspec = filter_spectrum(ka, l1, tabs)
        y_hy = hyena_mixer(xs, kspec, hy_bias[l], tabs)
        qk3, v3, gr3 = qk.reshape(B, S, -1), v.reshape(B, S, -1), gr.reshape(B, S, -1)
        states = retention_states(qk3, v3, tabs)
        y_ret = retention_out(qk3, v3, gr3, states, tabs)
        x2, h_ffn, route = merge_and_route(
            x2, y_hy, y_ret.reshape(T, -1), gm, mod, vec(norm_ffn_g),
            whb, wrb, wo, wr_split, vec(b_router), l)
        x2 = moe_ffn_residual(x2, h_ffn, route, mod, final_g.reshape(1, D), w_gate_up, bgu,
                              w_down, bdn, l, final=(l == L - 1))
    return x2.reshape(B, S, D)
```

```python
import functools
import math

import numpy as np
import jax
import jax.numpy as jnp
from jax import lax
from jax.experimental import pallas as pl
from jax.experimental.pallas import tpu as pltpu

F32 = jnp.float32
BF16 = jnp.bfloat16
HIGHEST = lax.Precision.HIGHEST

D_MODEL = 1024
BATCH = 4
SEQ = 8192
DEPTH = 4
HY_WIDTH = 512
HY_ORDER = 2
HY_EMB = 33
HY_EMB_PAD = 64
HY_FILT_HIDDEN = 64
HY_FAST_DECAY = 0.3
HY_SLOW_DECAY = 1.5
HY_DECAY_TARGET = 1e-2
RET_HEADS = 4
RET_DK = 64
RET_DV = 128
RET_QK = RET_HEADS * RET_DK
RET_V = RET_HEADS * RET_DV
RET_CHUNK = 128
RET_DECAY_FWD = 5.0
RET_DECAY_BWD = 5.5
ROPE_BASE = 10000.0
N_EXPERTS = 32
TOP_K = 4
D_FF = D_MODEL
SWIGLU_ALPHA = 1.702
SWIGLU_LIMIT = 7.0
N_MOD = 6
EPS = 1e-6
IN_COLS = 3 * HY_WIDTH + 2 * RET_QK + 2 * RET_V + 2 * D_MODEL

LANES = 128
HY_SLABS = HY_WIDTH // LANES
ROW_TILES = D_MODEL // LANES
VMEM_LIMIT = 56 * 1024 * 1024

FFT_N = 2 * SEQ
FFT_N1 = 128
FFT_N2 = FFT_N // FFT_N1
FFT_H1 = FFT_N1 // 2

TOK_TILE = 256
MERGE_HALVES = 2
EXPERT_TILE = 512
RANK_TILE = 1024
DISPATCH_TILE = 512
COMBINE_TILE = 256
RET_GROUP = 8
FFT_N2_BLOCK = 8
FFT_K1_GROUP = 8


def _cparams(sem):
    return pltpu.CompilerParams(dimension_semantics=sem, vmem_limit_bytes=VMEM_LIMIT)


def _const_spec(shape):
    return pl.BlockSpec(shape, lambda *_: (0,) * len(shape))


def _tables():
    n1 = np.arange(FFT_N1)
    n2 = np.arange(FFT_N2)
    ang1 = 2.0 * np.pi * np.outer(n1, n1) / FFT_N1
    f1 = np.concatenate([np.cos(ang1), -np.sin(ang1)], axis=0)
    ang2 = 2.0 * np.pi * np.outer(n2, n2) / FFT_N2
    angt = 2.0 * np.pi * np.outer(n1, n2) / FFT_N
    f1r, f1i = np.cos(ang1)[:, :FFT_H1], -np.sin(ang1)[:, :FFT_H1]
    f1c = np.block([[f1r, -f1i], [f1i, f1r]])
    gr, gi = np.cos(ang1)[:FFT_H1] / FFT_N, np.sin(ang1)[:FFT_H1] / FFT_N
    gc = np.block([[gr, -gi], [gi, gr]])

    L = SEQ
    t = np.linspace(0.0, 1.0, L)
    bands = (HY_EMB - 1) // 2
    w = 2.0 * np.pi * np.arange(L) / L
    f = np.linspace(1e-4, bands - 1, bands)
    feats = np.concatenate([t[:, None], np.cos(f[None] * w[:, None]), -np.sin(f[None] * w[:, None])], -1)
    idx = np.concatenate([np.arange(L), [0], L - np.arange(1, L)])
    feats2 = np.zeros((2 * L, HY_EMB_PAD))
    feats2[:, :HY_EMB] = feats[idx]
    max_decay = math.log(HY_DECAY_TARGET) / HY_FAST_DECAY
    min_decay = math.log(HY_DECAY_TARGET) / HY_SLOW_DECAY
    deltas = np.abs(np.linspace(min_decay, max_decay, HY_WIDTH))

    C = RET_CHUNK
    hh = np.arange(RET_HEADS)
    lgf = np.log(1.0 - np.exp2(-(RET_DECAY_FWD + hh)))
    lgb = np.log(1.0 - np.exp2(-(RET_DECAY_BWD + hh)))
    pos = np.arange(C)
    diff = pos[:, None] - pos[None, :]
    dec = np.where(diff[None] >= 0, np.exp(np.maximum(diff, 0)[None] * lgf[:, None, None]),
                   np.exp(np.maximum(-diff, 0)[None] * lgb[:, None, None]))
    lane_h = np.repeat(hh, RET_DK)
    wq = np.stack([np.exp((pos[:, None] + 1.0) * lgf[lane_h][None]),
                   np.exp((C - pos[:, None]) * lgb[lane_h][None])])
    wk = np.stack([np.exp((C - 1.0 - pos[:, None]) * lgf[lane_h][None]),
                   np.exp(pos[:, None] * lgb[lane_h][None])]) * (RET_DK ** -0.5)
    cd = np.stack([np.broadcast_to(np.exp(C * lgf[lane_h])[:, None], (RET_QK, RET_V)),
                   np.broadcast_to(np.exp(C * lgb[lane_h])[:, None], (RET_QK, RET_V))])
    col_h = np.repeat(hh, RET_DV)
    bd = (lane_h[:, None] == col_h[None, :]).astype(np.float64)
    inv_freq = 1.0 / (ROPE_BASE ** (np.arange(0, RET_DK, 2) / RET_DK))
    ang = np.arange(SEQ)[:, None] * inv_freq[None, :]
    cc = np.tile(np.concatenate([np.cos(ang), np.cos(ang)], -1), (1, RET_HEADS))
    ss = np.tile(np.concatenate([-np.sin(ang), np.sin(ang)], -1), (1, RET_HEADS))

    return dict(
        f1=jnp.asarray(f1, BF16), f1c=jnp.asarray(f1c, BF16),
        f2r=jnp.asarray(np.cos(ang2), F32), f2i=jnp.asarray(-np.sin(ang2), F32),
        twr=jnp.asarray(np.cos(angt), F32), twi=jnp.asarray(-np.sin(angt), F32),
        gc=jnp.asarray(gc, BF16),
        feats2=jnp.asarray(feats2, F32), deltas=jnp.asarray(deltas[None], F32),
        dec=jnp.asarray(dec, F32), wq=jnp.asarray(wq, F32), wk=jnp.asarray(wk, F32),
        cd=jnp.asarray(cd, F32), bd=jnp.asarray(bd, F32),
        cc=jnp.asarray(cc, F32), ss=jnp.asarray(ss, F32),
    )


def _mod_kernel(c_ref, w_ref, b_ref, o_ref):
    c = c_ref[...]
    ca = c * jax.nn.sigmoid(c)
    o_ref[...] = jnp.dot(ca, w_ref[...], preferred_element_type=F32, precision=HIGHEST) + b_ref[...]


def modulation(c_pad, w_mod, b_mod):
    L, D, N = w_mod.shape
    tn = 1536
    rows = c_pad.shape[0]
    return pl.pallas_call(
        _mod_kernel,
        grid=(L, N // tn),
        in_specs=[
            _const_spec((rows, D)),
            pl.BlockSpec((None, D, tn), lambda l, j: (l, 0, j)),
            pl.BlockSpec((None, 1, tn), lambda l, j: (l, 0, j)),
        ],
        out_specs=pl.BlockSpec((None, rows, tn), lambda l, j: (l, 0, j)),
        out_shape=jax.ShapeDtypeStruct((L, rows, N), F32),
        compiler_params=_cparams(("arbitrary", "arbitrary")),
        name="modulation",
    )(c_pad, w_mod, b_mod.reshape(L, 1, N))


def _norm_mod(x, g, sc, sh):
    y = x * lax.rsqrt(jnp.mean(x * x, axis=-1, keepdims=True) + EPS)
    return (y * g) * (1.0 + sc) + sh


def _inproj_kernel(x_ref, g_ref, sh_ref, sc_ref, w_ref, u_ref, qk_ref, v_ref, gr_ref, gm_ref):
    h = _norm_mod(x_ref[...], g_ref[...], sc_ref[...], sh_ref[...]).astype(BF16)
    c0 = 0
    for o_ref in (u_ref, qk_ref, v_ref, gr_ref, gm_ref):
        c1 = c0 + o_ref.shape[-1]
        o_ref[...] = jnp.dot(h, w_ref[:, c0:c1], preferred_element_type=F32).astype(o_ref.dtype)
        c0 = c1


def in_projection(x2, g, mod, w_in_bf, layer):
    T, D = x2.shape
    tm = TOK_TILE
    per_b = SEQ // tm
    mspec = lambda piece: pl.BlockSpec((None, None, None, 1, D),
                                       lambda i: (layer, piece, i // per_b, 0, 0))
    widths = (3 * HY_WIDTH, 2 * RET_QK, RET_V, RET_V, 2 * D_MODEL)
    dtypes = (BF16, F32, BF16, F32, F32)
    return pl.pallas_call(
        _inproj_kernel,
        grid=(T // tm,),
        in_specs=[
            pl.BlockSpec((tm, D), lambda i: (i, 0)),
            pl.BlockSpec((None, 1, D), lambda i: (layer, 0, 0)),
            mspec(0), mspec(1),
            pl.BlockSpec((None, D, IN_COLS), lambda i: (layer, 0, 0)),
        ],
        out_specs=[pl.BlockSpec((tm, w), lambda i: (i, 0)) for w in widths],
        out_shape=[jax.ShapeDtypeStruct((T, w), dt) for w, dt in zip(widths, dtypes)],
        compiler_params=_cparams(("arbitrary",)),
        name="in_projection",
    )(x2, g, mod, mod, w_in_bf)


def _shortconv_kernel(u_ref, w_ref, b_ref, o_ref):
    u = u_ref[...].astype(F32)
    s = u.shape[0]
    row = lax.broadcasted_iota(jnp.int32, u.shape, 0)
    prev = jnp.where(row == 0, 0.0, pltpu.roll(u, 1, axis=0))
    nxt = jnp.where(row == s - 1, 0.0, pltpu.roll(u, s - 1, axis=0))
    w = w_ref[...]
    o_ref[...] = prev * w[0:1] + u * w[1:2] + nxt * w[2:3] + b_ref[...]


def short_conv(u, conv_w, conv_b, layer):
    B, S, C3 = u.shape
    cb = LANES
    per = HY_WIDTH // cb
    return pl.pallas_call(
        _shortconv_kernel,
        grid=(B, C3 // cb),
        in_specs=[
            pl.BlockSpec((None, S, cb), lambda b, j: (b, 0, j)),
            pl.BlockSpec((None, 3, cb), lambda b, j: (layer, 0, j)),
            pl.BlockSpec((None, 1, cb), lambda b, j: (layer, 0, j)),
        ],
        out_specs=pl.BlockSpec((None, None, None, S, cb), lambda b, j: (j // per, b, j % per, 0, 0)),
        out_shape=jax.ShapeDtypeStruct((3, B, per, S, cb), F32),
        compiler_params=_cparams(("arbitrary", "arbitrary")),
        name="short_conv",
    )(u, conv_w, conv_b)


def _filter_kernel(p_ref, w1_ref, b1_ref, w2_ref, b2_ref, w3_ref, b3_ref, w4_ref, fr_ref, dl_ref,
                   k_ref, l1_ref, *, rows):
    i = pl.program_id(0)
    p = p_ref[...]
    fr = fr_ref[...]
    dot = functools.partial(jnp.dot, preferred_element_type=F32, precision=HIGHEST)
    h = jnp.sin(fr * (dot(p, w1_ref[...]) + b1_ref[...]))
    h = jnp.sin(fr * (dot(h, w2_ref[...]) + b2_ref[...]))
    h = jnp.sin(fr * (dot(h, w3_ref[...]) + b3_ref[...]))
    h_hi = h.astype(BF16)
    h_lo = (h - h_hi.astype(F32)).astype(BF16)
    h = (jnp.dot(h_hi, w4_ref[0], preferred_element_type=F32)
         + jnp.dot(h_hi, w4_ref[1], preferred_element_type=F32)
         + jnp.dot(h_lo, w4_ref[0], preferred_element_type=F32))
    t = p[:, 0:1]
    win = jnp.exp(-t * dl_ref[...])
    win = jnp.concatenate([win] * HY_ORDER, axis=1)
    grow = i * rows + lax.broadcasted_iota(jnp.int32, (rows, 1), 0)
    k = jnp.where(grow == SEQ, 0.0, h * win)
    for s in range(k_ref.shape[0]):
        k_ref[s] = k[:, s * LANES:(s + 1) * LANES]

    @pl.when(i == 0)
    def _():
        l1_ref[...] = jnp.zeros_like(l1_ref)
    l1_ref[...] += jnp.sum(jnp.abs(k), axis=0, keepdims=True)


def hyena_filter(tabs, w1p, b1, w2, b2, w3, b3, w4, freq, layer):
    rows = 1024
    n = 2 * SEQ
    half = SEQ // rows
    H = HY_FILT_HIDDEN
    OC = HY_ORDER * HY_WIDTH
    vec = lambda: pl.BlockSpec((None, 1, H), lambda i: (layer, 0, 0))
    return pl.pallas_call(
        functools.partial(_filter_kernel, rows=rows),
        grid=(n // rows,),
        in_specs=[
            pl.BlockSpec((rows, HY_EMB_PAD), lambda i: (i, 0)),
            pl.BlockSpec((None, HY_EMB_PAD, H), lambda i: (layer, 0, 0)), vec(),
            pl.BlockSpec((None, H, H), lambda i: (layer, 0, 0)), vec(),
            pl.BlockSpec((None, H, H), lambda i: (layer, 0, 0)), vec(),
            pl.BlockSpec((None, 2, H, OC), lambda i: (layer, 0, 0, i // half)),
            vec(),
            _const_spec((1, HY_WIDTH)),
        ],
        out_specs=[pl.BlockSpec((OC // LANES, rows, LANES), lambda i: (0, i, 0)), _const_spec((1, OC))],
        out_shape=[jax.ShapeDtypeStruct((OC // LANES, n, LANES), F32),
                   jax.ShapeDtypeStruct((1, OC), F32)],
        compiler_params=_cparams(("arbitrary",)),
        name="hyena_filter",
    )(tabs["feats2"], w1p, b1, w2, b2, w3, b3, w4, freq, tabs["deltas"])


def _load_sub(ref, lead, j):
    *outer, r, s, l = ref.shape
    flat = ref.reshape(*outer, r * s, l)
    return flat[(*lead, pl.ds(j, r, stride=s), slice(None))]


def _store_sub(ref, lead, j, val):
    *outer, r, s, l = ref.shape
    flat = ref.reshape(*outer, r * s, l)
    flat[(*lead, pl.ds(j, r, stride=s), slice(None))] = val


def _pack_pair(re, im):
    return lax.bitcast_convert_type(pltpu.pack_elementwise([im, re], packed_dtype=BF16), jnp.uint32)


def _unpack_pair(p):
    p = lax.bitcast_convert_type(p, jnp.int32)
    im = pltpu.unpack_elementwise(p, index=0, packed_dtype=BF16, unpacked_dtype=F32)
    re = pltpu.unpack_elementwise(p, index=1, packed_dtype=BF16, unpacked_dtype=F32)
    return re, im


def _dft1_kernel(x_ref, f_ref, o_ref):
    for j in range(FFT_N2_BLOCK):
        xs = jnp.concatenate(
            [jnp.concatenate([_load_sub(x_ref, (p, c), j) for c in range(HY_SLABS)], axis=1)
             for p in range(x_ref.shape[0])], axis=0)
        a = jnp.dot(f_ref[...], xs.astype(BF16), preferred_element_type=F32)
        p = _pack_pair(a[:FFT_N1], a[FFT_N1:])
        for c in range(HY_SLABS):
            _store_sub(o_ref, (c,), j, p[:, c * LANES:(c + 1) * LANES])


def dft_first_axis(x6, which, f1, pair):
    _, B, _, rows, _, _ = x6.shape
    nb = FFT_N2_BLOCK
    return pl.pallas_call(
        _dft1_kernel,
        grid=(B // pair, FFT_N2 // nb),
        in_specs=[pl.BlockSpec((None, pair, HY_SLABS, rows, nb, LANES),
                               lambda b, j: (which, b, 0, 0, j, 0)),
                  _const_spec((2 * FFT_N1, pair * rows))],
        out_specs=pl.BlockSpec((None, HY_SLABS, FFT_N1, nb, LANES), lambda b, j: (b, 0, 0, j, 0)),
        out_shape=jax.ShapeDtypeStruct((B // pair, HY_SLABS, FFT_N1, FFT_N2, LANES), jnp.uint32),
        compiler_params=_cparams(("arbitrary", "arbitrary")),
        name="dft_first_axis",
    )(x6, f1)


def _twiddled_stack(f2r_ref, f2i_ref, twr_ref, twi_ref, k1):
    tr = twr_ref[pl.ds(k1, 1), :]
    ti = twi_ref[pl.ds(k1, 1), :]
    fr = f2r_ref[...]
    fi = f2i_ref[...]
    p = fr * tr - fi * ti
    q = fr * ti + fi * tr
    top = jnp.concatenate([p, -q], axis=1)
    bot = jnp.concatenate([q, p], axis=1)
    return jnp.concatenate([top, bot], axis=0).astype(BF16)


def _load_k1(a_ref, j):
    p = jnp.concatenate([a_ref[c, j] for c in range(HY_SLABS)], axis=1)
    re, im = _unpack_pair(p)
    return jnp.concatenate([re, im], axis=0).astype(BF16)


def _spectrum_kernel(a_ref, l1_ref, f2r_ref, f2i_ref, twr_ref, twi_ref, o_ref):
    base = pl.program_id(0) * FFT_K1_GROUP
    inv = 1.0 / l1_ref[...]
    for j in range(FFT_K1_GROUP):
        r = _twiddled_stack(f2r_ref, f2i_ref, twr_ref, twi_ref, base + j)
        x = jnp.dot(r, _load_k1(a_ref, j), preferred_element_type=F32) * inv
        o_ref[j, 0] = x[:FFT_N2].astype(o_ref.dtype)
        o_ref[j, 1] = x[FFT_N2:].astype(o_ref.dtype)


def filter_spectrum(a5, l1, tabs):
    g = FFT_K1_GROUP
    sq = lambda: _const_spec((FFT_N2, FFT_N2))
    return pl.pallas_call(
        _spectrum_kernel,
        grid=(FFT_N1 // g, HY_ORDER),
        in_specs=[pl.BlockSpec((None, HY_SLABS, g, FFT_N2, LANES), lambda i, o: (o, 0, i, 0, 0)),
                  pl.BlockSpec((1, HY_WIDTH), lambda i, o: (0, o)), sq(), sq(), sq(), sq()],
        out_specs=pl.BlockSpec((g, 2, FFT_N2, HY_WIDTH), lambda i, o: (i, 0, 0, o)),
        out_shape=jax.ShapeDtypeStruct((FFT_N1, 2, FFT_N2, HY_ORDER * HY_WIDTH), BF16),
        compiler_params=_cparams(("arbitrary", "arbitrary")),
        name="filter_spectrum",
    )(a5, l1, tabs["f2r"], tabs["f2i"], tabs["twr"], tabs["twi"])


def _convmid_kernel(a_ref, ks_ref, f2r_ref, f2i_ref, twr_ref, twi_ref, o_ref):
    base = pl.program_id(0) * FFT_K1_GROUP
    for j in range(FFT_K1_GROUP):
        r = _twiddled_stack(f2r_ref, f2i_ref, twr_ref, twi_ref, base + j)
        x = jnp.dot(r, _load_k1(a_ref, j), preferred_element_type=F32)
        xr, xi = x[:FFT_N2], x[FFT_N2:]
        kr, ki = ks_ref[j, 0].astype(F32), ks_ref[j, 1].astype(F32)
        y = jnp.concatenate([xr * kr - xi * ki, xr * ki + xi * kr], axis=0).astype(BF16)
        b = lax.dot_general(r, y, (((0,), (0,)), ((), ())), preferred_element_type=F32)
        p = _pack_pair(b[:FFT_N2], b[FFT_N2:])
        for c in range(HY_SLABS):
            o_ref[c, j] = p[:, c * LANES:(c + 1) * LANES]


def conv_mid(a5, kspec, order, tabs):
    B = a5.shape[0]
    g = FFT_K1_GROUP
    sq = lambda: _const_spec((FFT_N2, FFT_N2))
    blk = lambda: pl.BlockSpec((None, HY_SLABS, g, FFT_N2, LANES), lambda i, b: (b, 0, i, 0, 0))
    return pl.pallas_call(
        _convmid_kernel,
        grid=(FFT_N1 // g, B),
        in_specs=[blk(), pl.BlockSpec((g, 2, FFT_N2, HY_WIDTH), lambda i, b: (i, 0, 0, order)),
                  sq(), sq(), sq(), sq()],
        out_specs=blk(),
        out_shape=jax.ShapeDtypeStruct(a5.shape, jnp.uint32),
        compiler_params=_cparams(("arbitrary", "arbitrary")),
        name="conv_mid",
    )(a5, kspec, tabs["f2r"], tabs["f2i"], tabs["twr"], tabs["twi"])


def _convout_kernel(b_ref, g_ref, z_ref, gate_ref, bias_ref, o_ref):
    for j in range(FFT_N2_BLOCK):
        p = jnp.concatenate([_load_sub(b_ref, (c,), j) for c in range(HY_SLABS)], axis=1)
        re, im = _unpack_pair(p)
        bp = jnp.concatenate([re, im], axis=0).astype(BF16)
        y = jnp.dot(g_ref[...], bp, preferred_element_type=F32)
        for q in range(2):
            for c in range(HY_SLABS):
                _store_sub(o_ref, (q, c), j, y[q * FFT_H1:(q + 1) * FFT_H1, c * LANES:(c + 1) * LANES])
    o_ref[...] = gate_ref[...] * (o_ref[...] + z_ref[...] * bias_ref[...])


def conv_out(bp5, z6, z_which, gate6, gate_which, bias4, g):
    B2 = bp5.shape[0]
    nb = FFT_N2_BLOCK
    nat = lambda which: pl.BlockSpec((None, 2, HY_SLABS, FFT_H1, nb, LANES),
                                     lambda b, j: (which, b, 0, 0, j, 0))
    return pl.pallas_call(
        _convout_kernel,
        grid=(B2, FFT_N2 // nb),
        in_specs=[pl.BlockSpec((None, HY_SLABS, FFT_N1, nb, LANES), lambda b, j: (b, 0, 0, j, 0)),
                  _const_spec((2 * FFT_H1, 2 * FFT_N1)), nat(z_which), nat(gate_which),
                  _const_spec((HY_SLABS, 1, 1, LANES))],
        out_specs=pl.BlockSpec((2, HY_SLABS, FFT_H1, nb, LANES), lambda b, j: (b, 0, 0, j, 0)),
        out_shape=jax.ShapeDtypeStruct((2 * B2, HY_SLABS, FFT_H1, FFT_N2, LANES), F32),
        compiler_params=_cparams(("arbitrary", "arbitrary")),
        name="conv_out",
    )(bp5, g, z6, gate6, bias4)


def hyena_mixer(xs, kspec, bias, tabs):
    B = xs.shape[1]
    xs6 = xs.reshape(3, B, HY_SLABS, FFT_H1, FFT_N2, LANES)
    z6, z_which = xs6, 2
    for o in range(HY_ORDER):
        a = dft_first_axis(z6, z_which, tabs["f1c"], pair=2)
        bp = conv_mid(a, kspec, o, tabs)
        z = conv_out(bp, z6, z_which, xs6, o, bias[o].reshape(HY_SLABS, 1, 1, LANES), tabs["gc"])
        z6, z_which = z[None], 0
    return z.reshape(B, HY_SLABS, SEQ, LANES)


def _rope(x, cc, ss):
    lane = lax.broadcasted_iota(jnp.int32, x.shape, 1)
    n = x.shape[1]
    half = RET_DK // 2
    swapped = jnp.where(lane % RET_DK < half, pltpu.roll(x, n - half, axis=1),
                        pltpu.roll(x, half, axis=1))
    return x * cc + swapped * ss


def _retstate_kernel(k_ref, v_ref, cc_ref, ss_ref, wk_ref, cd_ref, o_ref, s_ref):
    d = pl.program_id(1)
    g = pl.program_id(2)
    C = RET_CHUNK

    @pl.when(g == 0)
    def _():
        s_ref[...] = jnp.zeros_like(s_ref)

    for j in range(RET_GROUP):
        ci = jnp.where(d == 0, j, RET_GROUP - 1 - j)
        r0 = pl.multiple_of(ci * C, C)
        s = s_ref[...]
        o_ref[ci] = jnp.concatenate(
            [s[h * RET_DK:(h + 1) * RET_DK, h * RET_DV:(h + 1) * RET_DV] for h in range(RET_HEADS)],
            axis=1)
        k = _rope(k_ref[pl.ds(r0, C), :], cc_ref[pl.ds(r0, C), :], ss_ref[pl.ds(r0, C), :])
        kw = (k * wk_ref[...]).astype(BF16)
        ds = lax.dot_general(kw, v_ref[pl.ds(r0, C), :], (((0,), (0,)), ((), ())),
                             preferred_element_type=F32)
        s_ref[...] = s * cd_ref[...] + ds


def retention_states(qk, v, tabs):
    B, S, _ = qk.shape
    rows = RET_GROUP * RET_CHUNK
    G = S // rows
    grp = lambda d, g: jnp.where(d == 0, g, G - 1 - g)
    return pl.pallas_call(
        _retstate_kernel,
        grid=(B, 2, G),
        in_specs=[
            pl.BlockSpec((None, rows, RET_QK), lambda b, d, g: (b, grp(d, g), 1)),
            pl.BlockSpec((None, rows, RET_V), lambda b, d, g: (b, grp(d, g), 0)),
            pl.BlockSpec((rows, RET_QK), lambda b, d, g: (grp(d, g), 0)),
            pl.BlockSpec((rows, RET_QK), lambda b, d, g: (grp(d, g), 0)),
            pl.BlockSpec((None, RET_CHUNK, RET_QK), lambda b, d, g: (d, 0, 0)),
            pl.BlockSpec((None, RET_QK, RET_V), lambda b, d, g: (d, 0, 0)),
        ],
        out_specs=pl.BlockSpec((None, None, RET_GROUP, RET_DK, RET_V),
                               lambda b, d, g: (b, d, grp(d, g), 0, 0)),
        out_shape=jax.ShapeDtypeStruct((B, 2, S // RET_CHUNK, RET_DK, RET_V), F32),
        scratch_shapes=[pltpu.VMEM((RET_QK, RET_V), F32)],
        compiler_params=_cparams(("arbitrary", "arbitrary", "arbitrary")),
        name="retention_states",
    )(qk, v, tabs["cc"], tabs["ss"], tabs["wk"], tabs["cd"])


def _retout_kernel(qk_ref, v_ref, gr_ref, cc_ref, ss_ref, st_ref, dec_ref, wq_ref, wk_ref, bd_ref,
                   o_ref):
    C = RET_CHUNK
    bd = bd_ref[...]
    lane = lax.broadcasted_iota(jnp.int32, (C, RET_QK), 1)
    for j in range(RET_GROUP):
        r0 = j * C
        cc = cc_ref[r0:r0 + C, :]
        ss = ss_ref[r0:r0 + C, :]
        q = _rope(qk_ref[r0:r0 + C, :RET_QK], cc, ss)
        k = (_rope(qk_ref[r0:r0 + C, RET_QK:], cc, ss) * (RET_DK ** -0.5)).astype(BF16)
        v = v_ref[r0:r0 + C, :]
        inner = []
        for h in range(RET_HEADS):
            qh = jnp.where(lane // RET_DK == h, q, 0.0).astype(BF16)
            s = lax.dot_general(qh, k, (((1,), (1,)), ((), ())), preferred_element_type=F32)
            s = (s * dec_ref[h]).astype(BF16)
            inner.append(jnp.dot(s, v[:, h * RET_DV:(h + 1) * RET_DV], preferred_element_type=F32))
        qq = jnp.concatenate([q * wq_ref[0], q * wq_ref[1]], axis=1).astype(BF16)
        sf = jnp.concatenate([st_ref[0, j]] * RET_HEADS, axis=0) * bd
        sb = jnp.concatenate([st_ref[1, j]] * RET_HEADS, axis=0) * bd
        sbd = jnp.concatenate([sf, sb], axis=0).astype(BF16)
        o = jnp.concatenate(inner, axis=1) + jnp.dot(qq, sbd, preferred_element_type=F32)
        outs = []
        for h in range(RET_HEADS):
            oh = o[:, h * RET_DV:(h + 1) * RET_DV]
            outs.append(oh * lax.rsqrt(jnp.mean(oh * oh, axis=-1, keepdims=True) + EPS))
        gr = gr_ref[r0:r0 + C, :]
        o_ref[r0:r0 + C, :] = (gr * jax.nn.sigmoid(gr)) * jnp.concatenate(outs, axis=1)


def retention_out(qk, v, gr, states, tabs):
    B, S, _ = qk.shape
    rows = RET_GROUP * RET_CHUNK
    G = S // rows
    return pl.pallas_call(
        _retout_kernel,
        grid=(B, G),
        in_specs=[
            pl.BlockSpec((None, rows, 2 * RET_QK), lambda b, g: (b, g, 0)),
            pl.BlockSpec((None, rows, RET_V), lambda b, g: (b, g, 0)),
            pl.BlockSpec((None, rows, RET_V), lambda b, g: (b, g, 0)),
            pl.BlockSpec((rows, RET_QK), lambda b, g: (g, 0)),
            pl.BlockSpec((rows, RET_QK), lambda b, g: (g, 0)),
            pl.BlockSpec((None, 2, RET_GROUP, RET_DK, RET_V), lambda b, g: (b, 0, g, 0, 0)),
            _const_spec((RET_HEADS, RET_CHUNK, RET_CHUNK)),
            _const_spec((2, RET_CHUNK, RET_QK)),
            _const_spec((2, RET_CHUNK, RET_QK)),
            _const_spec((RET_QK, RET_V)),
        ],
        out_specs=pl.BlockSpec((None, rows, RET_V), lambda b, g: (b, g, 0)),
        out_shape=jax.ShapeDtypeStruct((B, S, RET_V), F32),
        compiler_params=_cparams(("arbitrary", "arbitrary")),
        name="retention_out",
    )(qk, v, gr, tabs["cc"], tabs["ss"], states, tabs["dec"], tabs["wq"], tabs["wk"], tabs["bd"])


def _merge_rows(x, yh, yr, gm, ga, whb_ref, wrb_ref, wo_ref, g, sh, sc, wr_ref, br):
    ghy = 0.5 * jnp.tanh(0.5 * gm[:, :D_MODEL]) + 0.5
    grt = 0.5 * jnp.tanh(0.5 * gm[:, D_MODEL:]) + 0.5
    m = (ghy * jnp.dot(yh.astype(BF16), whb_ref[...], preferred_element_type=F32)
         + grt * jnp.dot(yr.astype(BF16), wrb_ref[...], preferred_element_type=F32))
    out = jnp.dot(m.astype(BF16), wo_ref[...], preferred_element_type=F32)
    x = x + ga * out
    h = _norm_mod(x, g, sc, sh)
    h_hi = h.astype(BF16)
    h_lo = (h - h_hi.astype(F32)).astype(BF16)
    logits = (jnp.dot(h_hi, wr_ref[0], preferred_element_type=F32)
              + jnp.dot(h_hi, wr_ref[1], preferred_element_type=F32)
              + jnp.dot(h_lo, wr_ref[0], preferred_element_type=F32)) + br
    tm = logits.shape[0]
    lane_e = lax.broadcasted_iota(jnp.int32, logits.shape, 1).astype(F32)
    lane_o = lax.broadcasted_iota(jnp.int32, (tm, LANES), 1)
    route = jnp.zeros((tm, LANES), F32)
    vals = []
    work = logits
    for r in range(TOP_K):
        m_r = jnp.max(work, axis=-1, keepdims=True)
        i_r = jnp.min(jnp.where(work == m_r, lane_e, float(N_EXPERTS)), axis=-1, keepdims=True)
        work = jnp.where(lane_e == i_r, -jnp.inf, work)
        vals.append(m_r)
        route = jnp.where(lane_o == r, i_r, route)
    exps = [jnp.exp(v - vals[0]) for v in vals]
    den = exps[0] + exps[1] + exps[2] + exps[3]
    for r in range(TOP_K):
        route = jnp.where(lane_o == TOP_K + r, exps[r] / den, route)
    return x, h, route


def _merge_kernel(x_ref, yh_ref, yr_ref, gm_ref, ga_ref, whb_ref, wrb_ref, wo_ref,
                  g_ref, sh_ref, sc_ref, wr_ref, br_ref, xo_ref, h_ref, rt_ref, cnt_ref):
    tm = TOK_TILE

    @pl.when(pl.program_id(0) == 0)
    def _():
        cnt_ref[...] = jnp.zeros_like(cnt_ref)

    lane_e = lax.broadcasted_iota(jnp.int32, (tm, N_EXPERTS), 1).astype(F32)
    for half in range(MERGE_HALVES):
        rows = slice(half * tm, (half + 1) * tm)
        yh = jnp.concatenate([yh_ref[c, rows, :] for c in range(HY_SLABS)], axis=1)
        x, h, route = _merge_rows(x_ref[rows, :], yh, yr_ref[rows, :], gm_ref[rows, :], ga_ref[...],
                                  whb_ref, wrb_ref, wo_ref, g_ref[...], sh_ref[...], sc_ref[...],
                                  wr_ref, br_ref[...])
        xo_ref[rows, :] = x
        rt_ref[rows, :] = route
        chosen = sum((lane_e == route[:, r:r + 1]).astype(F32) for r in range(TOP_K))
        cnt_ref[...] += jnp.sum(chosen, axis=0, keepdims=True)
        h_half = h_ref.at[rows]
        for j in range(ROW_TILES):
            _store_sub(h_half, (), j, h[:, j * LANES:(j + 1) * LANES])


def merge_and_route(x2, yh, yr, gm, mod, g_ffn, whb, wrb, wo, w_router, b_router, layer):
    T, D = x2.shape
    tm = TOK_TILE * MERGE_HALVES
    per_b = SEQ // tm
    mspec = lambda piece: pl.BlockSpec((None, None, None, 1, D),
                                       lambda i: (layer, piece, i // per_b, 0, 0))
    row = lambda w: pl.BlockSpec((tm, w), lambda i: (i, 0))
    wsp = lambda a, b: pl.BlockSpec((None, a, b), lambda i: (layer, 0, 0))
    return pl.pallas_call(
        _merge_kernel,
        grid=(T // tm,),
        in_specs=[row(D),
                  pl.BlockSpec((None, HY_SLABS, tm, LANES), lambda i: (i // per_b, 0, i % per_b, 0)),
                  row(RET_V), row(2 * D), mspec(2),
                  wsp(HY_WIDTH, D), wsp(RET_V, D), wsp(D, D),
                  wsp(1, D), mspec(3), mspec(4),
                  pl.BlockSpec((None, 2, D, N_EXPERTS), lambda i: (layer, 0, 0, 0)),
                  wsp(1, N_EXPERTS)],
        out_specs=[row(D), pl.BlockSpec((tm, ROW_TILES, LANES), lambda i: (i, 0, 0)), row(LANES),
                   _const_spec((1, N_EXPERTS))],
        out_shape=[jax.ShapeDtypeStruct((T, D), F32), jax.ShapeDtypeStruct((T, ROW_TILES, LANES), F32),
                   jax.ShapeDtypeStruct((T, LANES), F32), jax.ShapeDtypeStruct((1, N_EXPERTS), F32)],
        compiler_params=_cparams(("arbitrary",)),
        name="merge_and_route",
    )(x2, yh, yr, gm, mod, whb, wrb, wo, g_ffn, mod, mod, w_router, b_router)


def _slot_kernel(rt_ref, cnt_ref, tril_ref, upper_ref, o_ref, carry_ref):
    @pl.when(pl.program_id(0) == 0)
    def _():
        carry_ref[...] = jnp.zeros_like(carry_ref)

    padded = jnp.floor((cnt_ref[...] + (EXPERT_TILE - 1)) * (1.0 / EXPERT_TILE)) * EXPERT_TILE
    pad_start = jnp.dot(jnp.broadcast_to(padded, (8, N_EXPERTS)), upper_ref[...],
                        preferred_element_type=F32, precision=HIGHEST)[0:1]

    rt = rt_ref[...]
    tm = rt.shape[0]
    lane_e = lax.broadcasted_iota(jnp.int32, (tm, N_EXPERTS), 1).astype(F32)
    lane_o = lax.broadcasted_iota(jnp.int32, (tm, LANES), 1)
    hits = [lane_e == rt[:, r:r + 1] for r in range(TOP_K)]
    onehot = sum(h.astype(F32) for h in hits)
    before = (jnp.dot(tril_ref[...], onehot.astype(BF16), preferred_element_type=F32)
              + carry_ref[...] + pad_start)
    out = jnp.zeros((tm, LANES), F32)
    for r in range(TOP_K):
        slot_r = jnp.sum(jnp.where(hits[r], before, 0.0), axis=-1, keepdims=True)
        out = jnp.where(lane_o == r, slot_r, out)
    o_ref[...] = out.astype(jnp.int32)
    carry_ref[...] += jnp.sum(onehot, axis=0, keepdims=True)


def assignment_slots(route, counts):
    T = route.shape[0]
    tm = RANK_TILE
    tril = jnp.asarray(np.tril(np.ones((tm, tm)), -1), BF16)
    upper = jnp.asarray(np.triu(np.ones((N_EXPERTS, N_EXPERTS)), 1), F32)
    return pl.pallas_call(
        _slot_kernel,
        grid=(T // tm,),
        in_specs=[pl.BlockSpec((tm, LANES), lambda i: (i, 0)), _const_spec((1, N_EXPERTS)),
                  _const_spec((tm, tm)), _const_spec((N_EXPERTS, N_EXPERTS))],
        out_specs=pl.BlockSpec((tm, LANES), lambda i: (i, 0)),
        out_shape=jax.ShapeDtypeStruct((T, LANES), jnp.int32),
        scratch_shapes=[pltpu.VMEM((1, N_EXPERTS), F32)],
        compiler_params=_cparams(("arbitrary",)),
        name="assignment_slots",
    )(route, counts, tril, upper)


def _dispatch_kernel(dest_ref, pe_ref, h_ref, xs_hbm, zeros, sem, zsem):
    tm = DISPATCH_TILE
    t0 = pl.program_id(0) * tm

    @pl.when(pl.program_id(0) == 0)
    def _():
        zeros[...] = jnp.zeros_like(zeros)

        def zero_block(e):
            end = pe_ref[e]
            begin = pe_ref[jnp.maximum(e - 1, 0)]
            nonempty = jnp.logical_or(jnp.logical_and(e == 0, end > 0), end > begin)
            start = pl.multiple_of(jnp.maximum(end - EXPERT_TILE, 0), EXPERT_TILE)
            return nonempty, pltpu.make_async_copy(zeros, xs_hbm.at[pl.ds(start, EXPERT_TILE)], zsem)

        def start_zero(e, carry):
            nonempty, cp = zero_block(e)

            @pl.when(nonempty)
            def _():
                cp.start()
            return carry
        lax.fori_loop(0, N_EXPERTS, start_zero, 0)

        def wait_zero(e, carry):
            nonempty, cp = zero_block(e)

            @pl.when(nonempty)
            def _():
                cp.wait()
            return carry
        lax.fori_loop(0, N_EXPERTS, wait_zero, 0)

    def row_copy(i, d):
        return pltpu.make_async_copy(h_ref.at[i], xs_hbm.at[d], sem)

    def issue(i, carry):
        for r in range(TOP_K):
            row_copy(i, dest_ref[(t0 + i) * TOP_K + r]).start(priority=r % 2)
        return carry
    lax.fori_loop(0, tm, issue, 0, unroll=4)

    def drain(i, carry):
        row_copy(0, 0).wait()
        return carry
    lax.fori_loop(0, tm * TOP_K, drain, 0, unroll=8)


def dispatch(dest, pad_ends, h, n_rows):
    T = h.shape[0]
    tm = DISPATCH_TILE
    grid_spec = pltpu.PrefetchScalarGridSpec(
        num_scalar_prefetch=2,
        grid=(T // tm,),
        in_specs=[pl.BlockSpec((tm, ROW_TILES, LANES), lambda i, d, pe: (i, 0, 0))],
        out_specs=pl.BlockSpec(memory_space=pl.ANY),
        scratch_shapes=[pltpu.VMEM((EXPERT_TILE, ROW_TILES, LANES), h.dtype), pltpu.SemaphoreType.DMA,
                        pltpu.SemaphoreType.DMA],
    )
    return pl.pallas_call(
        _dispatch_kernel,
        grid_spec=grid_spec,
        out_shape=jax.ShapeDtypeStruct((n_rows, ROW_TILES, LANES), h.dtype),
        compiler_params=_cparams(("arbitrary",)),
        name="dispatch",
    )(dest, pad_ends, h)


def _expert_kernel(be_ref, nb_ref, x_ref, wgu_ref, bgu_ref, wd_ref, bd_ref, perm_ref, o_ref,
                   wgu_s, wd_s):
    i = pl.program_id(0)
    prev = be_ref[jnp.maximum(i - 1, 0)]
    new_expert = jnp.logical_or(i == 0, be_ref[i] != prev)
    n_chunk = 2 * D_FF // (2 * LANES)

    @pl.when(new_expert)
    def _():
        for c in range(n_chunk):
            cols = slice(c * 2 * LANES, (c + 1) * 2 * LANES)
            wgu_s[:, cols] = jnp.dot(wgu_ref[:, cols].astype(BF16), perm_ref[...],
                                     preferred_element_type=F32).astype(BF16)
        wd_s[...] = wd_ref[...].astype(BF16)

    @pl.when(i < nb_ref[0])
    def _():
        x = jnp.concatenate([_load_sub(x_ref, (), j) for j in range(ROW_TILES)], axis=1)
        gu = jnp.dot(x.astype(BF16), wgu_s[...], preferred_element_type=F32) + bgu_ref[...]
        acts = []
        for c in range(n_chunk):
            gate = jnp.minimum(gu[:, c * 2 * LANES:c * 2 * LANES + LANES], SWIGLU_LIMIT)
            up = jnp.clip(gu[:, c * 2 * LANES + LANES:(c + 1) * 2 * LANES], -SWIGLU_LIMIT, SWIGLU_LIMIT)
            acts.append((up + 1.0) * (gate * jax.nn.sigmoid(SWIGLU_ALPHA * gate)))
        act = jnp.concatenate(acts, axis=1).astype(BF16)
        y = jnp.dot(act, wd_s[...], preferred_element_type=F32) + bd_ref[...]
        for j in range(ROW_TILES):
            _store_sub(o_ref, (), j, y[:, j * LANES:(j + 1) * LANES])

    @pl.when(i >= nb_ref[0])
    def _():
        o_ref[...] = jnp.zeros_like(o_ref)


def expert_ffn(block_e, n_used, xs, w_gate_up, bgu, w_down, bd, layer):
    P = xs.shape[0]
    D = D_MODEL
    tm = EXPERT_TILE
    perm = np.zeros((2 * LANES, 2 * LANES))
    j = np.arange(LANES)
    perm[2 * j, j] = 1.0
    perm[2 * j + 1, LANES + j] = 1.0
    grid_spec = pltpu.PrefetchScalarGridSpec(
        num_scalar_prefetch=2,
        grid=(P // tm,),
        in_specs=[
            pl.BlockSpec((tm, ROW_TILES, LANES), lambda i, be, nb: (jnp.minimum(i, nb[0] - 1), 0, 0)),
            pl.BlockSpec((None, None, D, 2 * D_FF), lambda i, be, nb: (layer, be[i], 0, 0)),
            pl.BlockSpec((None, None, 1, 2 * D_FF), lambda i, be, nb: (layer, be[i], 0, 0)),
            pl.BlockSpec((None, None, D_FF, D), lambda i, be, nb: (layer, be[i], 0, 0)),
            pl.BlockSpec((None, None, 1, D), lambda i, be, nb: (layer, be[i], 0, 0)),
            pl.BlockSpec((2 * LANES, 2 * LANES), lambda i, be, nb: (0, 0)),
        ],
        out_specs=pl.BlockSpec((tm, ROW_TILES, LANES), lambda i, be, nb: (i, 0, 0)),
        scratch_shapes=[pltpu.VMEM((D, 2 * D_FF), BF16), pltpu.VMEM((D_FF, D), BF16)],
    )
    return pl.pallas_call(
        _expert_kernel,
        grid_spec=grid_spec,
        out_shape=jax.ShapeDtypeStruct((P, ROW_TILES, LANES), F32),
        compiler_params=_cparams(("arbitrary",)),
        name="expert_ffn",
    )(block_e, n_used, xs, w_gate_up, bgu, w_down, bd, jnp.asarray(perm, BF16))


def _combine_kernel(dest_ref, x_ref, rt_ref, ga_ref, g_ref, yb_hbm, o_ref, buf, sem, *, final):
    tm = COMBINE_TILE
    step = pl.program_id(0)
    slot = step % 2

    def row_copy(sl, i, r, d):
        return pltpu.make_async_copy(yb_hbm.at[d], buf.at[sl, r, i], sem.at[sl])

    def issue_step(st, sl):
        def issue(i, carry):
            for r in range(TOP_K):
                row_copy(sl, i, r, dest_ref[(st * tm + i) * TOP_K + r]).start(priority=r % 2)
            return carry
        lax.fori_loop(0, tm, issue, 0, unroll=4)

    @pl.when(step == 0)
    def _():
        issue_step(0, 0)

    @pl.when(step + 1 < pl.num_programs(0))
    def _():
        issue_step(step + 1, 1 - slot)

    def drain(i, carry):
        row_copy(slot, 0, 0, 0).wait()
        return carry
    lax.fori_loop(0, tm * TOP_K, drain, 0, unroll=8)

    rt = rt_ref[...]
    gates = [jnp.broadcast_to(rt[:, TOP_K + r:TOP_K + r + 1], (tm, LANES)) for r in range(TOP_K)]
    cols = []
    for j in range(ROW_TILES):
        s = _load_sub(buf, (slot, 0), j) * gates[0]
        for r in range(1, TOP_K):
            s = s + _load_sub(buf, (slot, r), j) * gates[r]
        cols.append(s)
    y = jnp.concatenate(cols, axis=1)
    x = x_ref[...] + ga_ref[...] * y
    if final:
        x = (x * lax.rsqrt(jnp.mean(x * x, axis=-1, keepdims=True) + EPS)) * g_ref[...]
    o_ref[...] = x


def combine(dest, x2, route, mod, g_final, yb, layer, final):
    T, D = x2.shape
    tm = COMBINE_TILE
    per_b = SEQ // tm
    grid_spec = pltpu.PrefetchScalarGridSpec(
        num_scalar_prefetch=1,
        grid=(T // tm,),
        in_specs=[
            pl.BlockSpec((tm, D), lambda i, d: (i, 0)),
            pl.BlockSpec((tm, LANES), lambda i, d: (i, 0)),
            pl.BlockSpec((None, None, None, 1, D), lambda i, d: (layer, 5, i // per_b, 0, 0)),
            pl.BlockSpec((1, D), lambda i, d: (0, 0)),
            pl.BlockSpec(memory_space=pl.ANY),
        ],
        out_specs=pl.BlockSpec((tm, D), lambda i, d: (i, 0)),
        scratch_shapes=[pltpu.VMEM((2, TOP_K, tm, ROW_TILES, LANES), F32),
                        pltpu.SemaphoreType.DMA((2,))],
    )
    return pl.pallas_call(
        functools.partial(_combine_kernel, final=final),
        grid_spec=grid_spec,
        out_shape=jax.ShapeDtypeStruct((T, D), F32),
        compiler_params=_cparams(("arbitrary",)),
        name="combine",
    )(dest, x2, route, mod, g_final, yb)


def moe_ffn_residual(x2, h, route, counts, mod, g_final, w_gate_up, bgu, w_down, bd, layer, final):
    T = h.shape[0]
    tm = EXPERT_TILE
    slots = assignment_slots(route, counts)
    counts = counts[0].astype(jnp.int32)
    padded = ((counts + tm - 1) // tm) * tm
    pad_ends = jnp.cumsum(padded)
    n_blocks = -(-(T * TOP_K + N_EXPERTS * (tm - 1)) // tm)
    block_start = jnp.arange(n_blocks, dtype=jnp.int32) * tm
    block_e = jnp.minimum(jnp.sum(pad_ends[None, :] <= block_start[:, None], axis=1),
                          N_EXPERTS - 1).astype(jnp.int32)
    n_used = (pad_ends[-1:] // tm).astype(jnp.int32)
    dest = slots[:, :TOP_K].reshape(-1)
    xs = dispatch(dest, pad_ends.astype(jnp.int32), h, n_blocks * tm)
    yb = expert_ffn(block_e, n_used, xs, w_gate_up, bgu, w_down, bd, layer)
    return combine(dest, x2, route, mod, g_final, yb, layer, final)


def kernel(x, c, norm_mix_g, norm_ffn_g, w_mod, b_mod, w_in, hy_conv_w, hy_conv_b, hy_w1, hy_b1, hy_w2, hy_b2, hy_w3, hy_b3, hy_w4, hy_freq, hy_bias, w_hy_br, w_ret_br, w_out, w_router, b_router, w_gate_up, b_gate_up, w_down, b_down, final_g):
    B, S, D = x.shape
    L = w_mod.shape[0]
    T = B * S
    tabs = _tables()

    w_in_bf = w_in.astype(BF16)
    whb = w_hy_br.astype(BF16)
    wrb = w_ret_br.astype(BF16)
    wo = w_out.astype(BF16)
    wr_hi = w_router.astype(BF16)
    wr_split = jnp.stack([wr_hi, (w_router - wr_hi.astype(F32)).astype(BF16)], axis=1)
    E, F = N_EXPERTS, D_FF
    bgu = b_gate_up.reshape(L, E, F // LANES, LANES, 2).transpose(0, 1, 2, 4, 3).reshape(L, E, 1, 2 * F)
    bdn = b_down.reshape(L, E, 1, D)
    w1p = jnp.pad(hy_w1, ((0, 0), (0, HY_EMB_PAD - HY_EMB), (0, 0)))
    w4_hi = hy_w4.astype(BF16)
    w4_split = jnp.stack([w4_hi, (hy_w4 - w4_hi.astype(F32)).astype(BF16)], axis=1)
    vec = lambda a: a.reshape(L, 1, -1)

    c_pad = jnp.pad(c, ((0, 8 - B), (0, 0)))
    mod = modulation(c_pad, w_mod, b_mod)[:, :B]
    mod = mod.reshape(L, B, N_MOD, 1, D).transpose(0, 2, 1, 3, 4)

    x2 = x.reshape(T, D)
    for l in range(L):
        u, qk, v, gr, gm = in_projection(x2, vec(norm_mix_g), mod, w_in_bf, l)
        xs = short_conv(u.reshape(B, S, -1), hy_conv_w, vec(hy_conv_b), l)
        kraw, l1 = hyena_filter(tabs, w1p, vec(hy_b1), hy_w2, vec(hy_b2), hy_w3, vec(hy_b3),
                                w4_split, vec(hy_freq), l)
        ka = dft_first_axis(kraw.reshape(1, HY_ORDER, HY_SLABS, FFT_N1, FFT_N2, LANES), 0, tabs["f1"],
                            pair=1)
        kspec = filter_spectrum(ka, l1, tabs)
        y_hy = hyena_mixer(xs, kspec, hy_bias[l], tabs)
        qk3, v3, gr3 = qk.reshape(B, S, -1), v.reshape(B, S, -1), gr.reshape(B, S, -1)
        states = retention_states(qk3, v3, tabs)
        y_ret = retention_out(qk3, v3, gr3, states, tabs)
        x2, h_ffn, route, counts = merge_and_route(
            x2, y_hy, y_ret.reshape(T, -1), gm, mod, vec(norm_ffn_g),
            whb, wrb, wo, wr_split, vec(b_router), l)
        x2 = moe_ffn_residual(x2, h_ffn, route, counts, mod, final_g.reshape(1, D), w_gate_up, bgu,
                              w_down, bdn, l, final=(l == L - 1))
    return x2.reshape(B, S, D)
```

```python
import functools
import math

import numpy as np
import jax
import jax.numpy as jnp
from jax import lax
from jax.experimental import pallas as pl
from jax.experimental.pallas import tpu as pltpu

F32 = jnp.float32
BF16 = jnp.bfloat16
HIGHEST = lax.Precision.HIGHEST

D_MODEL = 1024
BATCH = 4
SEQ = 8192
DEPTH = 4
HY_WIDTH = 512
HY_ORDER = 2
HY_EMB = 33
HY_EMB_PAD = 64
HY_FILT_HIDDEN = 64
HY_FAST_DECAY = 0.3
HY_SLOW_DECAY = 1.5
HY_DECAY_TARGET = 1e-2
RET_HEADS = 4
RET_DK = 64
RET_DV = 128
RET_QK = RET_HEADS * RET_DK
RET_V = RET_HEADS * RET_DV
RET_CHUNK = 128
RET_DECAY_FWD = 5.0
RET_DECAY_BWD = 5.5
ROPE_BASE = 10000.0
N_EXPERTS = 32
TOP_K = 4
D_FF = D_MODEL
SWIGLU_ALPHA = 1.702
SWIGLU_LIMIT = 7.0
N_MOD = 6
EPS = 1e-6
IN_COLS = 3 * HY_WIDTH + 2 * RET_QK + 2 * RET_V + 2 * D_MODEL

LANES = 128
HY_SLABS = HY_WIDTH // LANES
ROW_TILES = D_MODEL // LANES
VMEM_LIMIT = 56 * 1024 * 1024

FFT_N = 2 * SEQ
FFT_N1 = 128
FFT_N2 = FFT_N // FFT_N1
FFT_H1 = FFT_N1 // 2

TOK_TILE = 256
MERGE_HALVES = 2
EXPERT_TILE = 512
RANK_TILE = 1024
DISPATCH_TILE = 1024
COMBINE_TILE = 512
RET_GROUP = 8
FFT_N2_BLOCK = 8
FFT_K1_GROUP = 8


def _cparams(sem):
    return pltpu.CompilerParams(dimension_semantics=sem, vmem_limit_bytes=VMEM_LIMIT)


def _const_spec(shape):
    return pl.BlockSpec(shape, lambda *_: (0,) * len(shape))


def _tables():
    n1 = np.arange(FFT_N1)
    n2 = np.arange(FFT_N2)
    ang1 = 2.0 * np.pi * np.outer(n1, n1) / FFT_N1
    f1 = np.concatenate([np.cos(ang1), -np.sin(ang1)], axis=0)
    ang2 = 2.0 * np.pi * np.outer(n2, n2) / FFT_N2
    angt = 2.0 * np.pi * np.outer(n1, n2) / FFT_N
    f1r, f1i = np.cos(ang1)[:, :FFT_H1], -np.sin(ang1)[:, :FFT_H1]
    f1c = np.block([[f1r, -f1i], [f1i, f1r]])
    gr, gi = np.cos(ang1)[:FFT_H1] / FFT_N, np.sin(ang1)[:FFT_H1] / FFT_N
    gc = np.block([[gr, -gi], [gi, gr]])

    L = SEQ
    t = np.linspace(0.0, 1.0, L)
    bands = (HY_EMB - 1) // 2
    w = 2.0 * np.pi * np.arange(L) / L
    f = np.linspace(1e-4, bands - 1, bands)
    feats = np.concatenate([t[:, None], np.cos(f[None] * w[:, None]), -np.sin(f[None] * w[:, None])], -1)
    idx = np.concatenate([np.arange(L), [0], L - np.arange(1, L)])
    feats2 = np.zeros((2 * L, HY_EMB_PAD))
    feats2[:, :HY_EMB] = feats[idx]
    max_decay = math.log(HY_DECAY_TARGET) / HY_FAST_DECAY
    min_decay = math.log(HY_DECAY_TARGET) / HY_SLOW_DECAY
    deltas = np.abs(np.linspace(min_decay, max_decay, HY_WIDTH))

    C = RET_CHUNK
    hh = np.arange(RET_HEADS)
    lgf = np.log(1.0 - np.exp2(-(RET_DECAY_FWD + hh)))
    lgb = np.log(1.0 - np.exp2(-(RET_DECAY_BWD + hh)))
    pos = np.arange(C)
    diff = pos[:, None] - pos[None, :]
    dec = np.where(diff[None] >= 0, np.exp(np.maximum(diff, 0)[None] * lgf[:, None, None]),
                   np.exp(np.maximum(-diff, 0)[None] * lgb[:, None, None]))
    lane_h = np.repeat(hh, RET_DK)
    wq = np.stack([np.exp((pos[:, None] + 1.0) * lgf[lane_h][None]),
                   np.exp((C - pos[:, None]) * lgb[lane_h][None])])
    wk = np.stack([np.exp((C - 1.0 - pos[:, None]) * lgf[lane_h][None]),
                   np.exp(pos[:, None] * lgb[lane_h][None])]) * (RET_DK ** -0.5)
    cd = np.stack([np.broadcast_to(np.exp(C * lgf[lane_h])[:, None], (RET_QK, RET_V)),
                   np.broadcast_to(np.exp(C * lgb[lane_h])[:, None], (RET_QK, RET_V))])
    col_h = np.repeat(hh, RET_DV)
    bd = (lane_h[:, None] == col_h[None, :]).astype(np.float64)
    inv_freq = 1.0 / (ROPE_BASE ** (np.arange(0, RET_DK, 2) / RET_DK))
    ang = np.arange(SEQ)[:, None] * inv_freq[None, :]
    cc = np.tile(np.concatenate([np.cos(ang), np.cos(ang)], -1), (1, RET_HEADS))
    ss = np.tile(np.concatenate([-np.sin(ang), np.sin(ang)], -1), (1, RET_HEADS))

    return dict(
        f1=jnp.asarray(f1, BF16), f1c=jnp.asarray(f1c, BF16),
        f2r=jnp.asarray(np.cos(ang2), F32), f2i=jnp.asarray(-np.sin(ang2), F32),
        twr=jnp.asarray(np.cos(angt), F32), twi=jnp.asarray(-np.sin(angt), F32),
        gc=jnp.asarray(gc, BF16),
        feats2=jnp.asarray(feats2, F32), feats2t=jnp.asarray(feats2.T, F32),
        deltas=jnp.asarray(deltas[None], F32),
        dec=jnp.asarray(dec, F32), wq=jnp.asarray(wq, F32), wk=jnp.asarray(wk, F32),
        cd=jnp.asarray(cd, F32), bd=jnp.asarray(bd, F32),
        cc=jnp.asarray(cc, F32), ss=jnp.asarray(ss, F32),
    )


def _mod_kernel(c_ref, w_ref, b_ref, o_ref):
    c = c_ref[...]
    ca = c * jax.nn.sigmoid(c)
    o_ref[...] = jnp.dot(ca, w_ref[...], preferred_element_type=F32, precision=HIGHEST) + b_ref[...]


def modulation(c_pad, w_mod, b_mod):
    L, D, N = w_mod.shape
    tn = 1536
    rows = c_pad.shape[0]
    return pl.pallas_call(
        _mod_kernel,
        grid=(L, N // tn),
        in_specs=[
            _const_spec((rows, D)),
            pl.BlockSpec((None, D, tn), lambda l, j: (l, 0, j)),
            pl.BlockSpec((None, 1, tn), lambda l, j: (l, 0, j)),
        ],
        out_specs=pl.BlockSpec((None, rows, tn), lambda l, j: (l, 0, j)),
        out_shape=jax.ShapeDtypeStruct((L, rows, N), F32),
        compiler_params=_cparams(("arbitrary", "arbitrary")),
        name="modulation",
    )(c_pad, w_mod, b_mod.reshape(L, 1, N))


def _norm_mod(x, g, sc, sh):
    y = x * lax.rsqrt(jnp.mean(x * x, axis=-1, keepdims=True) + EPS)
    return (y * g) * (1.0 + sc) + sh


def _inproj_kernel(x_ref, g_ref, sh_ref, sc_ref, w_ref, u_ref, qk_ref, v_ref, gr_ref, gm_ref):
    h = _norm_mod(x_ref[...], g_ref[...], sc_ref[...], sh_ref[...]).astype(BF16)
    c0 = 0
    for o_ref in (u_ref, qk_ref, v_ref, gr_ref, gm_ref):
        c1 = c0 + o_ref.shape[-1]
        o_ref[...] = jnp.dot(h, w_ref[:, c0:c1], preferred_element_type=F32).astype(o_ref.dtype)
        c0 = c1


def in_projection(x2, g, mod, w_in_bf, layer):
    T, D = x2.shape
    tm = TOK_TILE
    per_b = SEQ // tm
    mspec = lambda piece: pl.BlockSpec((None, None, None, 1, D),
                                       lambda i: (layer, piece, i // per_b, 0, 0))
    widths = (3 * HY_WIDTH, 2 * RET_QK, RET_V, RET_V, 2 * D_MODEL)
    dtypes = (BF16, F32, BF16, F32, F32)
    return pl.pallas_call(
        _inproj_kernel,
        grid=(T // tm,),
        in_specs=[
            pl.BlockSpec((tm, D), lambda i: (i, 0)),
            pl.BlockSpec((None, 1, D), lambda i: (layer, 0, 0)),
            mspec(0), mspec(1),
            pl.BlockSpec((None, D, IN_COLS), lambda i: (layer, 0, 0)),
        ],
        out_specs=[pl.BlockSpec((tm, w), lambda i: (i, 0)) for w in widths],
        out_shape=[jax.ShapeDtypeStruct((T, w), dt) for w, dt in zip(widths, dtypes)],
        compiler_params=_cparams(("arbitrary",)),
        name="in_projection",
    )(x2, g, mod, mod, w_in_bf)


def _shortconv_kernel(u_ref, w_ref, b_ref, o_ref):
    u = u_ref[...].astype(F32)
    s = u.shape[0]
    row = lax.broadcasted_iota(jnp.int32, u.shape, 0)
    prev = jnp.where(row == 0, 0.0, pltpu.roll(u, 1, axis=0))
    nxt = jnp.where(row == s - 1, 0.0, pltpu.roll(u, s - 1, axis=0))
    w = w_ref[...]
    o_ref[...] = prev * w[0:1] + u * w[1:2] + nxt * w[2:3] + b_ref[...]


def short_conv(u, conv_w, conv_b, layer):
    B, S, C3 = u.shape
    cb = LANES
    per = HY_WIDTH // cb
    return pl.pallas_call(
        _shortconv_kernel,
        grid=(B, C3 // cb),
        in_specs=[
            pl.BlockSpec((None, S, cb), lambda b, j: (b, 0, j)),
            pl.BlockSpec((None, 3, cb), lambda b, j: (layer, 0, j)),
            pl.BlockSpec((None, 1, cb), lambda b, j: (layer, 0, j)),
        ],
        out_specs=pl.BlockSpec((None, None, None, S, cb), lambda b, j: (j // per, b, j % per, 0, 0)),
        out_shape=jax.ShapeDtypeStruct((3, B, per, S, cb), F32),
        compiler_params=_cparams(("arbitrary", "arbitrary")),
        name="short_conv",
    )(u, conv_w, conv_b)


def _filter_kernel(p_ref, pt_ref, w1_ref, b1_ref, w2_ref, b2_ref, w3_ref, b3_ref, w4_ref, fr_ref, dl_ref,
                   k_ref, l1_ref, *, rows):
    i = pl.program_id(0)
    fr = fr_ref[...]
    tdot = lambda w, x: lax.dot_general(w, x, (((0,), (0,)), ((), ())),
                                        preferred_element_type=F32, precision=HIGHEST)
    h = jnp.sin(fr * (tdot(w1_ref[...], pt_ref[...]) + b1_ref[...]))
    h = jnp.sin(fr * (tdot(w2_ref[...], h) + b2_ref[...]))
    h = jnp.sin(fr * (tdot(w3_ref[...], h) + b3_ref[...]))
    h_hi = h.astype(BF16)
    h_lo = (h - h_hi.astype(F32)).astype(BF16)
    bdot = lambda a, b: lax.dot_general(a, b, (((0,), (0,)), ((), ())), preferred_element_type=F32)
    h = bdot(h_hi, w4_ref[0]) + bdot(h_hi, w4_ref[1]) + bdot(h_lo, w4_ref[0])
    t = p_ref[:, 0:1]
    win = jnp.exp(-t * dl_ref[...])
    win = jnp.concatenate([win] * HY_ORDER, axis=1)
    grow = i * rows + lax.broadcasted_iota(jnp.int32, (rows, 1), 0)
    k = jnp.where(grow == SEQ, 0.0, h * win)
    for s in range(k_ref.shape[0]):
        k_ref[s] = k[:, s * LANES:(s + 1) * LANES]

    @pl.when(i == 0)
    def _():
        l1_ref[...] = jnp.zeros_like(l1_ref)
    l1_ref[...] += jnp.sum(jnp.abs(k), axis=0, keepdims=True)


def hyena_filter(tabs, w1p, b1, w2, b2, w3, b3, w4, freq, layer):
    rows = 1024
    n = 2 * SEQ
    half = SEQ // rows
    H = HY_FILT_HIDDEN
    OC = HY_ORDER * HY_WIDTH
    vec = lambda: pl.BlockSpec((None, H, 1), lambda i: (layer, 0, 0))
    return pl.pallas_call(
        functools.partial(_filter_kernel, rows=rows),
        grid=(n // rows,),
        in_specs=[
            pl.BlockSpec((rows, HY_EMB_PAD), lambda i: (i, 0)),
            pl.BlockSpec((HY_EMB_PAD, rows), lambda i: (0, i)),
            pl.BlockSpec((None, HY_EMB_PAD, H), lambda i: (layer, 0, 0)), vec(),
            pl.BlockSpec((None, H, H), lambda i: (layer, 0, 0)), vec(),
            pl.BlockSpec((None, H, H), lambda i: (layer, 0, 0)), vec(),
            pl.BlockSpec((None, 2, H, OC), lambda i: (layer, 0, 0, i // half)),
            vec(),
            _const_spec((1, HY_WIDTH)),
        ],
        out_specs=[pl.BlockSpec((OC // LANES, rows, LANES), lambda i: (0, i, 0)), _const_spec((1, OC))],
        out_shape=[jax.ShapeDtypeStruct((OC // LANES, n, LANES), F32),
                   jax.ShapeDtypeStruct((1, OC), F32)],
        compiler_params=_cparams(("arbitrary",)),
        name="hyena_filter",
    )(tabs["feats2"], tabs["feats2t"], w1p, b1, w2, b2, w3, b3, w4, freq, tabs["deltas"])


def _load_sub(ref, lead, j):
    *outer, r, s, l = ref.shape
    flat = ref.reshape(*outer, r * s, l)
    return flat[(*lead, pl.ds(j, r, stride=s), slice(None))]


def _store_sub(ref, lead, j, val):
    *outer, r, s, l = ref.shape
    flat = ref.reshape(*outer, r * s, l)
    flat[(*lead, pl.ds(j, r, stride=s), slice(None))] = val


def _pack_pair(re, im):
    return lax.bitcast_convert_type(pltpu.pack_elementwise([im, re], packed_dtype=BF16), jnp.uint32)


def _unpack_pair(p):
    p = lax.bitcast_convert_type(p, jnp.int32)
    im = pltpu.unpack_elementwise(p, index=0, packed_dtype=BF16, unpacked_dtype=F32)
    re = pltpu.unpack_elementwise(p, index=1, packed_dtype=BF16, unpacked_dtype=F32)
    return re, im


def _dft1_kernel(x_ref, f_ref, o_ref):
    for j in range(FFT_N2_BLOCK):
        xs = jnp.concatenate(
            [jnp.concatenate([_load_sub(x_ref, (p, c), j) for c in range(HY_SLABS)], axis=1)
             for p in range(x_ref.shape[0])], axis=0)
        a = jnp.dot(f_ref[...], xs.astype(BF16), preferred_element_type=F32)
        p = _pack_pair(a[:FFT_N1], a[FFT_N1:])
        for c in range(HY_SLABS):
            _store_sub(o_ref, (c,), j, p[:, c * LANES:(c + 1) * LANES])


def dft_first_axis(x6, which, f1, pair):
    _, B, _, rows, _, _ = x6.shape
    nb = FFT_N2_BLOCK
    return pl.pallas_call(
        _dft1_kernel,
        grid=(B // pair, FFT_N2 // nb),
        in_specs=[pl.BlockSpec((None, pair, HY_SLABS, rows, nb, LANES),
                               lambda b, j: (which, b, 0, 0, j, 0)),
                  _const_spec((2 * FFT_N1, pair * rows))],
        out_specs=pl.BlockSpec((None, HY_SLABS, FFT_N1, nb, LANES), lambda b, j: (b, 0, 0, j, 0)),
        out_shape=jax.ShapeDtypeStruct((B // pair, HY_SLABS, FFT_N1, FFT_N2, LANES), jnp.uint32),
        compiler_params=_cparams(("arbitrary", "arbitrary")),
        name="dft_first_axis",
    )(x6, f1)


def _twiddled_stack(f2r_ref, f2i_ref, twr_ref, twi_ref, k1):
    tr = twr_ref[pl.ds(k1, 1), :]
    ti = twi_ref[pl.ds(k1, 1), :]
    fr = f2r_ref[...]
    fi = f2i_ref[...]
    p = fr * tr - fi * ti
    q = fr * ti + fi * tr
    top = jnp.concatenate([p, -q], axis=1)
    bot = jnp.concatenate([q, p], axis=1)
    return jnp.concatenate([top, bot], axis=0).astype(BF16)


def _load_k1(a_ref, j):
    p = jnp.concatenate([a_ref[c, j] for c in range(HY_SLABS)], axis=1)
    re, im = _unpack_pair(p)
    return jnp.concatenate([re, im], axis=0).astype(BF16)


def _spectrum_kernel(a_ref, l1_ref, f2r_ref, f2i_ref, twr_ref, twi_ref, o_ref):
    base = pl.program_id(0) * FFT_K1_GROUP
    inv = 1.0 / l1_ref[...]
    for j in range(FFT_K1_GROUP):
        r = _twiddled_stack(f2r_ref, f2i_ref, twr_ref, twi_ref, base + j)
        x = jnp.dot(r, _load_k1(a_ref, j), preferred_element_type=F32) * inv
        o_ref[j, 0] = x[:FFT_N2].astype(o_ref.dtype)
        o_ref[j, 1] = x[FFT_N2:].astype(o_ref.dtype)


def filter_spectrum(a5, l1, tabs):
    g = FFT_K1_GROUP
    sq = lambda: _const_spec((FFT_N2, FFT_N2))
    return pl.pallas_call(
        _spectrum_kernel,
        grid=(FFT_N1 // g, HY_ORDER),
        in_specs=[pl.BlockSpec((None, HY_SLABS, g, FFT_N2, LANES), lambda i, o: (o, 0, i, 0, 0)),
                  pl.BlockSpec((1, HY_WIDTH), lambda i, o: (0, o)), sq(), sq(), sq(), sq()],
        out_specs=pl.BlockSpec((g, 2, FFT_N2, HY_WIDTH), lambda i, o: (i, 0, 0, o)),
        out_shape=jax.ShapeDtypeStruct((FFT_N1, 2, FFT_N2, HY_ORDER * HY_WIDTH), BF16),
        compiler_params=_cparams(("arbitrary", "arbitrary")),
        name="filter_spectrum",
    )(a5, l1, tabs["f2r"], tabs["f2i"], tabs["twr"], tabs["twi"])


def _convmid_kernel(a_ref, ks_ref, f2r_ref, f2i_ref, twr_ref, twi_ref, o_ref):
    base = pl.program_id(0) * FFT_K1_GROUP
    for j in range(FFT_K1_GROUP):
        r = _twiddled_stack(f2r_ref, f2i_ref, twr_ref, twi_ref, base + j)
        x = jnp.dot(r, _load_k1(a_ref, j), preferred_element_type=F32)
        xr, xi = x[:FFT_N2], x[FFT_N2:]
        kr, ki = ks_ref[j, 0].astype(F32), ks_ref[j, 1].astype(F32)
        y = jnp.concatenate([xr * kr - xi * ki, xr * ki + xi * kr], axis=0).astype(BF16)
        b = lax.dot_general(r, y, (((0,), (0,)), ((), ())), preferred_element_type=F32)
        p = _pack_pair(b[:FFT_N2], b[FFT_N2:])
        for c in range(HY_SLABS):
            o_ref[c, j] = p[:, c * LANES:(c + 1) * LANES]


def conv_mid(a5, kspec, order, tabs):
    B = a5.shape[0]
    g = FFT_K1_GROUP
    sq = lambda: _const_spec((FFT_N2, FFT_N2))
    blk = lambda: pl.BlockSpec((None, HY_SLABS, g, FFT_N2, LANES), lambda i, b: (b, 0, i, 0, 0))
    return pl.pallas_call(
        _convmid_kernel,
        grid=(FFT_N1 // g, B),
        in_specs=[blk(), pl.BlockSpec((g, 2, FFT_N2, HY_WIDTH), lambda i, b: (i, 0, 0, order)),
                  sq(), sq(), sq(), sq()],
        out_specs=blk(),
        out_shape=jax.ShapeDtypeStruct(a5.shape, jnp.uint32),
        compiler_params=_cparams(("arbitrary", "arbitrary")),
        name="conv_mid",
    )(a5, kspec, tabs["f2r"], tabs["f2i"], tabs["twr"], tabs["twi"])


def _convout_kernel(b_ref, g_ref, z_ref, gate_ref, bias_ref, o_ref):
    for j in range(FFT_N2_BLOCK):
        p = jnp.concatenate([_load_sub(b_ref, (c,), j) for c in range(HY_SLABS)], axis=1)
        re, im = _unpack_pair(p)
        bp = jnp.concatenate([re, im], axis=0).astype(BF16)
        y = jnp.dot(g_ref[...], bp, preferred_element_type=F32)
        for q in range(2):
            for c in range(HY_SLABS):
                _store_sub(o_ref, (q, c), j, y[q * FFT_H1:(q + 1) * FFT_H1, c * LANES:(c + 1) * LANES])
    o_ref[...] = gate_ref[...] * (o_ref[...] + z_ref[...] * bias_ref[...])


def conv_out(bp5, z6, z_which, gate6, gate_which, bias4, g):
    B2 = bp5.shape[0]
    nb = FFT_N2_BLOCK
    nat = lambda which: pl.BlockSpec((None, 2, HY_SLABS, FFT_H1, nb, LANES),
                                     lambda b, j: (which, b, 0, 0, j, 0))
    return pl.pallas_call(
        _convout_kernel,
        grid=(B2, FFT_N2 // nb),
        in_specs=[pl.BlockSpec((None, HY_SLABS, FFT_N1, nb, LANES), lambda b, j: (b, 0, 0, j, 0)),
                  _const_spec((2 * FFT_H1, 2 * FFT_N1)), nat(z_which), nat(gate_which),
                  _const_spec((HY_SLABS, 1, 1, LANES))],
        out_specs=pl.BlockSpec((2, HY_SLABS, FFT_H1, nb, LANES), lambda b, j: (b, 0, 0, j, 0)),
        out_shape=jax.ShapeDtypeStruct((2 * B2, HY_SLABS, FFT_H1, FFT_N2, LANES), F32),
        compiler_params=_cparams(("arbitrary", "arbitrary")),
        name="conv_out",
    )(bp5, g, z6, gate6, bias4)


def hyena_mixer(xs, kspec, bias, tabs):
    B = xs.shape[1]
    xs6 = xs.reshape(3, B, HY_SLABS, FFT_H1, FFT_N2, LANES)
    z6, z_which = xs6, 2
    for o in range(HY_ORDER):
        a = dft_first_axis(z6, z_which, tabs["f1c"], pair=2)
        bp = conv_mid(a, kspec, o, tabs)
        z = conv_out(bp, z6, z_which, xs6, o, bias[o].reshape(HY_SLABS, 1, 1, LANES), tabs["gc"])
        z6, z_which = z[None], 0
    return z.reshape(B, HY_SLABS, SEQ, LANES)


def _rope(x, cc, ss):
    lane = lax.broadcasted_iota(jnp.int32, x.shape, 1)
    n = x.shape[1]
    half = RET_DK // 2
    swapped = jnp.where(lane % RET_DK < half, pltpu.roll(x, n - half, axis=1),
                        pltpu.roll(x, half, axis=1))
    return x * cc + swapped * ss


def _retstate_kernel(k_ref, v_ref, cc_ref, ss_ref, wk_ref, cd_ref, o_ref, s_ref):
    d = pl.program_id(1)
    g = pl.program_id(2)
    C = RET_CHUNK

    @pl.when(g == 0)
    def _():
        s_ref[...] = jnp.zeros_like(s_ref)

    for j in range(RET_GROUP):
        ci = jnp.where(d == 0, j, RET_GROUP - 1 - j)
        r0 = pl.multiple_of(ci * C, C)
        s = s_ref[...]
        o_ref[ci] = jnp.concatenate(
            [s[h * RET_DK:(h + 1) * RET_DK, h * RET_DV:(h + 1) * RET_DV] for h in range(RET_HEADS)],
            axis=1)
        k = _rope(k_ref[pl.ds(r0, C), :], cc_ref[pl.ds(r0, C), :], ss_ref[pl.ds(r0, C), :])
        kw = (k * wk_ref[...]).astype(BF16)
        ds = lax.dot_general(kw, v_ref[pl.ds(r0, C), :], (((0,), (0,)), ((), ())),
                             preferred_element_type=F32)
        s_ref[...] = s * cd_ref[...] + ds


def retention_states(qk, v, tabs):
    B, S, _ = qk.shape
    rows = RET_GROUP * RET_CHUNK
    G = S // rows
    grp = lambda d, g: jnp.where(d == 0, g, G - 1 - g)
    return pl.pallas_call(
        _retstate_kernel,
        grid=(B, 2, G),
        in_specs=[
            pl.BlockSpec((None, rows, RET_QK), lambda b, d, g: (b, grp(d, g), 1)),
            pl.BlockSpec((None, rows, RET_V), lambda b, d, g: (b, grp(d, g), 0)),
            pl.BlockSpec((rows, RET_QK), lambda b, d, g: (grp(d, g), 0)),
            pl.BlockSpec((rows, RET_QK), lambda b, d, g: (grp(d, g), 0)),
            pl.BlockSpec((None, RET_CHUNK, RET_QK), lambda b, d, g: (d, 0, 0)),
            pl.BlockSpec((None, RET_QK, RET_V), lambda b, d, g: (d, 0, 0)),
        ],
        out_specs=pl.BlockSpec((None, None, RET_GROUP, RET_DK, RET_V),
                               lambda b, d, g: (b, d, grp(d, g), 0, 0)),
        out_shape=jax.ShapeDtypeStruct((B, 2, S // RET_CHUNK, RET_DK, RET_V), F32),
        scratch_shapes=[pltpu.VMEM((RET_QK, RET_V), F32)],
        compiler_params=_cparams(("arbitrary", "arbitrary", "arbitrary")),
        name="retention_states",
    )(qk, v, tabs["cc"], tabs["ss"], tabs["wk"], tabs["cd"])


def _retout_kernel(qk_ref, v_ref, gr_ref, cc_ref, ss_ref, st_ref, dec_ref, wq_ref, wk_ref, bd_ref,
                   o_ref):
    C = RET_CHUNK
    bd = bd_ref[...]
    lane = lax.broadcasted_iota(jnp.int32, (C, RET_QK), 1)
    for j in range(RET_GROUP):
        r0 = j * C
        cc = cc_ref[r0:r0 + C, :]
        ss = ss_ref[r0:r0 + C, :]
        q = _rope(qk_ref[r0:r0 + C, :RET_QK], cc, ss)
        k = (_rope(qk_ref[r0:r0 + C, RET_QK:], cc, ss) * (RET_DK ** -0.5)).astype(BF16)
        v = v_ref[r0:r0 + C, :]
        inner = []
        for h in range(RET_HEADS):
            qh = jnp.where(lane // RET_DK == h, q, 0.0).astype(BF16)
            s = lax.dot_general(qh, k, (((1,), (1,)), ((), ())), preferred_element_type=F32)
            s = (s * dec_ref[h]).astype(BF16)
            inner.append(jnp.dot(s, v[:, h * RET_DV:(h + 1) * RET_DV], preferred_element_type=F32))
        qq = jnp.concatenate([q * wq_ref[0], q * wq_ref[1]], axis=1).astype(BF16)
        sf = jnp.concatenate([st_ref[0, j]] * RET_HEADS, axis=0) * bd
        sb = jnp.concatenate([st_ref[1, j]] * RET_HEADS, axis=0) * bd
        sbd = jnp.concatenate([sf, sb], axis=0).astype(BF16)
        o = jnp.concatenate(inner, axis=1) + jnp.dot(qq, sbd, preferred_element_type=F32)
        outs = []
        for h in range(RET_HEADS):
            oh = o[:, h * RET_DV:(h + 1) * RET_DV]
            outs.append(oh * lax.rsqrt(jnp.mean(oh * oh, axis=-1, keepdims=True) + EPS))
        gr = gr_ref[r0:r0 + C, :]
        o_ref[r0:r0 + C, :] = (gr * jax.nn.sigmoid(gr)) * jnp.concatenate(outs, axis=1)


def retention_out(qk, v, gr, states, tabs):
    B, S, _ = qk.shape
    rows = RET_GROUP * RET_CHUNK
    G = S // rows
    return pl.pallas_call(
        _retout_kernel,
        grid=(B, G),
        in_specs=[
            pl.BlockSpec((None, rows, 2 * RET_QK), lambda b, g: (b, g, 0)),
            pl.BlockSpec((None, rows, RET_V), lambda b, g: (b, g, 0)),
            pl.BlockSpec((None, rows, RET_V), lambda b, g: (b, g, 0)),
            pl.BlockSpec((rows, RET_QK), lambda b, g: (g, 0)),
            pl.BlockSpec((rows, RET_QK), lambda b, g: (g, 0)),
            pl.BlockSpec((None, 2, RET_GROUP, RET_DK, RET_V), lambda b, g: (b, 0, g, 0, 0)),
            _const_spec((RET_HEADS, RET_CHUNK, RET_CHUNK)),
            _const_spec((2, RET_CHUNK, RET_QK)),
            _const_spec((2, RET_CHUNK, RET_QK)),
            _const_spec((RET_QK, RET_V)),
        ],
        out_specs=pl.BlockSpec((None, rows, RET_V), lambda b, g: (b, g, 0)),
        out_shape=jax.ShapeDtypeStruct((B, S, RET_V), F32),
        compiler_params=_cparams(("arbitrary", "arbitrary")),
        name="retention_out",
    )(qk, v, gr, tabs["cc"], tabs["ss"], states, tabs["dec"], tabs["wq"], tabs["wk"], tabs["bd"])


def _merge_rows(x, yh, yr, gm, ga, whb_ref, wrb_ref, wo_ref, g, sh, sc, wr_ref, br):
    ghy = 0.5 * jnp.tanh(0.5 * gm[:, :D_MODEL]) + 0.5
    grt = 0.5 * jnp.tanh(0.5 * gm[:, D_MODEL:]) + 0.5
    m = (ghy * jnp.dot(yh.astype(BF16), whb_ref[...], preferred_element_type=F32)
         + grt * jnp.dot(yr.astype(BF16), wrb_ref[...], preferred_element_type=F32))
    out = jnp.dot(m.astype(BF16), wo_ref[...], preferred_element_type=F32)
    x = x + ga * out
    h = _norm_mod(x, g, sc, sh)
    h_hi = h.astype(BF16)
    h_lo = (h - h_hi.astype(F32)).astype(BF16)
    logits = (jnp.dot(h_hi, wr_ref[0], preferred_element_type=F32)
              + jnp.dot(h_hi, wr_ref[1], preferred_element_type=F32)
              + jnp.dot(h_lo, wr_ref[0], preferred_element_type=F32)) + br
    tm = logits.shape[0]
    lane_e = lax.broadcasted_iota(jnp.int32, logits.shape, 1).astype(F32)
    lane_o = lax.broadcasted_iota(jnp.int32, (tm, LANES), 1)
    route = jnp.zeros((tm, LANES), F32)
    vals = []
    work = logits
    for r in range(TOP_K):
        m_r = jnp.max(work, axis=-1, keepdims=True)
        i_r = jnp.min(jnp.where(work == m_r, lane_e, float(N_EXPERTS)), axis=-1, keepdims=True)
        work = jnp.where(lane_e == i_r, -jnp.inf, work)
        vals.append(m_r)
        route = jnp.where(lane_o == r, i_r, route)
    exps = [jnp.exp(v - vals[0]) for v in vals]
    den = exps[0] + exps[1] + exps[2] + exps[3]
    for r in range(TOP_K):
        route = jnp.where(lane_o == TOP_K + r, exps[r] / den, route)
    return x, h, route


def _merge_kernel(x_ref, yh_ref, yr_ref, gm_ref, ga_ref, whb_ref, wrb_ref, wo_ref,
                  g_ref, sh_ref, sc_ref, wr_ref, br_ref, xo_ref, h_ref, rt_ref, cnt_ref):
    tm = TOK_TILE

    @pl.when(pl.program_id(0) == 0)
    def _():
        cnt_ref[...] = jnp.zeros_like(cnt_ref)

    lane_e = lax.broadcasted_iota(jnp.int32, (tm, N_EXPERTS), 1).astype(F32)
    for half in range(MERGE_HALVES):
        rows = slice(half * tm, (half + 1) * tm)
        yh = jnp.concatenate([yh_ref[c, rows, :] for c in range(HY_SLABS)], axis=1)
        x, h, route = _merge_rows(x_ref[rows, :], yh, yr_ref[rows, :], gm_ref[rows, :], ga_ref[...],
                                  whb_ref, wrb_ref, wo_ref, g_ref[...], sh_ref[...], sc_ref[...],
                                  wr_ref, br_ref[...])
        xo_ref[rows, :] = x
        rt_ref[rows, :] = route
        chosen = sum((lane_e == route[:, r:r + 1]).astype(F32) for r in range(TOP_K))
        cnt_ref[...] += jnp.sum(chosen, axis=0, keepdims=True)
        h_half = h_ref.at[rows]
        for j in range(ROW_TILES):
            _store_sub(h_half, (), j, h[:, j * LANES:(j + 1) * LANES])


def merge_and_route(x2, yh, yr, gm, mod, g_ffn, whb, wrb, wo, w_router, b_router, layer):
    T, D = x2.shape
    tm = TOK_TILE * MERGE_HALVES
    per_b = SEQ // tm
    mspec = lambda piece: pl.BlockSpec((None, None, None, 1, D),
                                       lambda i: (layer, piece, i // per_b, 0, 0))
    row = lambda w: pl.BlockSpec((tm, w), lambda i: (i, 0))
    wsp = lambda a, b: pl.BlockSpec((None, a, b), lambda i: (layer, 0, 0))
    return pl.pallas_call(
        _merge_kernel,
        grid=(T // tm,),
        in_specs=[row(D),
                  pl.BlockSpec((None, HY_SLABS, tm, LANES), lambda i: (i // per_b, 0, i % per_b, 0)),
                  row(RET_V), row(2 * D), mspec(2),
                  wsp(HY_WIDTH, D), wsp(RET_V, D), wsp(D, D),
                  wsp(1, D), mspec(3), mspec(4),
                  pl.BlockSpec((None, 2, D, N_EXPERTS), lambda i: (layer, 0, 0, 0)),
                  wsp(1, N_EXPERTS)],
        out_specs=[row(D), pl.BlockSpec((tm, ROW_TILES, LANES), lambda i: (i, 0, 0)), row(LANES),
                   _const_spec((1, N_EXPERTS))],
        out_shape=[jax.ShapeDtypeStruct((T, D), F32), jax.ShapeDtypeStruct((T, ROW_TILES, LANES), F32),
                   jax.ShapeDtypeStruct((T, LANES), F32), jax.ShapeDtypeStruct((1, N_EXPERTS), F32)],
        compiler_params=_cparams(("arbitrary",)),
        name="merge_and_route",
    )(x2, yh, yr, gm, mod, whb, wrb, wo, g_ffn, mod, mod, w_router, b_router)


def _slot_kernel(rt_ref, cnt_ref, tril_ref, upper_ref, o_ref, carry_ref):
    @pl.when(pl.program_id(0) == 0)
    def _():
        carry_ref[...] = jnp.zeros_like(carry_ref)

    padded = jnp.floor((cnt_ref[...] + (EXPERT_TILE - 1)) * (1.0 / EXPERT_TILE)) * EXPERT_TILE
    pad_start = jnp.dot(jnp.broadcast_to(padded, (8, N_EXPERTS)), upper_ref[...],
                        preferred_element_type=F32, precision=HIGHEST)[0:1]

    rt = rt_ref[...]
    tm = rt.shape[0]
    lane_e = lax.broadcasted_iota(jnp.int32, (tm, N_EXPERTS), 1).astype(F32)
    lane_o = lax.broadcasted_iota(jnp.int32, (tm, LANES), 1)
    hits = [lane_e == rt[:, r:r + 1] for r in range(TOP_K)]
    onehot = sum(h.astype(F32) for h in hits)
    before = (jnp.dot(tril_ref[...], onehot.astype(BF16), preferred_element_type=F32)
              + carry_ref[...] + pad_start)
    out = jnp.zeros((tm, LANES), F32)
    for r in range(TOP_K):
        slot_r = jnp.sum(jnp.where(hits[r], before, 0.0), axis=-1, keepdims=True)
        out = jnp.where(lane_o == r, slot_r, out)
    o_ref[...] = out.astype(jnp.int32)
    carry_ref[...] += jnp.sum(onehot, axis=0, keepdims=True)


def assignment_slots(route, counts):
    T = route.shape[0]
    tm = RANK_TILE
    tril = jnp.asarray(np.tril(np.ones((tm, tm)), -1), BF16)
    upper = jnp.asarray(np.triu(np.ones((N_EXPERTS, N_EXPERTS)), 1), F32)
    return pl.pallas_call(
        _slot_kernel,
        grid=(T // tm,),
        in_specs=[pl.BlockSpec((tm, LANES), lambda i: (i, 0)), _const_spec((1, N_EXPERTS)),
                  _const_spec((tm, tm)), _const_spec((N_EXPERTS, N_EXPERTS))],
        out_specs=pl.BlockSpec((tm, LANES), lambda i: (i, 0)),
        out_shape=jax.ShapeDtypeStruct((T, LANES), jnp.int32),
        scratch_shapes=[pltpu.VMEM((1, N_EXPERTS), F32)],
        compiler_params=_cparams(("arbitrary",)),
        name="assignment_slots",
    )(route, counts, tril, upper)


def _dispatch_kernel(dest_ref, pe_ref, h_ref, xs_hbm, zeros, sem, zsem):
    tm = DISPATCH_TILE
    t0 = pl.program_id(0) * tm

    @pl.when(pl.program_id(0) == 0)
    def _():
        zeros[...] = jnp.zeros_like(zeros)

        def zero_block(e):
            end = pe_ref[e]
            begin = pe_ref[jnp.maximum(e - 1, 0)]
            nonempty = jnp.logical_or(jnp.logical_and(e == 0, end > 0), end > begin)
            start = pl.multiple_of(jnp.maximum(end - EXPERT_TILE, 0), EXPERT_TILE)
            return nonempty, pltpu.make_async_copy(zeros, xs_hbm.at[pl.ds(start, EXPERT_TILE)], zsem)

        def start_zero(e, carry):
            nonempty, cp = zero_block(e)

            @pl.when(nonempty)
            def _():
                cp.start()
            return carry
        lax.fori_loop(0, N_EXPERTS, start_zero, 0)

        def wait_zero(e, carry):
            nonempty, cp = zero_block(e)

            @pl.when(nonempty)
            def _():
                cp.wait()
            return carry
        lax.fori_loop(0, N_EXPERTS, wait_zero, 0)

    def row_copy(i, d):
        return pltpu.make_async_copy(h_ref.at[i], xs_hbm.at[d], sem)

    def issue(i, carry):
        for r in range(TOP_K):
            row_copy(i, dest_ref[(t0 + i) * TOP_K + r]).start(priority=r % 2)
        return carry
    lax.fori_loop(0, tm, issue, 0, unroll=4)

    def drain(i, carry):
        row_copy(0, 0).wait()
        return carry
    lax.fori_loop(0, tm * TOP_K, drain, 0, unroll=8)


def dispatch(dest, pad_ends, h, n_rows):
    T = h.shape[0]
    tm = DISPATCH_TILE
    grid_spec = pltpu.PrefetchScalarGridSpec(
        num_scalar_prefetch=2,
        grid=(T // tm,),
        in_specs=[pl.BlockSpec((tm, ROW_TILES, LANES), lambda i, d, pe: (i, 0, 0))],
        out_specs=pl.BlockSpec(memory_space=pl.ANY),
        scratch_shapes=[pltpu.VMEM((EXPERT_TILE, ROW_TILES, LANES), h.dtype), pltpu.SemaphoreType.DMA,
                        pltpu.SemaphoreType.DMA],
    )
    return pl.pallas_call(
        _dispatch_kernel,
        grid_spec=grid_spec,
        out_shape=jax.ShapeDtypeStruct((n_rows, ROW_TILES, LANES), h.dtype),
        compiler_params=_cparams(("arbitrary",)),
        name="dispatch",
    )(dest, pad_ends, h)


def _expert_kernel(be_ref, nb_ref, x_ref, wgu_ref, bgu_ref, wd_ref, bd_ref, perm_ref, o_ref,
                   wgu_s, wd_s):
    i = pl.program_id(0)
    prev = be_ref[jnp.maximum(i - 1, 0)]
    new_expert = jnp.logical_or(i == 0, be_ref[i] != prev)
    n_chunk = 2 * D_FF // (2 * LANES)

    @pl.when(new_expert)
    def _():
        for c in range(n_chunk):
            cols = slice(c * 2 * LANES, (c + 1) * 2 * LANES)
            wgu_s[:, cols] = jnp.dot(wgu_ref[:, cols].astype(BF16), perm_ref[...],
                                     preferred_element_type=F32).astype(BF16)
        wd_s[...] = wd_ref[...].astype(BF16)

    @pl.when(i < nb_ref[0])
    def _():
        x = jnp.concatenate([_load_sub(x_ref, (), j) for j in range(ROW_TILES)], axis=1)
        gu = jnp.dot(x.astype(BF16), wgu_s[...], preferred_element_type=F32) + bgu_ref[...]
        acts = []
        for c in range(n_chunk):
            gate = jnp.minimum(gu[:, c * 2 * LANES:c * 2 * LANES + LANES], SWIGLU_LIMIT)
            up = jnp.clip(gu[:, c * 2 * LANES + LANES:(c + 1) * 2 * LANES], -SWIGLU_LIMIT, SWIGLU_LIMIT)
            acts.append((up + 1.0) * (gate * jax.nn.sigmoid(SWIGLU_ALPHA * gate)))
        act = jnp.concatenate(acts, axis=1).astype(BF16)
        y = jnp.dot(act, wd_s[...], preferred_element_type=F32) + bd_ref[...]
        for j in range(ROW_TILES):
            _store_sub(o_ref, (), j, y[:, j * LANES:(j + 1) * LANES])

    @pl.when(i >= nb_ref[0])
    def _():
        o_ref[...] = jnp.zeros_like(o_ref)


def expert_ffn(block_e, n_used, xs, w_gate_up, bgu, w_down, bd, layer):
    P = xs.shape[0]
    D = D_MODEL
    tm = EXPERT_TILE
    perm = np.zeros((2 * LANES, 2 * LANES))
    j = np.arange(LANES)
    perm[2 * j, j] = 1.0
    perm[2 * j + 1, LANES + j] = 1.0
    grid_spec = pltpu.PrefetchScalarGridSpec(
        num_scalar_prefetch=2,
        grid=(P // tm,),
        in_specs=[
            pl.BlockSpec((tm, ROW_TILES, LANES), lambda i, be, nb: (jnp.minimum(i, nb[0] - 1), 0, 0)),
            pl.BlockSpec((None, None, D, 2 * D_FF), lambda i, be, nb: (layer, be[i], 0, 0)),
            pl.BlockSpec((None, None, 1, 2 * D_FF), lambda i, be, nb: (layer, be[i], 0, 0)),
            pl.BlockSpec((None, None, D_FF, D), lambda i, be, nb: (layer, be[i], 0, 0)),
            pl.BlockSpec((None, None, 1, D), lambda i, be, nb: (layer, be[i], 0, 0)),
            pl.BlockSpec((2 * LANES, 2 * LANES), lambda i, be, nb: (0, 0)),
        ],
        out_specs=pl.BlockSpec((tm, ROW_TILES, LANES), lambda i, be, nb: (i, 0, 0)),
        scratch_shapes=[pltpu.VMEM((D, 2 * D_FF), BF16), pltpu.VMEM((D_FF, D), BF16)],
    )
    return pl.pallas_call(
        _expert_kernel,
        grid_spec=grid_spec,
        out_shape=jax.ShapeDtypeStruct((P, ROW_TILES, LANES), F32),
        compiler_params=_cparams(("arbitrary",)),
        name="expert_ffn",
    )(block_e, n_used, xs, w_gate_up, bgu, w_down, bd, jnp.asarray(perm, BF16))


def _combine_kernel(dest_ref, x_ref, rt_ref, ga_ref, g_ref, yb_hbm, o_ref, buf, sem, *, final):
    tm = COMBINE_TILE
    step = pl.program_id(0)
    slot = step % 2

    def row_copy(sl, i, r, d):
        return pltpu.make_async_copy(yb_hbm.at[d], buf.at[sl, r, i], sem.at[sl])

    def issue_step(st, sl):
        def issue(i, carry):
            for r in range(TOP_K):
                row_copy(sl, i, r, dest_ref[(st * tm + i) * TOP_K + r]).start(priority=r % 2)
            return carry
        lax.fori_loop(0, tm, issue, 0, unroll=4)

    @pl.when(step == 0)
    def _():
        issue_step(0, 0)

    @pl.when(step + 1 < pl.num_programs(0))
    def _():
        issue_step(step + 1, 1 - slot)

    def drain(i, carry):
        row_copy(slot, 0, 0, 0).wait()
        return carry
    lax.fori_loop(0, tm * TOP_K, drain, 0, unroll=8)

    rt = rt_ref[...]
    gates = [jnp.broadcast_to(rt[:, TOP_K + r:TOP_K + r + 1], (tm, LANES)) for r in range(TOP_K)]
    cols = []
    for j in range(ROW_TILES):
        s = _load_sub(buf, (slot, 0), j) * gates[0]
        for r in range(1, TOP_K):
            s = s + _load_sub(buf, (slot, r), j) * gates[r]
        cols.append(s)
    y = jnp.concatenate(cols, axis=1)
    x = x_ref[...] + ga_ref[...] * y
    if final:
        x = (x * lax.rsqrt(jnp.mean(x * x, axis=-1, keepdims=True) + EPS)) * g_ref[...]
    o_ref[...] = x


def combine(dest, x2, route, mod, g_final, yb, layer, final):
    T, D = x2.shape
    tm = COMBINE_TILE
    per_b = SEQ // tm
    grid_spec = pltpu.PrefetchScalarGridSpec(
        num_scalar_prefetch=1,
        grid=(T // tm,),
        in_specs=[
            pl.BlockSpec((tm, D), lambda i, d: (i, 0)),
            pl.BlockSpec((tm, LANES), lambda i, d: (i, 0)),
            pl.BlockSpec((None, None, None, 1, D), lambda i, d: (layer, 5, i // per_b, 0, 0)),
            pl.BlockSpec((1, D), lambda i, d: (0, 0)),
            pl.BlockSpec(memory_space=pl.ANY),
        ],
        out_specs=pl.BlockSpec((tm, D), lambda i, d: (i, 0)),
        scratch_shapes=[pltpu.VMEM((2, TOP_K, tm, ROW_TILES, LANES), F32),
                        pltpu.SemaphoreType.DMA((2,))],
    )
    return pl.pallas_call(
        functools.partial(_combine_kernel, final=final),
        grid_spec=grid_spec,
        out_shape=jax.ShapeDtypeStruct((T, D), F32),
        compiler_params=_cparams(("arbitrary",)),
        name="combine",
    )(dest, x2, route, mod, g_final, yb)


def moe_ffn_residual(x2, h, route, counts, mod, g_final, w_gate_up, bgu, w_down, bd, layer, final):
    T = h.shape[0]
    tm = EXPERT_TILE
    slots = assignment_slots(route, counts)
    counts = counts[0].astype(jnp.int32)
    padded = ((counts + tm - 1) // tm) * tm
    pad_ends = jnp.cumsum(padded)
    n_blocks = -(-(T * TOP_K + N_EXPERTS * (tm - 1)) // tm)
    block_start = jnp.arange(n_blocks, dtype=jnp.int32) * tm
    block_e = jnp.minimum(jnp.sum(pad_ends[None, :] <= block_start[:, None], axis=1),
                          N_EXPERTS - 1).astype(jnp.int32)
    n_used = (pad_ends[-1:] // tm).astype(jnp.int32)
    dest = slots[:, :TOP_K].reshape(-1)
    xs = dispatch(dest, pad_ends.astype(jnp.int32), h, n_blocks * tm)
    yb = expert_ffn(block_e, n_used, xs, w_gate_up, bgu, w_down, bd, layer)
    return combine(dest, x2, route, mod, g_final, yb, layer, final)


def kernel(x, c, norm_mix_g, norm_ffn_g, w_mod, b_mod, w_in, hy_conv_w, hy_conv_b, hy_w1, hy_b1, hy_w2, hy_b2, hy_w3, hy_b3, hy_w4, hy_freq, hy_bias, w_hy_br, w_ret_br, w_out, w_router, b_router, w_gate_up, b_gate_up, w_down, b_down, final_g):
    B, S, D = x.shape
    L = w_mod.shape[0]
    T = B * S
    tabs = _tables()

    w_in_bf = w_in.astype(BF16)
    whb = w_hy_br.astype(BF16)
    wrb = w_ret_br.astype(BF16)
    wo = w_out.astype(BF16)
    wr_hi = w_router.astype(BF16)
    wr_split = jnp.stack([wr_hi, (w_router - wr_hi.astype(F32)).astype(BF16)], axis=1)
    E, F = N_EXPERTS, D_FF
    bgu = b_gate_up.reshape(L, E, F // LANES, LANES, 2).transpose(0, 1, 2, 4, 3).reshape(L, E, 1, 2 * F)
    bdn = b_down.reshape(L, E, 1, D)
    w1p = jnp.pad(hy_w1, ((0, 0), (0, HY_EMB_PAD - HY_EMB), (0, 0)))
    w4_hi = hy_w4.astype(BF16)
    w4_split = jnp.stack([w4_hi, (hy_w4 - w4_hi.astype(F32)).astype(BF16)], axis=1)
    vec = lambda a: a.reshape(L, 1, -1)

    c_pad = jnp.pad(c, ((0, 8 - B), (0, 0)))
    mod = modulation(c_pad, w_mod, b_mod)[:, :B]
    mod = mod.reshape(L, B, N_MOD, 1, D).transpose(0, 2, 1, 3, 4)

    x2 = x.reshape(T, D)
    for l in range(L):
        u, qk, v, gr, gm = in_projection(x2, vec(norm_mix_g), mod, w_in_bf, l)
        xs = short_conv(u.reshape(B, S, -1), hy_conv_w, vec(hy_conv_b), l)
        col = lambda a: a.reshape(L, -1, 1)
        kraw, l1 = hyena_filter(tabs, w1p, col(hy_b1), hy_w2, col(hy_b2), hy_w3, col(hy_b3),
                                w4_split, col(hy_freq), l)
        ka = dft_first_axis(kraw.reshape(1, HY_ORDER, HY_SLABS, FFT_N1, FFT_N2, LANES), 0, tabs["f1"],
                            pair=1)
        kspec = filter_spectrum(ka, l1, tabs)
        y_hy = hyena_mixer(xs, kspec, hy_bias[l], tabs)
        qk3, v3, gr3 = qk.reshape(B, S, -1), v.reshape(B, S, -1), gr.reshape(B, S, -1)
        states = retention_states(qk3, v3, tabs)
        y_ret = retention_out(qk3, v3, gr3, states, tabs)
        x2, h_ffn, route, counts = merge_and_route(
            x2, y_hy, y_ret.reshape(T, -1), gm, mod, vec(norm_ffn_g),
            whb, wrb, wo, wr_split, vec(b_router), l)
        x2 = moe_ffn_residual(x2, h_ffn, route, counts, mod, final_g.reshape(1, D), w_gate_up, bgu,
                              w_down, bdn, l, final=(l == L - 1))
    return x2.reshape(B, S, D)
```

```python
import functools
import math

import numpy as np
import jax
import jax.numpy as jnp
from jax import lax
from jax.experimental import pallas as pl
from jax.experimental.pallas import tpu as pltpu

F32 = jnp.float32
BF16 = jnp.bfloat16
HIGHEST = lax.Precision.HIGHEST

D_MODEL = 1024
BATCH = 4
SEQ = 8192
DEPTH = 4
HY_WIDTH = 512
HY_ORDER = 2
HY_EMB = 33
HY_EMB_PAD = 64
HY_FILT_HIDDEN = 64
HY_FAST_DECAY = 0.3
HY_SLOW_DECAY = 1.5
HY_DECAY_TARGET = 1e-2
RET_HEADS = 4
RET_DK = 64
RET_DV = 128
RET_QK = RET_HEADS * RET_DK
RET_V = RET_HEADS * RET_DV
RET_CHUNK = 128
RET_DECAY_FWD = 5.0
RET_DECAY_BWD = 5.5
ROPE_BASE = 10000.0
N_EXPERTS = 32
TOP_K = 4
D_FF = D_MODEL
SWIGLU_ALPHA = 1.702
SWIGLU_LIMIT = 7.0
N_MOD = 6
EPS = 1e-6
IN_COLS = 3 * HY_WIDTH + 2 * RET_QK + 2 * RET_V + 2 * D_MODEL

LANES = 128
HY_SLABS = HY_WIDTH // LANES
ROW_TILES = D_MODEL // LANES
VMEM_LIMIT = 56 * 1024 * 1024

FFT_N = 2 * SEQ
FFT_N1 = 128
FFT_N2 = FFT_N // FFT_N1
FFT_H1 = FFT_N1 // 2

TOK_TILE = 256
MERGE_HALVES = 2
EXPERT_TILE = 512
RANK_TILE = 1024
DISPATCH_TILE = 512
COMBINE_TILE = 256
RET_GROUP = 8
FFT_N2_BLOCK = 8
FFT_K1_GROUP = 8


def _cparams(sem):
    return pltpu.CompilerParams(dimension_semantics=sem, vmem_limit_bytes=VMEM_LIMIT)


def _const_spec(shape):
    return pl.BlockSpec(shape, lambda *_: (0,) * len(shape))


def _tables():
    n1 = np.arange(FFT_N1)
    n2 = np.arange(FFT_N2)
    ang1 = 2.0 * np.pi * np.outer(n1, n1) / FFT_N1
    f1 = np.concatenate([np.cos(ang1), -np.sin(ang1)], axis=0)
    ang2 = 2.0 * np.pi * np.outer(n2, n2) / FFT_N2
    angt = 2.0 * np.pi * np.outer(n1, n2) / FFT_N
    f1r, f1i = np.cos(ang1)[:, :FFT_H1], -np.sin(ang1)[:, :FFT_H1]
    f1c = np.block([[f1r, -f1i], [f1i, f1r]])
    gr, gi = np.cos(ang1)[:FFT_H1] / FFT_N, np.sin(ang1)[:FFT_H1] / FFT_N
    gc = np.block([[gr, -gi], [gi, gr]])

    L = SEQ
    t = np.linspace(0.0, 1.0, L)
    bands = (HY_EMB - 1) // 2
    w = 2.0 * np.pi * np.arange(L) / L
    f = np.linspace(1e-4, bands - 1, bands)
    feats = np.concatenate([t[:, None], np.cos(f[None] * w[:, None]), -np.sin(f[None] * w[:, None])], -1)
    idx = np.concatenate([np.arange(L), [0], L - np.arange(1, L)])
    feats2 = np.zeros((2 * L, HY_EMB_PAD))
    feats2[:, :HY_EMB] = feats[idx]
    max_decay = math.log(HY_DECAY_TARGET) / HY_FAST_DECAY
    min_decay = math.log(HY_DECAY_TARGET) / HY_SLOW_DECAY
    deltas = np.abs(np.linspace(min_decay, max_decay, HY_WIDTH))

    C = RET_CHUNK
    hh = np.arange(RET_HEADS)
    lgf = np.log(1.0 - np.exp2(-(RET_DECAY_FWD + hh)))
    lgb = np.log(1.0 - np.exp2(-(RET_DECAY_BWD + hh)))
    pos = np.arange(C)
    diff = pos[:, None] - pos[None, :]
    dec = np.where(diff[None] >= 0, np.exp(np.maximum(diff, 0)[None] * lgf[:, None, None]),
                   np.exp(np.maximum(-diff, 0)[None] * lgb[:, None, None]))
    lane_h = np.repeat(hh, RET_DK)
    wq = np.stack([np.exp((pos[:, None] + 1.0) * lgf[lane_h][None]),
                   np.exp((C - pos[:, None]) * lgb[lane_h][None])])
    wk = np.stack([np.exp((C - 1.0 - pos[:, None]) * lgf[lane_h][None]),
                   np.exp(pos[:, None] * lgb[lane_h][None])]) * (RET_DK ** -0.5)
    cd = np.stack([np.broadcast_to(np.exp(C * lgf[lane_h])[:, None], (RET_QK, RET_V)),
                   np.broadcast_to(np.exp(C * lgb[lane_h])[:, None], (RET_QK, RET_V))])
    col_h = np.repeat(hh, RET_DV)
    bd = (lane_h[:, None] == col_h[None, :]).astype(np.float64)
    inv_freq = 1.0 / (ROPE_BASE ** (np.arange(0, RET_DK, 2) / RET_DK))
    ang = np.arange(SEQ)[:, None] * inv_freq[None, :]
    cc = np.tile(np.concatenate([np.cos(ang), np.cos(ang)], -1), (1, RET_HEADS))
    ss = np.tile(np.concatenate([-np.sin(ang), np.sin(ang)], -1), (1, RET_HEADS))

    return dict(
        f1=jnp.asarray(f1, BF16), f1c=jnp.asarray(f1c, BF16),
        f2r=jnp.asarray(np.cos(ang2), F32), f2i=jnp.asarray(-np.sin(ang2), F32),
        twr=jnp.asarray(np.cos(angt), F32), twi=jnp.asarray(-np.sin(angt), F32),
        gc=jnp.asarray(gc, BF16),
        feats2=jnp.asarray(feats2, F32), feats2t=jnp.asarray(feats2.T, F32),
        deltas=jnp.asarray(deltas[None], F32),
        dec=jnp.asarray(dec, F32), wq=jnp.asarray(wq, F32), wk=jnp.asarray(wk, F32),
        cd=jnp.asarray(cd, F32), bd=jnp.asarray(bd, F32),
        cc=jnp.asarray(cc, F32), ss=jnp.asarray(ss, F32),
    )


def _mod_kernel(c_ref, w_ref, b_ref, o_ref):
    c = c_ref[...]
    ca = c * jax.nn.sigmoid(c)
    o_ref[...] = jnp.dot(ca, w_ref[...], preferred_element_type=F32, precision=HIGHEST) + b_ref[...]


def modulation(c_pad, w_mod, b_mod):
    L, D, N = w_mod.shape
    tn = 1536
    rows = c_pad.shape[0]
    return pl.pallas_call(
        _mod_kernel,
        grid=(L, N // tn),
        in_specs=[
            _const_spec((rows, D)),
            pl.BlockSpec((None, D, tn), lambda l, j: (l, 0, j)),
            pl.BlockSpec((None, 1, tn), lambda l, j: (l, 0, j)),
        ],
        out_specs=pl.BlockSpec((None, rows, tn), lambda l, j: (l, 0, j)),
        out_shape=jax.ShapeDtypeStruct((L, rows, N), F32),
        compiler_params=_cparams(("arbitrary", "arbitrary")),
        name="modulation",
    )(c_pad, w_mod, b_mod.reshape(L, 1, N))


def _norm_mod(x, g, sc, sh):
    y = x * lax.rsqrt(jnp.mean(x * x, axis=-1, keepdims=True) + EPS)
    return (y * g) * (1.0 + sc) + sh


def _inproj_kernel(x_ref, g_ref, sh_ref, sc_ref, w_ref, u_ref, qk_ref, v_ref, gr_ref, gm_ref):
    h = _norm_mod(x_ref[...], g_ref[...], sc_ref[...], sh_ref[...]).astype(BF16)
    c0 = 0
    for o_ref in (u_ref, qk_ref, v_ref, gr_ref, gm_ref):
        c1 = c0 + o_ref.shape[-1]
        o_ref[...] = jnp.dot(h, w_ref[:, c0:c1], preferred_element_type=F32).astype(o_ref.dtype)
        c0 = c1


def in_projection(x2, g, mod, w_in_bf, layer):
    T, D = x2.shape
    tm = TOK_TILE
    per_b = SEQ // tm
    mspec = lambda piece: pl.BlockSpec((None, None, None, 1, D),
                                       lambda i: (layer, piece, i // per_b, 0, 0))
    widths = (3 * HY_WIDTH, 2 * RET_QK, RET_V, RET_V, 2 * D_MODEL)
    dtypes = (BF16, F32, BF16, F32, BF16)
    return pl.pallas_call(
        _inproj_kernel,
        grid=(T // tm,),
        in_specs=[
            pl.BlockSpec((tm, D), lambda i: (i, 0)),
            pl.BlockSpec((None, 1, D), lambda i: (layer, 0, 0)),
            mspec(0), mspec(1),
            pl.BlockSpec((None, D, IN_COLS), lambda i: (layer, 0, 0)),
        ],
        out_specs=[pl.BlockSpec((tm, w), lambda i: (i, 0)) for w in widths],
        out_shape=[jax.ShapeDtypeStruct((T, w), dt) for w, dt in zip(widths, dtypes)],
        compiler_params=_cparams(("arbitrary",)),
        name="in_projection",
    )(x2, g, mod, mod, w_in_bf)


def _shortconv_kernel(u_ref, w_ref, b_ref, o_ref):
    u = u_ref[...].astype(F32)
    s = u.shape[0]
    row = lax.broadcasted_iota(jnp.int32, u.shape, 0)
    prev = jnp.where(row == 0, 0.0, pltpu.roll(u, 1, axis=0))
    nxt = jnp.where(row == s - 1, 0.0, pltpu.roll(u, s - 1, axis=0))
    w = w_ref[...]
    o_ref[...] = prev * w[0:1] + u * w[1:2] + nxt * w[2:3] + b_ref[...]


def short_conv(u, conv_w, conv_b, layer):
    B, S, C3 = u.shape
    cb = LANES
    per = HY_WIDTH // cb
    return pl.pallas_call(
        _shortconv_kernel,
        grid=(B, C3 // cb),
        in_specs=[
            pl.BlockSpec((None, S, cb), lambda b, j: (b, 0, j)),
            pl.BlockSpec((None, 3, cb), lambda b, j: (layer, 0, j)),
            pl.BlockSpec((None, 1, cb), lambda b, j: (layer, 0, j)),
        ],
        out_specs=pl.BlockSpec((None, None, None, S, cb), lambda b, j: (j // per, b, j % per, 0, 0)),
        out_shape=jax.ShapeDtypeStruct((3, B, per, S, cb), F32),
        compiler_params=_cparams(("arbitrary", "arbitrary")),
        name="short_conv",
    )(u, conv_w, conv_b)


def _filter_kernel(p_ref, pt_ref, w1_ref, b1_ref, w2_ref, b2_ref, w3_ref, b3_ref, w4_ref, fr_ref, dl_ref,
                   k_ref, l1_ref, *, rows):
    i = pl.program_id(0)
    fr = fr_ref[...]
    tdot = lambda w, x: lax.dot_general(w, x, (((0,), (0,)), ((), ())),
                                        preferred_element_type=F32, precision=HIGHEST)
    h = jnp.sin(fr * (tdot(w1_ref[...], pt_ref[...]) + b1_ref[...]))
    h = jnp.sin(fr * (tdot(w2_ref[...], h) + b2_ref[...]))
    h = jnp.sin(fr * (tdot(w3_ref[...], h) + b3_ref[...]))
    h_hi = h.astype(BF16)
    h_lo = (h - h_hi.astype(F32)).astype(BF16)
    bdot = lambda a, b: lax.dot_general(a, b, (((0,), (0,)), ((), ())), preferred_element_type=F32)
    h = bdot(h_hi, w4_ref[0]) + bdot(h_hi, w4_ref[1]) + bdot(h_lo, w4_ref[0])
    t = p_ref[:, 0:1]
    win = jnp.exp(-t * dl_ref[...])
    win = jnp.concatenate([win] * HY_ORDER, axis=1)
    grow = i * rows + lax.broadcasted_iota(jnp.int32, (rows, 1), 0)
    k = jnp.where(grow == SEQ, 0.0, h * win)
    for s in range(k_ref.shape[0]):
        k_ref[s] = k[:, s * LANES:(s + 1) * LANES]

    @pl.when(i == 0)
    def _():
        l1_ref[...] = jnp.zeros_like(l1_ref)
    l1_ref[...] += jnp.sum(jnp.abs(k), axis=0, keepdims=True)


def hyena_filter(tabs, w1p, b1, w2, b2, w3, b3, w4, freq, layer):
    rows = 1024
    n = 2 * SEQ
    half = SEQ // rows
    H = HY_FILT_HIDDEN
    OC = HY_ORDER * HY_WIDTH
    vec = lambda: pl.BlockSpec((None, H, 1), lambda i: (layer, 0, 0))
    return pl.pallas_call(
        functools.partial(_filter_kernel, rows=rows),
        grid=(n // rows,),
        in_specs=[
            pl.BlockSpec((rows, HY_EMB_PAD), lambda i: (i, 0)),
            pl.BlockSpec((HY_EMB_PAD, rows), lambda i: (0, i)),
            pl.BlockSpec((None, HY_EMB_PAD, H), lambda i: (layer, 0, 0)), vec(),
            pl.BlockSpec((None, H, H), lambda i: (layer, 0, 0)), vec(),
            pl.BlockSpec((None, H, H), lambda i: (layer, 0, 0)), vec(),
            pl.BlockSpec((None, 2, H, OC), lambda i: (layer, 0, 0, i // half)),
            vec(),
            _const_spec((1, HY_WIDTH)),
        ],
        out_specs=[pl.BlockSpec((OC // LANES, rows, LANES), lambda i: (0, i, 0)), _const_spec((1, OC))],
        out_shape=[jax.ShapeDtypeStruct((OC // LANES, n, LANES), F32),
                   jax.ShapeDtypeStruct((1, OC), F32)],
        compiler_params=_cparams(("arbitrary",)),
        name="hyena_filter",
    )(tabs["feats2"], tabs["feats2t"], w1p, b1, w2, b2, w3, b3, w4, freq, tabs["deltas"])


def _load_sub(ref, lead, j):
    *outer, r, s, l = ref.shape
    flat = ref.reshape(*outer, r * s, l)
    return flat[(*lead, pl.ds(j, r, stride=s), slice(None))]


def _store_sub(ref, lead, j, val):
    *outer, r, s, l = ref.shape
    flat = ref.reshape(*outer, r * s, l)
    flat[(*lead, pl.ds(j, r, stride=s), slice(None))] = val


def _pack_pair(re, im):
    return lax.bitcast_convert_type(pltpu.pack_elementwise([im, re], packed_dtype=BF16), jnp.uint32)


def _unpack_pair(p):
    p = lax.bitcast_convert_type(p, jnp.int32)
    im = pltpu.unpack_elementwise(p, index=0, packed_dtype=BF16, unpacked_dtype=F32)
    re = pltpu.unpack_elementwise(p, index=1, packed_dtype=BF16, unpacked_dtype=F32)
    return re, im


def _dft1_kernel(x_ref, f_ref, o_ref):
    for j in range(FFT_N2_BLOCK):
        xs = jnp.concatenate(
            [jnp.concatenate([_load_sub(x_ref, (p, c), j) for c in range(HY_SLABS)], axis=1)
             for p in range(x_ref.shape[0])], axis=0)
        a = jnp.dot(f_ref[...], xs.astype(BF16), preferred_element_type=F32)
        p = _pack_pair(a[:FFT_N1], a[FFT_N1:])
        for c in range(HY_SLABS):
            _store_sub(o_ref, (c,), j, p[:, c * LANES:(c + 1) * LANES])


def dft_first_axis(x6, which, f1, pair):
    _, B, _, rows, _, _ = x6.shape
    nb = FFT_N2_BLOCK
    return pl.pallas_call(
        _dft1_kernel,
        grid=(B // pair, FFT_N2 // nb),
        in_specs=[pl.BlockSpec((None, pair, HY_SLABS, rows, nb, LANES),
                               lambda b, j: (which, b, 0, 0, j, 0)),
                  _const_spec((2 * FFT_N1, pair * rows))],
        out_specs=pl.BlockSpec((None, HY_SLABS, FFT_N1, nb, LANES), lambda b, j: (b, 0, 0, j, 0)),
        out_shape=jax.ShapeDtypeStruct((B // pair, HY_SLABS, FFT_N1, FFT_N2, LANES), jnp.uint32),
        compiler_params=_cparams(("arbitrary", "arbitrary")),
        name="dft_first_axis",
    )(x6, f1)


def _twiddled_stack(f2r_ref, f2i_ref, twr_ref, twi_ref, k1):
    tr = twr_ref[pl.ds(k1, 1), :]
    ti = twi_ref[pl.ds(k1, 1), :]
    fr = f2r_ref[...]
    fi = f2i_ref[...]
    p = fr * tr - fi * ti
    q = fr * ti + fi * tr
    top = jnp.concatenate([p, -q], axis=1)
    bot = jnp.concatenate([q, p], axis=1)
    return jnp.concatenate([top, bot], axis=0).astype(BF16)


def _load_k1(a_ref, j):
    p = jnp.concatenate([a_ref[c, j] for c in range(HY_SLABS)], axis=1)
    re, im = _unpack_pair(p)
    return jnp.concatenate([re, im], axis=0).astype(BF16)


def _spectrum_kernel(a_ref, l1_ref, f2r_ref, f2i_ref, twr_ref, twi_ref, o_ref):
    base = pl.program_id(0) * FFT_K1_GROUP
    inv = 1.0 / l1_ref[...]
    for j in range(FFT_K1_GROUP):
        r = _twiddled_stack(f2r_ref, f2i_ref, twr_ref, twi_ref, base + j)
        x = jnp.dot(r, _load_k1(a_ref, j), preferred_element_type=F32) * inv
        o_ref[j, 0] = x[:FFT_N2].astype(o_ref.dtype)
        o_ref[j, 1] = x[FFT_N2:].astype(o_ref.dtype)


def filter_spectrum(a5, l1, tabs):
    g = FFT_K1_GROUP
    sq = lambda: _const_spec((FFT_N2, FFT_N2))
    return pl.pallas_call(
        _spectrum_kernel,
        grid=(FFT_N1 // g, HY_ORDER),
        in_specs=[pl.BlockSpec((None, HY_SLABS, g, FFT_N2, LANES), lambda i, o: (o, 0, i, 0, 0)),
                  pl.BlockSpec((1, HY_WIDTH), lambda i, o: (0, o)), sq(), sq(), sq(), sq()],
        out_specs=pl.BlockSpec((g, 2, FFT_N2, HY_WIDTH), lambda i, o: (i, 0, 0, o)),
        out_shape=jax.ShapeDtypeStruct((FFT_N1, 2, FFT_N2, HY_ORDER * HY_WIDTH), BF16),
        compiler_params=_cparams(("arbitrary", "arbitrary")),
        name="filter_spectrum",
    )(a5, l1, tabs["f2r"], tabs["f2i"], tabs["twr"], tabs["twi"])


def _convmid_kernel(a_ref, ks_ref, f2r_ref, f2i_ref, twr_ref, twi_ref, o_ref):
    base = pl.program_id(0) * FFT_K1_GROUP
    for j in range(FFT_K1_GROUP):
        r = _twiddled_stack(f2r_ref, f2i_ref, twr_ref, twi_ref, base + j)
        x = jnp.dot(r, _load_k1(a_ref, j), preferred_element_type=F32)
        xr, xi = x[:FFT_N2], x[FFT_N2:]
        kr, ki = ks_ref[j, 0].astype(F32), ks_ref[j, 1].astype(F32)
        y = jnp.concatenate([xr * kr - xi * ki, xr * ki + xi * kr], axis=0).astype(BF16)
        b = lax.dot_general(r, y, (((0,), (0,)), ((), ())), preferred_element_type=F32)
        p = _pack_pair(b[:FFT_N2], b[FFT_N2:])
        for c in range(HY_SLABS):
            o_ref[c, j] = p[:, c * LANES:(c + 1) * LANES]


def conv_mid(a5, kspec, order, tabs):
    B = a5.shape[0]
    g = FFT_K1_GROUP
    sq = lambda: _const_spec((FFT_N2, FFT_N2))
    blk = lambda: pl.BlockSpec((None, HY_SLABS, g, FFT_N2, LANES), lambda i, b: (b, 0, i, 0, 0))
    return pl.pallas_call(
        _convmid_kernel,
        grid=(FFT_N1 // g, B),
        in_specs=[blk(), pl.BlockSpec((g, 2, FFT_N2, HY_WIDTH), lambda i, b: (i, 0, 0, order)),
                  sq(), sq(), sq(), sq()],
        out_specs=blk(),
        out_shape=jax.ShapeDtypeStruct(a5.shape, jnp.uint32),
        compiler_params=_cparams(("arbitrary", "arbitrary")),
        name="conv_mid",
    )(a5, kspec, tabs["f2r"], tabs["f2i"], tabs["twr"], tabs["twi"])


def _convout_kernel(b_ref, g_ref, z_ref, gate_ref, bias_ref, o_ref):
    for j in range(FFT_N2_BLOCK):
        p = jnp.concatenate([_load_sub(b_ref, (c,), j) for c in range(HY_SLABS)], axis=1)
        re, im = _unpack_pair(p)
        bp = jnp.concatenate([re, im], axis=0).astype(BF16)
        y = jnp.dot(g_ref[...], bp, preferred_element_type=F32)
        for q in range(2):
            for c in range(HY_SLABS):
                _store_sub(o_ref, (q, c), j, y[q * FFT_H1:(q + 1) * FFT_H1, c * LANES:(c + 1) * LANES])
    o_ref[...] = gate_ref[...] * (o_ref[...] + z_ref[...] * bias_ref[...])


def conv_out(bp5, z6, z_which, gate6, gate_which, bias4, g):
    B2 = bp5.shape[0]
    nb = FFT_N2_BLOCK
    nat = lambda which: pl.BlockSpec((None, 2, HY_SLABS, FFT_H1, nb, LANES),
                                     lambda b, j: (which, b, 0, 0, j, 0))
    return pl.pallas_call(
        _convout_kernel,
        grid=(B2, FFT_N2 // nb),
        in_specs=[pl.BlockSpec((None, HY_SLABS, FFT_N1, nb, LANES), lambda b, j: (b, 0, 0, j, 0)),
                  _const_spec((2 * FFT_H1, 2 * FFT_N1)), nat(z_which), nat(gate_which),
                  _const_spec((HY_SLABS, 1, 1, LANES))],
        out_specs=pl.BlockSpec((2, HY_SLABS, FFT_H1, nb, LANES), lambda b, j: (b, 0, 0, j, 0)),
        out_shape=jax.ShapeDtypeStruct((2 * B2, HY_SLABS, FFT_H1, FFT_N2, LANES), F32),
        compiler_params=_cparams(("arbitrary", "arbitrary")),
        name="conv_out",
    )(bp5, g, z6, gate6, bias4)


def hyena_mixer(xs, kspec, bias, tabs):
    B = xs.shape[1]
    xs6 = xs.reshape(3, B, HY_SLABS, FFT_H1, FFT_N2, LANES)
    z6, z_which = xs6, 2
    for o in range(HY_ORDER):
        a = dft_first_axis(z6, z_which, tabs["f1c"], pair=2)
        bp = conv_mid(a, kspec, o, tabs)
        z = conv_out(bp, z6, z_which, xs6, o, bias[o].reshape(HY_SLABS, 1, 1, LANES), tabs["gc"])
        z6, z_which = z[None], 0
    return z.reshape(B, HY_SLABS, SEQ, LANES)


def _rope(x, cc, ss):
    lane = lax.broadcasted_iota(jnp.int32, x.shape, 1)
    n = x.shape[1]
    half = RET_DK // 2
    swapped = jnp.where(lane % RET_DK < half, pltpu.roll(x, n - half, axis=1),
                        pltpu.roll(x, half, axis=1))
    return x * cc + swapped * ss


def _retstate_kernel(k_ref, v_ref, cc_ref, ss_ref, wk_ref, cd_ref, o_ref, s_ref):
    d = pl.program_id(1)
    g = pl.program_id(2)
    C = RET_CHUNK

    @pl.when(g == 0)
    def _():
        s_ref[...] = jnp.zeros_like(s_ref)

    for j in range(RET_GROUP):
        ci = jnp.where(d == 0, j, RET_GROUP - 1 - j)
        r0 = pl.multiple_of(ci * C, C)
        s = s_ref[...]
        o_ref[ci] = jnp.concatenate(
            [s[h * RET_DK:(h + 1) * RET_DK, h * RET_DV:(h + 1) * RET_DV] for h in range(RET_HEADS)],
            axis=1)
        k = _rope(k_ref[pl.ds(r0, C), :], cc_ref[pl.ds(r0, C), :], ss_ref[pl.ds(r0, C), :])
        kw = (k * wk_ref[...]).astype(BF16)
        ds = lax.dot_general(kw, v_ref[pl.ds(r0, C), :], (((0,), (0,)), ((), ())),
                             preferred_element_type=F32)
        s_ref[...] = s * cd_ref[...] + ds


def retention_states(qk, v, tabs):
    B, S, _ = qk.shape
    rows = RET_GROUP * RET_CHUNK
    G = S // rows
    grp = lambda d, g: jnp.where(d == 0, g, G - 1 - g)
    return pl.pallas_call(
        _retstate_kernel,
        grid=(B, 2, G),
        in_specs=[
            pl.BlockSpec((None, rows, RET_QK), lambda b, d, g: (b, grp(d, g), 1)),
            pl.BlockSpec((None, rows, RET_V), lambda b, d, g: (b, grp(d, g), 0)),
            pl.BlockSpec((rows, RET_QK), lambda b, d, g: (grp(d, g), 0)),
            pl.BlockSpec((rows, RET_QK), lambda b, d, g: (grp(d, g), 0)),
            pl.BlockSpec((None, RET_CHUNK, RET_QK), lambda b, d, g: (d, 0, 0)),
            pl.BlockSpec((None, RET_QK, RET_V), lambda b, d, g: (d, 0, 0)),
        ],
        out_specs=pl.BlockSpec((None, None, RET_GROUP, RET_DK, RET_V),
                               lambda b, d, g: (b, d, grp(d, g), 0, 0)),
        out_shape=jax.ShapeDtypeStruct((B, 2, S // RET_CHUNK, RET_DK, RET_V), F32),
        scratch_shapes=[pltpu.VMEM((RET_QK, RET_V), F32)],
        compiler_params=_cparams(("arbitrary", "arbitrary", "arbitrary")),
        name="retention_states",
    )(qk, v, tabs["cc"], tabs["ss"], tabs["wk"], tabs["cd"])


def _retout_kernel(qk_ref, v_ref, gr_ref, cc_ref, ss_ref, st_ref, dec_ref, wq_ref, wk_ref, bd_ref,
                   o_ref):
    C = RET_CHUNK
    bd = bd_ref[...]
    lane = lax.broadcasted_iota(jnp.int32, (C, RET_QK), 1)
    for j in range(RET_GROUP):
        r0 = j * C
        cc = cc_ref[r0:r0 + C, :]
        ss = ss_ref[r0:r0 + C, :]
        q = _rope(qk_ref[r0:r0 + C, :RET_QK], cc, ss)
        k = (_rope(qk_ref[r0:r0 + C, RET_QK:], cc, ss) * (RET_DK ** -0.5)).astype(BF16)
        v = v_ref[r0:r0 + C, :]
        inner = []
        for h in range(RET_HEADS):
            qh = jnp.where(lane // RET_DK == h, q, 0.0).astype(BF16)
            s = lax.dot_general(qh, k, (((1,), (1,)), ((), ())), preferred_element_type=F32)
            s = (s * dec_ref[h]).astype(BF16)
            inner.append(jnp.dot(s, v[:, h * RET_DV:(h + 1) * RET_DV], preferred_element_type=F32))
        qq = jnp.concatenate([q * wq_ref[0], q * wq_ref[1]], axis=1).astype(BF16)
        sf = jnp.concatenate([st_ref[0, j]] * RET_HEADS, axis=0) * bd
        sb = jnp.concatenate([st_ref[1, j]] * RET_HEADS, axis=0) * bd
        sbd = jnp.concatenate([sf, sb], axis=0).astype(BF16)
        o = jnp.concatenate(inner, axis=1) + jnp.dot(qq, sbd, preferred_element_type=F32)
        outs = []
        for h in range(RET_HEADS):
            oh = o[:, h * RET_DV:(h + 1) * RET_DV]
            outs.append(oh * lax.rsqrt(jnp.mean(oh * oh, axis=-1, keepdims=True) + EPS))
        gr = gr_ref[r0:r0 + C, :]
        o_ref[r0:r0 + C, :] = (gr * jax.nn.sigmoid(gr)) * jnp.concatenate(outs, axis=1)


def retention_out(qk, v, gr, states, tabs):
    B, S, _ = qk.shape
    rows = RET_GROUP * RET_CHUNK
    G = S // rows
    return pl.pallas_call(
        _retout_kernel,
        grid=(B, G),
        in_specs=[
            pl.BlockSpec((None, rows, 2 * RET_QK), lambda b, g: (b, g, 0)),
            pl.BlockSpec((None, rows, RET_V), lambda b, g: (b, g, 0)),
            pl.BlockSpec((None, rows, RET_V), lambda b, g: (b, g, 0)),
            pl.BlockSpec((rows, RET_QK), lambda b, g: (g, 0)),
            pl.BlockSpec((rows, RET_QK), lambda b, g: (g, 0)),
            pl.BlockSpec((None, 2, RET_GROUP, RET_DK, RET_V), lambda b, g: (b, 0, g, 0, 0)),
            _const_spec((RET_HEADS, RET_CHUNK, RET_CHUNK)),
            _const_spec((2, RET_CHUNK, RET_QK)),
            _const_spec((2, RET_CHUNK, RET_QK)),
            _const_spec((RET_QK, RET_V)),
        ],
        out_specs=pl.BlockSpec((None, rows, RET_V), lambda b, g: (b, g, 0)),
        out_shape=jax.ShapeDtypeStruct((B, S, RET_V), F32),
        compiler_params=_cparams(("arbitrary", "arbitrary")),
        name="retention_out",
    )(qk, v, gr, tabs["cc"], tabs["ss"], states, tabs["dec"], tabs["wq"], tabs["wk"], tabs["bd"])


def _merge_rows(x, yh, yr, gm, ga, whb_ref, wrb_ref, wo_ref, g, sh, sc, wr_ref, br):
    ghy = 0.5 * jnp.tanh(0.5 * gm[:, :D_MODEL]) + 0.5
    grt = 0.5 * jnp.tanh(0.5 * gm[:, D_MODEL:]) + 0.5
    m = (ghy * jnp.dot(yh.astype(BF16), whb_ref[...], preferred_element_type=F32)
         + grt * jnp.dot(yr.astype(BF16), wrb_ref[...], preferred_element_type=F32))
    out = jnp.dot(m.astype(BF16), wo_ref[...], preferred_element_type=F32)
    x = x + ga * out
    h = _norm_mod(x, g, sc, sh)
    h_hi = h.astype(BF16)
    h_lo = (h - h_hi.astype(F32)).astype(BF16)
    logits = (jnp.dot(h_hi, wr_ref[0], preferred_element_type=F32)
              + jnp.dot(h_hi, wr_ref[1], preferred_element_type=F32)
              + jnp.dot(h_lo, wr_ref[0], preferred_element_type=F32)) + br
    tm = logits.shape[0]
    lane_e = lax.broadcasted_iota(jnp.int32, logits.shape, 1).astype(F32)
    lane_o = lax.broadcasted_iota(jnp.int32, (tm, LANES), 1)
    route = jnp.zeros((tm, LANES), F32)
    vals = []
    work = logits
    for r in range(TOP_K):
        m_r = jnp.max(work, axis=-1, keepdims=True)
        i_r = jnp.min(jnp.where(work == m_r, lane_e, float(N_EXPERTS)), axis=-1, keepdims=True)
        work = jnp.where(lane_e == i_r, -jnp.inf, work)
        vals.append(m_r)
        route = jnp.where(lane_o == r, i_r, route)
    exps = [jnp.exp(v - vals[0]) for v in vals]
    den = exps[0] + exps[1] + exps[2] + exps[3]
    for r in range(TOP_K):
        route = jnp.where(lane_o == TOP_K + r, exps[r] / den, route)
    return x, h, route


def _merge_kernel(x_ref, yh_ref, yr_ref, gm_ref, ga_ref, whb_ref, wrb_ref, wo_ref,
                  g_ref, sh_ref, sc_ref, wr_ref, br_ref, xo_ref, h_ref, rt_ref, cnt_ref):
    tm = TOK_TILE

    @pl.when(pl.program_id(0) == 0)
    def _():
        cnt_ref[...] = jnp.zeros_like(cnt_ref)

    lane_e = lax.broadcasted_iota(jnp.int32, (tm, N_EXPERTS), 1).astype(F32)
    for half in range(MERGE_HALVES):
        rows = slice(half * tm, (half + 1) * tm)
        yh = jnp.concatenate([yh_ref[c, rows, :] for c in range(HY_SLABS)], axis=1)
        x, h, route = _merge_rows(x_ref[rows, :], yh, yr_ref[rows, :], gm_ref[rows, :].astype(F32),
                                  ga_ref[...],
                                  whb_ref, wrb_ref, wo_ref, g_ref[...], sh_ref[...], sc_ref[...],
                                  wr_ref, br_ref[...])
        xo_ref[rows, :] = x
        rt_ref[rows, :] = route
        chosen = sum((lane_e == route[:, r:r + 1]).astype(F32) for r in range(TOP_K))
        cnt_ref[...] += jnp.sum(chosen, axis=0, keepdims=True)
        h_half = h_ref.at[rows]
        for j in range(ROW_TILES):
            _store_sub(h_half, (), j, h[:, j * LANES:(j + 1) * LANES])


def merge_and_route(x2, yh, yr, gm, mod, g_ffn, whb, wrb, wo, w_router, b_router, layer):
    T, D = x2.shape
    tm = TOK_TILE * MERGE_HALVES
    per_b = SEQ // tm
    mspec = lambda piece: pl.BlockSpec((None, None, None, 1, D),
                                       lambda i: (layer, piece, i // per_b, 0, 0))
    row = lambda w: pl.BlockSpec((tm, w), lambda i: (i, 0))
    wsp = lambda a, b: pl.BlockSpec((None, a, b), lambda i: (layer, 0, 0))
    return pl.pallas_call(
        _merge_kernel,
        grid=(T // tm,),
        in_specs=[row(D),
                  pl.BlockSpec((None, HY_SLABS, tm, LANES), lambda i: (i // per_b, 0, i % per_b, 0)),
                  row(RET_V), row(2 * D), mspec(2),
                  wsp(HY_WIDTH, D), wsp(RET_V, D), wsp(D, D),
                  wsp(1, D), mspec(3), mspec(4),
                  pl.BlockSpec((None, 2, D, N_EXPERTS), lambda i: (layer, 0, 0, 0)),
                  wsp(1, N_EXPERTS)],
        out_specs=[row(D), pl.BlockSpec((tm, ROW_TILES, LANES), lambda i: (i, 0, 0)), row(LANES),
                   _const_spec((1, N_EXPERTS))],
        out_shape=[jax.ShapeDtypeStruct((T, D), F32), jax.ShapeDtypeStruct((T, ROW_TILES, LANES), F32),
                   jax.ShapeDtypeStruct((T, LANES), F32), jax.ShapeDtypeStruct((1, N_EXPERTS), F32)],
        compiler_params=_cparams(("arbitrary",)),
        name="merge_and_route",
    )(x2, yh, yr, gm, mod, whb, wrb, wo, g_ffn, mod, mod, w_router, b_router)


def _slot_kernel(rt_ref, cnt_ref, tril_ref, upper_ref, o_ref, carry_ref):
    @pl.when(pl.program_id(0) == 0)
    def _():
        carry_ref[...] = jnp.zeros_like(carry_ref)

    padded = jnp.floor((cnt_ref[...] + (EXPERT_TILE - 1)) * (1.0 / EXPERT_TILE)) * EXPERT_TILE
    pad_start = jnp.dot(jnp.broadcast_to(padded, (8, N_EXPERTS)), upper_ref[...],
                        preferred_element_type=F32, precision=HIGHEST)[0:1]

    rt = rt_ref[...]
    tm = rt.shape[0]
    lane_e = lax.broadcasted_iota(jnp.int32, (tm, N_EXPERTS), 1).astype(F32)
    lane_o = lax.broadcasted_iota(jnp.int32, (tm, LANES), 1)
    hits = [lane_e == rt[:, r:r + 1] for r in range(TOP_K)]
    onehot = sum(h.astype(F32) for h in hits)
    before = (jnp.dot(tril_ref[...], onehot.astype(BF16), preferred_element_type=F32)
              + carry_ref[...] + pad_start)
    out = jnp.zeros((tm, LANES), F32)
    for r in range(TOP_K):
        slot_r = jnp.sum(jnp.where(hits[r], before, 0.0), axis=-1, keepdims=True)
        out = jnp.where(lane_o == r, slot_r, out)
    o_ref[...] = out.astype(jnp.int32)
    carry_ref[...] += jnp.sum(onehot, axis=0, keepdims=True)


def assignment_slots(route, counts):
    T = route.shape[0]
    tm = RANK_TILE
    tril = jnp.asarray(np.tril(np.ones((tm, tm)), -1), BF16)
    upper = jnp.asarray(np.triu(np.ones((N_EXPERTS, N_EXPERTS)), 1), F32)
    return pl.pallas_call(
        _slot_kernel,
        grid=(T // tm,),
        in_specs=[pl.BlockSpec((tm, LANES), lambda i: (i, 0)), _const_spec((1, N_EXPERTS)),
                  _const_spec((tm, tm)), _const_spec((N_EXPERTS, N_EXPERTS))],
        out_specs=pl.BlockSpec((tm, LANES), lambda i: (i, 0)),
        out_shape=jax.ShapeDtypeStruct((T, LANES), jnp.int32),
        scratch_shapes=[pltpu.VMEM((1, N_EXPERTS), F32)],
        compiler_params=_cparams(("arbitrary",)),
        name="assignment_slots",
    )(route, counts, tril, upper)


def _dispatch_kernel(dest_ref, pe_ref, h_ref, xs_hbm, zeros, sem, zsem):
    tm = DISPATCH_TILE
    t0 = pl.program_id(0) * tm

    @pl.when(pl.program_id(0) == 0)
    def _():
        zeros[...] = jnp.zeros_like(zeros)

        def zero_block(e):
            end = pe_ref[e]
            begin = pe_ref[jnp.maximum(e - 1, 0)]
            nonempty = jnp.logical_or(jnp.logical_and(e == 0, end > 0), end > begin)
            start = pl.multiple_of(jnp.maximum(end - EXPERT_TILE, 0), EXPERT_TILE)
            return nonempty, pltpu.make_async_copy(zeros, xs_hbm.at[pl.ds(start, EXPERT_TILE)], zsem)

        def start_zero(e, carry):
            nonempty, cp = zero_block(e)

            @pl.when(nonempty)
            def _():
                cp.start()
            return carry
        lax.fori_loop(0, N_EXPERTS, start_zero, 0)

        def wait_zero(e, carry):
            nonempty, cp = zero_block(e)

            @pl.when(nonempty)
            def _():
                cp.wait()
            return carry
        lax.fori_loop(0, N_EXPERTS, wait_zero, 0)

    def row_copy(i, d):
        return pltpu.make_async_copy(h_ref.at[i], xs_hbm.at[d], sem)

    def issue(i, carry):
        for r in range(TOP_K):
            row_copy(i, dest_ref[(t0 + i) * TOP_K + r]).start(priority=r % 2)
        return carry
    lax.fori_loop(0, tm, issue, 0, unroll=4)

    def drain(i, carry):
        row_copy(0, 0).wait()
        return carry
    lax.fori_loop(0, tm * TOP_K, drain, 0, unroll=8)


def dispatch(dest, pad_ends, h, n_rows):
    T = h.shape[0]
    tm = DISPATCH_TILE
    grid_spec = pltpu.PrefetchScalarGridSpec(
        num_scalar_prefetch=2,
        grid=(T // tm,),
        in_specs=[pl.BlockSpec((tm, ROW_TILES, LANES), lambda i, d, pe: (i, 0, 0))],
        out_specs=pl.BlockSpec(memory_space=pl.ANY),
        scratch_shapes=[pltpu.VMEM((EXPERT_TILE, ROW_TILES, LANES), h.dtype), pltpu.SemaphoreType.DMA,
                        pltpu.SemaphoreType.DMA],
    )
    return pl.pallas_call(
        _dispatch_kernel,
        grid_spec=grid_spec,
        out_shape=jax.ShapeDtypeStruct((n_rows, ROW_TILES, LANES), h.dtype),
        compiler_params=_cparams(("arbitrary",)),
        name="dispatch",
    )(dest, pad_ends, h)


def _expert_kernel(be_ref, nb_ref, x_ref, wgu_ref, bgu_ref, wd_ref, bd_ref, perm_ref, o_ref,
                   wgu_s, wd_s):
    i = pl.program_id(0)
    prev = be_ref[jnp.maximum(i - 1, 0)]
    new_expert = jnp.logical_or(i == 0, be_ref[i] != prev)
    n_chunk = 2 * D_FF // (2 * LANES)

    @pl.when(new_expert)
    def _():
        for c in range(n_chunk):
            cols = slice(c * 2 * LANES, (c + 1) * 2 * LANES)
            wgu_s[:, cols] = jnp.dot(wgu_ref[:, cols].astype(BF16), perm_ref[...],
                                     preferred_element_type=F32).astype(BF16)
        wd_s[...] = wd_ref[...].astype(BF16)

    @pl.when(i < nb_ref[0])
    def _():
        x = jnp.concatenate([_load_sub(x_ref, (), j) for j in range(ROW_TILES)], axis=1)
        gu = jnp.dot(x.astype(BF16), wgu_s[...], preferred_element_type=F32) + bgu_ref[...]
        acts = []
        for c in range(n_chunk):
            gate = jnp.minimum(gu[:, c * 2 * LANES:c * 2 * LANES + LANES], SWIGLU_LIMIT)
            up = jnp.clip(gu[:, c * 2 * LANES + LANES:(c + 1) * 2 * LANES], -SWIGLU_LIMIT, SWIGLU_LIMIT)
            acts.append((up + 1.0) * (gate * jax.nn.sigmoid(SWIGLU_ALPHA * gate)))
        act = jnp.concatenate(acts, axis=1).astype(BF16)
        y = jnp.dot(act, wd_s[...], preferred_element_type=F32) + bd_ref[...]
        for j in range(ROW_TILES):
            _store_sub(o_ref, (), j, y[:, j * LANES:(j + 1) * LANES])

    @pl.when(i >= nb_ref[0])
    def _():
        o_ref[...] = jnp.zeros_like(o_ref)


def expert_ffn(block_e, n_used, xs, w_gate_up, bgu, w_down, bd, layer):
    P = xs.shape[0]
    D = D_MODEL
    tm = EXPERT_TILE
    perm = np.zeros((2 * LANES, 2 * LANES))
    j = np.arange(LANES)
    perm[2 * j, j] = 1.0
    perm[2 * j + 1, LANES + j] = 1.0
    grid_spec = pltpu.PrefetchScalarGridSpec(
        num_scalar_prefetch=2,
        grid=(P // tm,),
        in_specs=[
            pl.BlockSpec((tm, ROW_TILES, LANES), lambda i, be, nb: (jnp.minimum(i, nb[0] - 1), 0, 0)),
            pl.BlockSpec((None, None, D, 2 * D_FF), lambda i, be, nb: (layer, be[i], 0, 0)),
            pl.BlockSpec((None, None, 1, 2 * D_FF), lambda i, be, nb: (layer, be[i], 0, 0)),
            pl.BlockSpec((None, None, D_FF, D), lambda i, be, nb: (layer, be[i], 0, 0)),
            pl.BlockSpec((None, None, 1, D), lambda i, be, nb: (layer, be[i], 0, 0)),
            pl.BlockSpec((2 * LANES, 2 * LANES), lambda i, be, nb: (0, 0)),
        ],
        out_specs=pl.BlockSpec((tm, ROW_TILES, LANES), lambda i, be, nb: (i, 0, 0)),
        scratch_shapes=[pltpu.VMEM((D, 2 * D_FF), BF16), pltpu.VMEM((D_FF, D), BF16)],
    )
    return pl.pallas_call(
        _expert_kernel,
        grid_spec=grid_spec,
        out_shape=jax.ShapeDtypeStruct((P, ROW_TILES, LANES), F32),
        compiler_params=_cparams(("arbitrary",)),
        name="expert_ffn",
    )(block_e, n_used, xs, w_gate_up, bgu, w_down, bd, jnp.asarray(perm, BF16))


def _combine_kernel(dest_ref, x_ref, rt_ref, ga_ref, g_ref, yb_hbm, o_ref, buf, sem, *, final):
    tm = COMBINE_TILE
    step = pl.program_id(0)
    slot = step % 2

    def row_copy(sl, i, r, d):
        return pltpu.make_async_copy(yb_hbm.at[d], buf.at[sl, r, i], sem.at[sl])

    def issue_step(st, sl):
        def issue(i, carry):
            for r in range(TOP_K):
                row_copy(sl, i, r, dest_ref[(st * tm + i) * TOP_K + r]).start(priority=r % 2)
            return carry
        lax.fori_loop(0, tm, issue, 0, unroll=4)

    @pl.when(step == 0)
    def _():
        issue_step(0, 0)

    @pl.when(step + 1 < pl.num_programs(0))
    def _():
        issue_step(step + 1, 1 - slot)

    def drain(i, carry):
        row_copy(slot, 0, 0, 0).wait()
        return carry
    lax.fori_loop(0, tm * TOP_K, drain, 0, unroll=8)

    rt = rt_ref[...]
    gates = [jnp.broadcast_to(rt[:, TOP_K + r:TOP_K + r + 1], (tm, LANES)) for r in range(TOP_K)]
    cols = []
    for j in range(ROW_TILES):
        s = _load_sub(buf, (slot, 0), j) * gates[0]
        for r in range(1, TOP_K):
            s = s + _load_sub(buf, (slot, r), j) * gates[r]
        cols.append(s)
    y = jnp.concatenate(cols, axis=1)
    x = x_ref[...] + ga_ref[...] * y
    if final:
        x = (x * lax.rsqrt(jnp.mean(x * x, axis=-1, keepdims=True) + EPS)) * g_ref[...]
    o_ref[...] = x


def combine(dest, x2, route, mod, g_final, yb, layer, final):
    T, D = x2.shape
    tm = COMBINE_TILE
    per_b = SEQ // tm
    grid_spec = pltpu.PrefetchScalarGridSpec(
        num_scalar_prefetch=1,
        grid=(T // tm,),
        in_specs=[
            pl.BlockSpec((tm, D), lambda i, d: (i, 0)),
            pl.BlockSpec((tm, LANES), lambda i, d: (i, 0)),
            pl.BlockSpec((None, None, None, 1, D), lambda i, d: (layer, 5, i // per_b, 0, 0)),
            pl.BlockSpec((1, D), lambda i, d: (0, 0)),
            pl.BlockSpec(memory_space=pl.ANY),
        ],
        out_specs=pl.BlockSpec((tm, D), lambda i, d: (i, 0)),
        scratch_shapes=[pltpu.VMEM((2, TOP_K, tm, ROW_TILES, LANES), F32),
                        pltpu.SemaphoreType.DMA((2,))],
    )
    return pl.pallas_call(
        functools.partial(_combine_kernel, final=final),
        grid_spec=grid_spec,
        out_shape=jax.ShapeDtypeStruct((T, D), F32),
        compiler_params=_cparams(("arbitrary",)),
        name="combine",
    )(dest, x2, route, mod, g_final, yb)


def moe_ffn_residual(x2, h, route, counts, mod, g_final, w_gate_up, bgu, w_down, bd, layer, final):
    T = h.shape[0]
    tm = EXPERT_TILE
    slots = assignment_slots(route, counts)
    counts = counts[0].astype(jnp.int32)
    padded = ((counts + tm - 1) // tm) * tm
    pad_ends = jnp.cumsum(padded)
    n_blocks = -(-(T * TOP_K + N_EXPERTS * (tm - 1)) // tm)
    block_start = jnp.arange(n_blocks, dtype=jnp.int32) * tm
    block_e = jnp.minimum(jnp.sum(pad_ends[None, :] <= block_start[:, None], axis=1),
                          N_EXPERTS - 1).astype(jnp.int32)
    n_used = (pad_ends[-1:] // tm).astype(jnp.int32)
    dest = slots[:, :TOP_K].reshape(-1)
    xs = dispatch(dest, pad_ends.astype(jnp.int32), h, n_blocks * tm)
    yb = expert_ffn(block_e, n_used, xs, w_gate_up, bgu, w_down, bd, layer)
    return combine(dest, x2, route, mod, g_final, yb, layer, final)


def kernel(x, c, norm_mix_g, norm_ffn_g, w_mod, b_mod, w_in, hy_conv_w, hy_conv_b, hy_w1, hy_b1, hy_w2, hy_b2, hy_w3, hy_b3, hy_w4, hy_freq, hy_bias, w_hy_br, w_ret_br, w_out, w_router, b_router, w_gate_up, b_gate_up, w_down, b_down, final_g):
    B, S, D = x.shape
    L = w_mod.shape[0]
    T = B * S
    tabs = _tables()

    w_in_bf = w_in.astype(BF16)
    whb = w_hy_br.astype(BF16)
    wrb = w_ret_br.astype(BF16)
    wo = w_out.astype(BF16)
    wr_hi = w_router.astype(BF16)
    wr_split = jnp.stack([wr_hi, (w_router - wr_hi.astype(F32)).astype(BF16)], axis=1)
    E, F = N_EXPERTS, D_FF
    bgu = b_gate_up.reshape(L, E, F // LANES, LANES, 2).transpose(0, 1, 2, 4, 3).reshape(L, E, 1, 2 * F)
    bdn = b_down.reshape(L, E, 1, D)
    w1p = jnp.pad(hy_w1, ((0, 0), (0, HY_EMB_PAD - HY_EMB), (0, 0)))
    w4_hi = hy_w4.astype(BF16)
    w4_split = jnp.stack([w4_hi, (hy_w4 - w4_hi.astype(F32)).astype(BF16)], axis=1)
    vec = lambda a: a.reshape(L, 1, -1)

    c_pad = jnp.pad(c, ((0, 8 - B), (0, 0)))
    mod = modulation(c_pad, w_mod, b_mod)[:, :B]
    mod = mod.reshape(L, B, N_MOD, 1, D).transpose(0, 2, 1, 3, 4)

    x2 = x.reshape(T, D)
    for l in range(L):
        u, qk, v, gr, gm = in_projection(x2, vec(norm_mix_g), mod, w_in_bf, l)
        xs = short_conv(u.reshape(B, S, -1), hy_conv_w, vec(hy_conv_b), l)
        col = lambda a: a.reshape(L, -1, 1)
        kraw, l1 = hyena_filter(tabs, w1p, col(hy_b1), hy_w2, col(hy_b2), hy_w3, col(hy_b3),
                                w4_split, col(hy_freq), l)
        ka = dft_first_axis(kraw.reshape(1, HY_ORDER, HY_SLABS, FFT_N1, FFT_N2, LANES), 0, tabs["f1"],
                            pair=1)
        kspec = filter_spectrum(ka, l1, tabs)
        y_hy = hyena_mixer(xs, kspec, hy_bias[l], tabs)
        qk3, v3, gr3 = qk.reshape(B, S, -1), v.reshape(B, S, -1), gr.reshape(B, S, -1)
        states = retention_states(qk3, v3, tabs)
        y_ret = retention_out(qk3, v3, gr3, states, tabs)
        x2, h_ffn, route, counts = merge_and_route(
            x2, y_hy, y_ret.reshape(T, -1), gm, mod, vec(norm_ffn_g),
            whb, wrb, wo, wr_split, vec(b_router), l)
        x2 = moe_ffn_residual(x2, h_ffn, route, counts, mod, final_g.reshape(1, D), w_gate_up, bgu,
                              w_down, bdn, l, final=(l == L - 1))
    return x2.reshape(B, S, D)
```

```python
import functools
import math

import numpy as np
import jax
import jax.numpy as jnp
from jax import lax
from jax.experimental import pallas as pl
from jax.experimental.pallas import tpu as pltpu

F32 = jnp.float32
BF16 = jnp.bfloat16
HIGHEST = lax.Precision.HIGHEST

D_MODEL = 1024
BATCH = 4
SEQ = 8192
DEPTH = 4
HY_WIDTH = 512
HY_ORDER = 2
HY_EMB = 33
HY_EMB_PAD = 64
HY_FILT_HIDDEN = 64
HY_FAST_DECAY = 0.3
HY_SLOW_DECAY = 1.5
HY_DECAY_TARGET = 1e-2
RET_HEADS = 4
RET_DK = 64
RET_DV = 128
RET_QK = RET_HEADS * RET_DK
RET_V = RET_HEADS * RET_DV
RET_CHUNK = 128
RET_DECAY_FWD = 5.0
RET_DECAY_BWD = 5.5
ROPE_BASE = 10000.0
N_EXPERTS = 32
TOP_K = 4
D_FF = D_MODEL
SWIGLU_ALPHA = 1.702
SWIGLU_LIMIT = 7.0
N_MOD = 6
EPS = 1e-6
IN_COLS = 3 * HY_WIDTH + 2 * RET_QK + 2 * RET_V + 2 * D_MODEL

LANES = 128
HY_SLABS = HY_WIDTH // LANES
ROW_TILES = D_MODEL // LANES
VMEM_LIMIT = 56 * 1024 * 1024

FFT_N = 2 * SEQ
FFT_N1 = 128
FFT_N2 = FFT_N // FFT_N1
FFT_H1 = FFT_N1 // 2

TOK_TILE = 256
INPROJ_TILE = 512
MERGE_HALVES = 2
EXPERT_TILE = 512
RANK_TILE = 1024
DISPATCH_TILE = 512
COMBINE_TILE = 256
RET_GROUP = 8
FFT_N2_BLOCK = 8
FFT_K1_GROUP = 8


def _cparams(sem):
    return pltpu.CompilerParams(dimension_semantics=sem, vmem_limit_bytes=VMEM_LIMIT)


def _const_spec(shape):
    return pl.BlockSpec(shape, lambda *_: (0,) * len(shape))


def _tables():
    n1 = np.arange(FFT_N1)
    n2 = np.arange(FFT_N2)
    ang1 = 2.0 * np.pi * np.outer(n1, n1) / FFT_N1
    f1 = np.concatenate([np.cos(ang1), -np.sin(ang1)], axis=0)
    ang2 = 2.0 * np.pi * np.outer(n2, n2) / FFT_N2
    angt = 2.0 * np.pi * np.outer(n1, n2) / FFT_N
    f1r, f1i = np.cos(ang1)[:, :FFT_H1], -np.sin(ang1)[:, :FFT_H1]
    f1c = np.block([[f1r, -f1i], [f1i, f1r]])
    gr, gi = np.cos(ang1)[:FFT_H1] / FFT_N, np.sin(ang1)[:FFT_H1] / FFT_N
    gc = np.block([[gr, -gi], [gi, gr]])

    L = SEQ
    t = np.linspace(0.0, 1.0, L)
    bands = (HY_EMB - 1) // 2
    w = 2.0 * np.pi * np.arange(L) / L
    f = np.linspace(1e-4, bands - 1, bands)
    feats = np.concatenate([t[:, None], np.cos(f[None] * w[:, None]), -np.sin(f[None] * w[:, None])], -1)
    idx = np.concatenate([np.arange(L), [0], L - np.arange(1, L)])
    feats2 = np.zeros((2 * L, HY_EMB_PAD))
    feats2[:, :HY_EMB] = feats[idx]
    max_decay = math.log(HY_DECAY_TARGET) / HY_FAST_DECAY
    min_decay = math.log(HY_DECAY_TARGET) / HY_SLOW_DECAY
    deltas = np.abs(np.linspace(min_decay, max_decay, HY_WIDTH))

    C = RET_CHUNK
    hh = np.arange(RET_HEADS)
    lgf = np.log(1.0 - np.exp2(-(RET_DECAY_FWD + hh)))
    lgb = np.log(1.0 - np.exp2(-(RET_DECAY_BWD + hh)))
    pos = np.arange(C)
    diff = pos[:, None] - pos[None, :]
    dec = np.where(diff[None] >= 0, np.exp(np.maximum(diff, 0)[None] * lgf[:, None, None]),
                   np.exp(np.maximum(-diff, 0)[None] * lgb[:, None, None]))
    lane_h = np.repeat(hh, RET_DK)
    wq = np.stack([np.exp((pos[:, None] + 1.0) * lgf[lane_h][None]),
                   np.exp((C - pos[:, None]) * lgb[lane_h][None])])
    wk = np.stack([np.exp((C - 1.0 - pos[:, None]) * lgf[lane_h][None]),
                   np.exp(pos[:, None] * lgb[lane_h][None])]) * (RET_DK ** -0.5)
    cd = np.stack([np.broadcast_to(np.exp(C * lgf[lane_h])[:, None], (RET_QK, RET_V)),
                   np.broadcast_to(np.exp(C * lgb[lane_h])[:, None], (RET_QK, RET_V))])
    col_h = np.repeat(hh, RET_DV)
    bd = (lane_h[:, None] == col_h[None, :]).astype(np.float64)
    inv_freq = 1.0 / (ROPE_BASE ** (np.arange(0, RET_DK, 2) / RET_DK))
    ang = np.arange(SEQ)[:, None] * inv_freq[None, :]
    cc = np.tile(np.concatenate([np.cos(ang), np.cos(ang)], -1), (1, RET_HEADS))
    ss = np.tile(np.concatenate([-np.sin(ang), np.sin(ang)], -1), (1, RET_HEADS))

    return dict(
        f1=jnp.asarray(f1, BF16), f1c=jnp.asarray(f1c, BF16),
        f2r=jnp.asarray(np.cos(ang2), F32), f2i=jnp.asarray(-np.sin(ang2), F32),
        twr=jnp.asarray(np.cos(angt), F32), twi=jnp.asarray(-np.sin(angt), F32),
        gc=jnp.asarray(gc, BF16),
        feats2=jnp.asarray(feats2, F32), feats2t=jnp.asarray(feats2.T, F32),
        deltas=jnp.asarray(deltas[None], F32),
        dec=jnp.asarray(dec, F32), wq=jnp.asarray(wq, F32), wk=jnp.asarray(wk, F32),
        cd=jnp.asarray(cd, F32), bd=jnp.asarray(bd, F32),
        cc=jnp.asarray(cc, F32), ss=jnp.asarray(ss, F32),
    )


def _mod_kernel(c_ref, w_ref, b_ref, o_ref):
    c = c_ref[...]
    ca = c * jax.nn.sigmoid(c)
    o_ref[...] = jnp.dot(ca, w_ref[...], preferred_element_type=F32, precision=HIGHEST) + b_ref[...]


def modulation(c_pad, w_mod, b_mod):
    L, D, N = w_mod.shape
    tn = 1536
    rows = c_pad.shape[0]
    return pl.pallas_call(
        _mod_kernel,
        grid=(L, N // tn),
        in_specs=[
            _const_spec((rows, D)),
            pl.BlockSpec((None, D, tn), lambda l, j: (l, 0, j)),
            pl.BlockSpec((None, 1, tn), lambda l, j: (l, 0, j)),
        ],
        out_specs=pl.BlockSpec((None, rows, tn), lambda l, j: (l, 0, j)),
        out_shape=jax.ShapeDtypeStruct((L, rows, N), F32),
        compiler_params=_cparams(("arbitrary", "arbitrary")),
        name="modulation",
    )(c_pad, w_mod, b_mod.reshape(L, 1, N))


def _norm_mod(x, g, sc, sh):
    y = x * lax.rsqrt(jnp.mean(x * x, axis=-1, keepdims=True) + EPS)
    return (y * g) * (1.0 + sc) + sh


def _inproj_kernel(x_ref, g_ref, sh_ref, sc_ref, w_ref, u_ref, qk_ref, v_ref, gr_ref, gm_ref):
    h = _norm_mod(x_ref[...], g_ref[...], sc_ref[...], sh_ref[...]).astype(BF16)
    c0 = 0
    for o_ref in (u_ref, qk_ref, v_ref, gr_ref, gm_ref):
        c1 = c0 + o_ref.shape[-1]
        o_ref[...] = jnp.dot(h, w_ref[:, c0:c1], preferred_element_type=F32).astype(o_ref.dtype)
        c0 = c1


def in_projection(x2, g, mod, w_in_bf, layer):
    T, D = x2.shape
    tm = INPROJ_TILE
    per_b = SEQ // tm
    mspec = lambda piece: pl.BlockSpec((None, None, None, 1, D),
                                       lambda i: (layer, piece, i // per_b, 0, 0))
    widths = (3 * HY_WIDTH, 2 * RET_QK, RET_V, RET_V, 2 * D_MODEL)
    dtypes = (BF16, F32, BF16, F32, BF16)
    return pl.pallas_call(
        _inproj_kernel,
        grid=(T // tm,),
        in_specs=[
            pl.BlockSpec((tm, D), lambda i: (i, 0)),
            pl.BlockSpec((None, 1, D), lambda i: (layer, 0, 0)),
            mspec(0), mspec(1),
            pl.BlockSpec((None, D, IN_COLS), lambda i: (layer, 0, 0)),
        ],
        out_specs=[pl.BlockSpec((tm, w), lambda i: (i, 0)) for w in widths],
        out_shape=[jax.ShapeDtypeStruct((T, w), dt) for w, dt in zip(widths, dtypes)],
        compiler_params=_cparams(("arbitrary",)),
        name="in_projection",
    )(x2, g, mod, mod, w_in_bf)


def _shortconv_kernel(u_ref, w_ref, b_ref, o_ref):
    u = u_ref[...].astype(F32)
    s = u.shape[0]
    row = lax.broadcasted_iota(jnp.int32, u.shape, 0)
    prev = jnp.where(row == 0, 0.0, pltpu.roll(u, 1, axis=0))
    nxt = jnp.where(row == s - 1, 0.0, pltpu.roll(u, s - 1, axis=0))
    w = w_ref[...]
    o_ref[...] = prev * w[0:1] + u * w[1:2] + nxt * w[2:3] + b_ref[...]


def short_conv(u, conv_w, conv_b, layer):
    B, S, C3 = u.shape
    cb = LANES
    per = HY_WIDTH // cb
    return pl.pallas_call(
        _shortconv_kernel,
        grid=(B, C3 // cb),
        in_specs=[
            pl.BlockSpec((None, S, cb), lambda b, j: (b, 0, j)),
            pl.BlockSpec((None, 3, cb), lambda b, j: (layer, 0, j)),
            pl.BlockSpec((None, 1, cb), lambda b, j: (layer, 0, j)),
        ],
        out_specs=pl.BlockSpec((None, None, None, S, cb), lambda b, j: (j // per, b, j % per, 0, 0)),
        out_shape=jax.ShapeDtypeStruct((3, B, per, S, cb), F32),
        compiler_params=_cparams(("arbitrary", "arbitrary")),
        name="short_conv",
    )(u, conv_w, conv_b)


def _filter_kernel(p_ref, pt_ref, w1_ref, b1_ref, w2_ref, b2_ref, w3_ref, b3_ref, w4_ref, fr_ref, dl_ref,
                   k_ref, l1_ref, *, rows):
    i = pl.program_id(0)
    fr = fr_ref[...]
    tdot = lambda w, x: lax.dot_general(w, x, (((0,), (0,)), ((), ())),
                                        preferred_element_type=F32, precision=HIGHEST)
    h = jnp.sin(fr * (tdot(w1_ref[...], pt_ref[...]) + b1_ref[...]))
    h = jnp.sin(fr * (tdot(w2_ref[...], h) + b2_ref[...]))
    h = jnp.sin(fr * (tdot(w3_ref[...], h) + b3_ref[...]))
    h_hi = h.astype(BF16)
    h_lo = (h - h_hi.astype(F32)).astype(BF16)
    bdot = lambda a, b: lax.dot_general(a, b, (((0,), (0,)), ((), ())), preferred_element_type=F32)
    h = bdot(h_hi, w4_ref[0]) + bdot(h_hi, w4_ref[1]) + bdot(h_lo, w4_ref[0])
    t = p_ref[:, 0:1]
    win = jnp.exp(-t * dl_ref[...])
    win = jnp.concatenate([win] * HY_ORDER, axis=1)
    grow = i * rows + lax.broadcasted_iota(jnp.int32, (rows, 1), 0)
    k = jnp.where(grow == SEQ, 0.0, h * win)
    for s in range(k_ref.shape[0]):
        k_ref[s] = k[:, s * LANES:(s + 1) * LANES]

    @pl.when(i == 0)
    def _():
        l1_ref[...] = jnp.zeros_like(l1_ref)
    l1_ref[...] += jnp.sum(jnp.abs(k), axis=0, keepdims=True)


def hyena_filter(tabs, w1p, b1, w2, b2, w3, b3, w4, freq, layer):
    rows = 1024
    n = 2 * SEQ
    half = SEQ // rows
    H = HY_FILT_HIDDEN
    OC = HY_ORDER * HY_WIDTH
    vec = lambda: pl.BlockSpec((None, H, 1), lambda i: (layer, 0, 0))
    return pl.pallas_call(
        functools.partial(_filter_kernel, rows=rows),
        grid=(n // rows,),
        in_specs=[
            pl.BlockSpec((rows, HY_EMB_PAD), lambda i: (i, 0)),
            pl.BlockSpec((HY_EMB_PAD, rows), lambda i: (0, i)),
            pl.BlockSpec((None, HY_EMB_PAD, H), lambda i: (layer, 0, 0)), vec(),
            pl.BlockSpec((None, H, H), lambda i: (layer, 0, 0)), vec(),
            pl.BlockSpec((None, H, H), lambda i: (layer, 0, 0)), vec(),
            pl.BlockSpec((None, 2, H, OC), lambda i: (layer, 0, 0, i // half)),
            vec(),
            _const_spec((1, HY_WIDTH)),
        ],
        out_specs=[pl.BlockSpec((OC // LANES, rows, LANES), lambda i: (0, i, 0)), _const_spec((1, OC))],
        out_shape=[jax.ShapeDtypeStruct((OC // LANES, n, LANES), F32),
                   jax.ShapeDtypeStruct((1, OC), F32)],
        compiler_params=_cparams(("arbitrary",)),
        name="hyena_filter",
    )(tabs["feats2"], tabs["feats2t"], w1p, b1, w2, b2, w3, b3, w4, freq, tabs["deltas"])


def _load_sub(ref, lead, j):
    *outer, r, s, l = ref.shape
    flat = ref.reshape(*outer, r * s, l)
    return flat[(*lead, pl.ds(j, r, stride=s), slice(None))]


def _store_sub(ref, lead, j, val):
    *outer, r, s, l = ref.shape
    flat = ref.reshape(*outer, r * s, l)
    flat[(*lead, pl.ds(j, r, stride=s), slice(None))] = val


def _pack_pair(re, im):
    return lax.bitcast_convert_type(pltpu.pack_elementwise([im, re], packed_dtype=BF16), jnp.uint32)


def _unpack_pair(p):
    p = lax.bitcast_convert_type(p, jnp.int32)
    im = pltpu.unpack_elementwise(p, index=0, packed_dtype=BF16, unpacked_dtype=F32)
    re = pltpu.unpack_elementwise(p, index=1, packed_dtype=BF16, unpacked_dtype=F32)
    return re, im


def _dft1_kernel(x_ref, f_ref, o_ref):
    for j in range(FFT_N2_BLOCK):
        xs = jnp.concatenate(
            [jnp.concatenate([_load_sub(x_ref, (p, c), j) for c in range(HY_SLABS)], axis=1)
             for p in range(x_ref.shape[0])], axis=0)
        a = jnp.dot(f_ref[...], xs.astype(BF16), preferred_element_type=F32)
        p = _pack_pair(a[:FFT_N1], a[FFT_N1:])
        for c in range(HY_SLABS):
            _store_sub(o_ref, (c,), j, p[:, c * LANES:(c + 1) * LANES])


def dft_first_axis(x6, which, f1, pair):
    _, B, _, rows, _, _ = x6.shape
    nb = FFT_N2_BLOCK
    return pl.pallas_call(
        _dft1_kernel,
        grid=(B // pair, FFT_N2 // nb),
        in_specs=[pl.BlockSpec((None, pair, HY_SLABS, rows, nb, LANES),
                               lambda b, j: (which, b, 0, 0, j, 0)),
                  _const_spec((2 * FFT_N1, pair * rows))],
        out_specs=pl.BlockSpec((None, HY_SLABS, FFT_N1, nb, LANES), lambda b, j: (b, 0, 0, j, 0)),
        out_shape=jax.ShapeDtypeStruct((B // pair, HY_SLABS, FFT_N1, FFT_N2, LANES), jnp.uint32),
        compiler_params=_cparams(("arbitrary", "arbitrary")),
        name="dft_first_axis",
    )(x6, f1)


def _twiddled_stack(f2r_ref, f2i_ref, twr_ref, twi_ref, k1):
    tr = twr_ref[pl.ds(k1, 1), :]
    ti = twi_ref[pl.ds(k1, 1), :]
    fr = f2r_ref[...]
    fi = f2i_ref[...]
    p = fr * tr - fi * ti
    q = fr * ti + fi * tr
    top = jnp.concatenate([p, -q], axis=1)
    bot = jnp.concatenate([q, p], axis=1)
    return jnp.concatenate([top, bot], axis=0).astype(BF16)


def _load_k1(a_ref, j):
    p = jnp.concatenate([a_ref[c, j] for c in range(HY_SLABS)], axis=1)
    re, im = _unpack_pair(p)
    return jnp.concatenate([re, im], axis=0).astype(BF16)


def _spectrum_kernel(a_ref, l1_ref, f2r_ref, f2i_ref, twr_ref, twi_ref, o_ref):
    base = pl.program_id(0) * FFT_K1_GROUP
    inv = 1.0 / l1_ref[...]
    for j in range(FFT_K1_GROUP):
        r = _twiddled_stack(f2r_ref, f2i_ref, twr_ref, twi_ref, base + j)
        x = jnp.dot(r, _load_k1(a_ref, j), preferred_element_type=F32) * inv
        o_ref[j, 0] = x[:FFT_N2].astype(o_ref.dtype)
        o_ref[j, 1] = x[FFT_N2:].astype(o_ref.dtype)


def filter_spectrum(a5, l1, tabs):
    g = FFT_K1_GROUP
    sq = lambda: _const_spec((FFT_N2, FFT_N2))
    return pl.pallas_call(
        _spectrum_kernel,
        grid=(FFT_N1 // g, HY_ORDER),
        in_specs=[pl.BlockSpec((None, HY_SLABS, g, FFT_N2, LANES), lambda i, o: (o, 0, i, 0, 0)),
                  pl.BlockSpec((1, HY_WIDTH), lambda i, o: (0, o)), sq(), sq(), sq(), sq()],
        out_specs=pl.BlockSpec((g, 2, FFT_N2, HY_WIDTH), lambda i, o: (i, 0, 0, o)),
        out_shape=jax.ShapeDtypeStruct((FFT_N1, 2, FFT_N2, HY_ORDER * HY_WIDTH), BF16),
        compiler_params=_cparams(("arbitrary", "arbitrary")),
        name="filter_spectrum",
    )(a5, l1, tabs["f2r"], tabs["f2i"], tabs["twr"], tabs["twi"])


def _convmid_kernel(a_ref, ks_ref, f2r_ref, f2i_ref, twr_ref, twi_ref, o_ref):
    base = pl.program_id(0) * FFT_K1_GROUP
    for j in range(FFT_K1_GROUP):
        r = _twiddled_stack(f2r_ref, f2i_ref, twr_ref, twi_ref, base + j)
        x = jnp.dot(r, _load_k1(a_ref, j), preferred_element_type=F32)
        xr, xi = x[:FFT_N2], x[FFT_N2:]
        kr, ki = ks_ref[j, 0].astype(F32), ks_ref[j, 1].astype(F32)
        y = jnp.concatenate([xr * kr - xi * ki, xr * ki + xi * kr], axis=0).astype(BF16)
        b = lax.dot_general(r, y, (((0,), (0,)), ((), ())), preferred_element_type=F32)
        p = _pack_pair(b[:FFT_N2], b[FFT_N2:])
        for c in range(HY_SLABS):
            o_ref[c, j] = p[:, c * LANES:(c + 1) * LANES]


def conv_mid(a5, kspec, order, tabs):
    B = a5.shape[0]
    g = FFT_K1_GROUP
    sq = lambda: _const_spec((FFT_N2, FFT_N2))
    blk = lambda: pl.BlockSpec((None, HY_SLABS, g, FFT_N2, LANES), lambda i, b: (b, 0, i, 0, 0))
    return pl.pallas_call(
        _convmid_kernel,
        grid=(FFT_N1 // g, B),
        in_specs=[blk(), pl.BlockSpec((g, 2, FFT_N2, HY_WIDTH), lambda i, b: (i, 0, 0, order)),
                  sq(), sq(), sq(), sq()],
        out_specs=blk(),
        out_shape=jax.ShapeDtypeStruct(a5.shape, jnp.uint32),
        compiler_params=_cparams(("arbitrary", "arbitrary")),
        name="conv_mid",
    )(a5, kspec, tabs["f2r"], tabs["f2i"], tabs["twr"], tabs["twi"])


def _convout_kernel(b_ref, g_ref, z_ref, gate_ref, bias_ref, o_ref):
    for j in range(FFT_N2_BLOCK):
        p = jnp.concatenate([_load_sub(b_ref, (c,), j) for c in range(HY_SLABS)], axis=1)
        re, im = _unpack_pair(p)
        bp = jnp.concatenate([re, im], axis=0).astype(BF16)
        y = jnp.dot(g_ref[...], bp, preferred_element_type=F32)
        for q in range(2):
            for c in range(HY_SLABS):
                _store_sub(o_ref, (q, c), j, y[q * FFT_H1:(q + 1) * FFT_H1, c * LANES:(c + 1) * LANES])
    o_ref[...] = gate_ref[...] * (o_ref[...] + z_ref[...] * bias_ref[...])


def conv_out(bp5, z6, z_which, gate6, gate_which, bias4, g):
    B2 = bp5.shape[0]
    nb = FFT_N2_BLOCK
    nat = lambda which: pl.BlockSpec((None, 2, HY_SLABS, FFT_H1, nb, LANES),
                                     lambda b, j: (which, b, 0, 0, j, 0))
    return pl.pallas_call(
        _convout_kernel,
        grid=(B2, FFT_N2 // nb),
        in_specs=[pl.BlockSpec((None, HY_SLABS, FFT_N1, nb, LANES), lambda b, j: (b, 0, 0, j, 0)),
                  _const_spec((2 * FFT_H1, 2 * FFT_N1)), nat(z_which), nat(gate_which),
                  _const_spec((HY_SLABS, 1, 1, LANES))],
        out_specs=pl.BlockSpec((2, HY_SLABS, FFT_H1, nb, LANES), lambda b, j: (b, 0, 0, j, 0)),
        out_shape=jax.ShapeDtypeStruct((2 * B2, HY_SLABS, FFT_H1, FFT_N2, LANES), F32),
        compiler_params=_cparams(("arbitrary", "arbitrary")),
        name="conv_out",
    )(bp5, g, z6, gate6, bias4)


def hyena_mixer(xs, kspec, bias, tabs):
    B = xs.shape[1]
    xs6 = xs.reshape(3, B, HY_SLABS, FFT_H1, FFT_N2, LANES)
    z6, z_which = xs6, 2
    for o in range(HY_ORDER):
        a = dft_first_axis(z6, z_which, tabs["f1c"], pair=2)
        bp = conv_mid(a, kspec, o, tabs)
        z = conv_out(bp, z6, z_which, xs6, o, bias[o].reshape(HY_SLABS, 1, 1, LANES), tabs["gc"])
        z6, z_which = z[None], 0
    return z.reshape(B, HY_SLABS, SEQ, LANES)


def _rope(x, cc, ss):
    lane = lax.broadcasted_iota(jnp.int32, x.shape, 1)
    n = x.shape[1]
    half = RET_DK // 2
    swapped = jnp.where(lane % RET_DK < half, pltpu.roll(x, n - half, axis=1),
                        pltpu.roll(x, half, axis=1))
    return x * cc + swapped * ss


def _retstate_kernel(k_ref, v_ref, cc_ref, ss_ref, wk_ref, cd_ref, o_ref, s_ref):
    d = pl.program_id(1)
    g = pl.program_id(2)
    C = RET_CHUNK

    @pl.when(g == 0)
    def _():
        s_ref[...] = jnp.zeros_like(s_ref)

    for j in range(RET_GROUP):
        ci = jnp.where(d == 0, j, RET_GROUP - 1 - j)
        r0 = pl.multiple_of(ci * C, C)
        s = s_ref[...]
        o_ref[ci] = jnp.concatenate(
            [s[h * RET_DK:(h + 1) * RET_DK, h * RET_DV:(h + 1) * RET_DV] for h in range(RET_HEADS)],
            axis=1)
        k = _rope(k_ref[pl.ds(r0, C), :], cc_ref[pl.ds(r0, C), :], ss_ref[pl.ds(r0, C), :])
        kw = (k * wk_ref[...]).astype(BF16)
        ds = lax.dot_general(kw, v_ref[pl.ds(r0, C), :], (((0,), (0,)), ((), ())),
                             preferred_element_type=F32)
        s_ref[...] = s * cd_ref[...] + ds


def retention_states(qk, v, tabs):
    B, S, _ = qk.shape
    rows = RET_GROUP * RET_CHUNK
    G = S // rows
    grp = lambda d, g: jnp.where(d == 0, g, G - 1 - g)
    return pl.pallas_call(
        _retstate_kernel,
        grid=(B, 2, G),
        in_specs=[
            pl.BlockSpec((None, rows, RET_QK), lambda b, d, g: (b, grp(d, g), 1)),
            pl.BlockSpec((None, rows, RET_V), lambda b, d, g: (b, grp(d, g), 0)),
            pl.BlockSpec((rows, RET_QK), lambda b, d, g: (grp(d, g), 0)),
            pl.BlockSpec((rows, RET_QK), lambda b, d, g: (grp(d, g), 0)),
            pl.BlockSpec((None, RET_CHUNK, RET_QK), lambda b, d, g: (d, 0, 0)),
            pl.BlockSpec((None, RET_QK, RET_V), lambda b, d, g: (d, 0, 0)),
        ],
        out_specs=pl.BlockSpec((None, None, RET_GROUP, RET_DK, RET_V),
                               lambda b, d, g: (b, d, grp(d, g), 0, 0)),
        out_shape=jax.ShapeDtypeStruct((B, 2, S // RET_CHUNK, RET_DK, RET_V), F32),
        scratch_shapes=[pltpu.VMEM((RET_QK, RET_V), F32)],
        compiler_params=_cparams(("arbitrary", "arbitrary", "arbitrary")),
        name="retention_states",
    )(qk, v, tabs["cc"], tabs["ss"], tabs["wk"], tabs["cd"])


def _retout_kernel(qk_ref, v_ref, gr_ref, cc_ref, ss_ref, st_ref, dec_ref, wq_ref, wk_ref, bd_ref,
                   o_ref):
    C = RET_CHUNK
    bd = bd_ref[...]
    lane = lax.broadcasted_iota(jnp.int32, (C, RET_QK), 1)
    for j in range(RET_GROUP):
        r0 = j * C
        cc = cc_ref[r0:r0 + C, :]
        ss = ss_ref[r0:r0 + C, :]
        q = _rope(qk_ref[r0:r0 + C, :RET_QK], cc, ss)
        k = (_rope(qk_ref[r0:r0 + C, RET_QK:], cc, ss) * (RET_DK ** -0.5)).astype(BF16)
        v = v_ref[r0:r0 + C, :]
        inner = []
        for h in range(RET_HEADS):
            qh = jnp.where(lane // RET_DK == h, q, 0.0).astype(BF16)
            s = lax.dot_general(qh, k, (((1,), (1,)), ((), ())), preferred_element_type=F32)
            s = (s * dec_ref[h]).astype(BF16)
            inner.append(jnp.dot(s, v[:, h * RET_DV:(h + 1) * RET_DV], preferred_element_type=F32))
        qq = jnp.concatenate([q * wq_ref[0], q * wq_ref[1]], axis=1).astype(BF16)
        sf = jnp.concatenate([st_ref[0, j]] * RET_HEADS, axis=0) * bd
        sb = jnp.concatenate([st_ref[1, j]] * RET_HEADS, axis=0) * bd
        sbd = jnp.concatenate([sf, sb], axis=0).astype(BF16)
        o = jnp.concatenate(inner, axis=1) + jnp.dot(qq, sbd, preferred_element_type=F32)
        outs = []
        for h in range(RET_HEADS):
            oh = o[:, h * RET_DV:(h + 1) * RET_DV]
            outs.append(oh * lax.rsqrt(jnp.mean(oh * oh, axis=-1, keepdims=True) + EPS))
        gr = gr_ref[r0:r0 + C, :]
        o_ref[r0:r0 + C, :] = (gr * jax.nn.sigmoid(gr)) * jnp.concatenate(outs, axis=1)


def retention_out(qk, v, gr, states, tabs):
    B, S, _ = qk.shape
    rows = RET_GROUP * RET_CHUNK
    G = S // rows
    return pl.pallas_call(
        _retout_kernel,
        grid=(B, G),
        in_specs=[
            pl.BlockSpec((None, rows, 2 * RET_QK), lambda b, g: (b, g, 0)),
            pl.BlockSpec((None, rows, RET_V), lambda b, g: (b, g, 0)),
            pl.BlockSpec((None, rows, RET_V), lambda b, g: (b, g, 0)),
            pl.BlockSpec((rows, RET_QK), lambda b, g: (g, 0)),
            pl.BlockSpec((rows, RET_QK), lambda b, g: (g, 0)),
            pl.BlockSpec((None, 2, RET_GROUP, RET_DK, RET_V), lambda b, g: (b, 0, g, 0, 0)),
            _const_spec((RET_HEADS, RET_CHUNK, RET_CHUNK)),
            _const_spec((2, RET_CHUNK, RET_QK)),
            _const_spec((2, RET_CHUNK, RET_QK)),
            _const_spec((RET_QK, RET_V)),
        ],
        out_specs=pl.BlockSpec((None, rows, RET_V), lambda b, g: (b, g, 0)),
        out_shape=jax.ShapeDtypeStruct((B, S, RET_V), F32),
        compiler_params=_cparams(("arbitrary", "arbitrary")),
        name="retention_out",
    )(qk, v, gr, tabs["cc"], tabs["ss"], states, tabs["dec"], tabs["wq"], tabs["wk"], tabs["bd"])


def _merge_rows(x, yh, yr, gm, ga, whb_ref, wrb_ref, wo_ref, g, sh, sc, wr_ref, br):
    ghy = 0.5 * jnp.tanh(0.5 * gm[:, :D_MODEL]) + 0.5
    grt = 0.5 * jnp.tanh(0.5 * gm[:, D_MODEL:]) + 0.5
    m = (ghy * jnp.dot(yh.astype(BF16), whb_ref[...], preferred_element_type=F32)
         + grt * jnp.dot(yr.astype(BF16), wrb_ref[...], preferred_element_type=F32))
    out = jnp.dot(m.astype(BF16), wo_ref[...], preferred_element_type=F32)
    x = x + ga * out
    h = _norm_mod(x, g, sc, sh)
    h_hi = h.astype(BF16)
    h_lo = (h - h_hi.astype(F32)).astype(BF16)
    logits = (jnp.dot(h_hi, wr_ref[0], preferred_element_type=F32)
              + jnp.dot(h_hi, wr_ref[1], preferred_element_type=F32)
              + jnp.dot(h_lo, wr_ref[0], preferred_element_type=F32)) + br
    tm = logits.shape[0]
    lane_e = lax.broadcasted_iota(jnp.int32, logits.shape, 1).astype(F32)
    lane_o = lax.broadcasted_iota(jnp.int32, (tm, LANES), 1)
    route = jnp.zeros((tm, LANES), F32)
    vals = []
    work = logits
    chosen = jnp.zeros(logits.shape, F32)
    for r in range(TOP_K):
        m_r = jnp.max(work, axis=-1, keepdims=True)
        i_r = jnp.min(jnp.where(work == m_r, lane_e, float(N_EXPERTS)), axis=-1, keepdims=True)
        hit = lane_e == i_r
        work = jnp.where(hit, -jnp.inf, work)
        chosen = chosen + hit.astype(F32)
        vals.append(m_r)
        route = jnp.where(lane_o == r, i_r, route)
    exps = [jnp.exp(v - vals[0]) for v in vals]
    den = exps[0] + exps[1] + exps[2] + exps[3]
    for r in range(TOP_K):
        route = jnp.where(lane_o == TOP_K + r, exps[r] / den, route)
    return x, h, route, jnp.sum(chosen, axis=0, keepdims=True)


def _merge_kernel(x_ref, yh_ref, yr_ref, gm_ref, ga_ref, whb_ref, wrb_ref, wo_ref,
                  g_ref, sh_ref, sc_ref, wr_ref, br_ref, xo_ref, h_ref, rt_ref, cnt_ref):
    tm = TOK_TILE

    @pl.when(pl.program_id(0) == 0)
    def _():
        cnt_ref[...] = jnp.zeros_like(cnt_ref)

    for half in range(MERGE_HALVES):
        rows = slice(half * tm, (half + 1) * tm)
        yh = jnp.concatenate([yh_ref[c, rows, :] for c in range(HY_SLABS)], axis=1)
        x, h, route, counts = _merge_rows(
            x_ref[rows, :], yh, yr_ref[rows, :], gm_ref[rows, :].astype(F32), ga_ref[...],
            whb_ref, wrb_ref, wo_ref, g_ref[...], sh_ref[...], sc_ref[...], wr_ref, br_ref[...])
        xo_ref[rows, :] = x
        rt_ref[rows, :] = route
        cnt_ref[...] += counts
        h_half = h_ref.at[rows]
        for j in range(ROW_TILES):
            _store_sub(h_half, (), j, h[:, j * LANES:(j + 1) * LANES])


def merge_and_route(x2, yh, yr, gm, mod, g_ffn, whb, wrb, wo, w_router, b_router, layer):
    T, D = x2.shape
    tm = TOK_TILE * MERGE_HALVES
    per_b = SEQ // tm
    mspec = lambda piece: pl.BlockSpec((None, None, None, 1, D),
                                       lambda i: (layer, piece, i // per_b, 0, 0))
    row = lambda w: pl.BlockSpec((tm, w), lambda i: (i, 0))
    wsp = lambda a, b: pl.BlockSpec((None, a, b), lambda i: (layer, 0, 0))
    return pl.pallas_call(
        _merge_kernel,
        grid=(T // tm,),
        in_specs=[row(D),
                  pl.BlockSpec((None, HY_SLABS, tm, LANES), lambda i: (i // per_b, 0, i % per_b, 0)),
                  row(RET_V), row(2 * D), mspec(2),
                  wsp(HY_WIDTH, D), wsp(RET_V, D), wsp(D, D),
                  wsp(1, D), mspec(3), mspec(4),
                  pl.BlockSpec((None, 2, D, N_EXPERTS), lambda i: (layer, 0, 0, 0)),
                  wsp(1, N_EXPERTS)],
        out_specs=[row(D), pl.BlockSpec((tm, ROW_TILES, LANES), lambda i: (i, 0, 0)), row(LANES),
                   _const_spec((1, N_EXPERTS))],
        out_shape=[jax.ShapeDtypeStruct((T, D), F32), jax.ShapeDtypeStruct((T, ROW_TILES, LANES), F32),
                   jax.ShapeDtypeStruct((T, LANES), F32), jax.ShapeDtypeStruct((1, N_EXPERTS), F32)],
        compiler_params=_cparams(("arbitrary",)),
        name="merge_and_route",
    )(x2, yh, yr, gm, mod, whb, wrb, wo, g_ffn, mod, mod, w_router, b_router)


def _slot_kernel(rt_ref, cnt_ref, tril_ref, upper_ref, o_ref, carry_ref):
    @pl.when(pl.program_id(0) == 0)
    def _():
        carry_ref[...] = jnp.zeros_like(carry_ref)

    padded = jnp.floor((cnt_ref[...] + (EXPERT_TILE - 1)) * (1.0 / EXPERT_TILE)) * EXPERT_TILE
    pad_start = jnp.dot(jnp.broadcast_to(padded, (8, N_EXPERTS)), upper_ref[...],
                        preferred_element_type=F32, precision=HIGHEST)[0:1]

    rt = rt_ref[...]
    tm = rt.shape[0]
    lane_e = lax.broadcasted_iota(jnp.int32, (tm, N_EXPERTS), 1).astype(F32)
    lane_o = lax.broadcasted_iota(jnp.int32, (tm, LANES), 1)
    hits = [lane_e == rt[:, r:r + 1] for r in range(TOP_K)]
    onehot = sum(h.astype(F32) for h in hits)
    before = (jnp.dot(tril_ref[...], onehot.astype(BF16), preferred_element_type=F32)
              + carry_ref[...] + pad_start)
    out = jnp.zeros((tm, LANES), F32)
    for r in range(TOP_K):
        slot_r = jnp.sum(jnp.where(hits[r], before, 0.0), axis=-1, keepdims=True)
        out = jnp.where(lane_o == r, slot_r, out)
    o_ref[...] = out.astype(jnp.int32)
    carry_ref[...] += jnp.sum(onehot, axis=0, keepdims=True)


def assignment_slots(route, counts):
    T = route.shape[0]
    tm = RANK_TILE
    tril = jnp.asarray(np.tril(np.ones((tm, tm)), -1), BF16)
    upper = jnp.asarray(np.triu(np.ones((N_EXPERTS, N_EXPERTS)), 1), F32)
    return pl.pallas_call(
        _slot_kernel,
        grid=(T // tm,),
        in_specs=[pl.BlockSpec((tm, LANES), lambda i: (i, 0)), _const_spec((1, N_EXPERTS)),
                  _const_spec((tm, tm)), _const_spec((N_EXPERTS, N_EXPERTS))],
        out_specs=pl.BlockSpec((tm, LANES), lambda i: (i, 0)),
        out_shape=jax.ShapeDtypeStruct((T, LANES), jnp.int32),
        scratch_shapes=[pltpu.VMEM((1, N_EXPERTS), F32)],
        compiler_params=_cparams(("arbitrary",)),
        name="assignment_slots",
    )(route, counts, tril, upper)


def _dispatch_kernel(dest_ref, pe_ref, h_ref, xs_hbm, zeros, sem, zsem):
    tm = DISPATCH_TILE
    t0 = pl.program_id(0) * tm

    @pl.when(pl.program_id(0) == 0)
    def _():
        zeros[...] = jnp.zeros_like(zeros)

        def zero_block(e):
            end = pe_ref[e]
            begin = pe_ref[jnp.maximum(e - 1, 0)]
            nonempty = jnp.logical_or(jnp.logical_and(e == 0, end > 0), end > begin)
            start = pl.multiple_of(jnp.maximum(end - EXPERT_TILE, 0), EXPERT_TILE)
            return nonempty, pltpu.make_async_copy(zeros, xs_hbm.at[pl.ds(start, EXPERT_TILE)], zsem)

        def start_zero(e, carry):
            nonempty, cp = zero_block(e)

            @pl.when(nonempty)
            def _():
                cp.start()
            return carry
        lax.fori_loop(0, N_EXPERTS, start_zero, 0)

        def wait_zero(e, carry):
            nonempty, cp = zero_block(e)

            @pl.when(nonempty)
            def _():
                cp.wait()
            return carry
        lax.fori_loop(0, N_EXPERTS, wait_zero, 0)

    def row_copy(i, d):
        return pltpu.make_async_copy(h_ref.at[i], xs_hbm.at[d], sem)

    def issue(i, carry):
        for r in range(TOP_K):
            row_copy(i, dest_ref[(t0 + i) * TOP_K + r]).start(priority=r % 2)
        return carry
    lax.fori_loop(0, tm, issue, 0, unroll=4)

    def drain(i, carry):
        row_copy(0, 0).wait()
        return carry
    lax.fori_loop(0, tm * TOP_K, drain, 0, unroll=8)


def dispatch(dest, pad_ends, h, n_rows):
    T = h.shape[0]
    tm = DISPATCH_TILE
    grid_spec = pltpu.PrefetchScalarGridSpec(
        num_scalar_prefetch=2,
        grid=(T // tm,),
        in_specs=[pl.BlockSpec((tm, ROW_TILES, LANES), lambda i, d, pe: (i, 0, 0))],
        out_specs=pl.BlockSpec(memory_space=pl.ANY),
        scratch_shapes=[pltpu.VMEM((EXPERT_TILE, ROW_TILES, LANES), h.dtype), pltpu.SemaphoreType.DMA,
                        pltpu.SemaphoreType.DMA],
    )
    return pl.pallas_call(
        _dispatch_kernel,
        grid_spec=grid_spec,
        out_shape=jax.ShapeDtypeStruct((n_rows, ROW_TILES, LANES), h.dtype),
        compiler_params=_cparams(("arbitrary",)),
        name="dispatch",
    )(dest, pad_ends, h)


def _expert_kernel(be_ref, nb_ref, x_ref, wgu_ref, bgu_ref, wd_ref, bd_ref, perm_ref, o_ref,
                   wgu_s, wd_s):
    i = pl.program_id(0)
    prev = be_ref[jnp.maximum(i - 1, 0)]
    new_expert = jnp.logical_or(i == 0, be_ref[i] != prev)
    n_chunk = 2 * D_FF // (2 * LANES)

    @pl.when(new_expert)
    def _():
        for c in range(n_chunk):
            cols = slice(c * 2 * LANES, (c + 1) * 2 * LANES)
            wgu_s[:, cols] = jnp.dot(wgu_ref[:, cols].astype(BF16), perm_ref[...],
                                     preferred_element_type=F32).astype(BF16)
        wd_s[...] = wd_ref[...].astype(BF16)

    @pl.when(i < nb_ref[0])
    def _():
        x = jnp.concatenate([_load_sub(x_ref, (), j) for j in range(ROW_TILES)], axis=1)
        gu = jnp.dot(x.astype(BF16), wgu_s[...], preferred_element_type=F32) + bgu_ref[...]
        acts = []
        for c in range(n_chunk):
            gate = jnp.minimum(gu[:, c * 2 * LANES:c * 2 * LANES + LANES], SWIGLU_LIMIT)
            up = jnp.clip(gu[:, c * 2 * LANES + LANES:(c + 1) * 2 * LANES], -SWIGLU_LIMIT, SWIGLU_LIMIT)
            acts.append((up + 1.0) * (gate * jax.nn.sigmoid(SWIGLU_ALPHA * gate)))
        act = jnp.concatenate(acts, axis=1).astype(BF16)
        y = jnp.dot(act, wd_s[...], preferred_element_type=F32) + bd_ref[...]
        for j in range(ROW_TILES):
            _store_sub(o_ref, (), j, y[:, j * LANES:(j + 1) * LANES])

    @pl.when(i >= nb_ref[0])
    def _():
        o_ref[...] = jnp.zeros_like(o_ref)


def expert_ffn(block_e, n_used, xs, w_gate_up, bgu, w_down, bd, layer):
    P = xs.shape[0]
    D = D_MODEL
    tm = EXPERT_TILE
    perm = np.zeros((2 * LANES, 2 * LANES))
    j = np.arange(LANES)
    perm[2 * j, j] = 1.0
    perm[2 * j + 1, LANES + j] = 1.0
    grid_spec = pltpu.PrefetchScalarGridSpec(
        num_scalar_prefetch=2,
        grid=(P // tm,),
        in_specs=[
            pl.BlockSpec((tm, ROW_TILES, LANES), lambda i, be, nb: (jnp.minimum(i, nb[0] - 1), 0, 0)),
            pl.BlockSpec((None, None, D, 2 * D_FF), lambda i, be, nb: (layer, be[i], 0, 0)),
            pl.BlockSpec((None, None, 1, 2 * D_FF), lambda i, be, nb: (layer, be[i], 0, 0)),
            pl.BlockSpec((None, None, D_FF, D), lambda i, be, nb: (layer, be[i], 0, 0)),
            pl.BlockSpec((None, None, 1, D), lambda i, be, nb: (layer, be[i], 0, 0)),
            pl.BlockSpec((2 * LANES, 2 * LANES), lambda i, be, nb: (0, 0)),
        ],
        out_specs=pl.BlockSpec((tm, ROW_TILES, LANES), lambda i, be, nb: (i, 0, 0)),
        scratch_shapes=[pltpu.VMEM((D, 2 * D_FF), BF16), pltpu.VMEM((D_FF, D), BF16)],
    )
    return pl.pallas_call(
        _expert_kernel,
        grid_spec=grid_spec,
        out_shape=jax.ShapeDtypeStruct((P, ROW_TILES, LANES), F32),
        compiler_params=_cparams(("arbitrary",)),
        name="expert_ffn",
    )(block_e, n_used, xs, w_gate_up, bgu, w_down, bd, jnp.asarray(perm, BF16))


def _combine_kernel(dest_ref, x_ref, rt_ref, ga_ref, g_ref, yb_hbm, o_ref, buf, sem, *, final):
    tm = COMBINE_TILE
    step = pl.program_id(0)
    slot = step % 2

    def row_copy(sl, i, r, d):
        return pltpu.make_async_copy(yb_hbm.at[d], buf.at[sl, r, i], sem.at[sl])

    def issue_step(st, sl):
        def issue(i, carry):
            for r in range(TOP_K):
                row_copy(sl, i, r, dest_ref[(st * tm + i) * TOP_K + r]).start(priority=r % 2)
            return carry
        lax.fori_loop(0, tm, issue, 0, unroll=4)

    @pl.when(step == 0)
    def _():
        issue_step(0, 0)

    @pl.when(step + 1 < pl.num_programs(0))
    def _():
        issue_step(step + 1, 1 - slot)

    def drain(i, carry):
        row_copy(slot, 0, 0, 0).wait()
        return carry
    lax.fori_loop(0, tm * TOP_K, drain, 0, unroll=8)

    rt = rt_ref[...]
    gates = [jnp.broadcast_to(rt[:, TOP_K + r:TOP_K + r + 1], (tm, LANES)) for r in range(TOP_K)]
    cols = []
    for j in range(ROW_TILES):
        s = _load_sub(buf, (slot, 0), j) * gates[0]
        for r in range(1, TOP_K):
            s = s + _load_sub(buf, (slot, r), j) * gates[r]
        cols.append(s)
    y = jnp.concatenate(cols, axis=1)
    x = x_ref[...] + ga_ref[...] * y
    if final:
        x = (x * lax.rsqrt(jnp.mean(x * x, axis=-1, keepdims=True) + EPS)) * g_ref[...]
    o_ref[...] = x


def combine(dest, x2, route, mod, g_final, yb, layer, final):
    T, D = x2.shape
    tm = COMBINE_TILE
    per_b = SEQ // tm
    grid_spec = pltpu.PrefetchScalarGridSpec(
        num_scalar_prefetch=1,
        grid=(T // tm,),
        in_specs=[
            pl.BlockSpec((tm, D), lambda i, d: (i, 0)),
            pl.BlockSpec((tm, LANES), lambda i, d: (i, 0)),
            pl.BlockSpec((None, None, None, 1, D), lambda i, d: (layer, 5, i // per_b, 0, 0)),
            pl.BlockSpec((1, D), lambda i, d: (0, 0)),
            pl.BlockSpec(memory_space=pl.ANY),
        ],
        out_specs=pl.BlockSpec((tm, D), lambda i, d: (i, 0)),
        scratch_shapes=[pltpu.VMEM((2, TOP_K, tm, ROW_TILES, LANES), F32),
                        pltpu.SemaphoreType.DMA((2,))],
    )
    return pl.pallas_call(
        functools.partial(_combine_kernel, final=final),
        grid_spec=grid_spec,
        out_shape=jax.ShapeDtypeStruct((T, D), F32),
        compiler_params=_cparams(("arbitrary",)),
        name="combine",
    )(dest, x2, route, mod, g_final, yb)


def moe_ffn_residual(x2, h, route, counts, mod, g_final, w_gate_up, bgu, w_down, bd, layer, final):
    T = h.shape[0]
    tm = EXPERT_TILE
    slots = assignment_slots(route, counts)
    counts = counts[0].astype(jnp.int32)
    padded = ((counts + tm - 1) // tm) * tm
    pad_ends = jnp.cumsum(padded)
    n_blocks = -(-(T * TOP_K + N_EXPERTS * (tm - 1)) // tm)
    block_start = jnp.arange(n_blocks, dtype=jnp.int32) * tm
    block_e = jnp.minimum(jnp.sum(pad_ends[None, :] <= block_start[:, None], axis=1),
                          N_EXPERTS - 1).astype(jnp.int32)
    n_used = (pad_ends[-1:] // tm).astype(jnp.int32)
    dest = slots[:, :TOP_K].reshape(-1)
    xs = dispatch(dest, pad_ends.astype(jnp.int32), h, n_blocks * tm)
    yb = expert_ffn(block_e, n_used, xs, w_gate_up, bgu, w_down, bd, layer)
    return combine(dest, x2, route, mod, g_final, yb, layer, final)


def kernel(x, c, norm_mix_g, norm_ffn_g, w_mod, b_mod, w_in, hy_conv_w, hy_conv_b, hy_w1, hy_b1, hy_w2, hy_b2, hy_w3, hy_b3, hy_w4, hy_freq, hy_bias, w_hy_br, w_ret_br, w_out, w_router, b_router, w_gate_up, b_gate_up, w_down, b_down, final_g):
    B, S, D = x.shape
    L = w_mod.shape[0]
    T = B * S
    tabs = _tables()

    w_in_bf = w_in.astype(BF16)
    whb = w_hy_br.astype(BF16)
    wrb = w_ret_br.astype(BF16)
    wo = w_out.astype(BF16)
    wr_hi = w_router.astype(BF16)
    wr_split = jnp.stack([wr_hi, (w_router - wr_hi.astype(F32)).astype(BF16)], axis=1)
    E, F = N_EXPERTS, D_FF
    bgu = b_gate_up.reshape(L, E, F // LANES, LANES, 2).transpose(0, 1, 2, 4, 3).reshape(L, E, 1, 2 * F)
    bdn = b_down.reshape(L, E, 1, D)
    w1p = jnp.pad(hy_w1, ((0, 0), (0, HY_EMB_PAD - HY_EMB), (0, 0)))
    w4_hi = hy_w4.astype(BF16)
    w4_split = jnp.stack([w4_hi, (hy_w4 - w4_hi.astype(F32)).astype(BF16)], axis=1)
    vec = lambda a: a.reshape(L, 1, -1)

    c_pad = jnp.pad(c, ((0, 8 - B), (0, 0)))
    mod = modulation(c_pad, w_mod, b_mod)[:, :B]
    mod = mod.reshape(L, B, N_MOD, 1, D).transpose(0, 2, 1, 3, 4)

    x2 = x.reshape(T, D)
    for l in range(L):
        u, qk, v, gr, gm = in_projection(x2, vec(norm_mix_g), mod, w_in_bf, l)
        xs = short_conv(u.reshape(B, S, -1), hy_conv_w, vec(hy_conv_b), l)
        col = lambda a: a.reshape(L, -1, 1)
        kraw, l1 = hyena_filter(tabs, w1p, col(hy_b1), hy_w2, col(hy_b2), hy_w3, col(hy_b3),
                                w4_split, col(hy_freq), l)
        ka = dft_first_axis(kraw.reshape(1, HY_ORDER, HY_SLABS, FFT_N1, FFT_N2, LANES), 0, tabs["f1"],
                            pair=1)
        kspec = filter_spectrum(ka, l1, tabs)
        y_hy = hyena_mixer(xs, kspec, hy_bias[l], tabs)
        qk3, v3, gr3 = qk.reshape(B, S, -1), v.reshape(B, S, -1), gr.reshape(B, S, -1)
        states = retention_states(qk3, v3, tabs)
        y_ret = retention_out(qk3, v3, gr3, states, tabs)
        x2, h_ffn, route, counts = merge_and_route(
            x2, y_hy, y_ret.reshape(T, -1), gm, mod, vec(norm_ffn_g),
            whb, wrb, wo, wr_split, vec(b_router), l)
        x2 = moe_ffn_residual(x2, h_ffn, route, counts, mod, final_g.reshape(1, D), w_gate_up, bgu,
                              w_down, bdn, l, final=(l == L - 1))
    return x2.reshape(B, S, D)
```

```python
import functools
import math

import numpy as np
import jax
import jax.numpy as jnp
from jax import lax
from jax.experimental import pallas as pl
from jax.experimental.pallas import tpu as pltpu

F32 = jnp.float32
BF16 = jnp.bfloat16
HIGHEST = lax.Precision.HIGHEST

D_MODEL = 1024
BATCH = 4
SEQ = 8192
DEPTH = 4
HY_WIDTH = 512
HY_ORDER = 2
HY_EMB = 33
HY_EMB_PAD = 64
HY_FILT_HIDDEN = 64
HY_FAST_DECAY = 0.3
HY_SLOW_DECAY = 1.5
HY_DECAY_TARGET = 1e-2
RET_HEADS = 4
RET_DK = 64
RET_DV = 128
RET_QK = RET_HEADS * RET_DK
RET_V = RET_HEADS * RET_DV
RET_CHUNK = 128
RET_DECAY_FWD = 5.0
RET_DECAY_BWD = 5.5
ROPE_BASE = 10000.0
N_EXPERTS = 32
TOP_K = 4
D_FF = D_MODEL
SWIGLU_ALPHA = 1.702
SWIGLU_LIMIT = 7.0
N_MOD = 6
EPS = 1e-6
IN_COLS = 3 * HY_WIDTH + 2 * RET_QK + 2 * RET_V + 2 * D_MODEL

LANES = 128
HY_SLABS = HY_WIDTH // LANES
ROW_TILES = D_MODEL // LANES
VMEM_LIMIT = 56 * 1024 * 1024

FFT_N = 2 * SEQ
FFT_N1 = 128
FFT_N2 = FFT_N // FFT_N1
FFT_H1 = FFT_N1 // 2

TOK_TILE = 256
INPROJ_TILE = 512
MERGE_HALVES = 2
EXPERT_TILE = 512
RANK_TILE = 1024
DISPATCH_TILE = 512
COMBINE_TILE = 256
RET_GROUP = 16
FFT_N2_BLOCK = 8
FFT_K1_GROUP = 16


def _cparams(sem):
    return pltpu.CompilerParams(dimension_semantics=sem, vmem_limit_bytes=VMEM_LIMIT)


def _const_spec(shape):
    return pl.BlockSpec(shape, lambda *_: (0,) * len(shape))


def _tables():
    n1 = np.arange(FFT_N1)
    n2 = np.arange(FFT_N2)
    ang1 = 2.0 * np.pi * np.outer(n1, n1) / FFT_N1
    f1 = np.concatenate([np.cos(ang1), -np.sin(ang1)], axis=0)
    ang2 = 2.0 * np.pi * np.outer(n2, n2) / FFT_N2
    angt = 2.0 * np.pi * np.outer(n1, n2) / FFT_N
    f1r, f1i = np.cos(ang1)[:, :FFT_H1], -np.sin(ang1)[:, :FFT_H1]
    f1c = np.block([[f1r, -f1i], [f1i, f1r]])
    gr, gi = np.cos(ang1)[:FFT_H1] / FFT_N, np.sin(ang1)[:FFT_H1] / FFT_N
    gc = np.block([[gr, -gi], [gi, gr]])

    L = SEQ
    t = np.linspace(0.0, 1.0, L)
    bands = (HY_EMB - 1) // 2
    w = 2.0 * np.pi * np.arange(L) / L
    f = np.linspace(1e-4, bands - 1, bands)
    feats = np.concatenate([t[:, None], np.cos(f[None] * w[:, None]), -np.sin(f[None] * w[:, None])], -1)
    idx = np.concatenate([np.arange(L), [0], L - np.arange(1, L)])
    feats2 = np.zeros((2 * L, HY_EMB_PAD))
    feats2[:, :HY_EMB] = feats[idx]
    max_decay = math.log(HY_DECAY_TARGET) / HY_FAST_DECAY
    min_decay = math.log(HY_DECAY_TARGET) / HY_SLOW_DECAY
    deltas = np.abs(np.linspace(min_decay, max_decay, HY_WIDTH))

    C = RET_CHUNK
    hh = np.arange(RET_HEADS)
    lgf = np.log(1.0 - np.exp2(-(RET_DECAY_FWD + hh)))
    lgb = np.log(1.0 - np.exp2(-(RET_DECAY_BWD + hh)))
    pos = np.arange(C)
    diff = pos[:, None] - pos[None, :]
    dec = np.where(diff[None] >= 0, np.exp(np.maximum(diff, 0)[None] * lgf[:, None, None]),
                   np.exp(np.maximum(-diff, 0)[None] * lgb[:, None, None]))
    lane_h = np.repeat(hh, RET_DK)
    wq = np.stack([np.exp((pos[:, None] + 1.0) * lgf[lane_h][None]),
                   np.exp((C - pos[:, None]) * lgb[lane_h][None])])
    wk = np.stack([np.exp((C - 1.0 - pos[:, None]) * lgf[lane_h][None]),
                   np.exp(pos[:, None] * lgb[lane_h][None])]) * (RET_DK ** -0.5)
    cd = np.stack([np.broadcast_to(np.exp(C * lgf[lane_h])[:, None], (RET_QK, RET_V)),
                   np.broadcast_to(np.exp(C * lgb[lane_h])[:, None], (RET_QK, RET_V))])
    col_h = np.repeat(hh, RET_DV)
    bd = (lane_h[:, None] == col_h[None, :]).astype(np.float64)
    inv_freq = 1.0 / (ROPE_BASE ** (np.arange(0, RET_DK, 2) / RET_DK))
    ang = np.arange(SEQ)[:, None] * inv_freq[None, :]
    cc = np.tile(np.concatenate([np.cos(ang), np.cos(ang)], -1), (1, RET_HEADS))
    ss = np.tile(np.concatenate([-np.sin(ang), np.sin(ang)], -1), (1, RET_HEADS))

    return dict(
        f1=jnp.asarray(f1, BF16), f1c=jnp.asarray(f1c, BF16),
        f2r=jnp.asarray(np.cos(ang2), F32), f2i=jnp.asarray(-np.sin(ang2), F32),
        twr=jnp.asarray(np.cos(angt), F32), twi=jnp.asarray(-np.sin(angt), F32),
        gc=jnp.asarray(gc, BF16),
        feats2=jnp.asarray(feats2, F32), feats2t=jnp.asarray(feats2.T, F32),
        deltas=jnp.asarray(deltas[None], F32),
        dec=jnp.asarray(dec, F32), wq=jnp.asarray(wq, F32), wk=jnp.asarray(wk, F32),
        cd=jnp.asarray(cd, F32), bd=jnp.asarray(bd, F32),
        cc=jnp.asarray(cc, F32), ss=jnp.asarray(ss, F32),
    )


def _mod_kernel(c_ref, w_ref, b_ref, o_ref):
    c = c_ref[...]
    ca = c * jax.nn.sigmoid(c)
    o_ref[...] = jnp.dot(ca, w_ref[...], preferred_element_type=F32, precision=HIGHEST) + b_ref[...]


def modulation(c_pad, w_mod, b_mod):
    L, D, N = w_mod.shape
    tn = 1536
    rows = c_pad.shape[0]
    return pl.pallas_call(
        _mod_kernel,
        grid=(L, N // tn),
        in_specs=[
            _const_spec((rows, D)),
            pl.BlockSpec((None, D, tn), lambda l, j: (l, 0, j)),
            pl.BlockSpec((None, 1, tn), lambda l, j: (l, 0, j)),
        ],
        out_specs=pl.BlockSpec((None, rows, tn), lambda l, j: (l, 0, j)),
        out_shape=jax.ShapeDtypeStruct((L, rows, N), F32),
        compiler_params=_cparams(("arbitrary", "arbitrary")),
        name="modulation",
    )(c_pad, w_mod, b_mod.reshape(L, 1, N))


def _norm_mod(x, g, sc, sh):
    y = x * lax.rsqrt(jnp.mean(x * x, axis=-1, keepdims=True) + EPS)
    return (y * g) * (1.0 + sc) + sh


def _inproj_kernel(x_ref, g_ref, sh_ref, sc_ref, w_ref, u_ref, qk_ref, v_ref, gr_ref, gm_ref):
    h = _norm_mod(x_ref[...], g_ref[...], sc_ref[...], sh_ref[...]).astype(BF16)
    c0 = 0
    for o_ref in (u_ref, qk_ref, v_ref, gr_ref, gm_ref):
        c1 = c0 + o_ref.shape[-1]
        o_ref[...] = jnp.dot(h, w_ref[:, c0:c1], preferred_element_type=F32).astype(o_ref.dtype)
        c0 = c1


def in_projection(x2, g, mod, w_in_bf, layer):
    T, D = x2.shape
    tm = INPROJ_TILE
    per_b = SEQ // tm
    mspec = lambda piece: pl.BlockSpec((None, None, None, 1, D),
                                       lambda i: (layer, piece, i // per_b, 0, 0))
    widths = (3 * HY_WIDTH, 2 * RET_QK, RET_V, RET_V, 2 * D_MODEL)
    dtypes = (BF16, F32, BF16, F32, BF16)
    return pl.pallas_call(
        _inproj_kernel,
        grid=(T // tm,),
        in_specs=[
            pl.BlockSpec((tm, D), lambda i: (i, 0)),
            pl.BlockSpec((None, 1, D), lambda i: (layer, 0, 0)),
            mspec(0), mspec(1),
            pl.BlockSpec((None, D, IN_COLS), lambda i: (layer, 0, 0)),
        ],
        out_specs=[pl.BlockSpec((tm, w), lambda i: (i, 0)) for w in widths],
        out_shape=[jax.ShapeDtypeStruct((T, w), dt) for w, dt in zip(widths, dtypes)],
        compiler_params=_cparams(("arbitrary",)),
        name="in_projection",
    )(x2, g, mod, mod, w_in_bf)


def _shortconv_kernel(u_ref, w_ref, b_ref, o_ref):
    u = u_ref[...].astype(F32)
    s = u.shape[0]
    row = lax.broadcasted_iota(jnp.int32, u.shape, 0)
    prev = jnp.where(row == 0, 0.0, pltpu.roll(u, 1, axis=0))
    nxt = jnp.where(row == s - 1, 0.0, pltpu.roll(u, s - 1, axis=0))
    w = w_ref[...]
    o_ref[...] = prev * w[0:1] + u * w[1:2] + nxt * w[2:3] + b_ref[...]


def short_conv(u, conv_w, conv_b, layer):
    B, S, C3 = u.shape
    cb = LANES
    per = HY_WIDTH // cb
    return pl.pallas_call(
        _shortconv_kernel,
        grid=(B, C3 // cb),
        in_specs=[
            pl.BlockSpec((None, S, cb), lambda b, j: (b, 0, j)),
            pl.BlockSpec((None, 3, cb), lambda b, j: (layer, 0, j)),
            pl.BlockSpec((None, 1, cb), lambda b, j: (layer, 0, j)),
        ],
        out_specs=pl.BlockSpec((None, None, None, S, cb), lambda b, j: (j // per, b, j % per, 0, 0)),
        out_shape=jax.ShapeDtypeStruct((3, B, per, S, cb), F32),
        compiler_params=_cparams(("arbitrary", "arbitrary")),
        name="short_conv",
    )(u, conv_w, conv_b)


def _filter_kernel(p_ref, pt_ref, w1_ref, b1_ref, w2_ref, b2_ref, w3_ref, b3_ref, w4_ref, fr_ref, dl_ref,
                   k_ref, l1_ref, *, rows):
    i = pl.program_id(0)
    fr = fr_ref[...]
    tdot = lambda w, x: lax.dot_general(w, x, (((0,), (0,)), ((), ())),
                                        preferred_element_type=F32, precision=HIGHEST)
    h = jnp.sin(fr * (tdot(w1_ref[...], pt_ref[...]) + b1_ref[...]))
    h = jnp.sin(fr * (tdot(w2_ref[...], h) + b2_ref[...]))
    h = jnp.sin(fr * (tdot(w3_ref[...], h) + b3_ref[...]))
    h_hi = h.astype(BF16)
    h_lo = (h - h_hi.astype(F32)).astype(BF16)
    bdot = lambda a, b: lax.dot_general(a, b, (((0,), (0,)), ((), ())), preferred_element_type=F32)
    h = bdot(h_hi, w4_ref[0]) + bdot(h_hi, w4_ref[1]) + bdot(h_lo, w4_ref[0])
    t = p_ref[:, 0:1]
    win = jnp.exp(-t * dl_ref[...])
    win = jnp.concatenate([win] * HY_ORDER, axis=1)
    grow = i * rows + lax.broadcasted_iota(jnp.int32, (rows, 1), 0)
    k = jnp.where(grow == SEQ, 0.0, h * win)
    for s in range(k_ref.shape[0]):
        k_ref[s] = k[:, s * LANES:(s + 1) * LANES]

    @pl.when(i == 0)
    def _():
        l1_ref[...] = jnp.zeros_like(l1_ref)
    l1_ref[...] += jnp.sum(jnp.abs(k), axis=0, keepdims=True)


def hyena_filter(tabs, w1p, b1, w2, b2, w3, b3, w4, freq, layer):
    rows = 1024
    n = 2 * SEQ
    half = SEQ // rows
    H = HY_FILT_HIDDEN
    OC = HY_ORDER * HY_WIDTH
    vec = lambda: pl.BlockSpec((None, H, 1), lambda i: (layer, 0, 0))
    return pl.pallas_call(
        functools.partial(_filter_kernel, rows=rows),
        grid=(n // rows,),
        in_specs=[
            pl.BlockSpec((rows, HY_EMB_PAD), lambda i: (i, 0)),
            pl.BlockSpec((HY_EMB_PAD, rows), lambda i: (0, i)),
            pl.BlockSpec((None, HY_EMB_PAD, H), lambda i: (layer, 0, 0)), vec(),
            pl.BlockSpec((None, H, H), lambda i: (layer, 0, 0)), vec(),
            pl.BlockSpec((None, H, H), lambda i: (layer, 0, 0)), vec(),
            pl.BlockSpec((None, 2, H, OC), lambda i: (layer, 0, 0, i // half)),
            vec(),
            _const_spec((1, HY_WIDTH)),
        ],
        out_specs=[pl.BlockSpec((OC // LANES, rows, LANES), lambda i: (0, i, 0)), _const_spec((1, OC))],
        out_shape=[jax.ShapeDtypeStruct((OC // LANES, n, LANES), F32),
                   jax.ShapeDtypeStruct((1, OC), F32)],
        compiler_params=_cparams(("arbitrary",)),
        name="hyena_filter",
    )(tabs["feats2"], tabs["feats2t"], w1p, b1, w2, b2, w3, b3, w4, freq, tabs["deltas"])


def _load_sub(ref, lead, j):
    *outer, r, s, l = ref.shape
    flat = ref.reshape(*outer, r * s, l)
    return flat[(*lead, pl.ds(j, r, stride=s), slice(None))]


def _store_sub(ref, lead, j, val):
    *outer, r, s, l = ref.shape
    flat = ref.reshape(*outer, r * s, l)
    flat[(*lead, pl.ds(j, r, stride=s), slice(None))] = val


def _pack_pair(re, im):
    return lax.bitcast_convert_type(pltpu.pack_elementwise([im, re], packed_dtype=BF16), jnp.uint32)


def _unpack_pair(p):
    p = lax.bitcast_convert_type(p, jnp.int32)
    im = pltpu.unpack_elementwise(p, index=0, packed_dtype=BF16, unpacked_dtype=F32)
    re = pltpu.unpack_elementwise(p, index=1, packed_dtype=BF16, unpacked_dtype=F32)
    return re, im


def _dft1_kernel(x_ref, f_ref, o_ref):
    for j in range(FFT_N2_BLOCK):
        xs = jnp.concatenate(
            [jnp.concatenate([_load_sub(x_ref, (p, c), j) for c in range(HY_SLABS)], axis=1)
             for p in range(x_ref.shape[0])], axis=0)
        a = jnp.dot(f_ref[...], xs.astype(BF16), preferred_element_type=F32)
        p = _pack_pair(a[:FFT_N1], a[FFT_N1:])
        for c in range(HY_SLABS):
            _store_sub(o_ref, (c,), j, p[:, c * LANES:(c + 1) * LANES])


def dft_first_axis(x6, which, f1, pair):
    _, B, _, rows, _, _ = x6.shape
    nb = FFT_N2_BLOCK
    return pl.pallas_call(
        _dft1_kernel,
        grid=(B // pair, FFT_N2 // nb),
        in_specs=[pl.BlockSpec((None, pair, HY_SLABS, rows, nb, LANES),
                               lambda b, j: (which, b, 0, 0, j, 0)),
                  _const_spec((2 * FFT_N1, pair * rows))],
        out_specs=pl.BlockSpec((None, HY_SLABS, FFT_N1, nb, LANES), lambda b, j: (b, 0, 0, j, 0)),
        out_shape=jax.ShapeDtypeStruct((B // pair, HY_SLABS, FFT_N1, FFT_N2, LANES), jnp.uint32),
        compiler_params=_cparams(("arbitrary", "arbitrary")),
        name="dft_first_axis",
    )(x6, f1)


def _twiddled_stack(f2r_ref, f2i_ref, twr_ref, twi_ref, k1):
    tr = twr_ref[pl.ds(k1, 1), :]
    ti = twi_ref[pl.ds(k1, 1), :]
    fr = f2r_ref[...]
    fi = f2i_ref[...]
    p = fr * tr - fi * ti
    q = fr * ti + fi * tr
    top = jnp.concatenate([p, -q], axis=1)
    bot = jnp.concatenate([q, p], axis=1)
    return jnp.concatenate([top, bot], axis=0).astype(BF16)


def _load_k1(a_ref, j):
    p = jnp.concatenate([a_ref[c, j] for c in range(HY_SLABS)], axis=1)
    re, im = _unpack_pair(p)
    return jnp.concatenate([re, im], axis=0).astype(BF16)


def _spectrum_kernel(a_ref, l1_ref, f2r_ref, f2i_ref, twr_ref, twi_ref, o_ref):
    base = pl.program_id(0) * FFT_K1_GROUP
    inv = 1.0 / l1_ref[...]
    for j in range(FFT_K1_GROUP):
        r = _twiddled_stack(f2r_ref, f2i_ref, twr_ref, twi_ref, base + j)
        x = jnp.dot(r, _load_k1(a_ref, j), preferred_element_type=F32) * inv
        o_ref[j, 0] = x[:FFT_N2].astype(o_ref.dtype)
        o_ref[j, 1] = x[FFT_N2:].astype(o_ref.dtype)


def filter_spectrum(a5, l1, tabs):
    g = FFT_K1_GROUP
    sq = lambda: _const_spec((FFT_N2, FFT_N2))
    return pl.pallas_call(
        _spectrum_kernel,
        grid=(FFT_N1 // g, HY_ORDER),
        in_specs=[pl.BlockSpec((None, HY_SLABS, g, FFT_N2, LANES), lambda i, o: (o, 0, i, 0, 0)),
                  pl.BlockSpec((1, HY_WIDTH), lambda i, o: (0, o)), sq(), sq(), sq(), sq()],
        out_specs=pl.BlockSpec((g, 2, FFT_N2, HY_WIDTH), lambda i, o: (i, 0, 0, o)),
        out_shape=jax.ShapeDtypeStruct((FFT_N1, 2, FFT_N2, HY_ORDER * HY_WIDTH), BF16),
        compiler_params=_cparams(("arbitrary", "arbitrary")),
        name="filter_spectrum",
    )(a5, l1, tabs["f2r"], tabs["f2i"], tabs["twr"], tabs["twi"])


def _convmid_kernel(a_ref, ks_ref, f2r_ref, f2i_ref, twr_ref, twi_ref, o_ref):
    base = pl.program_id(0) * FFT_K1_GROUP
    for j in range(FFT_K1_GROUP):
        r = _twiddled_stack(f2r_ref, f2i_ref, twr_ref, twi_ref, base + j)
        x = jnp.dot(r, _load_k1(a_ref, j), preferred_element_type=F32)
        xr, xi = x[:FFT_N2], x[FFT_N2:]
        kr, ki = ks_ref[j, 0].astype(F32), ks_ref[j, 1].astype(F32)
        y = jnp.concatenate([xr * kr - xi * ki, xr * ki + xi * kr], axis=0).astype(BF16)
        b = lax.dot_general(r, y, (((0,), (0,)), ((), ())), preferred_element_type=F32)
        p = _pack_pair(b[:FFT_N2], b[FFT_N2:])
        for c in range(HY_SLABS):
            o_ref[c, j] = p[:, c * LANES:(c + 1) * LANES]


def conv_mid(a5, kspec, order, tabs):
    B = a5.shape[0]
    g = FFT_K1_GROUP
    sq = lambda: _const_spec((FFT_N2, FFT_N2))
    blk = lambda: pl.BlockSpec((None, HY_SLABS, g, FFT_N2, LANES), lambda i, b: (b, 0, i, 0, 0))
    return pl.pallas_call(
        _convmid_kernel,
        grid=(FFT_N1 // g, B),
        in_specs=[blk(), pl.BlockSpec((g, 2, FFT_N2, HY_WIDTH), lambda i, b: (i, 0, 0, order)),
                  sq(), sq(), sq(), sq()],
        out_specs=blk(),
        out_shape=jax.ShapeDtypeStruct(a5.shape, jnp.uint32),
        compiler_params=_cparams(("arbitrary", "arbitrary")),
        name="conv_mid",
    )(a5, kspec, tabs["f2r"], tabs["f2i"], tabs["twr"], tabs["twi"])


def _convout_kernel(b_ref, g_ref, z_ref, gate_ref, bias_ref, o_ref):
    for j in range(FFT_N2_BLOCK):
        p = jnp.concatenate([_load_sub(b_ref, (c,), j) for c in range(HY_SLABS)], axis=1)
        re, im = _unpack_pair(p)
        bp = jnp.concatenate([re, im], axis=0).astype(BF16)
        y = jnp.dot(g_ref[...], bp, preferred_element_type=F32)
        for q in range(2):
            for c in range(HY_SLABS):
                _store_sub(o_ref, (q, c), j, y[q * FFT_H1:(q + 1) * FFT_H1, c * LANES:(c + 1) * LANES])
    o_ref[...] = gate_ref[...] * (o_ref[...] + z_ref[...] * bias_ref[...])


def conv_out(bp5, z6, z_which, gate6, gate_which, bias4, g):
    B2 = bp5.shape[0]
    nb = FFT_N2_BLOCK
    nat = lambda which: pl.BlockSpec((None, 2, HY_SLABS, FFT_H1, nb, LANES),
                                     lambda b, j: (which, b, 0, 0, j, 0))
    return pl.pallas_call(
        _convout_kernel,
        grid=(B2, FFT_N2 // nb),
        in_specs=[pl.BlockSpec((None, HY_SLABS, FFT_N1, nb, LANES), lambda b, j: (b, 0, 0, j, 0)),
                  _const_spec((2 * FFT_H1, 2 * FFT_N1)), nat(z_which), nat(gate_which),
                  _const_spec((HY_SLABS, 1, 1, LANES))],
        out_specs=pl.BlockSpec((2, HY_SLABS, FFT_H1, nb, LANES), lambda b, j: (b, 0, 0, j, 0)),
        out_shape=jax.ShapeDtypeStruct((2 * B2, HY_SLABS, FFT_H1, FFT_N2, LANES), F32),
        compiler_params=_cparams(("arbitrary", "arbitrary")),
        name="conv_out",
    )(bp5, g, z6, gate6, bias4)


def hyena_mixer(xs, kspec, bias, tabs):
    B = xs.shape[1]
    xs6 = xs.reshape(3, B, HY_SLABS, FFT_H1, FFT_N2, LANES)
    z6, z_which = xs6, 2
    for o in range(HY_ORDER):
        a = dft_first_axis(z6, z_which, tabs["f1c"], pair=2)
        bp = conv_mid(a, kspec, o, tabs)
        z = conv_out(bp, z6, z_which, xs6, o, bias[o].reshape(HY_SLABS, 1, 1, LANES), tabs["gc"])
        z6, z_which = z[None], 0
    return z.reshape(B, HY_SLABS, SEQ, LANES)


def _rope(x, cc, ss):
    lane = lax.broadcasted_iota(jnp.int32, x.shape, 1)
    n = x.shape[1]
    half = RET_DK // 2
    swapped = jnp.where(lane % RET_DK < half, pltpu.roll(x, n - half, axis=1),
                        pltpu.roll(x, half, axis=1))
    return x * cc + swapped * ss


def _retstate_kernel(k_ref, v_ref, cc_ref, ss_ref, wk_ref, cd_ref, o_ref, s_ref):
    d = pl.program_id(1)
    g = pl.program_id(2)
    C = RET_CHUNK

    @pl.when(g == 0)
    def _():
        s_ref[...] = jnp.zeros_like(s_ref)

    for j in range(RET_GROUP):
        ci = jnp.where(d == 0, j, RET_GROUP - 1 - j)
        r0 = pl.multiple_of(ci * C, C)
        s = s_ref[...]
        o_ref[ci] = jnp.concatenate(
            [s[h * RET_DK:(h + 1) * RET_DK, h * RET_DV:(h + 1) * RET_DV] for h in range(RET_HEADS)],
            axis=1)
        k = _rope(k_ref[pl.ds(r0, C), :], cc_ref[pl.ds(r0, C), :], ss_ref[pl.ds(r0, C), :])
        kw = (k * wk_ref[...]).astype(BF16)
        ds = lax.dot_general(kw, v_ref[pl.ds(r0, C), :], (((0,), (0,)), ((), ())),
                             preferred_element_type=F32)
        s_ref[...] = s * cd_ref[...] + ds


def retention_states(qk, v, tabs):
    B, S, _ = qk.shape
    rows = RET_GROUP * RET_CHUNK
    G = S // rows
    grp = lambda d, g: jnp.where(d == 0, g, G - 1 - g)
    return pl.pallas_call(
        _retstate_kernel,
        grid=(B, 2, G),
        in_specs=[
            pl.BlockSpec((None, rows, RET_QK), lambda b, d, g: (b, grp(d, g), 1)),
            pl.BlockSpec((None, rows, RET_V), lambda b, d, g: (b, grp(d, g), 0)),
            pl.BlockSpec((rows, RET_QK), lambda b, d, g: (grp(d, g), 0)),
            pl.BlockSpec((rows, RET_QK), lambda b, d, g: (grp(d, g), 0)),
            pl.BlockSpec((None, RET_CHUNK, RET_QK), lambda b, d, g: (d, 0, 0)),
            pl.BlockSpec((None, RET_QK, RET_V), lambda b, d, g: (d, 0, 0)),
        ],
        out_specs=pl.BlockSpec((None, None, RET_GROUP, RET_DK, RET_V),
                               lambda b, d, g: (b, d, grp(d, g), 0, 0)),
        out_shape=jax.ShapeDtypeStruct((B, 2, S // RET_CHUNK, RET_DK, RET_V), F32),
        scratch_shapes=[pltpu.VMEM((RET_QK, RET_V), F32)],
        compiler_params=_cparams(("arbitrary", "arbitrary", "arbitrary")),
        name="retention_states",
    )(qk, v, tabs["cc"], tabs["ss"], tabs["wk"], tabs["cd"])


def _retout_kernel(qk_ref, v_ref, gr_ref, cc_ref, ss_ref, st_ref, dec_ref, wq_ref, wk_ref, bd_ref,
                   o_ref):
    C = RET_CHUNK
    bd = bd_ref[...]
    lane = lax.broadcasted_iota(jnp.int32, (C, RET_QK), 1)
    for j in range(RET_GROUP):
        r0 = j * C
        cc = cc_ref[r0:r0 + C, :]
        ss = ss_ref[r0:r0 + C, :]
        q = _rope(qk_ref[r0:r0 + C, :RET_QK], cc, ss)
        k = (_rope(qk_ref[r0:r0 + C, RET_QK:], cc, ss) * (RET_DK ** -0.5)).astype(BF16)
        v = v_ref[r0:r0 + C, :]
        inner = []
        for h in range(RET_HEADS):
            qh = jnp.where(lane // RET_DK == h, q, 0.0).astype(BF16)
            s = lax.dot_general(qh, k, (((1,), (1,)), ((), ())), preferred_element_type=F32)
            s = (s * dec_ref[h]).astype(BF16)
            inner.append(jnp.dot(s, v[:, h * RET_DV:(h + 1) * RET_DV], preferred_element_type=F32))
        qq = jnp.concatenate([q * wq_ref[0], q * wq_ref[1]], axis=1).astype(BF16)
        sf = jnp.concatenate([st_ref[0, j]] * RET_HEADS, axis=0) * bd
        sb = jnp.concatenate([st_ref[1, j]] * RET_HEADS, axis=0) * bd
        sbd = jnp.concatenate([sf, sb], axis=0).astype(BF16)
        o = jnp.concatenate(inner, axis=1) + jnp.dot(qq, sbd, preferred_element_type=F32)
        outs = []
        for h in range(RET_HEADS):
            oh = o[:, h * RET_DV:(h + 1) * RET_DV]
            outs.append(oh * lax.rsqrt(jnp.mean(oh * oh, axis=-1, keepdims=True) + EPS))
        gr = gr_ref[r0:r0 + C, :]
        o_ref[r0:r0 + C, :] = (gr * jax.nn.sigmoid(gr)) * jnp.concatenate(outs, axis=1)


def retention_out(qk, v, gr, states, tabs):
    B, S, _ = qk.shape
    rows = RET_GROUP * RET_CHUNK
    G = S // rows
    return pl.pallas_call(
        _retout_kernel,
        grid=(B, G),
        in_specs=[
            pl.BlockSpec((None, rows, 2 * RET_QK), lambda b, g: (b, g, 0)),
            pl.BlockSpec((None, rows, RET_V), lambda b, g: (b, g, 0)),
            pl.BlockSpec((None, rows, RET_V), lambda b, g: (b, g, 0)),
            pl.BlockSpec((rows, RET_QK), lambda b, g: (g, 0)),
            pl.BlockSpec((rows, RET_QK), lambda b, g: (g, 0)),
            pl.BlockSpec((None, 2, RET_GROUP, RET_DK, RET_V), lambda b, g: (b, 0, g, 0, 0)),
            _const_spec((RET_HEADS, RET_CHUNK, RET_CHUNK)),
            _const_spec((2, RET_CHUNK, RET_QK)),
            _const_spec((2, RET_CHUNK, RET_QK)),
            _const_spec((RET_QK, RET_V)),
        ],
        out_specs=pl.BlockSpec((None, rows, RET_V), lambda b, g: (b, g, 0)),
        out_shape=jax.ShapeDtypeStruct((B, S, RET_V), F32),
        compiler_params=_cparams(("arbitrary", "arbitrary")),
        name="retention_out",
    )(qk, v, gr, tabs["cc"], tabs["ss"], states, tabs["dec"], tabs["wq"], tabs["wk"], tabs["bd"])


def _merge_rows(x, yh, yr, gm, ga, whb_ref, wrb_ref, wo_ref, g, sh, sc, wr_ref, br):
    ghy = 0.5 * jnp.tanh(0.5 * gm[:, :D_MODEL]) + 0.5
    grt = 0.5 * jnp.tanh(0.5 * gm[:, D_MODEL:]) + 0.5
    m = (ghy * jnp.dot(yh.astype(BF16), whb_ref[...], preferred_element_type=F32)
         + grt * jnp.dot(yr.astype(BF16), wrb_ref[...], preferred_element_type=F32))
    out = jnp.dot(m.astype(BF16), wo_ref[...], preferred_element_type=F32)
    x = x + ga * out
    h = _norm_mod(x, g, sc, sh)
    h_hi = h.astype(BF16)
    h_lo = (h - h_hi.astype(F32)).astype(BF16)
    logits = (jnp.dot(h_hi, wr_ref[0], preferred_element_type=F32)
              + jnp.dot(h_hi, wr_ref[1], preferred_element_type=F32)
              + jnp.dot(h_lo, wr_ref[0], preferred_element_type=F32)) + br
    tm = logits.shape[0]
    lane_e = lax.broadcasted_iota(jnp.int32, logits.shape, 1).astype(F32)
    lane_o = lax.broadcasted_iota(jnp.int32, (tm, LANES), 1)
    route = jnp.zeros((tm, LANES), F32)
    vals = []
    work = logits
    chosen = jnp.zeros(logits.shape, F32)
    for r in range(TOP_K):
        m_r = jnp.max(work, axis=-1, keepdims=True)
        i_r = jnp.min(jnp.where(work == m_r, lane_e, float(N_EXPERTS)), axis=-1, keepdims=True)
        hit = lane_e == i_r
        work = jnp.where(hit, -jnp.inf, work)
        chosen = chosen + hit.astype(F32)
        vals.append(m_r)
        route = jnp.where(lane_o == r, i_r, route)
    exps = [jnp.exp(v - vals[0]) for v in vals]
    den = exps[0] + exps[1] + exps[2] + exps[3]
    for r in range(TOP_K):
        route = jnp.where(lane_o == TOP_K + r, exps[r] / den, route)
    return x, h, route, jnp.sum(chosen, axis=0, keepdims=True)


def _merge_kernel(x_ref, yh_ref, yr_ref, gm_ref, ga_ref, whb_ref, wrb_ref, wo_ref,
                  g_ref, sh_ref, sc_ref, wr_ref, br_ref, xo_ref, h_ref, rt_ref, cnt_ref):
    tm = TOK_TILE

    @pl.when(pl.program_id(0) == 0)
    def _():
        cnt_ref[...] = jnp.zeros_like(cnt_ref)

    for half in range(MERGE_HALVES):
        rows = slice(half * tm, (half + 1) * tm)
        yh = jnp.concatenate([yh_ref[c, rows, :] for c in range(HY_SLABS)], axis=1)
        x, h, route, counts = _merge_rows(
            x_ref[rows, :], yh, yr_ref[rows, :], gm_ref[rows, :].astype(F32), ga_ref[...],
            whb_ref, wrb_ref, wo_ref, g_ref[...], sh_ref[...], sc_ref[...], wr_ref, br_ref[...])
        xo_ref[rows, :] = x
        rt_ref[rows, :] = route
        cnt_ref[...] += counts
        h_half = h_ref.at[rows]
        for j in range(ROW_TILES):
            _store_sub(h_half, (), j, h[:, j * LANES:(j + 1) * LANES])


def merge_and_route(x2, yh, yr, gm, mod, g_ffn, whb, wrb, wo, w_router, b_router, layer):
    T, D = x2.shape
    tm = TOK_TILE * MERGE_HALVES
    per_b = SEQ // tm
    mspec = lambda piece: pl.BlockSpec((None, None, None, 1, D),
                                       lambda i: (layer, piece, i // per_b, 0, 0))
    row = lambda w: pl.BlockSpec((tm, w), lambda i: (i, 0))
    wsp = lambda a, b: pl.BlockSpec((None, a, b), lambda i: (layer, 0, 0))
    return pl.pallas_call(
        _merge_kernel,
        grid=(T // tm,),
        in_specs=[row(D),
                  pl.BlockSpec((None, HY_SLABS, tm, LANES), lambda i: (i // per_b, 0, i % per_b, 0)),
                  row(RET_V), row(2 * D), mspec(2),
                  wsp(HY_WIDTH, D), wsp(RET_V, D), wsp(D, D),
                  wsp(1, D), mspec(3), mspec(4),
                  pl.BlockSpec((None, 2, D, N_EXPERTS), lambda i: (layer, 0, 0, 0)),
                  wsp(1, N_EXPERTS)],
        out_specs=[row(D), pl.BlockSpec((tm, ROW_TILES, LANES), lambda i: (i, 0, 0)), row(LANES),
                   _const_spec((1, N_EXPERTS))],
        out_shape=[jax.ShapeDtypeStruct((T, D), F32), jax.ShapeDtypeStruct((T, ROW_TILES, LANES), F32),
                   jax.ShapeDtypeStruct((T, LANES), F32), jax.ShapeDtypeStruct((1, N_EXPERTS), F32)],
        compiler_params=_cparams(("arbitrary",)),
        name="merge_and_route",
    )(x2, yh, yr, gm, mod, whb, wrb, wo, g_ffn, mod, mod, w_router, b_router)


def _slot_kernel(rt_ref, cnt_ref, tril_ref, upper_ref, o_ref, carry_ref):
    @pl.when(pl.program_id(0) == 0)
    def _():
        carry_ref[...] = jnp.zeros_like(carry_ref)

    padded = jnp.floor((cnt_ref[...] + (EXPERT_TILE - 1)) * (1.0 / EXPERT_TILE)) * EXPERT_TILE
    pad_start = jnp.dot(jnp.broadcast_to(padded, (8, N_EXPERTS)), upper_ref[...],
                        preferred_element_type=F32, precision=HIGHEST)[0:1]

    rt = rt_ref[...]
    tm = rt.shape[0]
    lane_e = lax.broadcasted_iota(jnp.int32, (tm, N_EXPERTS), 1).astype(F32)
    lane_o = lax.broadcasted_iota(jnp.int32, (tm, LANES), 1)
    hits = [lane_e == rt[:, r:r + 1] for r in range(TOP_K)]
    onehot = sum(h.astype(F32) for h in hits)
    before = (jnp.dot(tril_ref[...], onehot.astype(BF16), preferred_element_type=F32)
              + carry_ref[...] + pad_start)
    out = jnp.zeros((tm, LANES), F32)
    for r in range(TOP_K):
        slot_r = jnp.sum(jnp.where(hits[r], before, 0.0), axis=-1, keepdims=True)
        out = jnp.where(lane_o == r, slot_r, out)
    o_ref[...] = out.astype(jnp.int32)
    carry_ref[...] += jnp.sum(onehot, axis=0, keepdims=True)


def assignment_slots(route, counts):
    T = route.shape[0]
    tm = RANK_TILE
    tril = jnp.asarray(np.tril(np.ones((tm, tm)), -1), BF16)
    upper = jnp.asarray(np.triu(np.ones((N_EXPERTS, N_EXPERTS)), 1), F32)
    return pl.pallas_call(
        _slot_kernel,
        grid=(T // tm,),
        in_specs=[pl.BlockSpec((tm, LANES), lambda i: (i, 0)), _const_spec((1, N_EXPERTS)),
                  _const_spec((tm, tm)), _const_spec((N_EXPERTS, N_EXPERTS))],
        out_specs=pl.BlockSpec((tm, LANES), lambda i: (i, 0)),
        out_shape=jax.ShapeDtypeStruct((T, LANES), jnp.int32),
        scratch_shapes=[pltpu.VMEM((1, N_EXPERTS), F32)],
        compiler_params=_cparams(("arbitrary",)),
        name="assignment_slots",
    )(route, counts, tril, upper)


def _dispatch_kernel(dest_ref, pe_ref, h_ref, xs_hbm, zeros, sem, zsem):
    tm = DISPATCH_TILE
    t0 = pl.program_id(0) * tm

    @pl.when(pl.program_id(0) == 0)
    def _():
        zeros[...] = jnp.zeros_like(zeros)

        def zero_block(e):
            end = pe_ref[e]
            begin = pe_ref[jnp.maximum(e - 1, 0)]
            nonempty = jnp.logical_or(jnp.logical_and(e == 0, end > 0), end > begin)
            start = pl.multiple_of(jnp.maximum(end - EXPERT_TILE, 0), EXPERT_TILE)
            return nonempty, pltpu.make_async_copy(zeros, xs_hbm.at[pl.ds(start, EXPERT_TILE)], zsem)

        def start_zero(e, carry):
            nonempty, cp = zero_block(e)

            @pl.when(nonempty)
            def _():
                cp.start()
            return carry
        lax.fori_loop(0, N_EXPERTS, start_zero, 0)

        def wait_zero(e, carry):
            nonempty, cp = zero_block(e)

            @pl.when(nonempty)
            def _():
                cp.wait()
            return carry
        lax.fori_loop(0, N_EXPERTS, wait_zero, 0)

    def row_copy(i, d):
        return pltpu.make_async_copy(h_ref.at[i], xs_hbm.at[d], sem)

    def issue(i, carry):
        for r in range(TOP_K):
            row_copy(i, dest_ref[(t0 + i) * TOP_K + r]).start(priority=r % 2)
        return carry
    lax.fori_loop(0, tm, issue, 0, unroll=4)

    def drain(i, carry):
        row_copy(0, 0).wait()
        return carry
    lax.fori_loop(0, tm * TOP_K, drain, 0, unroll=8)


def dispatch(dest, pad_ends, h, n_rows):
    T = h.shape[0]
    tm = DISPATCH_TILE
    grid_spec = pltpu.PrefetchScalarGridSpec(
        num_scalar_prefetch=2,
        grid=(T // tm,),
        in_specs=[pl.BlockSpec((tm, ROW_TILES, LANES), lambda i, d, pe: (i, 0, 0))],
        out_specs=pl.BlockSpec(memory_space=pl.ANY),
        scratch_shapes=[pltpu.VMEM((EXPERT_TILE, ROW_TILES, LANES), h.dtype), pltpu.SemaphoreType.DMA,
                        pltpu.SemaphoreType.DMA],
    )
    return pl.pallas_call(
        _dispatch_kernel,
        grid_spec=grid_spec,
        out_shape=jax.ShapeDtypeStruct((n_rows, ROW_TILES, LANES), h.dtype),
        compiler_params=_cparams(("arbitrary",)),
        name="dispatch",
    )(dest, pad_ends, h)


def _expert_kernel(be_ref, nb_ref, x_ref, wgu_ref, bgu_ref, wd_ref, bd_ref, perm_ref, o_ref,
                   wgu_s, wd_s):
    i = pl.program_id(0)
    prev = be_ref[jnp.maximum(i - 1, 0)]
    new_expert = jnp.logical_or(i == 0, be_ref[i] != prev)
    n_chunk = 2 * D_FF // (2 * LANES)

    @pl.when(new_expert)
    def _():
        for c in range(n_chunk):
            cols = slice(c * 2 * LANES, (c + 1) * 2 * LANES)
            wgu_s[:, cols] = jnp.dot(wgu_ref[:, cols].astype(BF16), perm_ref[...],
                                     preferred_element_type=F32).astype(BF16)
        wd_s[...] = wd_ref[...].astype(BF16)

    @pl.when(i < nb_ref[0])
    def _():
        x = jnp.concatenate([_load_sub(x_ref, (), j) for j in range(ROW_TILES)], axis=1)
        gu = jnp.dot(x.astype(BF16), wgu_s[...], preferred_element_type=F32) + bgu_ref[...]
        acts = []
        for c in range(n_chunk):
            gate = jnp.minimum(gu[:, c * 2 * LANES:c * 2 * LANES + LANES], SWIGLU_LIMIT)
            up = jnp.clip(gu[:, c * 2 * LANES + LANES:(c + 1) * 2 * LANES], -SWIGLU_LIMIT, SWIGLU_LIMIT)
            acts.append((up + 1.0) * (gate * jax.nn.sigmoid(SWIGLU_ALPHA * gate)))
        act = jnp.concatenate(acts, axis=1).astype(BF16)
        y = jnp.dot(act, wd_s[...], preferred_element_type=F32) + bd_ref[...]
        for j in range(ROW_TILES):
            _store_sub(o_ref, (), j, y[:, j * LANES:(j + 1) * LANES])

    @pl.when(i >= nb_ref[0])
    def _():
        o_ref[...] = jnp.zeros_like(o_ref)


def expert_ffn(block_e, n_used, xs, w_gate_up, bgu, w_down, bd, layer):
    P = xs.shape[0]
    D = D_MODEL
    tm = EXPERT_TILE
    perm = np.zeros((2 * LANES, 2 * LANES))
    j = np.arange(LANES)
    perm[2 * j, j] = 1.0
    perm[2 * j + 1, LANES + j] = 1.0
    grid_spec = pltpu.PrefetchScalarGridSpec(
        num_scalar_prefetch=2,
        grid=(P // tm,),
        in_specs=[
            pl.BlockSpec((tm, ROW_TILES, LANES), lambda i, be, nb: (jnp.minimum(i, nb[0] - 1), 0, 0)),
            pl.BlockSpec((None, None, D, 2 * D_FF), lambda i, be, nb: (layer, be[i], 0, 0)),
            pl.BlockSpec((None, None, 1, 2 * D_FF), lambda i, be, nb: (layer, be[i], 0, 0)),
            pl.BlockSpec((None, None, D_FF, D), lambda i, be, nb: (layer, be[i], 0, 0)),
            pl.BlockSpec((None, None, 1, D), lambda i, be, nb: (layer, be[i], 0, 0)),
            pl.BlockSpec((2 * LANES, 2 * LANES), lambda i, be, nb: (0, 0)),
        ],
        out_specs=pl.BlockSpec((tm, ROW_TILES, LANES), lambda i, be, nb: (i, 0, 0)),
        scratch_shapes=[pltpu.VMEM((D, 2 * D_FF), BF16), pltpu.VMEM((D_FF, D), BF16)],
    )
    return pl.pallas_call(
        _expert_kernel,
        grid_spec=grid_spec,
        out_shape=jax.ShapeDtypeStruct((P, ROW_TILES, LANES), F32),
        compiler_params=_cparams(("arbitrary",)),
        name="expert_ffn",
    )(block_e, n_used, xs, w_gate_up, bgu, w_down, bd, jnp.asarray(perm, BF16))


def _combine_kernel(dest_ref, x_ref, rt_ref, ga_ref, g_ref, yb_hbm, o_ref, buf, sem, *, final):
    tm = COMBINE_TILE
    step = pl.program_id(0)
    slot = step % 2

    def row_copy(sl, i, r, d):
        return pltpu.make_async_copy(yb_hbm.at[d], buf.at[sl, r, i], sem.at[sl])

    def issue_step(st, sl):
        def issue(i, carry):
            for r in range(TOP_K):
                row_copy(sl, i, r, dest_ref[(st * tm + i) * TOP_K + r]).start(priority=r % 2)
            return carry
        lax.fori_loop(0, tm, issue, 0, unroll=4)

    @pl.when(step == 0)
    def _():
        issue_step(0, 0)

    @pl.when(step + 1 < pl.num_programs(0))
    def _():
        issue_step(step + 1, 1 - slot)

    def drain(i, carry):
        row_copy(slot, 0, 0, 0).wait()
        return carry
    lax.fori_loop(0, tm * TOP_K, drain, 0, unroll=8)

    rt = rt_ref[...]
    gates = [jnp.broadcast_to(rt[:, TOP_K + r:TOP_K + r + 1], (tm, LANES)) for r in range(TOP_K)]
    cols = []
    for j in range(ROW_TILES):
        s = _load_sub(buf, (slot, 0), j) * gates[0]
        for r in range(1, TOP_K):
            s = s + _load_sub(buf, (slot, r), j) * gates[r]
        cols.append(s)
    y = jnp.concatenate(cols, axis=1)
    x = x_ref[...] + ga_ref[...] * y
    if final:
        x = (x * lax.rsqrt(jnp.mean(x * x, axis=-1, keepdims=True) + EPS)) * g_ref[...]
    o_ref[...] = x


def combine(dest, x2, route, mod, g_final, yb, layer, final):
    T, D = x2.shape
    tm = COMBINE_TILE
    per_b = SEQ // tm
    grid_spec = pltpu.PrefetchScalarGridSpec(
        num_scalar_prefetch=1,
        grid=(T // tm,),
        in_specs=[
            pl.BlockSpec((tm, D), lambda i, d: (i, 0)),
            pl.BlockSpec((tm, LANES), lambda i, d: (i, 0)),
            pl.BlockSpec((None, None, None, 1, D), lambda i, d: (layer, 5, i // per_b, 0, 0)),
            pl.BlockSpec((1, D), lambda i, d: (0, 0)),
            pl.BlockSpec(memory_space=pl.ANY),
        ],
        out_specs=pl.BlockSpec((tm, D), lambda i, d: (i, 0)),
        scratch_shapes=[pltpu.VMEM((2, TOP_K, tm, ROW_TILES, LANES), F32),
                        pltpu.SemaphoreType.DMA((2,))],
    )
    return pl.pallas_call(
        functools.partial(_combine_kernel, final=final),
        grid_spec=grid_spec,
        out_shape=jax.ShapeDtypeStruct((T, D), F32),
        compiler_params=_cparams(("arbitrary",)),
        name="combine",
    )(dest, x2, route, mod, g_final, yb)


def moe_ffn_residual(x2, h, route, counts, mod, g_final, w_gate_up, bgu, w_down, bd, layer, final):
    T = h.shape[0]
    tm = EXPERT_TILE
    slots = assignment_slots(route, counts)
    counts = counts[0].astype(jnp.int32)
    padded = ((counts + tm - 1) // tm) * tm
    pad_ends = jnp.cumsum(padded)
    n_blocks = -(-(T * TOP_K + N_EXPERTS * (tm - 1)) // tm)
    block_start = jnp.arange(n_blocks, dtype=jnp.int32) * tm
    block_e = jnp.minimum(jnp.sum(pad_ends[None, :] <= block_start[:, None], axis=1),
                          N_EXPERTS - 1).astype(jnp.int32)
    n_used = (pad_ends[-1:] // tm).astype(jnp.int32)
    dest = slots[:, :TOP_K].reshape(-1)
    xs = dispatch(dest, pad_ends.astype(jnp.int32), h, n_blocks * tm)
    yb = expert_ffn(block_e, n_used, xs, w_gate_up, bgu, w_down, bd, layer)
    return combine(dest, x2, route, mod, g_final, yb, layer, final)


def kernel(x, c, norm_mix_g, norm_ffn_g, w_mod, b_mod, w_in, hy_conv_w, hy_conv_b, hy_w1, hy_b1, hy_w2, hy_b2, hy_w3, hy_b3, hy_w4, hy_freq, hy_bias, w_hy_br, w_ret_br, w_out, w_router, b_router, w_gate_up, b_gate_up, w_down, b_down, final_g):
    B, S, D = x.shape
    L = w_mod.shape[0]
    T = B * S
    tabs = _tables()

    w_in_bf = w_in.astype(BF16)
    whb = w_hy_br.astype(BF16)
    wrb = w_ret_br.astype(BF16)
    wo = w_out.astype(BF16)
    wr_hi = w_router.astype(BF16)
    wr_split = jnp.stack([wr_hi, (w_router - wr_hi.astype(F32)).astype(BF16)], axis=1)
    E, F = N_EXPERTS, D_FF
    bgu = b_gate_up.reshape(L, E, F // LANES, LANES, 2).transpose(0, 1, 2, 4, 3).reshape(L, E, 1, 2 * F)
    bdn = b_down.reshape(L, E, 1, D)
    w1p = jnp.pad(hy_w1, ((0, 0), (0, HY_EMB_PAD - HY_EMB), (0, 0)))
    w4_hi = hy_w4.astype(BF16)
    w4_split = jnp.stack([w4_hi, (hy_w4 - w4_hi.astype(F32)).astype(BF16)], axis=1)
    vec = lambda a: a.reshape(L, 1, -1)

    c_pad = jnp.pad(c, ((0, 8 - B), (0, 0)))
    mod = modulation(c_pad, w_mod, b_mod)[:, :B]
    mod = mod.reshape(L, B, N_MOD, 1, D).transpose(0, 2, 1, 3, 4)

    x2 = x.reshape(T, D)
    for l in range(L):
        u, qk, v, gr, gm = in_projection(x2, vec(norm_mix_g), mod, w_in_bf, l)
        xs = short_conv(u.reshape(B, S, -1), hy_conv_w, vec(hy_conv_b), l)
        col = lambda a: a.reshape(L, -1, 1)
        kraw, l1 = hyena_filter(tabs, w1p, col(hy_b1), hy_w2, col(hy_b2), hy_w3, col(hy_b3),
                                w4_split, col(hy_freq), l)
        ka = dft_first_axis(kraw.reshape(1, HY_ORDER, HY_SLABS, FFT_N1, FFT_N2, LANES), 0, tabs["f1"],
                            pair=1)
        kspec = filter_spectrum(ka, l1, tabs)
        y_hy = hyena_mixer(xs, kspec, hy_bias[l], tabs)
        qk3, v3, gr3 = qk.reshape(B, S, -1), v.reshape(B, S, -1), gr.reshape(B, S, -1)
        states = retention_states(qk3, v3, tabs)
        y_ret = retention_out(qk3, v3, gr3, states, tabs)
        x2, h_ffn, route, counts = merge_and_route(
            x2, y_hy, y_ret.reshape(T, -1), gm, mod, vec(norm_ffn_g),
            whb, wrb, wo, wr_split, vec(b_router), l)
        x2 = moe_ffn_residual(x2, h_ffn, route, counts, mod, final_g.reshape(1, D), w_gate_up, bgu,
                              w_down, bdn, l, final=(l == L - 1))
    return x2.reshape(B, S, D)
```

```python
import functools
import math

import numpy as np
import jax
import jax.numpy as jnp
from jax import lax
from jax.experimental import pallas as pl
from jax.experimental.pallas import tpu as pltpu

F32 = jnp.float32
BF16 = jnp.bfloat16
HIGHEST = lax.Precision.HIGHEST

D_MODEL = 1024
BATCH = 4
SEQ = 8192
DEPTH = 4
HY_WIDTH = 512
HY_ORDER = 2
HY_EMB = 33
HY_EMB_PAD = 64
HY_FILT_HIDDEN = 64
HY_FAST_DECAY = 0.3
HY_SLOW_DECAY = 1.5
HY_DECAY_TARGET = 1e-2
RET_HEADS = 4
RET_DK = 64
RET_DV = 128
RET_QK = RET_HEADS * RET_DK
RET_V = RET_HEADS * RET_DV
RET_CHUNK = 128
RET_DECAY_FWD = 5.0
RET_DECAY_BWD = 5.5
ROPE_BASE = 10000.0
N_EXPERTS = 32
TOP_K = 4
D_FF = D_MODEL
SWIGLU_ALPHA = 1.702
SWIGLU_LIMIT = 7.0
N_MOD = 6
EPS = 1e-6
IN_COLS = 3 * HY_WIDTH + 2 * RET_QK + 2 * RET_V + 2 * D_MODEL

LANES = 128
HY_SLABS = HY_WIDTH // LANES
ROW_TILES = D_MODEL // LANES
VMEM_LIMIT = 56 * 1024 * 1024

FFT_N = 2 * SEQ
FFT_N1 = 128
FFT_N2 = FFT_N // FFT_N1
FFT_H1 = FFT_N1 // 2

TOK_TILE = 256
INPROJ_TILE = 512
MERGE_HALVES = 2
EXPERT_TILE = 512
RANK_TILE = 1024
DISPATCH_TILE = 512
COMBINE_TILE = 256
RET_GROUP = 16
FFT_N2_BLOCK = 8
FFT_K1_GROUP = 16


def _cparams(sem):
    return pltpu.CompilerParams(dimension_semantics=sem, vmem_limit_bytes=VMEM_LIMIT)


def _const_spec(shape):
    return pl.BlockSpec(shape, lambda *_: (0,) * len(shape))


def _tables():
    n1 = np.arange(FFT_N1)
    n2 = np.arange(FFT_N2)
    ang1 = 2.0 * np.pi * np.outer(n1, n1) / FFT_N1
    f1 = np.concatenate([np.cos(ang1), -np.sin(ang1)], axis=0)
    ang2 = 2.0 * np.pi * np.outer(n2, n2) / FFT_N2
    angt = 2.0 * np.pi * np.outer(n1, n2) / FFT_N
    f1r, f1i = np.cos(ang1)[:, :FFT_H1], -np.sin(ang1)[:, :FFT_H1]
    f1c = np.block([[f1r, -f1i], [f1i, f1r]])
    gr, gi = np.cos(ang1)[:FFT_H1] / FFT_N, np.sin(ang1)[:FFT_H1] / FFT_N
    gc = np.block([[gr, -gi], [gi, gr]])

    L = SEQ
    t = np.linspace(0.0, 1.0, L)
    bands = (HY_EMB - 1) // 2
    w = 2.0 * np.pi * np.arange(L) / L
    f = np.linspace(1e-4, bands - 1, bands)
    feats = np.concatenate([t[:, None], np.cos(f[None] * w[:, None]), -np.sin(f[None] * w[:, None])], -1)
    idx = np.concatenate([np.arange(L), [0], L - np.arange(1, L)])
    feats2 = np.zeros((2 * L, HY_EMB_PAD))
    feats2[:, :HY_EMB] = feats[idx]
    max_decay = math.log(HY_DECAY_TARGET) / HY_FAST_DECAY
    min_decay = math.log(HY_DECAY_TARGET) / HY_SLOW_DECAY
    deltas = np.abs(np.linspace(min_decay, max_decay, HY_WIDTH))

    C = RET_CHUNK
    hh = np.arange(RET_HEADS)
    lgf = np.log(1.0 - np.exp2(-(RET_DECAY_FWD + hh)))
    lgb = np.log(1.0 - np.exp2(-(RET_DECAY_BWD + hh)))
    pos = np.arange(C)
    diff = pos[:, None] - pos[None, :]
    dec = np.where(diff[None] >= 0, np.exp(np.maximum(diff, 0)[None] * lgf[:, None, None]),
                   np.exp(np.maximum(-diff, 0)[None] * lgb[:, None, None]))
    lane_h = np.repeat(hh, RET_DK)
    wq = np.stack([np.exp((pos[:, None] + 1.0) * lgf[lane_h][None]),
                   np.exp((C - pos[:, None]) * lgb[lane_h][None])])
    wk = np.stack([np.exp((C - 1.0 - pos[:, None]) * lgf[lane_h][None]),
                   np.exp(pos[:, None] * lgb[lane_h][None])]) * (RET_DK ** -0.5)
    cd = np.stack([np.broadcast_to(np.exp(C * lgf[lane_h])[:, None], (RET_QK, RET_V)),
                   np.broadcast_to(np.exp(C * lgb[lane_h])[:, None], (RET_QK, RET_V))])
    col_h = np.repeat(hh, RET_DV)
    bd = (lane_h[:, None] == col_h[None, :]).astype(np.float64)
    inv_freq = 1.0 / (ROPE_BASE ** (np.arange(0, RET_DK, 2) / RET_DK))
    ang = np.arange(SEQ)[:, None] * inv_freq[None, :]
    cc = np.tile(np.concatenate([np.cos(ang), np.cos(ang)], -1), (1, RET_HEADS))
    ss = np.tile(np.concatenate([-np.sin(ang), np.sin(ang)], -1), (1, RET_HEADS))

    return dict(
        f1=jnp.asarray(f1, BF16), f1c=jnp.asarray(f1c, BF16),
        f2r=jnp.asarray(np.cos(ang2), F32), f2i=jnp.asarray(-np.sin(ang2), F32),
        twr=jnp.asarray(np.cos(angt), F32), twi=jnp.asarray(-np.sin(angt), F32),
        gc=jnp.asarray(gc, BF16),
        feats2=jnp.asarray(feats2, F32), feats2t=jnp.asarray(feats2.T, F32),
        deltas=jnp.asarray(deltas[None], F32),
        dec=jnp.asarray(dec, F32), wq=jnp.asarray(wq, F32), wk=jnp.asarray(wk, F32),
        cd=jnp.asarray(cd, F32), bd=jnp.asarray(bd, F32),
        cc=jnp.asarray(cc, F32), ss=jnp.asarray(ss, F32),
    )


def _mod_kernel(c_ref, w_ref, b_ref, o_ref):
    c = c_ref[...]
    ca = c * jax.nn.sigmoid(c)
    o_ref[...] = jnp.dot(ca, w_ref[...], preferred_element_type=F32, precision=HIGHEST) + b_ref[...]


def modulation(c_pad, w_mod, b_mod):
    L, D, N = w_mod.shape
    tn = 1536
    rows = c_pad.shape[0]
    return pl.pallas_call(
        _mod_kernel,
        grid=(L, N // tn),
        in_specs=[
            _const_spec((rows, D)),
            pl.BlockSpec((None, D, tn), lambda l, j: (l, 0, j)),
            pl.BlockSpec((None, 1, tn), lambda l, j: (l, 0, j)),
        ],
        out_specs=pl.BlockSpec((None, rows, tn), lambda l, j: (l, 0, j)),
        out_shape=jax.ShapeDtypeStruct((L, rows, N), F32),
        compiler_params=_cparams(("arbitrary", "arbitrary")),
        name="modulation",
    )(c_pad, w_mod, b_mod.reshape(L, 1, N))


def _norm_mod(x, g, sc, sh):
    y = x * lax.rsqrt(jnp.mean(x * x, axis=-1, keepdims=True) + EPS)
    return (y * g) * (1.0 + sc) + sh


def _inproj_kernel(x_ref, g_ref, sh_ref, sc_ref, w_ref, u_ref, qk_ref, v_ref, gr_ref, gm_ref):
    h = _norm_mod(x_ref[...], g_ref[...], sc_ref[...], sh_ref[...]).astype(BF16)
    c0 = 0
    for o_ref in (u_ref, qk_ref, v_ref, gr_ref, gm_ref):
        c1 = c0 + o_ref.shape[-1]
        o_ref[...] = jnp.dot(h, w_ref[:, c0:c1], preferred_element_type=F32).astype(o_ref.dtype)
        c0 = c1


def in_projection(x2, g, mod, w_in_bf, layer):
    T, D = x2.shape
    tm = INPROJ_TILE
    per_b = SEQ // tm
    mspec = lambda piece: pl.BlockSpec((None, None, None, 1, D),
                                       lambda i: (layer, piece, i // per_b, 0, 0))
    widths = (3 * HY_WIDTH, 2 * RET_QK, RET_V, RET_V, 2 * D_MODEL)
    dtypes = (BF16, F32, BF16, F32, BF16)
    return pl.pallas_call(
        _inproj_kernel,
        grid=(T // tm,),
        in_specs=[
            pl.BlockSpec((tm, D), lambda i: (i, 0)),
            pl.BlockSpec((None, 1, D), lambda i: (layer, 0, 0)),
            mspec(0), mspec(1),
            pl.BlockSpec((None, D, IN_COLS), lambda i: (layer, 0, 0)),
        ],
        out_specs=[pl.BlockSpec((tm, w), lambda i: (i, 0)) for w in widths],
        out_shape=[jax.ShapeDtypeStruct((T, w), dt) for w, dt in zip(widths, dtypes)],
        compiler_params=_cparams(("arbitrary",)),
        name="in_projection",
    )(x2, g, mod, mod, w_in_bf)


def _shortconv_kernel(u_ref, w_ref, b_ref, o_ref):
    u = u_ref[...].astype(F32)
    s = u.shape[0]
    row = lax.broadcasted_iota(jnp.int32, u.shape, 0)
    prev = jnp.where(row == 0, 0.0, pltpu.roll(u, 1, axis=0))
    nxt = jnp.where(row == s - 1, 0.0, pltpu.roll(u, s - 1, axis=0))
    w = w_ref[...]
    o_ref[...] = prev * w[0:1] + u * w[1:2] + nxt * w[2:3] + b_ref[...]


def short_conv(u, conv_w, conv_b, layer):
    B, S, C3 = u.shape
    cb = LANES
    per = HY_WIDTH // cb
    return pl.pallas_call(
        _shortconv_kernel,
        grid=(B, C3 // cb),
        in_specs=[
            pl.BlockSpec((None, S, cb), lambda b, j: (b, 0, j)),
            pl.BlockSpec((None, 3, cb), lambda b, j: (layer, 0, j)),
            pl.BlockSpec((None, 1, cb), lambda b, j: (layer, 0, j)),
        ],
        out_specs=pl.BlockSpec((None, None, None, S, cb), lambda b, j: (j // per, b, j % per, 0, 0)),
        out_shape=jax.ShapeDtypeStruct((3, B, per, S, cb), F32),
        compiler_params=_cparams(("arbitrary", "arbitrary")),
        name="short_conv",
    )(u, conv_w, conv_b)


def _filter_kernel(p_ref, pt_ref, w1_ref, b1_ref, w2_ref, b2_ref, w3_ref, b3_ref, w4_ref, fr_ref, dl_ref,
                   k_ref, l1_ref, *, rows):
    i = pl.program_id(0)
    fr = fr_ref[...]
    tdot = lambda w, x: lax.dot_general(w, x, (((0,), (0,)), ((), ())),
                                        preferred_element_type=F32, precision=HIGHEST)
    h = jnp.sin(fr * (tdot(w1_ref[...], pt_ref[...]) + b1_ref[...]))
    h = jnp.sin(fr * (tdot(w2_ref[...], h) + b2_ref[...]))
    h = jnp.sin(fr * (tdot(w3_ref[...], h) + b3_ref[...]))
    h_hi = h.astype(BF16)
    h_lo = (h - h_hi.astype(F32)).astype(BF16)
    bdot = lambda a, b: lax.dot_general(a, b, (((0,), (0,)), ((), ())), preferred_element_type=F32)
    h = bdot(h_hi, w4_ref[0]) + bdot(h_hi, w4_ref[1]) + bdot(h_lo, w4_ref[0])
    t = p_ref[:, 0:1]
    win = jnp.exp(-t * dl_ref[...])
    win = jnp.concatenate([win] * HY_ORDER, axis=1)
    grow = i * rows + lax.broadcasted_iota(jnp.int32, (rows, 1), 0)
    k = jnp.where(grow == SEQ, 0.0, h * win)
    for s in range(k_ref.shape[0]):
        k_ref[s] = k[:, s * LANES:(s + 1) * LANES]

    @pl.when(i == 0)
    def _():
        l1_ref[...] = jnp.zeros_like(l1_ref)
    l1_ref[...] += jnp.sum(jnp.abs(k), axis=0, keepdims=True)


def hyena_filter(tabs, w1p, b1, w2, b2, w3, b3, w4, freq, layer):
    rows = 1024
    n = 2 * SEQ
    half = SEQ // rows
    H = HY_FILT_HIDDEN
    OC = HY_ORDER * HY_WIDTH
    vec = lambda: pl.BlockSpec((None, H, 1), lambda i: (layer, 0, 0))
    return pl.pallas_call(
        functools.partial(_filter_kernel, rows=rows),
        grid=(n // rows,),
        in_specs=[
            pl.BlockSpec((rows, HY_EMB_PAD), lambda i: (i, 0)),
            pl.BlockSpec((HY_EMB_PAD, rows), lambda i: (0, i)),
            pl.BlockSpec((None, HY_EMB_PAD, H), lambda i: (layer, 0, 0)), vec(),
            pl.BlockSpec((None, H, H), lambda i: (layer, 0, 0)), vec(),
            pl.BlockSpec((None, H, H), lambda i: (layer, 0, 0)), vec(),
            pl.BlockSpec((None, 2, H, OC), lambda i: (layer, 0, 0, i // half)),
            vec(),
            _const_spec((1, HY_WIDTH)),
        ],
        out_specs=[pl.BlockSpec((OC // LANES, rows, LANES), lambda i: (0, i, 0)), _const_spec((1, OC))],
        out_shape=[jax.ShapeDtypeStruct((OC // LANES, n, LANES), F32),
                   jax.ShapeDtypeStruct((1, OC), F32)],
        compiler_params=_cparams(("arbitrary",)),
        name="hyena_filter",
    )(tabs["feats2"], tabs["feats2t"], w1p, b1, w2, b2, w3, b3, w4, freq, tabs["deltas"])


def _load_sub(ref, lead, j):
    *outer, r, s, l = ref.shape
    flat = ref.reshape(*outer, r * s, l)
    return flat[(*lead, pl.ds(j, r, stride=s), slice(None))]


def _store_sub(ref, lead, j, val):
    *outer, r, s, l = ref.shape
    flat = ref.reshape(*outer, r * s, l)
    flat[(*lead, pl.ds(j, r, stride=s), slice(None))] = val


def _pack_pair(re, im):
    return lax.bitcast_convert_type(pltpu.pack_elementwise([im, re], packed_dtype=BF16), jnp.uint32)


def _unpack_pair(p):
    p = lax.bitcast_convert_type(p, jnp.int32)
    im = pltpu.unpack_elementwise(p, index=0, packed_dtype=BF16, unpacked_dtype=F32)
    re = pltpu.unpack_elementwise(p, index=1, packed_dtype=BF16, unpacked_dtype=F32)
    return re, im


def _dft1_kernel(x_ref, f_ref, o_ref):
    n_out = o_ref.shape[0]
    pair = x_ref.shape[0] // n_out
    for q in range(n_out):
        for j in range(FFT_N2_BLOCK):
            xs = jnp.concatenate(
                [jnp.concatenate([_load_sub(x_ref, (q * pair + p, c), j) for c in range(HY_SLABS)], axis=1)
                 for p in range(pair)], axis=0)
            a = jnp.dot(f_ref[...], xs.astype(BF16), preferred_element_type=F32)
            pk = _pack_pair(a[:FFT_N1], a[FFT_N1:])
            for c in range(HY_SLABS):
                _store_sub(o_ref, (q, c), j, pk[:, c * LANES:(c + 1) * LANES])


def dft_first_axis(x6, which, f1, pair):
    _, B, _, rows, _, _ = x6.shape
    nb = FFT_N2_BLOCK
    return pl.pallas_call(
        _dft1_kernel,
        grid=(FFT_N2 // nb,),
        in_specs=[pl.BlockSpec((None, B, HY_SLABS, rows, nb, LANES), lambda j: (which, 0, 0, 0, j, 0)),
                  _const_spec((2 * FFT_N1, pair * rows))],
        out_specs=pl.BlockSpec((B // pair, HY_SLABS, FFT_N1, nb, LANES), lambda j: (0, 0, 0, j, 0)),
        out_shape=jax.ShapeDtypeStruct((B // pair, HY_SLABS, FFT_N1, FFT_N2, LANES), jnp.uint32),
        compiler_params=_cparams(("arbitrary",)),
        name="dft_first_axis",
    )(x6, f1)


def _twiddled_stack(f2r_ref, f2i_ref, twr_ref, twi_ref, k1):
    tr = twr_ref[pl.ds(k1, 1), :]
    ti = twi_ref[pl.ds(k1, 1), :]
    fr = f2r_ref[...]
    fi = f2i_ref[...]
    p = fr * tr - fi * ti
    q = fr * ti + fi * tr
    top = jnp.concatenate([p, -q], axis=1)
    bot = jnp.concatenate([q, p], axis=1)
    return jnp.concatenate([top, bot], axis=0).astype(BF16)


def _load_k1(a_ref, j):
    p = jnp.concatenate([a_ref[c, j] for c in range(HY_SLABS)], axis=1)
    re, im = _unpack_pair(p)
    return jnp.concatenate([re, im], axis=0).astype(BF16)


def _spectrum_kernel(a_ref, l1_ref, f2r_ref, f2i_ref, twr_ref, twi_ref, o_ref):
    base = pl.program_id(0) * FFT_K1_GROUP
    inv = 1.0 / l1_ref[...]
    for j in range(FFT_K1_GROUP):
        r = _twiddled_stack(f2r_ref, f2i_ref, twr_ref, twi_ref, base + j)
        x = jnp.dot(r, _load_k1(a_ref, j), preferred_element_type=F32) * inv
        o_ref[j, 0] = x[:FFT_N2].astype(o_ref.dtype)
        o_ref[j, 1] = x[FFT_N2:].astype(o_ref.dtype)


def filter_spectrum(a5, l1, tabs):
    g = FFT_K1_GROUP
    sq = lambda: _const_spec((FFT_N2, FFT_N2))
    return pl.pallas_call(
        _spectrum_kernel,
        grid=(FFT_N1 // g, HY_ORDER),
        in_specs=[pl.BlockSpec((None, HY_SLABS, g, FFT_N2, LANES), lambda i, o: (o, 0, i, 0, 0)),
                  pl.BlockSpec((1, HY_WIDTH), lambda i, o: (0, o)), sq(), sq(), sq(), sq()],
        out_specs=pl.BlockSpec((g, 2, FFT_N2, HY_WIDTH), lambda i, o: (i, 0, 0, o)),
        out_shape=jax.ShapeDtypeStruct((FFT_N1, 2, FFT_N2, HY_ORDER * HY_WIDTH), BF16),
        compiler_params=_cparams(("arbitrary", "arbitrary")),
        name="filter_spectrum",
    )(a5, l1, tabs["f2r"], tabs["f2i"], tabs["twr"], tabs["twi"])


def _convmid_kernel(a_ref, ks_ref, f2r_ref, f2i_ref, twr_ref, twi_ref, o_ref):
    base = pl.program_id(0) * FFT_K1_GROUP
    for j in range(FFT_K1_GROUP):
        r = _twiddled_stack(f2r_ref, f2i_ref, twr_ref, twi_ref, base + j)
        x = jnp.dot(r, _load_k1(a_ref, j), preferred_element_type=F32)
        xr, xi = x[:FFT_N2], x[FFT_N2:]
        kr, ki = ks_ref[j, 0].astype(F32), ks_ref[j, 1].astype(F32)
        y = jnp.concatenate([xr * kr - xi * ki, xr * ki + xi * kr], axis=0).astype(BF16)
        b = lax.dot_general(r, y, (((0,), (0,)), ((), ())), preferred_element_type=F32)
        p = _pack_pair(b[:FFT_N2], b[FFT_N2:])
        for c in range(HY_SLABS):
            o_ref[c, j] = p[:, c * LANES:(c + 1) * LANES]


def conv_mid(a5, kspec, order, tabs):
    B = a5.shape[0]
    g = FFT_K1_GROUP
    sq = lambda: _const_spec((FFT_N2, FFT_N2))
    blk = lambda: pl.BlockSpec((None, HY_SLABS, g, FFT_N2, LANES), lambda i, b: (b, 0, i, 0, 0))
    return pl.pallas_call(
        _convmid_kernel,
        grid=(FFT_N1 // g, B),
        in_specs=[blk(), pl.BlockSpec((g, 2, FFT_N2, HY_WIDTH), lambda i, b: (i, 0, 0, order)),
                  sq(), sq(), sq(), sq()],
        out_specs=blk(),
        out_shape=jax.ShapeDtypeStruct(a5.shape, jnp.uint32),
        compiler_params=_cparams(("arbitrary", "arbitrary")),
        name="conv_mid",
    )(a5, kspec, tabs["f2r"], tabs["f2i"], tabs["twr"], tabs["twi"])


def _convout_kernel(b_ref, g_ref, z_ref, gate_ref, bias_ref, o_ref):
    for t in range(b_ref.shape[0]):
        for j in range(FFT_N2_BLOCK):
            p = jnp.concatenate([_load_sub(b_ref, (t, c), j) for c in range(HY_SLABS)], axis=1)
            re, im = _unpack_pair(p)
            bp = jnp.concatenate([re, im], axis=0).astype(BF16)
            y = jnp.dot(g_ref[...], bp, preferred_element_type=F32)
            for q in range(2):
                for c in range(HY_SLABS):
                    _store_sub(o_ref, (2 * t + q, c), j,
                               y[q * FFT_H1:(q + 1) * FFT_H1, c * LANES:(c + 1) * LANES])
    o_ref[...] = gate_ref[...] * (o_ref[...] + z_ref[...] * bias_ref[...])


def conv_out(bp5, z6, z_which, gate6, gate_which, bias4, g):
    B2 = bp5.shape[0]
    nb = FFT_N2_BLOCK
    nat = lambda which: pl.BlockSpec((None, 2 * B2, HY_SLABS, FFT_H1, nb, LANES),
                                     lambda j: (which, 0, 0, 0, j, 0))
    return pl.pallas_call(
        _convout_kernel,
        grid=(FFT_N2 // nb,),
        in_specs=[pl.BlockSpec((B2, HY_SLABS, FFT_N1, nb, LANES), lambda j: (0, 0, 0, j, 0)),
                  _const_spec((2 * FFT_H1, 2 * FFT_N1)), nat(z_which), nat(gate_which),
                  _const_spec((HY_SLABS, 1, 1, LANES))],
        out_specs=pl.BlockSpec((2 * B2, HY_SLABS, FFT_H1, nb, LANES), lambda j: (0, 0, 0, j, 0)),
        out_shape=jax.ShapeDtypeStruct((2 * B2, HY_SLABS, FFT_H1, FFT_N2, LANES), F32),
        compiler_params=_cparams(("arbitrary",)),
        name="conv_out",
    )(bp5, g, z6, gate6, bias4)


def hyena_mixer(xs, kspec, bias, tabs):
    B = xs.shape[1]
    xs6 = xs.reshape(3, B, HY_SLABS, FFT_H1, FFT_N2, LANES)
    z6, z_which = xs6, 2
    for o in range(HY_ORDER):
        a = dft_first_axis(z6, z_which, tabs["f1c"], pair=2)
        bp = conv_mid(a, kspec, o, tabs)
        z = conv_out(bp, z6, z_which, xs6, o, bias[o].reshape(HY_SLABS, 1, 1, LANES), tabs["gc"])
        z6, z_which = z[None], 0
    return z.reshape(B, HY_SLABS, SEQ, LANES)


def _rope(x, cc, ss):
    lane = lax.broadcasted_iota(jnp.int32, x.shape, 1)
    n = x.shape[1]
    half = RET_DK // 2
    swapped = jnp.where(lane % RET_DK < half, pltpu.roll(x, n - half, axis=1),
                        pltpu.roll(x, half, axis=1))
    return x * cc + swapped * ss


def _retstate_kernel(k_ref, v_ref, cc_ref, ss_ref, wk_ref, cd_ref, o_ref, s_ref):
    d = pl.program_id(1)
    g = pl.program_id(2)
    C = RET_CHUNK

    @pl.when(g == 0)
    def _():
        s_ref[...] = jnp.zeros_like(s_ref)

    for j in range(RET_GROUP):
        ci = jnp.where(d == 0, j, RET_GROUP - 1 - j)
        r0 = pl.multiple_of(ci * C, C)
        s = s_ref[...]
        o_ref[ci] = jnp.concatenate(
            [s[h * RET_DK:(h + 1) * RET_DK, h * RET_DV:(h + 1) * RET_DV] for h in range(RET_HEADS)],
            axis=1)
        k = _rope(k_ref[pl.ds(r0, C), :], cc_ref[pl.ds(r0, C), :], ss_ref[pl.ds(r0, C), :])
        kw = (k * wk_ref[...]).astype(BF16)
        ds = lax.dot_general(kw, v_ref[pl.ds(r0, C), :], (((0,), (0,)), ((), ())),
                             preferred_element_type=F32)
        s_ref[...] = s * cd_ref[...] + ds


def retention_states(qk, v, tabs):
    B, S, _ = qk.shape
    rows = RET_GROUP * RET_CHUNK
    G = S // rows
    grp = lambda d, g: jnp.where(d == 0, g, G - 1 - g)
    return pl.pallas_call(
        _retstate_kernel,
        grid=(B, 2, G),
        in_specs=[
            pl.BlockSpec((None, rows, RET_QK), lambda b, d, g: (b, grp(d, g), 1)),
            pl.BlockSpec((None, rows, RET_V), lambda b, d, g: (b, grp(d, g), 0)),
            pl.BlockSpec((rows, RET_QK), lambda b, d, g: (grp(d, g), 0)),
            pl.BlockSpec((rows, RET_QK), lambda b, d, g: (grp(d, g), 0)),
            pl.BlockSpec((None, RET_CHUNK, RET_QK), lambda b, d, g: (d, 0, 0)),
            pl.BlockSpec((None, RET_QK, RET_V), lambda b, d, g: (d, 0, 0)),
        ],
        out_specs=pl.BlockSpec((None, None, RET_GROUP, RET_DK, RET_V),
                               lambda b, d, g: (b, d, grp(d, g), 0, 0)),
        out_shape=jax.ShapeDtypeStruct((B, 2, S // RET_CHUNK, RET_DK, RET_V), F32),
        scratch_shapes=[pltpu.VMEM((RET_QK, RET_V), F32)],
        compiler_params=_cparams(("arbitrary", "arbitrary", "arbitrary")),
        name="retention_states",
    )(qk, v, tabs["cc"], tabs["ss"], tabs["wk"], tabs["cd"])


def _retout_kernel(qk_ref, v_ref, gr_ref, cc_ref, ss_ref, st_ref, dec_ref, wq_ref, wk_ref, bd_ref,
                   o_ref):
    C = RET_CHUNK
    bd = bd_ref[...]
    lane = lax.broadcasted_iota(jnp.int32, (C, RET_QK), 1)
    for j in range(RET_GROUP):
        r0 = j * C
        cc = cc_ref[r0:r0 + C, :]
        ss = ss_ref[r0:r0 + C, :]
        q = _rope(qk_ref[r0:r0 + C, :RET_QK], cc, ss)
        k = (_rope(qk_ref[r0:r0 + C, RET_QK:], cc, ss) * (RET_DK ** -0.5)).astype(BF16)
        v = v_ref[r0:r0 + C, :]
        inner = []
        for h in range(RET_HEADS):
            qh = jnp.where(lane // RET_DK == h, q, 0.0).astype(BF16)
            s = lax.dot_general(qh, k, (((1,), (1,)), ((), ())), preferred_element_type=F32)
            s = (s * dec_ref[h]).astype(BF16)
            inner.append(jnp.dot(s, v[:, h * RET_DV:(h + 1) * RET_DV], preferred_element_type=F32))
        qq = jnp.concatenate([q * wq_ref[0], q * wq_ref[1]], axis=1).astype(BF16)
        sf = jnp.concatenate([st_ref[0, j]] * RET_HEADS, axis=0) * bd
        sb = jnp.concatenate([st_ref[1, j]] * RET_HEADS, axis=0) * bd
        sbd = jnp.concatenate([sf, sb], axis=0).astype(BF16)
        o = jnp.concatenate(inner, axis=1) + jnp.dot(qq, sbd, preferred_element_type=F32)
        outs = []
        for h in range(RET_HEADS):
            oh = o[:, h * RET_DV:(h + 1) * RET_DV]
            outs.append(oh * lax.rsqrt(jnp.mean(oh * oh, axis=-1, keepdims=True) + EPS))
        gr = gr_ref[r0:r0 + C, :]
        o_ref[r0:r0 + C, :] = (gr * jax.nn.sigmoid(gr)) * jnp.concatenate(outs, axis=1)


def retention_out(qk, v, gr, states, tabs):
    B, S, _ = qk.shape
    rows = RET_GROUP * RET_CHUNK
    G = S // rows
    return pl.pallas_call(
        _retout_kernel,
        grid=(B, G),
        in_specs=[
            pl.BlockSpec((None, rows, 2 * RET_QK), lambda b, g: (b, g, 0)),
            pl.BlockSpec((None, rows, RET_V), lambda b, g: (b, g, 0)),
            pl.BlockSpec((None, rows, RET_V), lambda b, g: (b, g, 0)),
            pl.BlockSpec((rows, RET_QK), lambda b, g: (g, 0)),
            pl.BlockSpec((rows, RET_QK), lambda b, g: (g, 0)),
            pl.BlockSpec((None, 2, RET_GROUP, RET_DK, RET_V), lambda b, g: (b, 0, g, 0, 0)),
            _const_spec((RET_HEADS, RET_CHUNK, RET_CHUNK)),
            _const_spec((2, RET_CHUNK, RET_QK)),
            _const_spec((2, RET_CHUNK, RET_QK)),
            _const_spec((RET_QK, RET_V)),
        ],
        out_specs=pl.BlockSpec((None, rows, RET_V), lambda b, g: (b, g, 0)),
        out_shape=jax.ShapeDtypeStruct((B, S, RET_V), F32),
        compiler_params=_cparams(("arbitrary", "arbitrary")),
        name="retention_out",
    )(qk, v, gr, tabs["cc"], tabs["ss"], states, tabs["dec"], tabs["wq"], tabs["wk"], tabs["bd"])


def _merge_rows(x, yh, yr, gm, ga, whb_ref, wrb_ref, wo_ref, g, sh, sc, wr_ref, br):
    ghy = 0.5 * jnp.tanh(0.5 * gm[:, :D_MODEL]) + 0.5
    grt = 0.5 * jnp.tanh(0.5 * gm[:, D_MODEL:]) + 0.5
    m = (ghy * jnp.dot(yh.astype(BF16), whb_ref[...], preferred_element_type=F32)
         + grt * jnp.dot(yr.astype(BF16), wrb_ref[...], preferred_element_type=F32))
    out = jnp.dot(m.astype(BF16), wo_ref[...], preferred_element_type=F32)
    x = x + ga * out
    h = _norm_mod(x, g, sc, sh)
    h_hi = h.astype(BF16)
    h_lo = (h - h_hi.astype(F32)).astype(BF16)
    logits = (jnp.dot(h_hi, wr_ref[0], preferred_element_type=F32)
              + jnp.dot(h_hi, wr_ref[1], preferred_element_type=F32)
              + jnp.dot(h_lo, wr_ref[0], preferred_element_type=F32)) + br
    tm = logits.shape[0]
    lane_e = lax.broadcasted_iota(jnp.int32, logits.shape, 1).astype(F32)
    lane_o = lax.broadcasted_iota(jnp.int32, (tm, LANES), 1)
    route = jnp.zeros((tm, LANES), F32)
    vals = []
    work = logits
    chosen = jnp.zeros(logits.shape, F32)
    for r in range(TOP_K):
        m_r = jnp.max(work, axis=-1, keepdims=True)
        i_r = jnp.min(jnp.where(work == m_r, lane_e, float(N_EXPERTS)), axis=-1, keepdims=True)
        hit = lane_e == i_r
        work = jnp.where(hit, -jnp.inf, work)
        chosen = chosen + hit.astype(F32)
        vals.append(m_r)
        route = jnp.where(lane_o == r, i_r, route)
    exps = [jnp.exp(v - vals[0]) for v in vals]
    den = exps[0] + exps[1] + exps[2] + exps[3]
    for r in range(TOP_K):
        route = jnp.where(lane_o == TOP_K + r, exps[r] / den, route)
    return x, h, route, jnp.sum(chosen, axis=0, keepdims=True)


def _merge_kernel(x_ref, yh_ref, yr_ref, gm_ref, ga_ref, whb_ref, wrb_ref, wo_ref,
                  g_ref, sh_ref, sc_ref, wr_ref, br_ref, xo_ref, h_ref, rt_ref, cnt_ref):
    tm = TOK_TILE

    @pl.when(pl.program_id(0) == 0)
    def _():
        cnt_ref[...] = jnp.zeros_like(cnt_ref)

    for half in range(MERGE_HALVES):
        rows = slice(half * tm, (half + 1) * tm)
        yh = jnp.concatenate([yh_ref[c, rows, :] for c in range(HY_SLABS)], axis=1)
        x, h, route, counts = _merge_rows(
            x_ref[rows, :], yh, yr_ref[rows, :], gm_ref[rows, :].astype(F32), ga_ref[...],
            whb_ref, wrb_ref, wo_ref, g_ref[...], sh_ref[...], sc_ref[...], wr_ref, br_ref[...])
        xo_ref[rows, :] = x
        rt_ref[rows, :] = route
        cnt_ref[...] += counts
        h_half = h_ref.at[rows]
        for j in range(ROW_TILES):
            _store_sub(h_half, (), j, h[:, j * LANES:(j + 1) * LANES])


def merge_and_route(x2, yh, yr, gm, mod, g_ffn, whb, wrb, wo, w_router, b_router, layer):
    T, D = x2.shape
    tm = TOK_TILE * MERGE_HALVES
    per_b = SEQ // tm
    mspec = lambda piece: pl.BlockSpec((None, None, None, 1, D),
                                       lambda i: (layer, piece, i // per_b, 0, 0))
    row = lambda w: pl.BlockSpec((tm, w), lambda i: (i, 0))
    wsp = lambda a, b: pl.BlockSpec((None, a, b), lambda i: (layer, 0, 0))
    return pl.pallas_call(
        _merge_kernel,
        grid=(T // tm,),
        in_specs=[row(D),
                  pl.BlockSpec((None, HY_SLABS, tm, LANES), lambda i: (i // per_b, 0, i % per_b, 0)),
                  row(RET_V), row(2 * D), mspec(2),
                  wsp(HY_WIDTH, D), wsp(RET_V, D), wsp(D, D),
                  wsp(1, D), mspec(3), mspec(4),
                  pl.BlockSpec((None, 2, D, N_EXPERTS), lambda i: (layer, 0, 0, 0)),
                  wsp(1, N_EXPERTS)],
        out_specs=[row(D), pl.BlockSpec((tm, ROW_TILES, LANES), lambda i: (i, 0, 0)), row(LANES),
                   _const_spec((1, N_EXPERTS))],
        out_shape=[jax.ShapeDtypeStruct((T, D), F32), jax.ShapeDtypeStruct((T, ROW_TILES, LANES), F32),
                   jax.ShapeDtypeStruct((T, LANES), F32), jax.ShapeDtypeStruct((1, N_EXPERTS), F32)],
        compiler_params=_cparams(("arbitrary",)),
        name="merge_and_route",
    )(x2, yh, yr, gm, mod, whb, wrb, wo, g_ffn, mod, mod, w_router, b_router)


def _slot_kernel(rt_ref, cnt_ref, tril_ref, upper_ref, o_ref, carry_ref):
    @pl.when(pl.program_id(0) == 0)
    def _():
        carry_ref[...] = jnp.zeros_like(carry_ref)

    padded = jnp.floor((cnt_ref[...] + (EXPERT_TILE - 1)) * (1.0 / EXPERT_TILE)) * EXPERT_TILE
    pad_start = jnp.dot(jnp.broadcast_to(padded, (8, N_EXPERTS)), upper_ref[...],
                        preferred_element_type=F32, precision=HIGHEST)[0:1]

    rt = rt_ref[...]
    tm = rt.shape[0]
    lane_e = lax.broadcasted_iota(jnp.int32, (tm, N_EXPERTS), 1).astype(F32)
    lane_o = lax.broadcasted_iota(jnp.int32, (tm, LANES), 1)
    hits = [lane_e == rt[:, r:r + 1] for r in range(TOP_K)]
    onehot = sum(h.astype(F32) for h in hits)
    before = (jnp.dot(tril_ref[...], onehot.astype(BF16), preferred_element_type=F32)
              + carry_ref[...] + pad_start)
    out = jnp.zeros((tm, LANES), F32)
    for r in range(TOP_K):
        slot_r = jnp.sum(jnp.where(hits[r], before, 0.0), axis=-1, keepdims=True)
        out = jnp.where(lane_o == r, slot_r, out)
    o_ref[...] = out.astype(jnp.int32)
    carry_ref[...] += jnp.sum(onehot, axis=0, keepdims=True)


def assignment_slots(route, counts):
    T = route.shape[0]
    tm = RANK_TILE
    tril = jnp.asarray(np.tril(np.ones((tm, tm)), -1), BF16)
    upper = jnp.asarray(np.triu(np.ones((N_EXPERTS, N_EXPERTS)), 1), F32)
    return pl.pallas_call(
        _slot_kernel,
        grid=(T // tm,),
        in_specs=[pl.BlockSpec((tm, LANES), lambda i: (i, 0)), _const_spec((1, N_EXPERTS)),
                  _const_spec((tm, tm)), _const_spec((N_EXPERTS, N_EXPERTS))],
        out_specs=pl.BlockSpec((tm, LANES), lambda i: (i, 0)),
        out_shape=jax.ShapeDtypeStruct((T, LANES), jnp.int32),
        scratch_shapes=[pltpu.VMEM((1, N_EXPERTS), F32)],
        compiler_params=_cparams(("arbitrary",)),
        name="assignment_slots",
    )(route, counts, tril, upper)


def _dispatch_kernel(dest_ref, pe_ref, h_ref, xs_hbm, zeros, sem, zsem):
    tm = DISPATCH_TILE
    t0 = pl.program_id(0) * tm

    @pl.when(pl.program_id(0) == 0)
    def _():
        zeros[...] = jnp.zeros_like(zeros)

        def zero_block(e):
            end = pe_ref[e]
            begin = pe_ref[jnp.maximum(e - 1, 0)]
            nonempty = jnp.logical_or(jnp.logical_and(e == 0, end > 0), end > begin)
            start = pl.multiple_of(jnp.maximum(end - EXPERT_TILE, 0), EXPERT_TILE)
            return nonempty, pltpu.make_async_copy(zeros, xs_hbm.at[pl.ds(start, EXPERT_TILE)], zsem)

        def start_zero(e, carry):
            nonempty, cp = zero_block(e)

            @pl.when(nonempty)
            def _():
                cp.start()
            return carry
        lax.fori_loop(0, N_EXPERTS, start_zero, 0)

        def wait_zero(e, carry):
            nonempty, cp = zero_block(e)

            @pl.when(nonempty)
            def _():
                cp.wait()
            return carry
        lax.fori_loop(0, N_EXPERTS, wait_zero, 0)

    def row_copy(i, d):
        return pltpu.make_async_copy(h_ref.at[i], xs_hbm.at[d], sem)

    def issue(i, carry):
        for r in range(TOP_K):
            row_copy(i, dest_ref[(t0 + i) * TOP_K + r]).start(priority=r % 2)
        return carry
    lax.fori_loop(0, tm, issue, 0, unroll=4)

    def drain(i, carry):
        row_copy(0, 0).wait()
        return carry
    lax.fori_loop(0, tm * TOP_K, drain, 0, unroll=8)


def dispatch(dest, pad_ends, h, n_rows):
    T = h.shape[0]
    tm = DISPATCH_TILE
    grid_spec = pltpu.PrefetchScalarGridSpec(
        num_scalar_prefetch=2,
        grid=(T // tm,),
        in_specs=[pl.BlockSpec((tm, ROW_TILES, LANES), lambda i, d, pe: (i, 0, 0))],
        out_specs=pl.BlockSpec(memory_space=pl.ANY),
        scratch_shapes=[pltpu.VMEM((EXPERT_TILE, ROW_TILES, LANES), h.dtype), pltpu.SemaphoreType.DMA,
                        pltpu.SemaphoreType.DMA],
    )
    return pl.pallas_call(
        _dispatch_kernel,
        grid_spec=grid_spec,
        out_shape=jax.ShapeDtypeStruct((n_rows, ROW_TILES, LANES), h.dtype),
        compiler_params=_cparams(("arbitrary",)),
        name="dispatch",
    )(dest, pad_ends, h)


def _expert_kernel(be_ref, nb_ref, x_ref, wgu_ref, bgu_ref, wd_ref, bd_ref, perm_ref, o_ref,
                   wgu_s, wd_s):
    i = pl.program_id(0)
    prev = be_ref[jnp.maximum(i - 1, 0)]
    new_expert = jnp.logical_or(i == 0, be_ref[i] != prev)
    n_chunk = 2 * D_FF // (2 * LANES)

    @pl.when(new_expert)
    def _():
        for c in range(n_chunk):
            cols = slice(c * 2 * LANES, (c + 1) * 2 * LANES)
            wgu_s[:, cols] = jnp.dot(wgu_ref[:, cols].astype(BF16), perm_ref[...],
                                     preferred_element_type=F32).astype(BF16)
        wd_s[...] = wd_ref[...].astype(BF16)

    @pl.when(i < nb_ref[0])
    def _():
        x = jnp.concatenate([_load_sub(x_ref, (), j) for j in range(ROW_TILES)], axis=1)
        gu = jnp.dot(x.astype(BF16), wgu_s[...], preferred_element_type=F32) + bgu_ref[...]
        acts = []
        for c in range(n_chunk):
            gate = jnp.minimum(gu[:, c * 2 * LANES:c * 2 * LANES + LANES], SWIGLU_LIMIT)
            up = jnp.clip(gu[:, c * 2 * LANES + LANES:(c + 1) * 2 * LANES], -SWIGLU_LIMIT, SWIGLU_LIMIT)
            acts.append((up + 1.0) * (gate * jax.nn.sigmoid(SWIGLU_ALPHA * gate)))
        act = jnp.concatenate(acts, axis=1).astype(BF16)
        y = jnp.dot(act, wd_s[...], preferred_element_type=F32) + bd_ref[...]
        for j in range(ROW_TILES):
            _store_sub(o_ref, (), j, y[:, j * LANES:(j + 1) * LANES])

    @pl.when(i >= nb_ref[0])
    def _():
        o_ref[...] = jnp.zeros_like(o_ref)


def expert_ffn(block_e, n_used, xs, w_gate_up, bgu, w_down, bd, layer):
    P = xs.shape[0]
    D = D_MODEL
    tm = EXPERT_TILE
    perm = np.zeros((2 * LANES, 2 * LANES))
    j = np.arange(LANES)
    perm[2 * j, j] = 1.0
    perm[2 * j + 1, LANES + j] = 1.0
    grid_spec = pltpu.PrefetchScalarGridSpec(
        num_scalar_prefetch=2,
        grid=(P // tm,),
        in_specs=[
            pl.BlockSpec((tm, ROW_TILES, LANES), lambda i, be, nb: (jnp.minimum(i, nb[0] - 1), 0, 0)),
            pl.BlockSpec((None, None, D, 2 * D_FF), lambda i, be, nb: (layer, be[i], 0, 0)),
            pl.BlockSpec((None, None, 1, 2 * D_FF), lambda i, be, nb: (layer, be[i], 0, 0)),
            pl.BlockSpec((None, None, D_FF, D), lambda i, be, nb: (layer, be[i], 0, 0)),
            pl.BlockSpec((None, None, 1, D), lambda i, be, nb: (layer, be[i], 0, 0)),
            pl.BlockSpec((2 * LANES, 2 * LANES), lambda i, be, nb: (0, 0)),
        ],
        out_specs=pl.BlockSpec((tm, ROW_TILES, LANES), lambda i, be, nb: (i, 0, 0)),
        scratch_shapes=[pltpu.VMEM((D, 2 * D_FF), BF16), pltpu.VMEM((D_FF, D), BF16)],
    )
    return pl.pallas_call(
        _expert_kernel,
        grid_spec=grid_spec,
        out_shape=jax.ShapeDtypeStruct((P, ROW_TILES, LANES), F32),
        compiler_params=_cparams(("arbitrary",)),
        name="expert_ffn",
    )(block_e, n_used, xs, w_gate_up, bgu, w_down, bd, jnp.asarray(perm, BF16))


def _combine_kernel(dest_ref, x_ref, rt_ref, ga_ref, g_ref, yb_hbm, o_ref, buf, sem, *, final):
    tm = COMBINE_TILE
    step = pl.program_id(0)
    slot = step % 2

    def row_copy(sl, i, r, d):
        return pltpu.make_async_copy(yb_hbm.at[d], buf.at[sl, r, i], sem.at[sl])

    def issue_step(st, sl):
        def issue(i, carry):
            for r in range(TOP_K):
                row_copy(sl, i, r, dest_ref[(st * tm + i) * TOP_K + r]).start(priority=r % 2)
            return carry
        lax.fori_loop(0, tm, issue, 0, unroll=4)

    @pl.when(step == 0)
    def _():
        issue_step(0, 0)

    @pl.when(step + 1 < pl.num_programs(0))
    def _():
        issue_step(step + 1, 1 - slot)

    def drain(i, carry):
        row_copy(slot, 0, 0, 0).wait()
        return carry
    lax.fori_loop(0, tm * TOP_K, drain, 0, unroll=8)

    rt = rt_ref[...]
    gates = [jnp.broadcast_to(rt[:, TOP_K + r:TOP_K + r + 1], (tm, LANES)) for r in range(TOP_K)]
    cols = []
    for j in range(ROW_TILES):
        s = _load_sub(buf, (slot, 0), j) * gates[0]
        for r in range(1, TOP_K):
            s = s + _load_sub(buf, (slot, r), j) * gates[r]
        cols.append(s)
    y = jnp.concatenate(cols, axis=1)
    x = x_ref[...] + ga_ref[...] * y
    if final:
        x = (x * lax.rsqrt(jnp.mean(x * x, axis=-1, keepdims=True) + EPS)) * g_ref[...]
    o_ref[...] = x


def combine(dest, x2, route, mod, g_final, yb, layer, final):
    T, D = x2.shape
    tm = COMBINE_TILE
    per_b = SEQ // tm
    grid_spec = pltpu.PrefetchScalarGridSpec(
        num_scalar_prefetch=1,
        grid=(T // tm,),
        in_specs=[
            pl.BlockSpec((tm, D), lambda i, d: (i, 0)),
            pl.BlockSpec((tm, LANES), lambda i, d: (i, 0)),
            pl.BlockSpec((None, None, None, 1, D), lambda i, d: (layer, 5, i // per_b, 0, 0)),
            pl.BlockSpec((1, D), lambda i, d: (0, 0)),
            pl.BlockSpec(memory_space=pl.ANY),
        ],
        out_specs=pl.BlockSpec((tm, D), lambda i, d: (i, 0)),
        scratch_shapes=[pltpu.VMEM((2, TOP_K, tm, ROW_TILES, LANES), F32),
                        pltpu.SemaphoreType.DMA((2,))],
    )
    return pl.pallas_call(
        functools.partial(_combine_kernel, final=final),
        grid_spec=grid_spec,
        out_shape=jax.ShapeDtypeStruct((T, D), F32),
        compiler_params=_cparams(("arbitrary",)),
        name="combine",
    )(dest, x2, route, mod, g_final, yb)


def moe_ffn_residual(x2, h, route, counts, mod, g_final, w_gate_up, bgu, w_down, bd, layer, final):
    T = h.shape[0]
    tm = EXPERT_TILE
    slots = assignment_slots(route, counts)
    counts = counts[0].astype(jnp.int32)
    padded = ((counts + tm - 1) // tm) * tm
    pad_ends = jnp.cumsum(padded)
    n_blocks = -(-(T * TOP_K + N_EXPERTS * (tm - 1)) // tm)
    block_start = jnp.arange(n_blocks, dtype=jnp.int32) * tm
    block_e = jnp.minimum(jnp.sum(pad_ends[None, :] <= block_start[:, None], axis=1),
                          N_EXPERTS - 1).astype(jnp.int32)
    n_used = (pad_ends[-1:] // tm).astype(jnp.int32)
    dest = slots[:, :TOP_K].reshape(-1)
    xs = dispatch(dest, pad_ends.astype(jnp.int32), h, n_blocks * tm)
    yb = expert_ffn(block_e, n_used, xs, w_gate_up, bgu, w_down, bd, layer)
    return combine(dest, x2, route, mod, g_final, yb, layer, final)


def kernel(x, c, norm_mix_g, norm_ffn_g, w_mod, b_mod, w_in, hy_conv_w, hy_conv_b, hy_w1, hy_b1, hy_w2, hy_b2, hy_w3, hy_b3, hy_w4, hy_freq, hy_bias, w_hy_br, w_ret_br, w_out, w_router, b_router, w_gate_up, b_gate_up, w_down, b_down, final_g):
    B, S, D = x.shape
    L = w_mod.shape[0]
    T = B * S
    tabs = _tables()

    w_in_bf = w_in.astype(BF16)
    whb = w_hy_br.astype(BF16)
    wrb = w_ret_br.astype(BF16)
    wo = w_out.astype(BF16)
    wr_hi = w_router.astype(BF16)
    wr_split = jnp.stack([wr_hi, (w_router - wr_hi.astype(F32)).astype(BF16)], axis=1)
    E, F = N_EXPERTS, D_FF
    bgu = b_gate_up.reshape(L, E, F // LANES, LANES, 2).transpose(0, 1, 2, 4, 3).reshape(L, E, 1, 2 * F)
    bdn = b_down.reshape(L, E, 1, D)
    w1p = jnp.pad(hy_w1, ((0, 0), (0, HY_EMB_PAD - HY_EMB), (0, 0)))
    w4_hi = hy_w4.astype(BF16)
    w4_split = jnp.stack([w4_hi, (hy_w4 - w4_hi.astype(F32)).astype(BF16)], axis=1)
    vec = lambda a: a.reshape(L, 1, -1)

    c_pad = jnp.pad(c, ((0, 8 - B), (0, 0)))
    mod = modulation(c_pad, w_mod, b_mod)[:, :B]
    mod = mod.reshape(L, B, N_MOD, 1, D).transpose(0, 2, 1, 3, 4)

    x2 = x.reshape(T, D)
    for l in range(L):
        u, qk, v, gr, gm = in_projection(x2, vec(norm_mix_g), mod, w_in_bf, l)
        xs = short_conv(u.reshape(B, S, -1), hy_conv_w, vec(hy_conv_b), l)
        col = lambda a: a.reshape(L, -1, 1)
        kraw, l1 = hyena_filter(tabs, w1p, col(hy_b1), hy_w2, col(hy_b2), hy_w3, col(hy_b3),
                                w4_split, col(hy_freq), l)
        ka = dft_first_axis(kraw.reshape(1, HY_ORDER, HY_SLABS, FFT_N1, FFT_N2, LANES), 0, tabs["f1"],
                            pair=1)
        kspec = filter_spectrum(ka, l1, tabs)
        y_hy = hyena_mixer(xs, kspec, hy_bias[l], tabs)
        qk3, v3, gr3 = qk.reshape(B, S, -1), v.reshape(B, S, -1), gr.reshape(B, S, -1)
        states = retention_states(qk3, v3, tabs)
        y_ret = retention_out(qk3, v3, gr3, states, tabs)
        x2, h_ffn, route, counts = merge_and_route(
            x2, y_hy, y_ret.reshape(T, -1), gm, mod, vec(norm_ffn_g),
            whb, wrb, wo, wr_split, vec(b_router), l)
        x2 = moe_ffn_residual(x2, h_ffn, route, counts, mod, final_g.reshape(1, D), w_gate_up, bgu,
                              w_down, bdn, l, final=(l == L - 1))
    return x2.reshape(B, S, D)
```
